```python
import jax, jax.numpy as jnp
from jax import lax
import numpy as np

D_MODEL = 1024
BATCH = 2
SEQ = 8192
DEPTH = 2

GRID_W = 64
CTX_LEN = 256
EXPAND = 2
D_INNER = EXPAND * D_MODEL
HG_HEAD_DIM = 128
HG_HEADS = D_INNER // HG_HEAD_DIM
HG_STREAMS = 5
FT_GROUPS = 8
FT_GROUP_DIM = D_INNER // FT_GROUPS
CHUNK = 64
N_MIXERS = 2
N_HGRN = (DEPTH + 1) // 2
N_FOURIER = DEPTH // 2
EPS = 1e-6

kernel_name = "hgrn2_fnet_interleaved_prefix_dit"


def rms_norm(x, g):
    xf = x.astype(jnp.float32)
    var = jnp.mean(xf * xf, axis=-1, keepdims=True)
    return (xf * lax.rsqrt(var + EPS)).astype(x.dtype) * g


def adaln(cvec, w, b):
    m = jax.nn.silu(cvec) @ w + b
    return jnp.split(m, 3, axis=-1)


def chunk_scan(q, k, v, g, s0):
    B, H, N, Dk = q.shape
    Dv = v.shape[-1]
    nc = N // CHUNK

    def to_chunks(a):
        a = a.astype(jnp.float32).reshape(B, H, nc, CHUNK, a.shape[-1])
        return jnp.moveaxis(a, 2, 0)

    causal = jnp.tril(jnp.ones((CHUNK, CHUNK), dtype=bool))[:, :, None]

    def step(S, inp):
        qc, kc, vc, gc = inp
        b = jnp.cumsum(gc, axis=-2)
        diff = b[..., :, None, :] - b[..., None, :, :]
        decay = jnp.exp(jnp.where(causal, diff, -jnp.inf))
        scores = jnp.einsum('bhtc,bhtsc,bhsc->bhts', qc, decay, kc)
        o = (jnp.einsum('bhts,bhsv->bhtv', scores, vc)
             + jnp.einsum('bhtc,bhcv->bhtv', qc * jnp.exp(b), S))
        b_last = b[..., -1:, :]
        S_new = (jnp.exp(b_last)[..., 0, :, None] * S
                 + jnp.einsum('bhsc,bhsv->bhcv', kc * jnp.exp(b_last - b), vc))
        return S_new, o

    S_fin, o = lax.scan(step, s0, (to_chunks(q), to_chunks(k), to_chunks(v), to_chunks(g)))
    o = jnp.moveaxis(o, 0, 2).reshape(B, H, N, Dv)
    return o, S_fin


def hgrn_lower_bound(logits, j):
    p = jax.nn.softmax(logits.astype(jnp.float32), axis=0)
    return jnp.cumsum(p, axis=0)[j]


def hgrn_project(h, w_in, lb):
    B, N, _ = h.shape
    p = (h @ w_in).reshape(B, N, HG_STREAMS, HG_HEADS, HG_HEAD_DIM)
    p = jnp.transpose(p, (2, 0, 3, 1, 4))
    q = jax.nn.silu(p[0])
    lbh = lb.reshape(2, 1, HG_HEADS, 1, HG_HEAD_DIM)
    f = lbh + (1.0 - lbh) * jax.nn.sigmoid(p[1:3].astype(jnp.float32))
    return q, 1.0 - f, jnp.log(f), p[3], p[4]


def hgrn_readout(o, z, norm_g, w_out):
    y = rms_norm(o, norm_g).astype(z.dtype) * jax.nn.silu(z)
    B, H, N, d = y.shape
    return jnp.transpose(y, (0, 2, 1, 3)).reshape(B, N, H * d) @ w_out


def hgrn_mixer(h_ctx, h_lat, w_in, lb, norm_g, w_out, ctx_out):
    qc, kc, gc, vc, zc = hgrn_project(h_ctx, w_in, lb)
    ql, kl, gl, vl, zl = hgrn_project(h_lat, w_in, lb)
    B = h_lat.shape[0]
    s0 = jnp.zeros((B, HG_HEADS, HG_HEAD_DIM, HG_HEAD_DIM), jnp.float32)
    flip = lambda a: jnp.flip(a, axis=2)
    o_cf, s_cf = chunk_scan(qc, kc[0], vc, gc[0], s0)
    o_cb, s_cb = chunk_scan(flip(qc), flip(kc[1]), flip(vc), flip(gc[1]), s0)
    o_lf, _ = chunk_scan(ql, kl[0], vl, gl[0], s_cf)
    o_lb, _ = chunk_scan(flip(ql), flip(kl[1]), flip(vl), flip(gl[1]), s_cb)
    y_lat = hgrn_readout(o_lf + flip(o_lb), zl, norm_g, w_out)
    y_ctx = hgrn_readout(o_cf + flip(o_cb), zc, norm_g, w_out) if ctx_out else None
    return y_lat, y_ctx


def fourier_mixer(h, w_in, w_out):
    B, N, _ = h.shape
    u, z = jnp.split(h @ w_in, 2, axis=-1)
    ug = u.astype(jnp.float32).reshape(B, N, FT_GROUPS, FT_GROUP_DIM)
    y = jnp.fft.fft2(ug, axes=(1, 3), norm='ortho').real.reshape(B, N, D_INNER)
    return (y.astype(h.dtype) * jax.nn.silu(z)) @ w_out


def setup_inputs(seed: int = 0) -> dict:
    key = jax.random.key(seed)
    ks = jax.random.split(key, 14)
    D, E = D_MODEL, D_INNER
    nrm = jax.random.normal
    return {
        'x': nrm(ks[0], (BATCH, SEQ, D), jnp.float32),
        'c': nrm(ks[1], (BATCH, D), jnp.float32),
        'ctx': nrm(ks[2], (BATCH, CTX_LEN, D), jnp.float32),
        'c_ctx': nrm(ks[3], (D,), jnp.float32),
        'ada_w': nrm(ks[4], (DEPTH, D, 3 * D), jnp.float32) * (0.5 * D ** -0.5),
        'ada_b': nrm(ks[5], (DEPTH, 3 * D), jnp.float32) * 0.01,
        'norm_g': 1.0 + 0.05 * nrm(ks[6], (DEPTH, D), jnp.float32),
        'hg_w_in': nrm(ks[7], (N_HGRN, D, HG_STREAMS * E), jnp.float32) * D ** -0.5,
        'hg_lb_logits': 0.5 * nrm(ks[8], (N_HGRN + 1, 2, E), jnp.float32),
        'hg_norm_g': 1.0 + 0.05 * nrm(ks[9], (N_HGRN, HG_HEAD_DIM), jnp.float32),
        'hg_w_out': nrm(ks[10], (N_HGRN, E, D), jnp.float32) * E ** -0.5,
        'ft_w_in': nrm(ks[11], (N_FOURIER, D, 2 * E), jnp.float32) * D ** -0.5,
        'ft_w_out': nrm(ks[12], (N_FOURIER, E, D), jnp.float32) * E ** -0.5,
        'final_g': 1.0 + 0.05 * nrm(ks[13], (D,), jnp.float32),
    }


def reference(x, c, ctx, c_ctx, ada_w, ada_b, norm_g, hg_w_in, hg_lb_logits, hg_norm_g,
              hg_w_out, ft_w_in, ft_w_out, final_g):
    x_lat, x_ctx = x, ctx
    for i in range(DEPTH):
        ctx_out = i < DEPTH - 1
        j = i // N_MIXERS
        sh_l, sc_l, gt_l = adaln(c, ada_w[i], ada_b[i])
        h_lat = rms_norm(x_lat, norm_g[i]) * (1.0 + sc_l[:, None]) + sh_l[:, None]
        if i % N_MIXERS == 0:
            sh_c, sc_c, gt_c = adaln(c_ctx, ada_w[i], ada_b[i])
            h_ctx = rms_norm(x_ctx, norm_g[i]) * (1.0 + sc_c) + sh_c
            lb = hgrn_lower_bound(hg_lb_logits, j)
            y_lat, y_ctx = hgrn_mixer(h_ctx, h_lat, hg_w_in[j], lb, hg_norm_g[j], hg_w_out[j], ctx_out)
        else:
            y_lat = fourier_mixer(h_lat, ft_w_in[j], ft_w_out[j])
            if ctx_out:
                sh_c, sc_c, gt_c = adaln(c_ctx, ada_w[i], ada_b[i])
                h_ctx = rms_norm(x_ctx, norm_g[i]) * (1.0 + sc_c) + sh_c
                y_ctx = fourier_mixer(h_ctx, ft_w_in[j], ft_w_out[j])
        x_lat = x_lat + gt_l[:, None] * y_lat
        if ctx_out:
            x_ctx = x_ctx + gt_c * y_ctx
    return rms_norm(x_lat, final_g)
```

```python
import functools

import numpy as np
import jax
import jax.numpy as jnp
from jax import lax
from jax.experimental import pallas as pl
from jax.experimental.pallas import tpu as pltpu

F32 = jnp.float32
BF16 = jnp.bfloat16

EPS = 1e-6
HEAD_DIM = 128
HG_STREAMS = 5
FT_GROUPS = 8
SCAN_CHUNK = 64
SCAN_HEADS = 4
ROW_TILE = 512
MOD_ROWS = 8
VMEM_LIMIT = 56 * 1024 * 1024


def _cparams(sem):
    return pltpu.CompilerParams(dimension_semantics=sem, vmem_limit_bytes=VMEM_LIMIT)


def _sigmoid(x):
    return 1.0 / (1.0 + jnp.exp(-x))


def _silu(x):
    return x * _sigmoid(x)


def _modulated_norm(x, g, shift, scale):
    var = jnp.mean(x * x, axis=-1, keepdims=True)
    return (x * lax.rsqrt(var + EPS)) * g * (1.0 + scale) + shift


def _ada_kernel(cv_ref, w_ref, b_ref, o_ref):
    a = _silu(cv_ref[...])
    o_ref[...] = jnp.dot(a, w_ref[...], preferred_element_type=F32,
                         precision=lax.Precision.HIGHEST) + b_ref[...]


def _ada_table(cv, ada_w, ada_b):
    depth, d, d3 = ada_w.shape
    tn = 1024
    return pl.pallas_call(
        _ada_kernel,
        grid=(depth, d3 // tn),
        in_specs=[
            pl.BlockSpec((MOD_ROWS, d), lambda l, j: (0, 0)),
            pl.BlockSpec((None, d, tn), lambda l, j: (l, 0, j)),
            pl.BlockSpec((None, 1, tn), lambda l, j: (l, 0, j)),
        ],
        out_specs=pl.BlockSpec((None, MOD_ROWS, tn), lambda l, j: (l, 0, j)),
        out_shape=jax.ShapeDtypeStruct((depth, MOD_ROWS, d3), F32),
        compiler_params=_cparams(("arbitrary", "arbitrary")),
        name="ada_table",
    )(cv, ada_w, ada_b.reshape(depth, 1, d3))


def _proj0_kernel(ctx_ref, x_ref, mod_ref, ng_ref, w_ref, lbl_ref, a_ref, g_ref, h_ref,
                  *, d, e, tiles_per_batch, lb_index, sub):
    i = pl.program_id(0)
    j = pl.program_id(1)
    tm = h_ref.shape[0]

    def fill(src_ref, row):
        m = mod_ref[pl.ds(row, 1), :]
        h_ref[...] = _modulated_norm(src_ref[...], ng_ref[...], m[:, 0:d], m[:, d:2 * d]).astype(BF16)

    @pl.when(j == 0)
    def _():
        @pl.when(i == 0)
        def _():
            fill(ctx_ref, 2)

        @pl.when(i > 0)
        def _():
            fill(x_ref, (i - 1) // tiles_per_batch)

    def acc(c):
        return jnp.dot(h_ref[...], w_ref[:, c * sub:(c + 1) * sub], preferred_element_type=F32)

    nsub = e // sub

    @pl.when(j == 0)
    def _():
        for c in range(nsub):
            a_ref[:, c * sub:(c + 1) * sub] = _silu(acc(c)).astype(BF16)

    @pl.when((j == 1) | (j == 2))
    def _():
        dirn = j - 1
        logits = lbl_ref[:, pl.ds(dirn, 1), :]
        mx = jnp.max(logits, axis=0, keepdims=True)
        ex = jnp.exp(logits - mx)
        p = ex / jnp.sum(ex, axis=0, keepdims=True)
        lb_full = jnp.sum(p[0:lb_index + 1], axis=0)
        for c in range(nsub):
            lb = lb_full[:, c * sub:(c + 1) * sub]
            sig = _sigmoid(acc(c))
            f = lb + (1.0 - lb) * sig
            a_ref[:, c * sub:(c + 1) * sub] = ((1.0 - lb) * (1.0 - sig)).astype(BF16)
            g_ref[:, c * sub:(c + 1) * sub] = jnp.log(f).astype(BF16)

    @pl.when(j == 3)
    def _():
        for c in range(nsub):
            a_ref[:, c * sub:(c + 1) * sub] = acc(c).astype(BF16)

    @pl.when(j == 4)
    def _():
        for c in range(nsub):
            a_ref[:, c * sub:(c + 1) * sub] = _silu(acc(c)).astype(BF16)


def _proj0(ctx2d, x2d, mods, norm_g, w_bf, lb_logits, *, n_lat, lb_index):
    rows_ctx, d = ctx2d.shape
    rows_lat = x2d.shape[0]
    e = w_bf.shape[1] // HG_STREAMS
    tm = ROW_TILE
    assert rows_ctx == tm and n_lat % tm == 0
    n_tiles = 1 + rows_lat // tm
    rows = rows_ctx + rows_lat
    nl = lb_logits.shape[0]
    kern = functools.partial(_proj0_kernel, d=d, e=e, tiles_per_batch=n_lat // tm,
                             lb_index=lb_index, sub=512)
    return pl.pallas_call(
        kern,
        grid=(n_tiles, HG_STREAMS),
        in_specs=[
            pl.BlockSpec((tm, d), lambda i, j: (0, 0)),
            pl.BlockSpec((tm, d), lambda i, j: (jnp.maximum(i - 1, 0), 0)),
            pl.BlockSpec((None, MOD_ROWS, 3 * d), lambda i, j: (0, 0, 0)),
            pl.BlockSpec((None, 1, d), lambda i, j: (0, 0, 0)),
            pl.BlockSpec((d, e), lambda i, j: (0, j)),
            pl.BlockSpec((nl, 2, e), lambda i, j: (0, 0, 0)),
        ],
        out_specs=[
            pl.BlockSpec((tm, e), lambda i, j: (i, j)),
            pl.BlockSpec((tm, e), lambda i, j: (i, jnp.clip(j - 1, 0, 1))),
        ],
        out_shape=[
            jax.ShapeDtypeStruct((rows, HG_STREAMS * e), BF16),
            jax.ShapeDtypeStruct((rows, 2 * e), BF16),
        ],
        scratch_shapes=[pltpu.VMEM((tm, d), BF16)],
        compiler_params=_cparams(("arbitrary", "arbitrary")),
        name="proj0",
    )(ctx2d, x2d, mods, norm_g, w_bf, lb_logits)


def _scan_kernel(qf_ref, kf_ref, gf_ref, vf_ref, qb_ref, kb_ref, gb_ref, vb_ref,
                 of_ref, ob_ref, st_ref, *, n_chunks):
    s = pl.program_id(2)
    c_len = SCAN_CHUNK
    hd = HEAD_DIM

    @pl.when(s == 0)
    def _():
        st_ref[...] = jnp.zeros_like(st_ref)

    row = lax.broadcasted_iota(jnp.int32, (c_len, c_len), 0)
    col = lax.broadcasted_iota(jnp.int32, (c_len, c_len), 1)
    lower = col <= row
    upper = col >= row
    tri = (lower.astype(F32).astype(BF16), upper.astype(F32).astype(BF16))
    causal = (lower, upper)
    total_row = (c_len - 1, 0)
    mid = c_len // 2

    def chain(q_ref, k_ref, g_ref, v_ref, o_ref, dirn, hh, row0):
        rows = pl.ds(row0, c_len)
        lanes = slice(hh * hd, (hh + 1) * hd)
        q = q_ref[rows, lanes].astype(F32)
        k = k_ref[rows, lanes].astype(F32)
        g = g_ref[rows, lanes]
        v = v_ref[rows, lanes]
        b = jnp.dot(tri[dirn], g, preferred_element_type=F32)
        tot = b[total_row[dirn]:total_row[dirn] + 1, :]
        ref = b[mid:mid + 1, :]
        qt = (q * jnp.exp(b - ref)).astype(BF16)
        kt = (k * jnp.exp(ref - b)).astype(BF16)
        sc = lax.dot_general(qt, kt, (((1,), (1,)), ((), ())), preferred_element_type=F32)
        p = jnp.where(causal[dirn], sc, 0.0).astype(BF16)
        o_intra = jnp.dot(p, v, preferred_element_type=F32)
        st = st_ref[dirn, hh]
        qh = (q * jnp.exp(b)).astype(BF16)
        o_inter = lax.dot_general(qh, st.astype(BF16), (((1,), (1,)), ((), ())),
                                  preferred_element_type=F32)
        o_ref[rows, lanes] = (o_intra + o_inter).astype(BF16)
        kh = (k * jnp.exp(tot - b)).astype(BF16)
        upd = lax.dot_general(v, kh, (((0,), (0,)), ((), ())), preferred_element_type=F32)
        st_ref[dirn, hh] = st * jnp.exp(tot) + upd

    def body(c, carry):
        row_f = pl.multiple_of(c * c_len, c_len)
        row_b = pl.multiple_of((n_chunks - 1 - c) * c_len, c_len)
        for hh in range(SCAN_HEADS):
            chain(qf_ref, kf_ref, gf_ref, vf_ref, of_ref, 0, hh, row_f)
            chain(qb_ref, kb_ref, gb_ref, vb_ref, ob_ref, 1, hh, row_b)
        return carry

    lax.fori_loop(0, n_chunks, body, 0)


def _scan(a, g, *, batch, n_lat, n_ctx, e):
    ts = n_ctx
    assert ts % SCAN_CHUNK == 0 and n_lat % ts == 0
    lat_steps = n_lat // ts
    steps = 1 + lat_steps
    hw = SCAN_HEADS * HEAD_DIM
    groups = e // hw
    rows = a.shape[0]

    def row_f(b, s):
        return jnp.where(s == 0, b, batch + lat_steps * b + s - 1)

    def row_b(b, s):
        return jnp.where(s == 0, b, batch + lat_steps * b + lat_steps - s)

    def spec(stream, rfn):
        return pl.BlockSpec((ts, hw), lambda b, hg, s: (rfn(b, s), stream * groups + hg))

    kern = functools.partial(_scan_kernel, n_chunks=ts // SCAN_CHUNK)
    return pl.pallas_call(
        kern,
        grid=(batch, groups, steps),
        in_specs=[spec(0, row_f), spec(1, row_f), spec(0, row_f), spec(3, row_f),
                  spec(0, row_b), spec(2, row_b), spec(1, row_b), spec(3, row_b)],
        out_specs=[pl.BlockSpec((ts, hw), lambda b, hg, s: (row_f(b, s), hg)),
                   pl.BlockSpec((ts, hw), lambda b, hg, s: (row_b(b, s), hg))],
        out_shape=[jax.ShapeDtypeStruct((rows, e), BF16), jax.ShapeDtypeStruct((rows, e), BF16)],
        scratch_shapes=[pltpu.VMEM((2, SCAN_HEADS, HEAD_DIM, HEAD_DIM), F32)],
        compiler_params=_cparams(("arbitrary", "arbitrary", "arbitrary")),
        name="hgrn_scan",
    )(a, a, g, a, a, a, g, a)


def _out0_kernel(of_ref, ob_ref, z_ref, ctx_ref, x_ref, mod_ref, hg_ref, w_ref,
                 xo_ref, co_ref, y_ref, *, d, e, tiles_per_batch):
    i = pl.program_id(0)
    heads = e // HEAD_DIM
    for h in range(heads):
        lanes = slice(h * HEAD_DIM, (h + 1) * HEAD_DIM)
        o = of_ref[:, lanes].astype(F32) + ob_ref[:, lanes].astype(F32)
        var = jnp.mean(o * o, axis=-1, keepdims=True)
        yn = (o * lax.rsqrt(var + EPS)) * hg_ref[...]
        y_ref[:, lanes] = (yn * z_ref[:, lanes].astype(F32)).astype(BF16)
    mix = jnp.dot(y_ref[...], w_ref[...], preferred_element_type=F32)

    @pl.when(i == 0)
    def _():
        gate = mod_ref[2:3, 2 * d:3 * d]
        co_ref[...] = ctx_ref[...] + gate * mix

    @pl.when(i > 0)
    def _():
        gate = mod_ref[pl.ds((i - 1) // tiles_per_batch, 1), 2 * d:3 * d]
        xo_ref[...] = x_ref[...] + gate * mix


def _out0(o_f, o_b, a, ctx2d, x2d, mods, hg_norm_g, w_bf, *, n_lat):
    rows_ctx, d = ctx2d.shape
    rows_lat = x2d.shape[0]
    e = o_f.shape[1]
    tm = ROW_TILE
    n_tiles = 1 + rows_lat // tm
    kern = functools.partial(_out0_kernel, d=d, e=e, tiles_per_batch=n_lat // tm)
    lat_idx = lambda i: (jnp.maximum(i - 1, 0), 0)
    return pl.pallas_call(
        kern,
        grid=(n_tiles,),
        in_specs=[
            pl.BlockSpec((tm, e), lambda i: (i, 0)),
            pl.BlockSpec((tm, e), lambda i: (i, 0)),
            pl.BlockSpec((tm, e), lambda i: (i, HG_STREAMS - 1)),
            pl.BlockSpec((tm, d), lambda i: (0, 0)),
            pl.BlockSpec((tm, d), lat_idx),
            pl.BlockSpec((None, MOD_ROWS, 3 * d), lambda i: (0, 0, 0)),
            pl.BlockSpec((1, HEAD_DIM), lambda i: (0, 0)),
            pl.BlockSpec((e, d), lambda i: (0, 0)),
        ],
        out_specs=[pl.BlockSpec((tm, d), lat_idx), pl.BlockSpec((tm, d), lambda i: (0, 0))],
        out_shape=[jax.ShapeDtypeStruct((rows_lat, d), F32), jax.ShapeDtypeStruct((rows_ctx, d), F32)],
        scratch_shapes=[pltpu.VMEM((tm, e), BF16)],
        compiler_params=_cparams(("arbitrary",)),
        name="readout0",
    )(o_f, o_b, a, ctx2d, x2d, mods, hg_norm_g, w_bf)


def _proj1_kernel(x_ref, mod_ref, ng_ref, w_ref, o_ref, h_ref, *, d, e, tiles_per_batch, sub):
    i = pl.program_id(0)
    j = pl.program_id(1)

    @pl.when(j == 0)
    def _():
        m = mod_ref[pl.ds(i // tiles_per_batch, 1), :]
        h_ref[...] = _modulated_norm(x_ref[...], ng_ref[...], m[:, 0:d], m[:, d:2 * d]).astype(BF16)

    def acc(c):
        return jnp.dot(h_ref[...], w_ref[:, c * sub:(c + 1) * sub], preferred_element_type=F32)

    nsub = e // sub

    @pl.when(j == 0)
    def _():
        for c in range(nsub):
            o_ref[:, c * sub:(c + 1) * sub] = acc(c).astype(BF16)

    @pl.when(j == 1)
    def _():
        for c in range(nsub):
            o_ref[:, c * sub:(c + 1) * sub] = _silu(acc(c)).astype(BF16)


def _proj1(x2d, mods, norm_g, w_bf, *, n_lat, layer):
    rows, d = x2d.shape
    e = w_bf.shape[1] // 2
    tm = ROW_TILE
    kern = functools.partial(_proj1_kernel, d=d, e=e, tiles_per_batch=n_lat // tm, sub=512)
    return pl.pallas_call(
        kern,
        grid=(rows // tm, 2),
        in_specs=[
            pl.BlockSpec((tm, d), lambda i, j: (i, 0)),
            pl.BlockSpec((None, MOD_ROWS, 3 * d), lambda i, j: (layer, 0, 0)),
            pl.BlockSpec((None, 1, d), lambda i, j: (layer, 0, 0)),
            pl.BlockSpec((d, e), lambda i, j: (0, j)),
        ],
        out_specs=pl.BlockSpec((tm, e), lambda i, j: (i, j)),
        out_shape=jax.ShapeDtypeStruct((rows, 2 * e), BF16),
        scratch_shapes=[pltpu.VMEM((tm, d), BF16)],
        compiler_params=_cparams(("arbitrary", "arbitrary")),
        name="proj1",
    )(x2d, mods, norm_g, w_bf)


def _dft_factors(n):
    n1 = 128 if n % 128 == 0 else n
    return n1, n // n1


def _dft_tables(n, group_dim):
    n1, n2 = _dft_factors(n)
    a2 = np.arange(n2, dtype=np.float64)[:, None, None]
    k1 = np.arange(n1, dtype=np.float64)[None, :, None]
    m1 = np.arange(n1, dtype=np.float64)[None, None, :]
    ang = -2.0 * np.pi * (a2 * k1 / n + m1 * k1 / n1)
    stage1 = np.concatenate([np.cos(ang), np.sin(ang)], axis=1) / np.sqrt(n1)
    kk = np.arange(n2, dtype=np.float64)
    ang2 = -2.0 * np.pi * np.outer(kk, kk) / n2
    fr, fi = np.cos(ang2) / np.sqrt(n2), np.sin(ang2) / np.sqrt(n2)
    stage2 = np.block([[fr, -fi], [fi, fr]])
    cc = np.arange(group_dim, dtype=np.float64)
    ang3 = 2.0 * np.pi * np.outer(cc, cc) / group_dim
    chan = np.concatenate([np.cos(ang3), np.sin(ang3)], axis=0) / np.sqrt(group_dim)
    to = lambda t: jnp.asarray(t, dtype=F32).astype(BF16)
    return to(stage1), to(stage2), to(chan)


def _dft1_kernel(m_ref, x_ref, o_ref):
    o_ref[...] = jnp.dot(m_ref[...], x_ref[...], preferred_element_type=F32).astype(BF16)


def _dft1(uz, stage1, *, batch, n, e):
    n2, two_n1, n1 = stage1.shape
    u3 = uz.reshape(batch, n1, n2 * 2 * e)
    return pl.pallas_call(
        _dft1_kernel,
        grid=(batch, n2),
        in_specs=[
            pl.BlockSpec((None, two_n1, n1), lambda b, a: (a, 0, 0)),
            pl.BlockSpec((None, n1, e), lambda b, a: (b, 0, 2 * a)),
        ],
        out_specs=pl.BlockSpec((None, None, two_n1, e), lambda b, a: (b, a, 0, 0)),
        out_shape=jax.ShapeDtypeStruct((batch, n2, two_n1, e), BF16),
        compiler_params=_cparams(("arbitrary", "arbitrary")),
        name="dft_stage1",
    )(stage1, u3)


def _dft2_kernel(f_ref, tr_ref, ti_ref, gr_ref, gi_ref, *, n2):
    g = (jnp.dot(f_ref[:, 0:n2], tr_ref[...], preferred_element_type=F32)
         + jnp.dot(f_ref[:, n2:2 * n2], ti_ref[...], preferred_element_type=F32))
    gr_ref[...] = g[0:n2].astype(BF16)
    gi_ref[...] = g[n2:2 * n2].astype(BF16)


def _dft2(t, stage2, *, batch, n, e):
    _, n2, two_n1, _ = t.shape
    n1 = two_n1 // 2
    t3 = t.reshape(batch, n2, two_n1 * e)
    out = jax.ShapeDtypeStruct((batch, n2, n1 * e), BF16)
    kern = functools.partial(_dft2_kernel, n2=n2)
    gr, gi = pl.pallas_call(
        kern,
        grid=(batch, n1),
        in_specs=[
            pl.BlockSpec((2 * n2, 2 * n2), lambda b, k: (0, 0)),
            pl.BlockSpec((None, n2, e), lambda b, k: (b, 0, k)),
            pl.BlockSpec((None, n2, e), lambda b, k: (b, 0, n1 + k)),
        ],
        out_specs=[pl.BlockSpec((None, n2, e), lambda b, k: (b, 0, k)),
                   pl.BlockSpec((None, n2, e), lambda b, k: (b, 0, k))],
        out_shape=[out, out],
        compiler_params=_cparams(("arbitrary", "arbitrary")),
        name="dft_stage2",
    )(stage2, t3, t3)
    return gr.reshape(batch * n, e), gi.reshape(batch * n, e)


def _out1_kernel(gr_ref, gi_ref, z_ref, x_ref, mod_ref, cs_ref, w_ref, fg_ref, o_ref, y_ref,
                 *, d, e, tiles_per_batch):
    i = pl.program_id(0)
    gd = e // FT_GROUPS
    for gq in range(FT_GROUPS):
        lanes = slice(gq * gd, (gq + 1) * gd)
        yr = (jnp.dot(gr_ref[:, lanes], cs_ref[0:gd, :], preferred_element_type=F32)
              + jnp.dot(gi_ref[:, lanes], cs_ref[gd:2 * gd, :], preferred_element_type=F32))
        y_ref[:, lanes] = (yr * z_ref[:, lanes].astype(F32)).astype(BF16)
    mix = jnp.dot(y_ref[...], w_ref[...], preferred_element_type=F32)
    gate = mod_ref[pl.ds(i // tiles_per_batch, 1), 2 * d:3 * d]
    x = x_ref[...] + gate * mix
    var = jnp.mean(x * x, axis=-1, keepdims=True)
    o_ref[...] = (x * lax.rsqrt(var + EPS)) * fg_ref[...]


def _out1(gr, gi, uz, x2d, mods, chan, w_bf, final_g, *, n_lat, layer):
    rows, d = x2d.shape
    e = gr.shape[1]
    tm = ROW_TILE
    gd = e // FT_GROUPS
    kern = functools.partial(_out1_kernel, d=d, e=e, tiles_per_batch=n_lat // tm)
    return pl.pallas_call(
        kern,
        grid=(rows // tm,),
        in_specs=[
            pl.BlockSpec((tm, e), lambda i: (i, 0)),
            pl.BlockSpec((tm, e), lambda i: (i, 0)),
            pl.BlockSpec((tm, e), lambda i: (i, 1)),
            pl.BlockSpec((tm, d), lambda i: (i, 0)),
            pl.BlockSpec((None, MOD_ROWS, 3 * d), lambda i: (layer, 0, 0)),
            pl.BlockSpec((2 * gd, gd), lambda i: (0, 0)),
            pl.BlockSpec((e, d), lambda i: (0, 0)),
            pl.BlockSpec((1, d), lambda i: (0, 0)),
        ],
        out_specs=pl.BlockSpec((tm, d), lambda i: (i, 0)),
        out_shape=jax.ShapeDtypeStruct((rows, d), F32),
        scratch_shapes=[pltpu.VMEM((tm, e), BF16)],
        compiler_params=_cparams(("arbitrary",)),
        name="readout1",
    )(gr, gi, uz, x2d, mods, chan, w_bf, final_g)


def kernel(x, c, ctx, c_ctx, ada_w, ada_b, norm_g, hg_w_in, hg_lb_logits, hg_norm_g, hg_w_out,
           ft_w_in, ft_w_out, final_g):
    batch, n_lat, d = x.shape
    n_ctx = ctx.shape[1]
    depth = ada_w.shape[0]
    e = hg_w_out.shape[1]
    assert depth == 2 and batch == 2 and batch + 1 <= MOD_ROWS

    cv = jnp.concatenate([c, c_ctx[None, :], jnp.zeros((MOD_ROWS - batch - 1, d), F32)], axis=0)
    mods = _ada_table(cv, ada_w, ada_b)
    ng = norm_g.reshape(depth, 1, d)
    x2d = x.reshape(batch * n_lat, d)
    ctx2d = ctx.reshape(batch * n_ctx, d)

    a, g = _proj0(ctx2d, x2d, mods, ng, hg_w_in[0].astype(BF16), hg_lb_logits,
                  n_lat=n_lat, lb_index=0)
    o_f, o_b = _scan(a, g, batch=batch, n_lat=n_lat, n_ctx=n_ctx, e=e)
    x1, _ctx1 = _out0(o_f, o_b, a, ctx2d, x2d, mods, hg_norm_g[0:1], hg_w_out[0].astype(BF16),
                      n_lat=n_lat)

    uz = _proj1(x1, mods, ng, ft_w_in[0].astype(BF16), n_lat=n_lat, layer=1)
    stage1, stage2, chan = _dft_tables(n_lat, e // FT_GROUPS)
    t = _dft1(uz, stage1, batch=batch, n=n_lat, e=e)
    gr, gi = _dft2(t, stage2, batch=batch, n=n_lat, e=e)
    out = _out1(gr, gi, uz, x1, mods, chan, ft_w_out[0].astype(BF16), final_g.reshape(1, d),
                n_lat=n_lat, layer=1)
    return out.reshape(batch, n_lat, d)
```

```python
import functools

import numpy as np
import jax
import jax.numpy as jnp
from jax import lax
from jax.experimental import pallas as pl
from jax.experimental.pallas import tpu as pltpu

F32 = jnp.float32
BF16 = jnp.bfloat16

EPS = 1e-6
HEAD_DIM = 128
HG_STREAMS = 5
FT_GROUPS = 8
SCAN_CHUNK = 64
SCAN_HEADS = 4
ROW_TILE = 512
MOD_ROWS = 8
VMEM_LIMIT = 56 * 1024 * 1024


def _cparams(sem):
    return pltpu.CompilerParams(dimension_semantics=sem, vmem_limit_bytes=VMEM_LIMIT)


def _sigmoid(x):
    return 1.0 / (1.0 + jnp.exp(-x))


def _silu(x):
    return x * _sigmoid(x)


def _modulated_norm(x, g, shift, scale):
    var = jnp.mean(x * x, axis=-1, keepdims=True)
    return (x * lax.rsqrt(var + EPS)) * g * (1.0 + scale) + shift


def _ada_kernel(cv_ref, w_ref, b_ref, o_ref):
    a = _silu(cv_ref[...])
    o_ref[...] = jnp.dot(a, w_ref[...], preferred_element_type=F32,
                         precision=lax.Precision.HIGHEST) + b_ref[...]


def _ada_table(cv, ada_w, ada_b):
    depth, d, d3 = ada_w.shape
    tn = 1024
    return pl.pallas_call(
        _ada_kernel,
        grid=(depth, d3 // tn),
        in_specs=[
            pl.BlockSpec((MOD_ROWS, d), lambda l, j: (0, 0)),
            pl.BlockSpec((None, d, tn), lambda l, j: (l, 0, j)),
            pl.BlockSpec((None, 1, tn), lambda l, j: (l, 0, j)),
        ],
        out_specs=pl.BlockSpec((None, MOD_ROWS, tn), lambda l, j: (l, 0, j)),
        out_shape=jax.ShapeDtypeStruct((depth, MOD_ROWS, d3), F32),
        compiler_params=_cparams(("arbitrary", "arbitrary")),
        name="ada_table",
    )(cv, ada_w, ada_b.reshape(depth, 1, d3))


def _proj0_kernel(ctx_ref, x_ref, mod_ref, ng_ref, w_ref, lbl_ref, a_ref, g_ref, h_ref,
                  *, d, e, tiles_per_batch, lb_index, sub):
    i = pl.program_id(0)
    j = pl.program_id(1)
    tm = h_ref.shape[0]

    def fill(src_ref, row):
        m = mod_ref[pl.ds(row, 1), :]
        h_ref[...] = _modulated_norm(src_ref[...], ng_ref[...], m[:, 0:d], m[:, d:2 * d]).astype(BF16)

    @pl.when(j == 0)
    def _():
        @pl.when(i == 0)
        def _():
            fill(ctx_ref, 2)

        @pl.when(i > 0)
        def _():
            fill(x_ref, (i - 1) // tiles_per_batch)

    def acc(c):
        return jnp.dot(h_ref[...], w_ref[:, c * sub:(c + 1) * sub], preferred_element_type=F32)

    nsub = e // sub

    @pl.when(j == 0)
    def _():
        for c in range(nsub):
            a_ref[:, c * sub:(c + 1) * sub] = _silu(acc(c)).astype(BF16)

    @pl.when((j == 1) | (j == 2))
    def _():
        dirn = j - 1
        logits = lbl_ref[:, pl.ds(dirn, 1), :]
        mx = jnp.max(logits, axis=0, keepdims=True)
        ex = jnp.exp(logits - mx)
        p = ex / jnp.sum(ex, axis=0, keepdims=True)
        lb_full = jnp.sum(p[0:lb_index + 1], axis=0)
        for c in range(nsub):
            lb = lb_full[:, c * sub:(c + 1) * sub]
            sig = _sigmoid(acc(c))
            f = lb + (1.0 - lb) * sig
            a_ref[:, c * sub:(c + 1) * sub] = ((1.0 - lb) * (1.0 - sig)).astype(BF16)
            g_ref[:, c * sub:(c + 1) * sub] = jnp.log(f).astype(BF16)

    @pl.when(j == 3)
    def _():
        for c in range(nsub):
            a_ref[:, c * sub:(c + 1) * sub] = acc(c).astype(BF16)

    @pl.when(j == 4)
    def _():
        for c in range(nsub):
            a_ref[:, c * sub:(c + 1) * sub] = _silu(acc(c)).astype(BF16)


def _proj0(ctx2d, x2d, mods, norm_g, w_bf, lb_logits, *, n_lat, lb_index):
    rows_ctx, d = ctx2d.shape
    rows_lat = x2d.shape[0]
    e = w_bf.shape[1] // HG_STREAMS
    tm = ROW_TILE
    assert rows_ctx == tm and n_lat % tm == 0
    n_tiles = 1 + rows_lat // tm
    rows = rows_ctx + rows_lat
    nl = lb_logits.shape[0]
    kern = functools.partial(_proj0_kernel, d=d, e=e, tiles_per_batch=n_lat // tm,
                             lb_index=lb_index, sub=512)
    return pl.pallas_call(
        kern,
        grid=(n_tiles, HG_STREAMS),
        in_specs=[
            pl.BlockSpec((tm, d), lambda i, j: (0, 0)),
            pl.BlockSpec((tm, d), lambda i, j: (jnp.maximum(i - 1, 0), 0)),
            pl.BlockSpec((None, MOD_ROWS, 3 * d), lambda i, j: (0, 0, 0)),
            pl.BlockSpec((None, 1, d), lambda i, j: (0, 0, 0)),
            pl.BlockSpec((d, e), lambda i, j: (0, j)),
            pl.BlockSpec((nl, 2, e), lambda i, j: (0, 0, 0)),
        ],
        out_specs=[
            pl.BlockSpec((tm, e), lambda i, j: (i, j)),
            pl.BlockSpec((tm, e), lambda i, j: (i, jnp.clip(j - 1, 0, 1))),
        ],
        out_shape=[
            jax.ShapeDtypeStruct((rows, HG_STREAMS * e), BF16),
            jax.ShapeDtypeStruct((rows, 2 * e), BF16),
        ],
        scratch_shapes=[pltpu.VMEM((tm, d), BF16)],
        compiler_params=_cparams(("arbitrary", "arbitrary")),
        name="proj0",
    )(ctx2d, x2d, mods, norm_g, w_bf, lb_logits)


def _scan_kernel(qf_ref, kf_ref, gf_ref, vf_ref, qb_ref, kb_ref, gb_ref, vb_ref,
                 of_ref, ob_ref, st_ref, *, n_chunks):
    s = pl.program_id(2)
    c_len = SCAN_CHUNK
    hd = HEAD_DIM

    @pl.when(s == 0)
    def _():
        st_ref[...] = jnp.zeros_like(st_ref)

    row = lax.broadcasted_iota(jnp.int32, (c_len, c_len), 0)
    col = lax.broadcasted_iota(jnp.int32, (c_len, c_len), 1)
    causal = (col <= row, col >= row)
    tri = tuple(m.astype(F32).astype(BF16) for m in causal)
    end_row = (c_len - 1, 0)
    mid = c_len // 2
    refs = ((qf_ref, kf_ref, gf_ref, vf_ref, of_ref), (qb_ref, kb_ref, gb_ref, vb_ref, ob_ref))
    nt = (((1,), (1,)), ((), ()))
    tn = (((0,), (0,)), ((), ()))

    pre = {}
    for dirn in range(2):
        q_ref, k_ref, g_ref, v_ref, _ = refs[dirn]
        for c in range(n_chunks):
            rows = slice(c * c_len, (c + 1) * c_len)
            b = jnp.dot(tri[dirn], g_ref[rows, :], preferred_element_type=F32)
            tot = b[end_row[dirn]:end_row[dirn] + 1, :]
            ref = b[mid:mid + 1, :]
            q = q_ref[rows, :].astype(F32)
            k = k_ref[rows, :].astype(F32)
            pre[dirn, c] = dict(
                qt=(q * jnp.exp(b - ref)).astype(BF16),
                kt=(k * jnp.exp(ref - b)).astype(BF16),
                qh=(q * jnp.exp(b)).astype(BF16),
                kh=(k * jnp.exp(tot - b)).astype(BF16),
                dec=jnp.exp(tot),
                v=v_ref[rows, :],
            )

    prob, upd = {}, {}
    for dirn in range(2):
        for c in range(n_chunks):
            w = pre[dirn, c]
            for hh in range(SCAN_HEADS):
                lanes = slice(hh * hd, (hh + 1) * hd)
                sc = lax.dot_general(w["qt"][:, lanes], w["kt"][:, lanes], nt,
                                     preferred_element_type=F32)
                prob[dirn, c, hh] = jnp.where(causal[dirn], sc, 0.0).astype(BF16)
                upd[dirn, c, hh] = lax.dot_general(w["v"][:, lanes], w["kh"][:, lanes], tn,
                                                   preferred_element_type=F32)

    for dirn in range(2):
        o_ref = refs[dirn][4]
        order = range(n_chunks) if dirn == 0 else range(n_chunks - 1, -1, -1)
        for hh in range(SCAN_HEADS):
            lanes = slice(hh * hd, (hh + 1) * hd)
            st = st_ref[dirn, hh]
            for c in order:
                w = pre[dirn, c]
                rows = slice(c * c_len, (c + 1) * c_len)
                o = (jnp.dot(prob[dirn, c, hh], w["v"][:, lanes], preferred_element_type=F32)
                     + lax.dot_general(w["qh"][:, lanes], st.astype(BF16), nt,
                                       preferred_element_type=F32))
                o_ref[rows, lanes] = o.astype(BF16)
                st = st * w["dec"][:, lanes] + upd[dirn, c, hh]
            st_ref[dirn, hh] = st


def _scan(a, g, *, batch, n_lat, n_ctx, e):
    ts = n_ctx
    assert ts % SCAN_CHUNK == 0 and n_lat % ts == 0
    lat_steps = n_lat // ts
    steps = 1 + lat_steps
    hw = SCAN_HEADS * HEAD_DIM
    groups = e // hw
    rows = a.shape[0]

    def row_f(b, s):
        return jnp.where(s == 0, b, batch + lat_steps * b + s - 1)

    def row_b(b, s):
        return jnp.where(s == 0, b, batch + lat_steps * b + lat_steps - s)

    def spec(stream, rfn):
        return pl.BlockSpec((ts, hw), lambda b, hg, s: (rfn(b, s), stream * groups + hg))

    kern = functools.partial(_scan_kernel, n_chunks=ts // SCAN_CHUNK)
    return pl.pallas_call(
        kern,
        grid=(batch, groups, steps),
        in_specs=[spec(0, row_f), spec(1, row_f), spec(0, row_f), spec(3, row_f),
                  spec(0, row_b), spec(2, row_b), spec(1, row_b), spec(3, row_b)],
        out_specs=[pl.BlockSpec((ts, hw), lambda b, hg, s: (row_f(b, s), hg)),
                   pl.BlockSpec((ts, hw), lambda b, hg, s: (row_b(b, s), hg))],
        out_shape=[jax.ShapeDtypeStruct((rows, e), BF16), jax.ShapeDtypeStruct((rows, e), BF16)],
        scratch_shapes=[pltpu.VMEM((2, SCAN_HEADS, HEAD_DIM, HEAD_DIM), F32)],
        compiler_params=_cparams(("arbitrary", "arbitrary", "arbitrary")),
        name="hgrn_scan",
    )(a, a, g, a, a, a, g, a)


def _out0_kernel(of_ref, ob_ref, z_ref, ctx_ref, x_ref, mod_ref, hg_ref, w_ref,
                 xo_ref, co_ref, y_ref, *, d, e, tiles_per_batch):
    i = pl.program_id(0)
    heads = e // HEAD_DIM
    for h in range(heads):
        lanes = slice(h * HEAD_DIM, (h + 1) * HEAD_DIM)
        o = of_ref[:, lanes].astype(F32) + ob_ref[:, lanes].astype(F32)
        var = jnp.mean(o * o, axis=-1, keepdims=True)
        yn = (o * lax.rsqrt(var + EPS)) * hg_ref[...]
        y_ref[:, lanes] = (yn * z_ref[:, lanes].astype(F32)).astype(BF16)
    mix = jnp.dot(y_ref[...], w_ref[...], preferred_element_type=F32)

    @pl.when(i == 0)
    def _():
        gate = mod_ref[2:3, 2 * d:3 * d]
        co_ref[...] = ctx_ref[...] + gate * mix

    @pl.when(i > 0)
    def _():
        gate = mod_ref[pl.ds((i - 1) // tiles_per_batch, 1), 2 * d:3 * d]
        xo_ref[...] = x_ref[...] + gate * mix


def _out0(o_f, o_b, a, ctx2d, x2d, mods, hg_norm_g, w_bf, *, n_lat):
    rows_ctx, d = ctx2d.shape
    rows_lat = x2d.shape[0]
    e = o_f.shape[1]
    tm = ROW_TILE
    n_tiles = 1 + rows_lat // tm
    kern = functools.partial(_out0_kernel, d=d, e=e, tiles_per_batch=n_lat // tm)
    lat_idx = lambda i: (jnp.maximum(i - 1, 0), 0)
    return pl.pallas_call(
        kern,
        grid=(n_tiles,),
        in_specs=[
            pl.BlockSpec((tm, e), lambda i: (i, 0)),
            pl.BlockSpec((tm, e), lambda i: (i, 0)),
            pl.BlockSpec((tm, e), lambda i: (i, HG_STREAMS - 1)),
            pl.BlockSpec((tm, d), lambda i: (0, 0)),
            pl.BlockSpec((tm, d), lat_idx),
            pl.BlockSpec((None, MOD_ROWS, 3 * d), lambda i: (0, 0, 0)),
            pl.BlockSpec((1, HEAD_DIM), lambda i: (0, 0)),
            pl.BlockSpec((e, d), lambda i: (0, 0)),
        ],
        out_specs=[pl.BlockSpec((tm, d), lat_idx), pl.BlockSpec((tm, d), lambda i: (0, 0))],
        out_shape=[jax.ShapeDtypeStruct((rows_lat, d), F32), jax.ShapeDtypeStruct((rows_ctx, d), F32)],
        scratch_shapes=[pltpu.VMEM((tm, e), BF16)],
        compiler_params=_cparams(("arbitrary",)),
        name="readout0",
    )(o_f, o_b, a, ctx2d, x2d, mods, hg_norm_g, w_bf)


def _proj1_kernel(x_ref, mod_ref, ng_ref, w_ref, o_ref, h_ref, *, d, e, tiles_per_batch, sub):
    i = pl.program_id(0)
    j = pl.program_id(1)

    @pl.when(j == 0)
    def _():
        m = mod_ref[pl.ds(i // tiles_per_batch, 1), :]
        h_ref[...] = _modulated_norm(x_ref[...], ng_ref[...], m[:, 0:d], m[:, d:2 * d]).astype(BF16)

    def acc(c):
        return jnp.dot(h_ref[...], w_ref[:, c * sub:(c + 1) * sub], preferred_element_type=F32)

    nsub = e // sub

    @pl.when(j == 0)
    def _():
        for c in range(nsub):
            o_ref[:, c * sub:(c + 1) * sub] = acc(c).astype(BF16)

    @pl.when(j == 1)
    def _():
        for c in range(nsub):
            o_ref[:, c * sub:(c + 1) * sub] = _silu(acc(c)).astype(BF16)


def _proj1(x2d, mods, norm_g, w_bf, *, n_lat, layer):
    rows, d = x2d.shape
    e = w_bf.shape[1] // 2
    tm = ROW_TILE
    kern = functools.partial(_proj1_kernel, d=d, e=e, tiles_per_batch=n_lat // tm, sub=512)
    return pl.pallas_call(
        kern,
        grid=(rows // tm, 2),
        in_specs=[
            pl.BlockSpec((tm, d), lambda i, j: (i, 0)),
            pl.BlockSpec((None, MOD_ROWS, 3 * d), lambda i, j: (layer, 0, 0)),
            pl.BlockSpec((None, 1, d), lambda i, j: (layer, 0, 0)),
            pl.BlockSpec((d, e), lambda i, j: (0, j)),
        ],
        out_specs=pl.BlockSpec((tm, e), lambda i, j: (i, j)),
        out_shape=jax.ShapeDtypeStruct((rows, 2 * e), BF16),
        scratch_shapes=[pltpu.VMEM((tm, d), BF16)],
        compiler_params=_cparams(("arbitrary", "arbitrary")),
        name="proj1",
    )(x2d, mods, norm_g, w_bf)


def _dft_factors(n):
    n1 = 128 if n % 128 == 0 else n
    return n1, n // n1


def _dft_tables(n, group_dim):
    n1, n2 = _dft_factors(n)
    a2 = np.arange(n2, dtype=np.float64)[:, None, None]
    k1 = np.arange(n1, dtype=np.float64)[None, :, None]
    m1 = np.arange(n1, dtype=np.float64)[None, None, :]
    ang = -2.0 * np.pi * (a2 * k1 / n + m1 * k1 / n1)
    stage1 = np.concatenate([np.cos(ang), np.sin(ang)], axis=1) / np.sqrt(n1)
    kk = np.arange(n2, dtype=np.float64)
    ang2 = -2.0 * np.pi * np.outer(kk, kk) / n2
    fr, fi = np.cos(ang2) / np.sqrt(n2), np.sin(ang2) / np.sqrt(n2)
    stage2 = np.block([[fr, -fi], [fi, fr]])
    cc = np.arange(group_dim, dtype=np.float64)
    ang3 = 2.0 * np.pi * np.outer(cc, cc) / group_dim
    chan = np.concatenate([np.cos(ang3), np.sin(ang3)], axis=0) / np.sqrt(group_dim)
    to = lambda t: jnp.asarray(t, dtype=F32).astype(BF16)
    return to(stage1), to(stage2), to(chan)


def _dft1_kernel(m_ref, x_ref, o_ref):
    o_ref[...] = jnp.dot(m_ref[...], x_ref[...], preferred_element_type=F32).astype(BF16)


def _dft1(uz, stage1, *, batch, n, e):
    n2, two_n1, n1 = stage1.shape
    u3 = uz.reshape(batch, n1, n2 * 2 * e)
    return pl.pallas_call(
        _dft1_kernel,
        grid=(batch, n2),
        in_specs=[
            pl.BlockSpec((None, two_n1, n1), lambda b, a: (a, 0, 0)),
            pl.BlockSpec((None, n1, e), lambda b, a: (b, 0, 2 * a)),
        ],
        out_specs=pl.BlockSpec((None, None, two_n1, e), lambda b, a: (b, a, 0, 0)),
        out_shape=jax.ShapeDtypeStruct((batch, n2, two_n1, e), BF16),
        compiler_params=_cparams(("arbitrary", "arbitrary")),
        name="dft_stage1",
    )(stage1, u3)


def _dft2_kernel(f_ref, tr_ref, ti_ref, gr_ref, gi_ref, *, n2):
    g = (jnp.dot(f_ref[:, 0:n2], tr_ref[...], preferred_element_type=F32)
         + jnp.dot(f_ref[:, n2:2 * n2], ti_ref[...], preferred_element_type=F32))
    gr_ref[...] = g[0:n2].astype(BF16)
    gi_ref[...] = g[n2:2 * n2].astype(BF16)


def _dft2(t, stage2, *, batch, n, e):
    _, n2, two_n1, _ = t.shape
    n1 = two_n1 // 2
    t3 = t.reshape(batch, n2, two_n1 * e)
    out = jax.ShapeDtypeStruct((batch, n2, n1 * e), BF16)
    kern = functools.partial(_dft2_kernel, n2=n2)
    gr, gi = pl.pallas_call(
        kern,
        grid=(batch, n1),
        in_specs=[
            pl.BlockSpec((2 * n2, 2 * n2), lambda b, k: (0, 0)),
            pl.BlockSpec((None, n2, e), lambda b, k: (b, 0, k)),
            pl.BlockSpec((None, n2, e), lambda b, k: (b, 0, n1 + k)),
        ],
        out_specs=[pl.BlockSpec((None, n2, e), lambda b, k: (b, 0, k)),
                   pl.BlockSpec((None, n2, e), lambda b, k: (b, 0, k))],
        out_shape=[out, out],
        compiler_params=_cparams(("arbitrary", "arbitrary")),
        name="dft_stage2",
    )(stage2, t3, t3)
    return gr.reshape(batch * n, e), gi.reshape(batch * n, e)


def _out1_kernel(gr_ref, gi_ref, z_ref, x_ref, mod_ref, cs_ref, w_ref, fg_ref, o_ref, y_ref,
                 *, d, e, tiles_per_batch):
    i = pl.program_id(0)
    gd = e // FT_GROUPS
    for gq in range(FT_GROUPS):
        lanes = slice(gq * gd, (gq + 1) * gd)
        yr = (jnp.dot(gr_ref[:, lanes], cs_ref[0:gd, :], preferred_element_type=F32)
              + jnp.dot(gi_ref[:, lanes], cs_ref[gd:2 * gd, :], preferred_element_type=F32))
        y_ref[:, lanes] = (yr * z_ref[:, lanes].astype(F32)).astype(BF16)
    mix = jnp.dot(y_ref[...], w_ref[...], preferred_element_type=F32)
    gate = mod_ref[pl.ds(i // tiles_per_batch, 1), 2 * d:3 * d]
    x = x_ref[...] + gate * mix
    var = jnp.mean(x * x, axis=-1, keepdims=True)
    o_ref[...] = (x * lax.rsqrt(var + EPS)) * fg_ref[...]


def _out1(gr, gi, uz, x2d, mods, chan, w_bf, final_g, *, n_lat, layer):
    rows, d = x2d.shape
    e = gr.shape[1]
    tm = ROW_TILE
    gd = e // FT_GROUPS
    kern = functools.partial(_out1_kernel, d=d, e=e, tiles_per_batch=n_lat // tm)
    return pl.pallas_call(
        kern,
        grid=(rows // tm,),
        in_specs=[
            pl.BlockSpec((tm, e), lambda i: (i, 0)),
            pl.BlockSpec((tm, e), lambda i: (i, 0)),
            pl.BlockSpec((tm, e), lambda i: (i, 1)),
            pl.BlockSpec((tm, d), lambda i: (i, 0)),
            pl.BlockSpec((None, MOD_ROWS, 3 * d), lambda i: (layer, 0, 0)),
            pl.BlockSpec((2 * gd, gd), lambda i: (0, 0)),
            pl.BlockSpec((e, d), lambda i: (0, 0)),
            pl.BlockSpec((1, d), lambda i: (0, 0)),
        ],
        out_specs=pl.BlockSpec((tm, d), lambda i: (i, 0)),
        out_shape=jax.ShapeDtypeStruct((rows, d), F32),
        scratch_shapes=[pltpu.VMEM((tm, e), BF16)],
        compiler_params=_cparams(("arbitrary",)),
        name="readout1",
    )(gr, gi, uz, x2d, mods, chan, w_bf, final_g)


def kernel(x, c, ctx, c_ctx, ada_w, ada_b, norm_g, hg_w_in, hg_lb_logits, hg_norm_g, hg_w_out,
           ft_w_in, ft_w_out, final_g):
    batch, n_lat, d = x.shape
    n_ctx = ctx.shape[1]
    depth = ada_w.shape[0]
    e = hg_w_out.shape[1]
    assert depth == 2 and batch == 2 and batch + 1 <= MOD_ROWS

    cv = jnp.concatenate([c, c_ctx[None, :], jnp.zeros((MOD_ROWS - batch - 1, d), F32)], axis=0)
    mods = _ada_table(cv, ada_w, ada_b)
    ng = norm_g.reshape(depth, 1, d)
    x2d = x.reshape(batch * n_lat, d)
    ctx2d = ctx.reshape(batch * n_ctx, d)

    a, g = _proj0(ctx2d, x2d, mods, ng, hg_w_in[0].astype(BF16), hg_lb_logits,
                  n_lat=n_lat, lb_index=0)
    o_f, o_b = _scan(a, g, batch=batch, n_lat=n_lat, n_ctx=n_ctx, e=e)
    x1, _ctx1 = _out0(o_f, o_b, a, ctx2d, x2d, mods, hg_norm_g[0:1], hg_w_out[0].astype(BF16),
                      n_lat=n_lat)

    uz = _proj1(x1, mods, ng, ft_w_in[0].astype(BF16), n_lat=n_lat, layer=1)
    stage1, stage2, chan = _dft_tables(n_lat, e // FT_GROUPS)
    t = _dft1(uz, stage1, batch=batch, n=n_lat, e=e)
    gr, gi = _dft2(t, stage2, batch=batch, n=n_lat, e=e)
    out = _out1(gr, gi, uz, x1, mods, chan, ft_w_out[0].astype(BF16), final_g.reshape(1, d),
                n_lat=n_lat, layer=1)
    return out.reshape(batch, n_lat, d)
```

```python
import functools

import numpy as np
import jax
import jax.numpy as jnp
from jax import lax
from jax.experimental import pallas as pl
from jax.experimental.pallas import tpu as pltpu

F32 = jnp.float32
BF16 = jnp.bfloat16

EPS = 1e-6
HEAD_DIM = 128
HG_STREAMS = 5
FT_GROUPS = 8
SCAN_CHUNK = 64
SCAN_HEADS = 4
ROW_TILE = 512
MOD_ROWS = 8
SUBLANES = 8
LANES = 128
VMEM_LIMIT = 56 * 1024 * 1024


def _cparams(sem):
    return pltpu.CompilerParams(dimension_semantics=sem, vmem_limit_bytes=VMEM_LIMIT)


def _sigmoid(x):
    return 1.0 / (1.0 + jnp.exp(-x))


def _silu(x):
    return x * _sigmoid(x)


def _modulated_norm(x, g, shift, scale):
    var = jnp.mean(x * x, axis=-1, keepdims=True)
    return (x * lax.rsqrt(var + EPS)) * g * (1.0 + scale) + shift


def _ada_kernel(cv_ref, w_ref, b_ref, o_ref):
    a = _silu(cv_ref[...])
    o_ref[...] = jnp.dot(a, w_ref[...], preferred_element_type=F32,
                         precision=lax.Precision.HIGHEST) + b_ref[...]


def _ada_table(cv, ada_w, ada_b):
    depth, d, d3 = ada_w.shape
    tn = 1024
    return pl.pallas_call(
        _ada_kernel,
        grid=(depth, d3 // tn),
        in_specs=[
            pl.BlockSpec((MOD_ROWS, d), lambda l, j: (0, 0)),
            pl.BlockSpec((None, d, tn), lambda l, j: (l, 0, j)),
            pl.BlockSpec((None, 1, tn), lambda l, j: (l, 0, j)),
        ],
        out_specs=pl.BlockSpec((None, MOD_ROWS, tn), lambda l, j: (l, 0, j)),
        out_shape=jax.ShapeDtypeStruct((depth, MOD_ROWS, d3), F32),
        compiler_params=_cparams(("arbitrary", "arbitrary")),
        name="ada_table",
    )(cv, ada_w, ada_b.reshape(depth, 1, d3))


def _proj0_kernel(ctx_ref, x_ref, mod_ref, ng_ref, w_ref, lbl_ref, a_ref, g_ref, h_ref,
                  *, d, e, tiles_per_batch, lb_index, sub):
    i = pl.program_id(0)
    j = pl.program_id(1)
    tm = h_ref.shape[0]

    def fill(src_ref, row):
        m = mod_ref[pl.ds(row, 1), :]
        h_ref[...] = _modulated_norm(src_ref[...], ng_ref[...], m[:, 0:d], m[:, d:2 * d]).astype(BF16)

    @pl.when(j == 0)
    def _():
        @pl.when(i == 0)
        def _():
            fill(ctx_ref, 2)

        @pl.when(i > 0)
        def _():
            fill(x_ref, (i - 1) // tiles_per_batch)

    def acc(c):
        return jnp.dot(h_ref[...], w_ref[:, c * sub:(c + 1) * sub], preferred_element_type=F32)

    nsub = e // sub

    @pl.when(j == 0)
    def _():
        for c in range(nsub):
            a_ref[:, c * sub:(c + 1) * sub] = _silu(acc(c)).astype(BF16)

    @pl.when((j == 1) | (j == 2))
    def _():
        dirn = j - 1
        logits = lbl_ref[:, pl.ds(dirn, 1), :]
        mx = jnp.max(logits, axis=0, keepdims=True)
        ex = jnp.exp(logits - mx)
        p = ex / jnp.sum(ex, axis=0, keepdims=True)
        lb_full = jnp.sum(p[0:lb_index + 1], axis=0)
        for c in range(nsub):
            lb = lb_full[:, c * sub:(c + 1) * sub]
            sig = _sigmoid(acc(c))
            f = lb + (1.0 - lb) * sig
            a_ref[:, c * sub:(c + 1) * sub] = ((1.0 - lb) * (1.0 - sig)).astype(BF16)
            g_ref[:, c * sub:(c + 1) * sub] = jnp.log(f).astype(BF16)

    @pl.when(j == 3)
    def _():
        for c in range(nsub):
            a_ref[:, c * sub:(c + 1) * sub] = acc(c).astype(BF16)

    @pl.when(j == 4)
    def _():
        for c in range(nsub):
            a_ref[:, c * sub:(c + 1) * sub] = _silu(acc(c)).astype(BF16)


def _proj0(ctx2d, x2d, mods, norm_g, w_bf, lb_logits, *, n_lat, lb_index):
    rows_ctx, d = ctx2d.shape
    rows_lat = x2d.shape[0]
    e = w_bf.shape[1] // HG_STREAMS
    tm = ROW_TILE
    assert rows_ctx == tm and n_lat % tm == 0
    n_tiles = 1 + rows_lat // tm
    rows = rows_ctx + rows_lat
    nl = lb_logits.shape[0]
    kern = functools.partial(_proj0_kernel, d=d, e=e, tiles_per_batch=n_lat // tm,
                             lb_index=lb_index, sub=512)
    return pl.pallas_call(
        kern,
        grid=(n_tiles, HG_STREAMS),
        in_specs=[
            pl.BlockSpec((tm, d), lambda i, j: (0, 0)),
            pl.BlockSpec((tm, d), lambda i, j: (jnp.maximum(i - 1, 0), 0)),
            pl.BlockSpec((None, MOD_ROWS, 3 * d), lambda i, j: (0, 0, 0)),
            pl.BlockSpec((None, 1, d), lambda i, j: (0, 0, 0)),
            pl.BlockSpec((d, e), lambda i, j: (0, j)),
            pl.BlockSpec((nl, 2, e), lambda i, j: (0, 0, 0)),
        ],
        out_specs=[
            pl.BlockSpec((tm, e), lambda i, j: (i, j)),
            pl.BlockSpec((tm, e), lambda i, j: (i, jnp.clip(j - 1, 0, 1))),
        ],
        out_shape=[
            jax.ShapeDtypeStruct((rows, HG_STREAMS * e), BF16),
            jax.ShapeDtypeStruct((rows, 2 * e), BF16),
        ],
        scratch_shapes=[pltpu.VMEM((tm, d), BF16)],
        compiler_params=_cparams(("arbitrary", "arbitrary")),
        name="proj0",
    )(ctx2d, x2d, mods, norm_g, w_bf, lb_logits)


def _scan_kernel(qf_ref, kf_ref, gf_ref, vf_ref, qb_ref, kb_ref, gb_ref, vb_ref,
                 of_ref, ob_ref, st_ref, *, n_chunks):
    s = pl.program_id(2)
    c_len = SCAN_CHUNK
    hd = HEAD_DIM

    @pl.when(s == 0)
    def _():
        st_ref[...] = jnp.zeros_like(st_ref)

    row = lax.broadcasted_iota(jnp.int32, (c_len, c_len), 0)
    col = lax.broadcasted_iota(jnp.int32, (c_len, c_len), 1)
    causal = (col <= row, col >= row)
    tri = tuple(m.astype(F32).astype(BF16) for m in causal)
    end_row = (c_len - 1, 0)
    mid = c_len // 2
    refs = ((qf_ref, kf_ref, gf_ref, vf_ref, of_ref), (qb_ref, kb_ref, gb_ref, vb_ref, ob_ref))
    nt = (((1,), (1,)), ((), ()))
    tn = (((0,), (0,)), ((), ()))

    pre = {}
    for dirn in range(2):
        q_ref, k_ref, g_ref, v_ref, _ = refs[dirn]
        for c in range(n_chunks):
            rows = slice(c * c_len, (c + 1) * c_len)
            b = jnp.dot(tri[dirn], g_ref[rows, :], preferred_element_type=F32)
            tot = b[end_row[dirn]:end_row[dirn] + 1, :]
            ref = b[mid:mid + 1, :]
            q = q_ref[rows, :].astype(F32)
            k = k_ref[rows, :].astype(F32)
            pre[dirn, c] = dict(
                qt=(q * jnp.exp(b - ref)).astype(BF16),
                kt=(k * jnp.exp(ref - b)).astype(BF16),
                qh=(q * jnp.exp(b)).astype(BF16),
                kh=(k * jnp.exp(tot - b)).astype(BF16),
                dec=jnp.exp(tot),
                v=v_ref[rows, :],
            )

    prob, upd = {}, {}
    for dirn in range(2):
        for c in range(n_chunks):
            w = pre[dirn, c]
            for hh in range(SCAN_HEADS):
                lanes = slice(hh * hd, (hh + 1) * hd)
                sc = lax.dot_general(w["qt"][:, lanes], w["kt"][:, lanes], nt,
                                     preferred_element_type=F32)
                prob[dirn, c, hh] = jnp.where(causal[dirn], sc, 0.0).astype(BF16)
                upd[dirn, c, hh] = lax.dot_general(w["v"][:, lanes], w["kh"][:, lanes], tn,
                                                   preferred_element_type=F32)

    for dirn in range(2):
        o_ref = refs[dirn][4]
        order = range(n_chunks) if dirn == 0 else range(n_chunks - 1, -1, -1)
        for hh in range(SCAN_HEADS):
            lanes = slice(hh * hd, (hh + 1) * hd)
            st = st_ref[dirn, hh]
            for c in order:
                w = pre[dirn, c]
                rows = slice(c * c_len, (c + 1) * c_len)
                o = (jnp.dot(prob[dirn, c, hh], w["v"][:, lanes], preferred_element_type=F32)
                     + lax.dot_general(w["qh"][:, lanes], st.astype(BF16), nt,
                                       preferred_element_type=F32))
                o_ref[rows, lanes] = o.astype(BF16)
                st = st * w["dec"][:, lanes] + upd[dirn, c, hh]
            st_ref[dirn, hh] = st


def _scan(a, g, *, batch, n_lat, n_ctx, e):
    ts = n_ctx
    assert ts % SCAN_CHUNK == 0 and n_lat % ts == 0
    lat_steps = n_lat // ts
    steps = 1 + lat_steps
    hw = SCAN_HEADS * HEAD_DIM
    groups = e // hw
    rows = a.shape[0]

    def row_f(b, s):
        return jnp.where(s == 0, b, batch + lat_steps * b + s - 1)

    def row_b(b, s):
        return jnp.where(s == 0, b, batch + lat_steps * b + lat_steps - s)

    def spec(stream, rfn):
        return pl.BlockSpec((ts, hw), lambda b, hg, s: (rfn(b, s), stream * groups + hg))

    kern = functools.partial(_scan_kernel, n_chunks=ts // SCAN_CHUNK)
    return pl.pallas_call(
        kern,
        grid=(batch, groups, steps),
        in_specs=[spec(0, row_f), spec(1, row_f), spec(0, row_f), spec(3, row_f),
                  spec(0, row_b), spec(2, row_b), spec(1, row_b), spec(3, row_b)],
        out_specs=[pl.BlockSpec((ts, hw), lambda b, hg, s: (row_f(b, s), hg)),
                   pl.BlockSpec((ts, hw), lambda b, hg, s: (row_b(b, s), hg))],
        out_shape=[jax.ShapeDtypeStruct((rows, e), BF16), jax.ShapeDtypeStruct((rows, e), BF16)],
        scratch_shapes=[pltpu.VMEM((2, SCAN_HEADS, HEAD_DIM, HEAD_DIM), F32)],
        compiler_params=_cparams(("arbitrary", "arbitrary", "arbitrary")),
        name="hgrn_scan",
    )(a, a, g, a, a, a, g, a)


def _out0_kernel(of_ref, ob_ref, z_ref, ctx_ref, x_ref, mod_ref, hg_ref, w_ref,
                 xo_ref, co_ref, y_ref, *, d, e, tiles_per_batch):
    i = pl.program_id(0)
    heads = e // HEAD_DIM
    for h in range(heads):
        lanes = slice(h * HEAD_DIM, (h + 1) * HEAD_DIM)
        o = of_ref[:, lanes].astype(F32) + ob_ref[:, lanes].astype(F32)
        var = jnp.mean(o * o, axis=-1, keepdims=True)
        yn = (o * lax.rsqrt(var + EPS)) * hg_ref[...]
        y_ref[:, lanes] = (yn * z_ref[:, lanes].astype(F32)).astype(BF16)
    mix = jnp.dot(y_ref[...], w_ref[...], preferred_element_type=F32)

    @pl.when(i == 0)
    def _():
        gate = mod_ref[2:3, 2 * d:3 * d]
        co_ref[...] = ctx_ref[...] + gate * mix

    @pl.when(i > 0)
    def _():
        gate = mod_ref[pl.ds((i - 1) // tiles_per_batch, 1), 2 * d:3 * d]
        xo_ref[...] = x_ref[...] + gate * mix


def _out0(o_f, o_b, a, ctx2d, x2d, mods, hg_norm_g, w_bf, *, n_lat):
    rows_ctx, d = ctx2d.shape
    rows_lat = x2d.shape[0]
    e = o_f.shape[1]
    tm = ROW_TILE
    n_tiles = 1 + rows_lat // tm
    kern = functools.partial(_out0_kernel, d=d, e=e, tiles_per_batch=n_lat // tm)
    lat_idx = lambda i: (jnp.maximum(i - 1, 0), 0)
    return pl.pallas_call(
        kern,
        grid=(n_tiles,),
        in_specs=[
            pl.BlockSpec((tm, e), lambda i: (i, 0)),
            pl.BlockSpec((tm, e), lambda i: (i, 0)),
            pl.BlockSpec((tm, e), lambda i: (i, HG_STREAMS - 1)),
            pl.BlockSpec((tm, d), lambda i: (0, 0)),
            pl.BlockSpec((tm, d), lat_idx),
            pl.BlockSpec((None, MOD_ROWS, 3 * d), lambda i: (0, 0, 0)),
            pl.BlockSpec((1, HEAD_DIM), lambda i: (0, 0)),
            pl.BlockSpec((e, d), lambda i: (0, 0)),
        ],
        out_specs=[pl.BlockSpec((tm, d), lat_idx), pl.BlockSpec((tm, d), lambda i: (0, 0))],
        out_shape=[jax.ShapeDtypeStruct((rows_lat, d), F32), jax.ShapeDtypeStruct((rows_ctx, d), F32)],
        scratch_shapes=[pltpu.VMEM((tm, e), BF16)],
        compiler_params=_cparams(("arbitrary",)),
        name="readout0",
    )(o_f, o_b, a, ctx2d, x2d, mods, hg_norm_g, w_bf)


def _proj1_kernel(x_ref, mod_ref, ng_ref, w_ref, u_ref, z_ref, h_ref,
                  *, d, e, tiles_per_batch, sub, n2):
    i = pl.program_id(0)
    j = pl.program_id(1)
    tm = h_ref.shape[0]

    @pl.when(j == 0)
    def _():
        m = mod_ref[pl.ds(i // tiles_per_batch, 1), :]
        h_ref[...] = _modulated_norm(x_ref[...], ng_ref[...], m[:, 0:d], m[:, d:2 * d]).astype(BF16)

    def acc(c):
        return jnp.dot(h_ref[...], w_ref[:, c * sub:(c + 1) * sub], preferred_element_type=F32)

    nsub = e // sub

    @pl.when(j == 0)
    def _():
        for c in range(nsub):
            a = acc(c)
            for n1l in range(tm // n2):
                for m in range(n2 // SUBLANES):
                    src = n1l * n2 + m * SUBLANES
                    for cl in range(sub // LANES):
                        u_ref[m, c * (sub // LANES) + cl, n1l * SUBLANES:(n1l + 1) * SUBLANES, :] = (
                            a[src:src + SUBLANES, cl * LANES:(cl + 1) * LANES])

    @pl.when(j == 1)
    def _():
        for c in range(nsub):
            z_ref[:, c * sub:(c + 1) * sub] = _silu(acc(c)).astype(BF16)


def _proj1(x2d, mods, norm_g, w_bf, *, batch, n_lat, layer):
    rows, d = x2d.shape
    e = w_bf.shape[1] // 2
    tm = ROW_TILE
    n1, n2 = _dft_factors(n_lat)
    q = n2 // SUBLANES
    tpb = n_lat // tm
    rows_blk = (tm // n2) * SUBLANES
    kern = functools.partial(_proj1_kernel, d=d, e=e, tiles_per_batch=tpb, sub=512, n2=n2)
    return pl.pallas_call(
        kern,
        grid=(rows // tm, 2),
        in_specs=[
            pl.BlockSpec((tm, d), lambda i, j: (i, 0)),
            pl.BlockSpec((None, MOD_ROWS, 3 * d), lambda i, j: (layer, 0, 0)),
            pl.BlockSpec((None, 1, d), lambda i, j: (layer, 0, 0)),
            pl.BlockSpec((d, e), lambda i, j: (0, j)),
        ],
        out_specs=[pl.BlockSpec((None, q, e // LANES, rows_blk, LANES),
                                lambda i, j: (i // tpb, 0, 0, i % tpb, 0)),
                   pl.BlockSpec((tm, e), lambda i, j: (i, 0))],
        out_shape=[jax.ShapeDtypeStruct((batch, q, e // LANES, n1 * SUBLANES, LANES), F32),
                   jax.ShapeDtypeStruct((rows, e), BF16)],
        scratch_shapes=[pltpu.VMEM((tm, d), BF16)],
        compiler_params=_cparams(("arbitrary", "arbitrary")),
        name="proj1",
    )(x2d, mods, norm_g, w_bf)


def _dft_factors(n):
    n1 = 128 if n % 128 == 0 else n
    return n1, n // n1


def _dft_tables(n, group_dim):
    n1, n2 = _dft_factors(n)
    a2 = np.arange(n2, dtype=np.float64)[:, None, None]
    k1 = np.arange(n1, dtype=np.float64)[None, :, None]
    m1 = np.arange(n1, dtype=np.float64)[None, None, :]
    ang = -2.0 * np.pi * (a2 * k1 / n + m1 * k1 / n1)
    stage1 = np.concatenate([np.cos(ang), np.sin(ang)], axis=1) / np.sqrt(n1)
    kk = np.arange(n2, dtype=np.float64)
    ang2 = -2.0 * np.pi * np.outer(kk, kk) / n2
    fr, fi = np.cos(ang2) / np.sqrt(n2), np.sin(ang2) / np.sqrt(n2)
    stage2 = np.stack([np.concatenate([fr, fi], axis=0),
                       np.concatenate([-fi, fr], axis=0)])
    cc = np.arange(group_dim, dtype=np.float64)
    ang3 = 2.0 * np.pi * np.outer(cc, cc) / group_dim
    chan = np.concatenate([np.cos(ang3), np.sin(ang3)], axis=0) / np.sqrt(group_dim)
    to = lambda t: jnp.asarray(t, dtype=F32).astype(BF16)
    return to(stage1), to(stage2), to(chan)


def _dft1_kernel(m_ref, x_ref, tr_ref, ti_ref, *, n1):
    nlb = x_ref.shape[0]
    for r in range(SUBLANES):
        rows = pl.ds(r, n1, stride=SUBLANES)
        xr = jnp.concatenate([x_ref[lb, rows, :] for lb in range(nlb)], axis=1).astype(BF16)
        t = jnp.dot(m_ref[r], xr, preferred_element_type=F32)
        for lb in range(nlb):
            lanes = slice(lb * LANES, (lb + 1) * LANES)
            tr_ref[lb, rows, :] = t[0:n1, lanes]
            ti_ref[lb, rows, :] = t[n1:2 * n1, lanes]


def _dft1(u, stage1):
    batch, q, nlb, rows, _ = u.shape
    n2, two_n1, n1 = stage1.shape
    lbs = min(nlb, 8)
    m4 = stage1.reshape(q, SUBLANES, two_n1, n1)
    spec = pl.BlockSpec((None, None, lbs, rows, LANES), lambda b, m, c: (b, m, c, 0, 0))
    out = jax.ShapeDtypeStruct(u.shape, F32)
    return pl.pallas_call(
        functools.partial(_dft1_kernel, n1=n1),
        grid=(batch, q, nlb // lbs),
        in_specs=[pl.BlockSpec((None, SUBLANES, two_n1, n1), lambda b, m, c: (m, 0, 0, 0)), spec],
        out_specs=[spec, spec],
        out_shape=[out, out],
        compiler_params=_cparams(("arbitrary", "arbitrary", "arbitrary")),
        name="dft_stage1",
    )(m4, u)


def _dft2_kernel(f_ref, cs_ref, tr_ref, ti_ref, y_ref, *, n2, gd):
    q, nlb = tr_ref.shape[0], tr_ref.shape[1]
    per_group = gd // LANES

    def gather(ref, tile):
        return jnp.concatenate(
            [ref[:, lb, tile, :].reshape(n2, LANES) for lb in range(nlb)], axis=1).astype(BF16)

    for j in range(SUBLANES):
        tile = slice(j * SUBLANES, (j + 1) * SUBLANES)
        g = (jnp.dot(f_ref[0], gather(tr_ref, tile), preferred_element_type=F32)
             + jnp.dot(f_ref[1], gather(ti_ref, tile), preferred_element_type=F32))
        gr = g[0:n2].astype(BF16)
        gi = g[n2:2 * n2].astype(BF16)
        rows = pl.ds(j, n2, stride=SUBLANES)
        for gq in range(nlb // per_group):
            lanes = slice(gq * gd, (gq + 1) * gd)
            yg = (jnp.dot(gr[:, lanes], cs_ref[0:gd, :], preferred_element_type=F32)
                  + jnp.dot(gi[:, lanes], cs_ref[gd:2 * gd, :], preferred_element_type=F32))
            for cl in range(per_group):
                y_ref[gq * per_group + cl, rows, :] = yg[:, cl * LANES:(cl + 1) * LANES]


def _dft2(tr, ti, stage2, chan, *, n1):
    batch, q, nlb, rows, _ = tr.shape
    n2 = q * SUBLANES
    kt = n1 // SUBLANES
    gd = chan.shape[1]
    tile_rows = SUBLANES * SUBLANES
    shape_in = (batch, q, nlb, kt, tile_rows, LANES)
    in_spec = pl.BlockSpec((None, q, nlb, None, tile_rows, LANES), lambda b, k: (b, 0, 0, k, 0, 0))
    return pl.pallas_call(
        functools.partial(_dft2_kernel, n2=n2, gd=gd),
        grid=(batch, kt),
        in_specs=[pl.BlockSpec((2, 2 * n2, n2), lambda b, k: (0, 0, 0)),
                  pl.BlockSpec((2 * gd, gd), lambda b, k: (0, 0)),
                  in_spec, in_spec],
        out_specs=pl.BlockSpec((None, None, nlb, n2 * SUBLANES, LANES), lambda b, k: (b, k, 0, 0, 0)),
        out_shape=jax.ShapeDtypeStruct((batch, kt, nlb, n2 * SUBLANES, LANES), F32),
        compiler_params=_cparams(("arbitrary", "arbitrary")),
        name="dft_stage2",
    )(stage2, chan, tr.reshape(shape_in), ti.reshape(shape_in))


def _out1_kernel(y_in_ref, z_ref, x_ref, mod_ref, w_ref, fg_ref, o_ref, y_ref,
                 *, d, n1, tiles_per_batch):
    i = pl.program_id(0)
    kt, nlb = y_in_ref.shape[0], y_in_ref.shape[1]
    k2_per_tile = y_in_ref.shape[2] // SUBLANES
    pair = 2 * SUBLANES
    for k2l in range(k2_per_tile):
        src = slice(k2l * SUBLANES, (k2l + 1) * SUBLANES)
        for kp in range(kt // 2):
            dst = slice(k2l * n1 + kp * pair, k2l * n1 + (kp + 1) * pair)
            yv = jnp.concatenate(
                [jnp.concatenate([y_in_ref[2 * kp + a, lb, src, :] for lb in range(nlb)], axis=1)
                 for a in range(2)], axis=0)
            y_ref[dst, :] = (yv * z_ref[dst, :].astype(F32)).astype(BF16)
    mix = jnp.dot(y_ref[...], w_ref[...], preferred_element_type=F32)
    gate = mod_ref[pl.ds(i // tiles_per_batch, 1), 2 * d:3 * d]
    x = x_ref[...] + gate * mix
    var = jnp.mean(x * x, axis=-1, keepdims=True)
    o_ref[...] = (x * lax.rsqrt(var + EPS)) * fg_ref[...]


def _out1(y_in, zs, x2d, mods, w_bf, final_g, *, n_lat, n1, layer):
    rows, d = x2d.shape
    batch, kt, nlb, _, _ = y_in.shape
    e = nlb * LANES
    tm = ROW_TILE
    assert tm % n1 == 0
    tpb = n_lat // tm
    blk_rows = (tm // n1) * SUBLANES
    kern = functools.partial(_out1_kernel, d=d, n1=n1, tiles_per_batch=tpb)
    return pl.pallas_call(
        kern,
        grid=(rows // tm,),
        in_specs=[
            pl.BlockSpec((None, kt, nlb, blk_rows, LANES), lambda i: (i // tpb, 0, 0, i % tpb, 0)),
            pl.BlockSpec((tm, e), lambda i: (i, 0)),
            pl.BlockSpec((tm, d), lambda i: (i, 0)),
            pl.BlockSpec((None, MOD_ROWS, 3 * d), lambda i: (layer, 0, 0)),
            pl.BlockSpec((e, d), lambda i: (0, 0)),
            pl.BlockSpec((1, d), lambda i: (0, 0)),
        ],
        out_specs=pl.BlockSpec((tm, d), lambda i: (i, 0)),
        out_shape=jax.ShapeDtypeStruct((rows, d), F32),
        scratch_shapes=[pltpu.VMEM((tm, e), BF16)],
        compiler_params=_cparams(("arbitrary",)),
        name="readout1",
    )(y_in, zs, x2d, mods, w_bf, final_g)


def kernel(x, c, ctx, c_ctx, ada_w, ada_b, norm_g, hg_w_in, hg_lb_logits, hg_norm_g, hg_w_out,
           ft_w_in, ft_w_out, final_g):
    batch, n_lat, d = x.shape
    n_ctx = ctx.shape[1]
    depth = ada_w.shape[0]
    e = hg_w_out.shape[1]
    assert depth == 2 and batch == 2 and batch + 1 <= MOD_ROWS

    cv = jnp.concatenate([c, c_ctx[None, :], jnp.zeros((MOD_ROWS - batch - 1, d), F32)], axis=0)
    mods = _ada_table(cv, ada_w, ada_b)
    ng = norm_g.reshape(depth, 1, d)
    x2d = x.reshape(batch * n_lat, d)
    ctx2d = ctx.reshape(batch * n_ctx, d)

    a, g = _proj0(ctx2d, x2d, mods, ng, hg_w_in[0].astype(BF16), hg_lb_logits,
                  n_lat=n_lat, lb_index=0)
    o_f, o_b = _scan(a, g, batch=batch, n_lat=n_lat, n_ctx=n_ctx, e=e)
    x1, _ctx1 = _out0(o_f, o_b, a, ctx2d, x2d, mods, hg_norm_g[0:1], hg_w_out[0].astype(BF16),
                      n_lat=n_lat)

    u, zs = _proj1(x1, mods, ng, ft_w_in[0].astype(BF16), batch=batch, n_lat=n_lat, layer=1)
    stage1, stage2, chan = _dft_tables(n_lat, e // FT_GROUPS)
    n1 = stage1.shape[2]
    tr, ti = _dft1(u, stage1)
    y = _dft2(tr, ti, stage2, chan, n1=n1)
    out = _out1(y, zs, x1, mods, ft_w_out[0].astype(BF16), final_g.reshape(1, d),
                n_lat=n_lat, n1=n1, layer=1)
    return out.reshape(batch, n_lat, d)
```

```python
import functools

import numpy as np
import jax
import jax.numpy as jnp
from jax import lax
from jax.experimental import pallas as pl
from jax.experimental.pallas import tpu as pltpu

F32 = jnp.float32
BF16 = jnp.bfloat16

EPS = 1e-6
HEAD_DIM = 128
HG_STREAMS = 5
FT_GROUPS = 8
SCAN_CHUNK = 64
SCAN_HEADS = 4
ROW_TILE = 512
PROJ_TILE = 1024
MOD_ROWS = 8
SUBLANES = 8
LANES = 128
VMEM_LIMIT = 56 * 1024 * 1024


def _cparams(sem):
    return pltpu.CompilerParams(dimension_semantics=sem, vmem_limit_bytes=VMEM_LIMIT)


def _sigmoid(x):
    return 1.0 / (1.0 + jnp.exp(-x))


def _silu(x):
    return x * _sigmoid(x)


def _modulated_norm(x, g, shift, scale):
    var = jnp.mean(x * x, axis=-1, keepdims=True)
    return (x * lax.rsqrt(var + EPS)) * g * (1.0 + scale) + shift


def _ada_kernel(cv_ref, w_ref, b_ref, o_ref):
    a = _silu(cv_ref[...])
    o_ref[...] = jnp.dot(a, w_ref[...], preferred_element_type=F32,
                         precision=lax.Precision.HIGHEST) + b_ref[...]


def _ada_table(cv, ada_w, ada_b):
    depth, d, d3 = ada_w.shape
    tn = 1024
    return pl.pallas_call(
        _ada_kernel,
        grid=(depth, d3 // tn),
        in_specs=[
            pl.BlockSpec((MOD_ROWS, d), lambda l, j: (0, 0)),
            pl.BlockSpec((None, d, tn), lambda l, j: (l, 0, j)),
            pl.BlockSpec((None, 1, tn), lambda l, j: (l, 0, j)),
        ],
        out_specs=pl.BlockSpec((None, MOD_ROWS, tn), lambda l, j: (l, 0, j)),
        out_shape=jax.ShapeDtypeStruct((depth, MOD_ROWS, d3), F32),
        compiler_params=_cparams(("arbitrary", "arbitrary")),
        name="ada_table",
    )(cv, ada_w, ada_b.reshape(depth, 1, d3))


def _proj0_kernel(ctx_ref, x_ref, mod_ref, ng_ref, w_ref, lbl_ref, a_ref, g_ref, h_ref,
                  *, d, e, tiles_per_batch, lb_index, sub):
    i = pl.program_id(0)
    j = pl.program_id(1)
    tm = h_ref.shape[0]

    def fill(src_ref, row):
        m = mod_ref[pl.ds(row, 1), :]
        h_ref[...] = _modulated_norm(src_ref[...], ng_ref[...], m[:, 0:d], m[:, d:2 * d]).astype(BF16)

    @pl.when(j == 0)
    def _():
        @pl.when(i == 0)
        def _():
            fill(ctx_ref, 2)

        @pl.when(i > 0)
        def _():
            fill(x_ref, (i - 1) // tiles_per_batch)

    def acc(c):
        return jnp.dot(h_ref[...], w_ref[:, c * sub:(c + 1) * sub], preferred_element_type=F32)

    nsub = e // sub

    @pl.when(j == 0)
    def _():
        for c in range(nsub):
            a_ref[:, c * sub:(c + 1) * sub] = _silu(acc(c)).astype(BF16)

    @pl.when((j == 1) | (j == 2))
    def _():
        dirn = j - 1
        logits = lbl_ref[:, pl.ds(dirn, 1), :]
        mx = jnp.max(logits, axis=0, keepdims=True)
        ex = jnp.exp(logits - mx)
        p = ex / jnp.sum(ex, axis=0, keepdims=True)
        lb_full = jnp.sum(p[0:lb_index + 1], axis=0)
        for c in range(nsub):
            lb = lb_full[:, c * sub:(c + 1) * sub]
            sig = _sigmoid(acc(c))
            f = lb + (1.0 - lb) * sig
            a_ref[:, c * sub:(c + 1) * sub] = ((1.0 - lb) * (1.0 - sig)).astype(BF16)
            g_ref[:, c * sub:(c + 1) * sub] = jnp.log2(f).astype(BF16)

    @pl.when(j == 3)
    def _():
        for c in range(nsub):
            a_ref[:, c * sub:(c + 1) * sub] = acc(c).astype(BF16)

    @pl.when(j == 4)
    def _():
        for c in range(nsub):
            a_ref[:, c * sub:(c + 1) * sub] = _silu(acc(c)).astype(BF16)


def _proj0(ctx2d, x2d, mods, norm_g, w_bf, lb_logits, *, n_lat, lb_index):
    rows_ctx, d = ctx2d.shape
    rows_lat = x2d.shape[0]
    e = w_bf.shape[1] // HG_STREAMS
    tm = PROJ_TILE
    assert rows_ctx == tm and n_lat % tm == 0
    n_tiles = 1 + rows_lat // tm
    rows = tm + rows_lat
    nl = lb_logits.shape[0]
    kern = functools.partial(_proj0_kernel, d=d, e=e, tiles_per_batch=n_lat // tm,
                             lb_index=lb_index, sub=512)
    return pl.pallas_call(
        kern,
        grid=(n_tiles, HG_STREAMS),
        in_specs=[
            pl.BlockSpec((tm, d), lambda i, j: (0, 0)),
            pl.BlockSpec((tm, d), lambda i, j: (jnp.maximum(i - 1, 0), 0)),
            pl.BlockSpec((None, MOD_ROWS, 3 * d), lambda i, j: (0, 0, 0)),
            pl.BlockSpec((None, 1, d), lambda i, j: (0, 0, 0)),
            pl.BlockSpec((d, e), lambda i, j: (0, j)),
            pl.BlockSpec((nl, 2, e), lambda i, j: (0, 0, 0)),
        ],
        out_specs=[
            pl.BlockSpec((tm, e), lambda i, j: (i, j)),
            pl.BlockSpec((tm, e), lambda i, j: (i, jnp.clip(j - 1, 0, 1))),
        ],
        out_shape=[
            jax.ShapeDtypeStruct((rows, HG_STREAMS * e), BF16),
            jax.ShapeDtypeStruct((rows, 2 * e), BF16),
        ],
        scratch_shapes=[pltpu.VMEM((tm, d), BF16)],
        compiler_params=_cparams(("arbitrary", "arbitrary")),
        name="proj0",
    )(ctx2d, x2d, mods, norm_g, w_bf, lb_logits)


def _scan_kernel(qf_ref, kf_ref, gf_ref, vf_ref, qb_ref, kb_ref, gb_ref, vb_ref,
                 of_ref, ob_ref, st_ref, *, n_chunks):
    s = pl.program_id(2)
    c_len = SCAN_CHUNK
    hd = HEAD_DIM

    @pl.when(s == 0)
    def _():
        st_ref[...] = jnp.zeros_like(st_ref)

    row = lax.broadcasted_iota(jnp.int32, (c_len, c_len), 0)
    col = lax.broadcasted_iota(jnp.int32, (c_len, c_len), 1)
    causal = (col <= row, col >= row)
    tri = tuple(m.astype(F32).astype(BF16) for m in causal)
    end_row = (c_len - 1, 0)
    mid = c_len // 2
    refs = ((qf_ref, kf_ref, gf_ref, vf_ref, of_ref), (qb_ref, kb_ref, gb_ref, vb_ref, ob_ref))
    nt = (((1,), (1,)), ((), ()))
    tn = (((0,), (0,)), ((), ()))

    pre = {}
    for dirn in range(2):
        q_ref, k_ref, g_ref, v_ref, _ = refs[dirn]
        for c in range(n_chunks):
            rows = slice(c * c_len, (c + 1) * c_len)
            b = jnp.dot(tri[dirn], g_ref[rows, :], preferred_element_type=F32)
            tot = b[end_row[dirn]:end_row[dirn] + 1, :]
            ref = b[mid:mid + 1, :]
            q = q_ref[rows, :].astype(F32)
            k = k_ref[rows, :].astype(F32)
            pre[dirn, c] = dict(
                qt=(q * jnp.exp2(b - ref)).astype(BF16),
                kt=(k * jnp.exp2(ref - b)).astype(BF16),
                qh=(q * jnp.exp2(b)).astype(BF16),
                kh=(k * jnp.exp2(tot - b)).astype(BF16),
                dec=jnp.exp2(tot),
                v=v_ref[rows, :],
            )

    prob, upd = {}, {}
    for dirn in range(2):
        for c in range(n_chunks):
            w = pre[dirn, c]
            for hh in range(SCAN_HEADS):
                lanes = slice(hh * hd, (hh + 1) * hd)
                sc = lax.dot_general(w["qt"][:, lanes], w["kt"][:, lanes], nt,
                                     preferred_element_type=F32)
                prob[dirn, c, hh] = jnp.where(causal[dirn], sc, 0.0).astype(BF16)
                upd[dirn, c, hh] = lax.dot_general(w["v"][:, lanes], w["kh"][:, lanes], tn,
                                                   preferred_element_type=F32)

    for dirn in range(2):
        o_ref = refs[dirn][4]
        order = range(n_chunks) if dirn == 0 else range(n_chunks - 1, -1, -1)
        for hh in range(SCAN_HEADS):
            lanes = slice(hh * hd, (hh + 1) * hd)
            st = st_ref[dirn, hh]
            for c in order:
                w = pre[dirn, c]
                rows = slice(c * c_len, (c + 1) * c_len)
                o = (jnp.dot(prob[dirn, c, hh], w["v"][:, lanes], preferred_element_type=F32)
                     + lax.dot_general(w["qh"][:, lanes], st.astype(BF16), nt,
                                       preferred_element_type=F32))
                o_ref[rows, lanes] = o.astype(BF16)
                st = st * w["dec"][:, lanes] + upd[dirn, c, hh]
            st_ref[dirn, hh] = st


def _scan(a, g, *, batch, n_lat, n_ctx, e):
    ts = n_ctx
    assert ts % SCAN_CHUNK == 0 and n_lat % ts == 0
    lat_steps = n_lat // ts
    steps = 1 + lat_steps
    hw = SCAN_HEADS * HEAD_DIM
    groups = e // hw
    rows = a.shape[0]
    assert batch * n_ctx <= PROJ_TILE and PROJ_TILE % ts == 0
    lat0 = PROJ_TILE // ts

    def row_f(b, s):
        return jnp.where(s == 0, b, lat0 + lat_steps * b + s - 1)

    def row_b(b, s):
        return jnp.where(s == 0, b, lat0 + lat_steps * b + lat_steps - s)

    def spec(stream, rfn):
        return pl.BlockSpec((ts, hw), lambda b, hg, s: (rfn(b, s), stream * groups + hg))

    kern = functools.partial(_scan_kernel, n_chunks=ts // SCAN_CHUNK)
    return pl.pallas_call(
        kern,
        grid=(batch, groups, steps),
        in_specs=[spec(0, row_f), spec(1, row_f), spec(0, row_f), spec(3, row_f),
                  spec(0, row_b), spec(2, row_b), spec(1, row_b), spec(3, row_b)],
        out_specs=[pl.BlockSpec((ts, hw), lambda b, hg, s: (row_f(b, s), hg)),
                   pl.BlockSpec((ts, hw), lambda b, hg, s: (row_b(b, s), hg))],
        out_shape=[jax.ShapeDtypeStruct((rows, e), BF16), jax.ShapeDtypeStruct((rows, e), BF16)],
        scratch_shapes=[pltpu.VMEM((2, SCAN_HEADS, HEAD_DIM, HEAD_DIM), F32)],
        compiler_params=_cparams(("arbitrary", "arbitrary", "arbitrary")),
        name="hgrn_scan",
    )(a, a, g, a, a, a, g, a)


def _out0_kernel(of_ref, ob_ref, z_ref, ctx_ref, x_ref, mod_ref, hg_ref, w_ref,
                 xo_ref, co_ref, y_ref, *, d, e, tiles_per_batch):
    i = pl.program_id(0)
    heads = e // HEAD_DIM
    for h in range(heads):
        lanes = slice(h * HEAD_DIM, (h + 1) * HEAD_DIM)
        o = of_ref[:, lanes].astype(F32) + ob_ref[:, lanes].astype(F32)
        var = jnp.mean(o * o, axis=-1, keepdims=True)
        yn = (o * lax.rsqrt(var + EPS)) * hg_ref[...]
        y_ref[:, lanes] = (yn * z_ref[:, lanes].astype(F32)).astype(BF16)
    mix = jnp.dot(y_ref[...], w_ref[...], preferred_element_type=F32)

    @pl.when(i == 0)
    def _():
        gate = mod_ref[2:3, 2 * d:3 * d]
        co_ref[...] = ctx_ref[...] + gate * mix

    @pl.when(i > 0)
    def _():
        gate = mod_ref[pl.ds((i - 1) // tiles_per_batch, 1), 2 * d:3 * d]
        xo_ref[...] = x_ref[...] + gate * mix


def _out0(o_f, o_b, a, ctx2d, x2d, mods, hg_norm_g, w_bf, *, n_lat):
    rows_ctx, d = ctx2d.shape
    rows_lat = x2d.shape[0]
    e = o_f.shape[1]
    tm = ROW_TILE
    assert rows_ctx == tm and PROJ_TILE % tm == 0
    n_tiles = 1 + rows_lat // tm
    kern = functools.partial(_out0_kernel, d=d, e=e, tiles_per_batch=n_lat // tm)
    lat_idx = lambda i: (jnp.maximum(i - 1, 0), 0)
    skip = PROJ_TILE // tm - 1
    src_row = lambda i: jnp.where(i == 0, 0, i + skip)
    return pl.pallas_call(
        kern,
        grid=(n_tiles,),
        in_specs=[
            pl.BlockSpec((tm, e), lambda i: (src_row(i), 0)),
            pl.BlockSpec((tm, e), lambda i: (src_row(i), 0)),
            pl.BlockSpec((tm, e), lambda i: (src_row(i), HG_STREAMS - 1)),
            pl.BlockSpec((tm, d), lambda i: (0, 0)),
            pl.BlockSpec((tm, d), lat_idx),
            pl.BlockSpec((None, MOD_ROWS, 3 * d), lambda i: (0, 0, 0)),
            pl.BlockSpec((1, HEAD_DIM), lambda i: (0, 0)),
            pl.BlockSpec((e, d), lambda i: (0, 0)),
        ],
        out_specs=[pl.BlockSpec((tm, d), lat_idx), pl.BlockSpec((tm, d), lambda i: (0, 0))],
        out_shape=[jax.ShapeDtypeStruct((rows_lat, d), F32), jax.ShapeDtypeStruct((rows_ctx, d), F32)],
        scratch_shapes=[pltpu.VMEM((tm, e), BF16)],
        compiler_params=_cparams(("arbitrary",)),
        name="readout0",
    )(o_f, o_b, a, ctx2d, x2d, mods, hg_norm_g, w_bf)


def _proj1_kernel(x_ref, mod_ref, ng_ref, w_ref, u_ref, z_ref, h_ref,
                  *, d, e, tiles_per_batch, sub, n2):
    i = pl.program_id(0)
    j = pl.program_id(1)
    tm = h_ref.shape[0]

    @pl.when(j == 0)
    def _():
        m = mod_ref[pl.ds(i // tiles_per_batch, 1), :]
        h_ref[...] = _modulated_norm(x_ref[...], ng_ref[...], m[:, 0:d], m[:, d:2 * d]).astype(BF16)

    def acc(c):
        return jnp.dot(h_ref[...], w_ref[:, c * sub:(c + 1) * sub], preferred_element_type=F32)

    nsub = e // sub

    @pl.when(j == 0)
    def _():
        for c in range(nsub):
            a = acc(c)
            for n1l in range(tm // n2):
                for m in range(n2 // SUBLANES):
                    src = n1l * n2 + m * SUBLANES
                    for cl in range(sub // LANES):
                        u_ref[m, c * (sub // LANES) + cl, n1l * SUBLANES:(n1l + 1) * SUBLANES, :] = (
                            a[src:src + SUBLANES, cl * LANES:(cl + 1) * LANES])

    @pl.when(j == 1)
    def _():
        for c in range(nsub):
            z_ref[:, c * sub:(c + 1) * sub] = _silu(acc(c)).astype(BF16)


def _proj1(x2d, mods, norm_g, w_bf, *, batch, n_lat, layer):
    rows, d = x2d.shape
    e = w_bf.shape[1] // 2
    tm = PROJ_TILE
    n1, n2 = _dft_factors(n_lat)
    q = n2 // SUBLANES
    tpb = n_lat // tm
    rows_blk = (tm // n2) * SUBLANES
    kern = functools.partial(_proj1_kernel, d=d, e=e, tiles_per_batch=tpb, sub=512, n2=n2)
    return pl.pallas_call(
        kern,
        grid=(rows // tm, 2),
        in_specs=[
            pl.BlockSpec((tm, d), lambda i, j: (i, 0)),
            pl.BlockSpec((None, MOD_ROWS, 3 * d), lambda i, j: (layer, 0, 0)),
            pl.BlockSpec((None, 1, d), lambda i, j: (layer, 0, 0)),
            pl.BlockSpec((d, e), lambda i, j: (0, j)),
        ],
        out_specs=[pl.BlockSpec((None, q, e // LANES, rows_blk, LANES),
                                lambda i, j: (i // tpb, 0, 0, i % tpb, 0)),
                   pl.BlockSpec((tm, e), lambda i, j: (i, 0))],
        out_shape=[jax.ShapeDtypeStruct((batch, q, e // LANES, n1 * SUBLANES, LANES), F32),
                   jax.ShapeDtypeStruct((rows, e), BF16)],
        scratch_shapes=[pltpu.VMEM((tm, d), BF16)],
        compiler_params=_cparams(("arbitrary", "arbitrary")),
        name="proj1",
    )(x2d, mods, norm_g, w_bf)


def _dft_factors(n):
    n1 = 128 if n % 128 == 0 else n
    return n1, n // n1


def _dft_tables(n, group_dim):
    n1, n2 = _dft_factors(n)
    a2 = np.arange(n2, dtype=np.float64)[:, None, None]
    k1 = np.arange(n1, dtype=np.float64)[None, :, None]
    m1 = np.arange(n1, dtype=np.float64)[None, None, :]
    ang = -2.0 * np.pi * (a2 * k1 / n + m1 * k1 / n1)
    stage1 = np.concatenate([np.cos(ang), np.sin(ang)], axis=1) / np.sqrt(n1)
    kk = np.arange(n2, dtype=np.float64)
    ang2 = -2.0 * np.pi * np.outer(kk, kk) / n2
    fr, fi = np.cos(ang2) / np.sqrt(n2), np.sin(ang2) / np.sqrt(n2)
    stage2 = np.stack([np.concatenate([fr, fi], axis=0),
                       np.concatenate([-fi, fr], axis=0)])
    cc = np.arange(group_dim, dtype=np.float64)
    ang3 = 2.0 * np.pi * np.outer(cc, cc) / group_dim
    chan = np.concatenate([np.cos(ang3), np.sin(ang3)], axis=0) / np.sqrt(group_dim)
    to = lambda t: jnp.asarray(t, dtype=F32).astype(BF16)
    return to(stage1), to(stage2), to(chan)


def _dft1_kernel(m_ref, x_ref, tr_ref, ti_ref, *, n1):
    nlb = x_ref.shape[0]
    for r in range(SUBLANES):
        rows = pl.ds(r, n1, stride=SUBLANES)
        xr = jnp.concatenate([x_ref[lb, rows, :] for lb in range(nlb)], axis=1).astype(BF16)
        t = jnp.dot(m_ref[r], xr, preferred_element_type=F32)
        for lb in range(nlb):
            lanes = slice(lb * LANES, (lb + 1) * LANES)
            tr_ref[lb, rows, :] = t[0:n1, lanes]
            ti_ref[lb, rows, :] = t[n1:2 * n1, lanes]


def _dft1(u, stage1):
    batch, q, nlb, rows, _ = u.shape
    n2, two_n1, n1 = stage1.shape
    lbs = min(nlb, 8)
    m4 = stage1.reshape(q, SUBLANES, two_n1, n1)
    spec = pl.BlockSpec((None, None, lbs, rows, LANES), lambda b, m, c: (b, m, c, 0, 0))
    out = jax.ShapeDtypeStruct(u.shape, F32)
    return pl.pallas_call(
        functools.partial(_dft1_kernel, n1=n1),
        grid=(batch, q, nlb // lbs),
        in_specs=[pl.BlockSpec((None, SUBLANES, two_n1, n1), lambda b, m, c: (m, 0, 0, 0)), spec],
        out_specs=[spec, spec],
        out_shape=[out, out],
        compiler_params=_cparams(("arbitrary", "arbitrary", "arbitrary")),
        name="dft_stage1",
    )(m4, u)


def _dft2_kernel(f_ref, cs_ref, tr_ref, ti_ref, y_ref, *, n2, gd):
    q, nlb = tr_ref.shape[0], tr_ref.shape[1]
    per_group = gd // LANES

    def gather(ref, tile):
        return jnp.concatenate(
            [ref[:, lb, tile, :].reshape(n2, LANES) for lb in range(nlb)], axis=1).astype(BF16)

    for j in range(SUBLANES):
        tile = slice(j * SUBLANES, (j + 1) * SUBLANES)
        g = (jnp.dot(f_ref[0], gather(tr_ref, tile), preferred_element_type=F32)
             + jnp.dot(f_ref[1], gather(ti_ref, tile), preferred_element_type=F32))
        gr = g[0:n2].astype(BF16)
        gi = g[n2:2 * n2].astype(BF16)
        rows = pl.ds(j, n2, stride=SUBLANES)
        for gq in range(nlb // per_group):
            lanes = slice(gq * gd, (gq + 1) * gd)
            yg = (jnp.dot(gr[:, lanes], cs_ref[0:gd, :], preferred_element_type=F32)
                  + jnp.dot(gi[:, lanes], cs_ref[gd:2 * gd, :], preferred_element_type=F32))
            for cl in range(per_group):
                y_ref[gq * per_group + cl, rows, :] = yg[:, cl * LANES:(cl + 1) * LANES]


def _dft2(tr, ti, stage2, chan, *, n1):
    batch, q, nlb, rows, _ = tr.shape
    n2 = q * SUBLANES
    kt = n1 // SUBLANES
    gd = chan.shape[1]
    tile_rows = SUBLANES * SUBLANES
    shape_in = (batch, q, nlb, kt, tile_rows, LANES)
    in_spec = pl.BlockSpec((None, q, nlb, None, tile_rows, LANES), lambda b, k: (b, 0, 0, k, 0, 0))
    return pl.pallas_call(
        functools.partial(_dft2_kernel, n2=n2, gd=gd),
        grid=(batch, kt),
        in_specs=[pl.BlockSpec((2, 2 * n2, n2), lambda b, k: (0, 0, 0)),
                  pl.BlockSpec((2 * gd, gd), lambda b, k: (0, 0)),
                  in_spec, in_spec],
        out_specs=pl.BlockSpec((None, None, nlb, n2 * SUBLANES, LANES), lambda b, k: (b, k, 0, 0, 0)),
        out_shape=jax.ShapeDtypeStruct((batch, kt, nlb, n2 * SUBLANES, LANES), F32),
        compiler_params=_cparams(("arbitrary", "arbitrary")),
        name="dft_stage2",
    )(stage2, chan, tr.reshape(shape_in), ti.reshape(shape_in))


def _out1_kernel(y_in_ref, z_ref, x_ref, mod_ref, w_ref, fg_ref, o_ref, y_ref,
                 *, d, n1, tiles_per_batch):
    i = pl.program_id(0)
    kt, nlb = y_in_ref.shape[0], y_in_ref.shape[1]
    k2_per_tile = y_in_ref.shape[2] // SUBLANES
    pair = 2 * SUBLANES
    for k2l in range(k2_per_tile):
        src = slice(k2l * SUBLANES, (k2l + 1) * SUBLANES)
        for kp in range(kt // 2):
            dst = slice(k2l * n1 + kp * pair, k2l * n1 + (kp + 1) * pair)
            yv = jnp.concatenate(
                [jnp.concatenate([y_in_ref[2 * kp + a, lb, src, :] for lb in range(nlb)], axis=1)
                 for a in range(2)], axis=0)
            y_ref[dst, :] = (yv * z_ref[dst, :].astype(F32)).astype(BF16)
    mix = jnp.dot(y_ref[...], w_ref[...], preferred_element_type=F32)
    gate = mod_ref[pl.ds(i // tiles_per_batch, 1), 2 * d:3 * d]
    x = x_ref[...] + gate * mix
    var = jnp.mean(x * x, axis=-1, keepdims=True)
    o_ref[...] = (x * lax.rsqrt(var + EPS)) * fg_ref[...]


def _out1(y_in, zs, x2d, mods, w_bf, final_g, *, n_lat, n1, layer):
    rows, d = x2d.shape
    batch, kt, nlb, _, _ = y_in.shape
    e = nlb * LANES
    tm = ROW_TILE
    assert tm % n1 == 0
    tpb = n_lat // tm
    blk_rows = (tm // n1) * SUBLANES
    kern = functools.partial(_out1_kernel, d=d, n1=n1, tiles_per_batch=tpb)
    return pl.pallas_call(
        kern,
        grid=(rows // tm,),
        in_specs=[
            pl.BlockSpec((None, kt, nlb, blk_rows, LANES), lambda i: (i // tpb, 0, 0, i % tpb, 0)),
            pl.BlockSpec((tm, e), lambda i: (i, 0)),
            pl.BlockSpec((tm, d), lambda i: (i, 0)),
            pl.BlockSpec((None, MOD_ROWS, 3 * d), lambda i: (layer, 0, 0)),
            pl.BlockSpec((e, d), lambda i: (0, 0)),
            pl.BlockSpec((1, d), lambda i: (0, 0)),
        ],
        out_specs=pl.BlockSpec((tm, d), lambda i: (i, 0)),
        out_shape=jax.ShapeDtypeStruct((rows, d), F32),
        scratch_shapes=[pltpu.VMEM((tm, e), BF16)],
        compiler_params=_cparams(("arbitrary",)),
        name="readout1",
    )(y_in, zs, x2d, mods, w_bf, final_g)


def kernel(x, c, ctx, c_ctx, ada_w, ada_b, norm_g, hg_w_in, hg_lb_logits, hg_norm_g, hg_w_out,
           ft_w_in, ft_w_out, final_g):
    batch, n_lat, d = x.shape
    n_ctx = ctx.shape[1]
    depth = ada_w.shape[0]
    e = hg_w_out.shape[1]
    assert depth == 2 and batch == 2 and batch + 1 <= MOD_ROWS

    cv = jnp.concatenate([c, c_ctx[None, :], jnp.zeros((MOD_ROWS - batch - 1, d), F32)], axis=0)
    mods = _ada_table(cv, ada_w, ada_b)
    ng = norm_g.reshape(depth, 1, d)
    x2d = x.reshape(batch * n_lat, d)
    ctx2d = ctx.reshape(batch * n_ctx, d)

    ctx_pad = jnp.pad(ctx2d, ((0, PROJ_TILE - batch * n_ctx), (0, 0)))
    a, g = _proj0(ctx_pad, x2d, mods, ng, hg_w_in[0].astype(BF16), hg_lb_logits,
                  n_lat=n_lat, lb_index=0)
    o_f, o_b = _scan(a, g, batch=batch, n_lat=n_lat, n_ctx=n_ctx, e=e)
    x1, _ctx1 = _out0(o_f, o_b, a, ctx2d, x2d, mods, hg_norm_g[0:1], hg_w_out[0].astype(BF16),
                      n_lat=n_lat)

    u, zs = _proj1(x1, mods, ng, ft_w_in[0].astype(BF16), batch=batch, n_lat=n_lat, layer=1)
    stage1, stage2, chan = _dft_tables(n_lat, e // FT_GROUPS)
    n1 = stage1.shape[2]
    tr, ti = _dft1(u, stage1)
    y = _dft2(tr, ti, stage2, chan, n1=n1)
    out = _out1(y, zs, x1, mods, ft_w_out[0].astype(BF16), final_g.reshape(1, d),
                n_lat=n_lat, n1=n1, layer=1)
    return out.reshape(batch, n_lat, d)
```

```python
import functools

import numpy as np
import jax
import jax.numpy as jnp
from jax import lax
from jax.experimental import pallas as pl
from jax.experimental.pallas import tpu as pltpu

F32 = jnp.float32
BF16 = jnp.bfloat16

EPS = 1e-6
HEAD_DIM = 128
HG_STREAMS = 5
FT_GROUPS = 8
SCAN_CHUNK = 64
SCAN_HEADS = 4
ROW_TILE = 512
PROJ_TILE = 1024
MOD_ROWS = 8
SUBLANES = 8
LANES = 128
VMEM_LIMIT = 56 * 1024 * 1024


def _cparams(sem):
    return pltpu.CompilerParams(dimension_semantics=sem, vmem_limit_bytes=VMEM_LIMIT)


def _sigmoid(x):
    return 0.5 * jnp.tanh(0.5 * x) + 0.5


def _silu(x):
    h = 0.5 * x
    return h + h * jnp.tanh(h)


def _modulated_norm(x, g, shift, scale):
    var = jnp.mean(x * x, axis=-1, keepdims=True)
    return (x * lax.rsqrt(var + EPS)) * g * (1.0 + scale) + shift


def _ada_kernel(cv_ref, w_ref, b_ref, o_ref):
    a = _silu(cv_ref[...])
    o_ref[...] = jnp.dot(a, w_ref[...], preferred_element_type=F32,
                         precision=lax.Precision.HIGHEST) + b_ref[...]


def _ada_table(cv, ada_w, ada_b):
    depth, d, d3 = ada_w.shape
    tn = 1024
    return pl.pallas_call(
        _ada_kernel,
        grid=(depth, d3 // tn),
        in_specs=[
            pl.BlockSpec((MOD_ROWS, d), lambda l, j: (0, 0)),
            pl.BlockSpec((None, d, tn), lambda l, j: (l, 0, j)),
            pl.BlockSpec((None, 1, tn), lambda l, j: (l, 0, j)),
        ],
        out_specs=pl.BlockSpec((None, MOD_ROWS, tn), lambda l, j: (l, 0, j)),
        out_shape=jax.ShapeDtypeStruct((depth, MOD_ROWS, d3), F32),
        compiler_params=_cparams(("arbitrary", "arbitrary")),
        name="ada_table",
    )(cv, ada_w, ada_b.reshape(depth, 1, d3))


def _proj0_kernel(ctx_ref, x_ref, mod_ref, ng_ref, w_ref, lbl_ref, a_ref, g_ref, h_ref,
                  *, d, e, tiles_per_batch, lb_index, sub):
    i = pl.program_id(0)
    j = pl.program_id(1)
    tm = h_ref.shape[0]

    def fill(src_ref, row):
        m = mod_ref[pl.ds(row, 1), :]
        h_ref[...] = _modulated_norm(src_ref[...], ng_ref[...], m[:, 0:d], m[:, d:2 * d]).astype(BF16)

    @pl.when(j == 0)
    def _():
        @pl.when(i == 0)
        def _():
            fill(ctx_ref, 2)

        @pl.when(i > 0)
        def _():
            fill(x_ref, (i - 1) // tiles_per_batch)

    def acc(c):
        return jnp.dot(h_ref[...], w_ref[:, c * sub:(c + 1) * sub], preferred_element_type=F32)

    nsub = e // sub

    @pl.when(j == 0)
    def _():
        for c in range(nsub):
            a_ref[:, c * sub:(c + 1) * sub] = _silu(acc(c)).astype(BF16)

    @pl.when((j == 1) | (j == 2))
    def _():
        dirn = j - 1
        logits = lbl_ref[:, pl.ds(dirn, 1), :]
        mx = jnp.max(logits, axis=0, keepdims=True)
        ex = jnp.exp(logits - mx)
        p = ex / jnp.sum(ex, axis=0, keepdims=True)
        lb_full = jnp.sum(p[0:lb_index + 1], axis=0)
        half_full = 0.5 * (1.0 - lb_full)
        mid_full = lb_full + half_full
        for c in range(nsub):
            half = half_full[:, c * sub:(c + 1) * sub]
            t = jnp.tanh(0.5 * acc(c))
            a_ref[:, c * sub:(c + 1) * sub] = (half * (1.0 - t)).astype(BF16)
            g_ref[:, c * sub:(c + 1) * sub] = jnp.log2(
                mid_full[:, c * sub:(c + 1) * sub] + half * t).astype(BF16)

    @pl.when(j == 3)
    def _():
        for c in range(nsub):
            a_ref[:, c * sub:(c + 1) * sub] = acc(c).astype(BF16)

    @pl.when(j == 4)
    def _():
        for c in range(nsub):
            a_ref[:, c * sub:(c + 1) * sub] = _silu(acc(c)).astype(BF16)


def _proj0(ctx2d, x2d, mods, norm_g, w_bf, lb_logits, *, n_lat, lb_index):
    rows_ctx, d = ctx2d.shape
    rows_lat = x2d.shape[0]
    e = w_bf.shape[1] // HG_STREAMS
    tm = PROJ_TILE
    assert rows_ctx == tm and n_lat % tm == 0
    n_tiles = 1 + rows_lat // tm
    rows = tm + rows_lat
    nl = lb_logits.shape[0]
    kern = functools.partial(_proj0_kernel, d=d, e=e, tiles_per_batch=n_lat // tm,
                             lb_index=lb_index, sub=512)
    return pl.pallas_call(
        kern,
        grid=(n_tiles, HG_STREAMS),
        in_specs=[
            pl.BlockSpec((tm, d), lambda i, j: (0, 0)),
            pl.BlockSpec((tm, d), lambda i, j: (jnp.maximum(i - 1, 0), 0)),
            pl.BlockSpec((None, MOD_ROWS, 3 * d), lambda i, j: (0, 0, 0)),
            pl.BlockSpec((None, 1, d), lambda i, j: (0, 0, 0)),
            pl.BlockSpec((d, e), lambda i, j: (0, j)),
            pl.BlockSpec((nl, 2, e), lambda i, j: (0, 0, 0)),
        ],
        out_specs=[
            pl.BlockSpec((tm, e), lambda i, j: (i, j)),
            pl.BlockSpec((tm, e), lambda i, j: (i, jnp.clip(j - 1, 0, 1))),
        ],
        out_shape=[
            jax.ShapeDtypeStruct((rows, HG_STREAMS * e), BF16),
            jax.ShapeDtypeStruct((rows, 2 * e), BF16),
        ],
        scratch_shapes=[pltpu.VMEM((tm, d), BF16)],
        compiler_params=_cparams(("arbitrary", "arbitrary")),
        name="proj0",
    )(ctx2d, x2d, mods, norm_g, w_bf, lb_logits)


def _scan_kernel(qf_ref, kf_ref, gf_ref, vf_ref, qb_ref, kb_ref, gb_ref, vb_ref,
                 of_ref, ob_ref, st_ref, *, n_chunks):
    s = pl.program_id(2)
    c_len = SCAN_CHUNK
    hd = HEAD_DIM

    @pl.when(s == 0)
    def _():
        st_ref[...] = jnp.zeros_like(st_ref)

    row = lax.broadcasted_iota(jnp.int32, (c_len, c_len), 0)
    col = lax.broadcasted_iota(jnp.int32, (c_len, c_len), 1)
    causal = (col <= row, col >= row)
    tri = tuple(m.astype(F32).astype(BF16) for m in causal)
    end_row = (c_len - 1, 0)
    mid = c_len // 2
    refs = ((qf_ref, kf_ref, gf_ref, vf_ref, of_ref), (qb_ref, kb_ref, gb_ref, vb_ref, ob_ref))
    nt = (((1,), (1,)), ((), ()))
    tn = (((0,), (0,)), ((), ()))

    pre = {}
    for dirn in range(2):
        q_ref, k_ref, g_ref, v_ref, _ = refs[dirn]
        for c in range(n_chunks):
            rows = slice(c * c_len, (c + 1) * c_len)
            b = jnp.dot(tri[dirn], g_ref[rows, :], preferred_element_type=F32)
            tot = b[end_row[dirn]:end_row[dirn] + 1, :]
            ref = b[mid:mid + 1, :]
            q = q_ref[rows, :].astype(F32)
            k = k_ref[rows, :].astype(F32)
            pre[dirn, c] = dict(
                qt=(q * jnp.exp2(b - ref)).astype(BF16),
                kt=(k * jnp.exp2(ref - b)).astype(BF16),
                qh=(q * jnp.exp2(b)).astype(BF16),
                kh=(k * jnp.exp2(tot - b)).astype(BF16),
                dec=jnp.exp2(tot),
                v=v_ref[rows, :],
            )

    prob, upd = {}, {}
    for dirn in range(2):
        for c in range(n_chunks):
            w = pre[dirn, c]
            for hh in range(SCAN_HEADS):
                lanes = slice(hh * hd, (hh + 1) * hd)
                sc = lax.dot_general(w["qt"][:, lanes], w["kt"][:, lanes], nt,
                                     preferred_element_type=F32)
                prob[dirn, c, hh] = jnp.where(causal[dirn], sc, 0.0).astype(BF16)
                upd[dirn, c, hh] = lax.dot_general(w["v"][:, lanes], w["kh"][:, lanes], tn,
                                                   preferred_element_type=F32)

    for dirn in range(2):
        o_ref = refs[dirn][4]
        order = range(n_chunks) if dirn == 0 else range(n_chunks - 1, -1, -1)
        for hh in range(SCAN_HEADS):
            lanes = slice(hh * hd, (hh + 1) * hd)
            st = st_ref[dirn, hh]
            for c in order:
                w = pre[dirn, c]
                rows = slice(c * c_len, (c + 1) * c_len)
                o = (jnp.dot(prob[dirn, c, hh], w["v"][:, lanes], preferred_element_type=F32)
                     + lax.dot_general(w["qh"][:, lanes], st.astype(BF16), nt,
                                       preferred_element_type=F32))
                o_ref[rows, lanes] = o.astype(BF16)
                st = st * w["dec"][:, lanes] + upd[dirn, c, hh]
            st_ref[dirn, hh] = st


def _scan(a, g, *, batch, n_lat, n_ctx, e):
    ts = n_ctx
    assert ts % SCAN_CHUNK == 0 and n_lat % ts == 0
    lat_steps = n_lat // ts
    steps = 1 + lat_steps
    hw = SCAN_HEADS * HEAD_DIM
    groups = e // hw
    rows = a.shape[0]
    assert batch * n_ctx <= PROJ_TILE and PROJ_TILE % ts == 0
    lat0 = PROJ_TILE // ts

    def row_f(b, s):
        return jnp.where(s == 0, b, lat0 + lat_steps * b + s - 1)

    def row_b(b, s):
        return jnp.where(s == 0, b, lat0 + lat_steps * b + lat_steps - s)

    def spec(stream, rfn):
        return pl.BlockSpec((ts, hw), lambda b, hg, s: (rfn(b, s), stream * groups + hg))

    kern = functools.partial(_scan_kernel, n_chunks=ts // SCAN_CHUNK)
    return pl.pallas_call(
        kern,
        grid=(batch, groups, steps),
        in_specs=[spec(0, row_f), spec(1, row_f), spec(0, row_f), spec(3, row_f),
                  spec(0, row_b), spec(2, row_b), spec(1, row_b), spec(3, row_b)],
        out_specs=[pl.BlockSpec((ts, hw), lambda b, hg, s: (row_f(b, s), hg)),
                   pl.BlockSpec((ts, hw), lambda b, hg, s: (row_b(b, s), hg))],
        out_shape=[jax.ShapeDtypeStruct((rows, e), BF16), jax.ShapeDtypeStruct((rows, e), BF16)],
        scratch_shapes=[pltpu.VMEM((2, SCAN_HEADS, HEAD_DIM, HEAD_DIM), F32)],
        compiler_params=_cparams(("arbitrary", "arbitrary", "arbitrary")),
        name="hgrn_scan",
    )(a, a, g, a, a, a, g, a)


def _out0_kernel(of_ref, ob_ref, z_ref, ctx_ref, x_ref, mod_ref, hg_ref, w_ref,
                 xo_ref, co_ref, y_ref, *, d, e, tiles_per_batch):
    i = pl.program_id(0)
    heads = e // HEAD_DIM
    for h in range(heads):
        lanes = slice(h * HEAD_DIM, (h + 1) * HEAD_DIM)
        o = of_ref[:, lanes].astype(F32) + ob_ref[:, lanes].astype(F32)
        var = jnp.mean(o * o, axis=-1, keepdims=True)
        yn = (o * lax.rsqrt(var + EPS)) * hg_ref[...]
        y_ref[:, lanes] = (yn * z_ref[:, lanes].astype(F32)).astype(BF16)
    mix = jnp.dot(y_ref[...], w_ref[...], preferred_element_type=F32)

    @pl.when(i == 0)
    def _():
        gate = mod_ref[2:3, 2 * d:3 * d]
        co_ref[...] = ctx_ref[...] + gate * mix

    @pl.when(i > 0)
    def _():
        gate = mod_ref[pl.ds((i - 1) // tiles_per_batch, 1), 2 * d:3 * d]
        xo_ref[...] = x_ref[...] + gate * mix


def _out0(o_f, o_b, a, ctx2d, x2d, mods, hg_norm_g, w_bf, *, n_lat):
    rows_ctx, d = ctx2d.shape
    rows_lat = x2d.shape[0]
    e = o_f.shape[1]
    tm = ROW_TILE
    assert rows_ctx == tm and PROJ_TILE % tm == 0
    n_tiles = 1 + rows_lat // tm
    kern = functools.partial(_out0_kernel, d=d, e=e, tiles_per_batch=n_lat // tm)
    lat_idx = lambda i: (jnp.maximum(i - 1, 0), 0)
    skip = PROJ_TILE // tm - 1
    src_row = lambda i: jnp.where(i == 0, 0, i + skip)
    return pl.pallas_call(
        kern,
        grid=(n_tiles,),
        in_specs=[
            pl.BlockSpec((tm, e), lambda i: (src_row(i), 0)),
            pl.BlockSpec((tm, e), lambda i: (src_row(i), 0)),
            pl.BlockSpec((tm, e), lambda i: (src_row(i), HG_STREAMS - 1)),
            pl.BlockSpec((tm, d), lambda i: (0, 0)),
            pl.BlockSpec((tm, d), lat_idx),
            pl.BlockSpec((None, MOD_ROWS, 3 * d), lambda i: (0, 0, 0)),
            pl.BlockSpec((1, HEAD_DIM), lambda i: (0, 0)),
            pl.BlockSpec((e, d), lambda i: (0, 0)),
        ],
        out_specs=[pl.BlockSpec((tm, d), lat_idx), pl.BlockSpec((tm, d), lambda i: (0, 0))],
        out_shape=[jax.ShapeDtypeStruct((rows_lat, d), F32), jax.ShapeDtypeStruct((rows_ctx, d), F32)],
        scratch_shapes=[pltpu.VMEM((tm, e), BF16)],
        compiler_params=_cparams(("arbitrary",)),
        name="readout0",
    )(o_f, o_b, a, ctx2d, x2d, mods, hg_norm_g, w_bf)


def _proj1_kernel(x_ref, mod_ref, ng_ref, w_ref, u_ref, z_ref, h_ref,
                  *, d, e, tiles_per_batch, sub, n2):
    i = pl.program_id(0)
    j = pl.program_id(1)
    tm = h_ref.shape[0]

    @pl.when(j == 0)
    def _():
        m = mod_ref[pl.ds(i // tiles_per_batch, 1), :]
        h_ref[...] = _modulated_norm(x_ref[...], ng_ref[...], m[:, 0:d], m[:, d:2 * d]).astype(BF16)

    def acc(c):
        return jnp.dot(h_ref[...], w_ref[:, c * sub:(c + 1) * sub], preferred_element_type=F32)

    nsub = e // sub

    @pl.when(j == 0)
    def _():
        for c in range(nsub):
            a = acc(c)
            for n1l in range(tm // n2):
                for m in range(n2 // SUBLANES):
                    src = n1l * n2 + m * SUBLANES
                    for cl in range(sub // LANES):
                        u_ref[m, c * (sub // LANES) + cl, n1l * SUBLANES:(n1l + 1) * SUBLANES, :] = (
                            a[src:src + SUBLANES, cl * LANES:(cl + 1) * LANES])

    @pl.when(j == 1)
    def _():
        for c in range(nsub):
            z_ref[:, c * sub:(c + 1) * sub] = _silu(acc(c)).astype(BF16)


def _proj1(x2d, mods, norm_g, w_bf, *, batch, n_lat, layer):
    rows, d = x2d.shape
    e = w_bf.shape[1] // 2
    tm = PROJ_TILE
    n1, n2 = _dft_factors(n_lat)
    q = n2 // SUBLANES
    tpb = n_lat // tm
    rows_blk = (tm // n2) * SUBLANES
    kern = functools.partial(_proj1_kernel, d=d, e=e, tiles_per_batch=tpb, sub=512, n2=n2)
    return pl.pallas_call(
        kern,
        grid=(rows // tm, 2),
        in_specs=[
            pl.BlockSpec((tm, d), lambda i, j: (i, 0)),
            pl.BlockSpec((None, MOD_ROWS, 3 * d), lambda i, j: (layer, 0, 0)),
            pl.BlockSpec((None, 1, d), lambda i, j: (layer, 0, 0)),
            pl.BlockSpec((d, e), lambda i, j: (0, j)),
        ],
        out_specs=[pl.BlockSpec((None, q, e // LANES, rows_blk, LANES),
                                lambda i, j: (i // tpb, 0, 0, i % tpb, 0)),
                   pl.BlockSpec((tm, e), lambda i, j: (i, 0))],
        out_shape=[jax.ShapeDtypeStruct((batch, q, e // LANES, n1 * SUBLANES, LANES), F32),
                   jax.ShapeDtypeStruct((rows, e), BF16)],
        scratch_shapes=[pltpu.VMEM((tm, d), BF16)],
        compiler_params=_cparams(("arbitrary", "arbitrary")),
        name="proj1",
    )(x2d, mods, norm_g, w_bf)


def _dft_factors(n):
    n1 = 128 if n % 128 == 0 else n
    return n1, n // n1


def _dft_tables(n, group_dim):
    n1, n2 = _dft_factors(n)
    a2 = np.arange(n2, dtype=np.float64)[:, None, None]
    k1 = np.arange(n1, dtype=np.float64)[None, :, None]
    m1 = np.arange(n1, dtype=np.float64)[None, None, :]
    ang = -2.0 * np.pi * (a2 * k1 / n + m1 * k1 / n1)
    stage1 = np.concatenate([np.cos(ang), np.sin(ang)], axis=1) / np.sqrt(n1)
    kk = np.arange(n2, dtype=np.float64)
    ang2 = -2.0 * np.pi * np.outer(kk, kk) / n2
    fr, fi = np.cos(ang2) / np.sqrt(n2), np.sin(ang2) / np.sqrt(n2)
    stage2 = np.stack([np.concatenate([fr, fi], axis=0),
                       np.concatenate([-fi, fr], axis=0)])
    cc = np.arange(group_dim, dtype=np.float64)
    ang3 = 2.0 * np.pi * np.outer(cc, cc) / group_dim
    chan = np.concatenate([np.cos(ang3), np.sin(ang3)], axis=0) / np.sqrt(group_dim)
    to = lambda t: jnp.asarray(t, dtype=F32).astype(BF16)
    return to(stage1), to(stage2), to(chan)


def _dft1_kernel(m_ref, x_ref, tr_ref, ti_ref, *, n1):
    nlb = x_ref.shape[0]
    for r in range(SUBLANES):
        rows = pl.ds(r, n1, stride=SUBLANES)
        xr = jnp.concatenate([x_ref[lb, rows, :] for lb in range(nlb)], axis=1).astype(BF16)
        t = jnp.dot(m_ref[r], xr, preferred_element_type=F32)
        for lb in range(nlb):
            lanes = slice(lb * LANES, (lb + 1) * LANES)
            tr_ref[lb, rows, :] = t[0:n1, lanes]
            ti_ref[lb, rows, :] = t[n1:2 * n1, lanes]


def _dft1(u, stage1):
    batch, q, nlb, rows, _ = u.shape
    n2, two_n1, n1 = stage1.shape
    lbs = min(nlb, 8)
    m4 = stage1.reshape(q, SUBLANES, two_n1, n1)
    spec = pl.BlockSpec((None, None, lbs, rows, LANES), lambda b, m, c: (b, m, c, 0, 0))
    out = jax.ShapeDtypeStruct(u.shape, F32)
    return pl.pallas_call(
        functools.partial(_dft1_kernel, n1=n1),
        grid=(batch, q, nlb // lbs),
        in_specs=[pl.BlockSpec((None, SUBLANES, two_n1, n1), lambda b, m, c: (m, 0, 0, 0)), spec],
        out_specs=[spec, spec],
        out_shape=[out, out],
        compiler_params=_cparams(("arbitrary", "arbitrary", "arbitrary")),
        name="dft_stage1",
    )(m4, u)


def _dft2_kernel(f_ref, cs_ref, tr_ref, ti_ref, y_ref, *, n2, gd):
    q, nlb = tr_ref.shape[0], tr_ref.shape[1]
    per_group = gd // LANES

    def gather(ref, tile):
        return jnp.concatenate(
            [ref[:, lb, tile, :].reshape(n2, LANES) for lb in range(nlb)], axis=1).astype(BF16)

    for j in range(SUBLANES):
        tile = slice(j * SUBLANES, (j + 1) * SUBLANES)
        g = (jnp.dot(f_ref[0], gather(tr_ref, tile), preferred_element_type=F32)
             + jnp.dot(f_ref[1], gather(ti_ref, tile), preferred_element_type=F32))
        gr = g[0:n2].astype(BF16)
        gi = g[n2:2 * n2].astype(BF16)
        rows = pl.ds(j, n2, stride=SUBLANES)
        for gq in range(nlb // per_group):
            lanes = slice(gq * gd, (gq + 1) * gd)
            yg = (jnp.dot(gr[:, lanes], cs_ref[0:gd, :], preferred_element_type=F32)
                  + jnp.dot(gi[:, lanes], cs_ref[gd:2 * gd, :], preferred_element_type=F32))
            for cl in range(per_group):
                y_ref[gq * per_group + cl, rows, :] = yg[:, cl * LANES:(cl + 1) * LANES]


def _dft2(tr, ti, stage2, chan, *, n1):
    batch, q, nlb, rows, _ = tr.shape
    n2 = q * SUBLANES
    kt = n1 // SUBLANES
    gd = chan.shape[1]
    tile_rows = SUBLANES * SUBLANES
    shape_in = (batch, q, nlb, kt, tile_rows, LANES)
    in_spec = pl.BlockSpec((None, q, nlb, None, tile_rows, LANES), lambda b, k: (b, 0, 0, k, 0, 0))
    return pl.pallas_call(
        functools.partial(_dft2_kernel, n2=n2, gd=gd),
        grid=(batch, kt),
        in_specs=[pl.BlockSpec((2, 2 * n2, n2), lambda b, k: (0, 0, 0)),
                  pl.BlockSpec((2 * gd, gd), lambda b, k: (0, 0)),
                  in_spec, in_spec],
        out_specs=pl.BlockSpec((None, None, nlb, n2 * SUBLANES, LANES), lambda b, k: (b, k, 0, 0, 0)),
        out_shape=jax.ShapeDtypeStruct((batch, kt, nlb, n2 * SUBLANES, LANES), F32),
        compiler_params=_cparams(("arbitrary", "arbitrary")),
        name="dft_stage2",
    )(stage2, chan, tr.reshape(shape_in), ti.reshape(shape_in))


def _out1_kernel(y_in_ref, z_ref, x_ref, mod_ref, w_ref, fg_ref, o_ref, y_ref,
                 *, d, n1, tiles_per_batch):
    i = pl.program_id(0)
    kt, nlb = y_in_ref.shape[0], y_in_ref.shape[1]
    k2_per_tile = y_in_ref.shape[2] // SUBLANES
    pair = 2 * SUBLANES
    for k2l in range(k2_per_tile):
        src = slice(k2l * SUBLANES, (k2l + 1) * SUBLANES)
        for kp in range(kt // 2):
            dst = slice(k2l * n1 + kp * pair, k2l * n1 + (kp + 1) * pair)
            yv = jnp.concatenate(
                [jnp.concatenate([y_in_ref[2 * kp + a, lb, src, :] for lb in range(nlb)], axis=1)
                 for a in range(2)], axis=0)
            y_ref[dst, :] = (yv * z_ref[dst, :].astype(F32)).astype(BF16)
    mix = jnp.dot(y_ref[...], w_ref[...], preferred_element_type=F32)
    gate = mod_ref[pl.ds(i // tiles_per_batch, 1), 2 * d:3 * d]
    x = x_ref[...] + gate * mix
    var = jnp.mean(x * x, axis=-1, keepdims=True)
    o_ref[...] = (x * lax.rsqrt(var + EPS)) * fg_ref[...]


def _out1(y_in, zs, x2d, mods, w_bf, final_g, *, n_lat, n1, layer):
    rows, d = x2d.shape
    batch, kt, nlb, _, _ = y_in.shape
    e = nlb * LANES
    tm = ROW_TILE
    assert tm % n1 == 0
    tpb = n_lat // tm
    blk_rows = (tm // n1) * SUBLANES
    kern = functools.partial(_out1_kernel, d=d, n1=n1, tiles_per_batch=tpb)
    return pl.pallas_call(
        kern,
        grid=(rows // tm,),
        in_specs=[
            pl.BlockSpec((None, kt, nlb, blk_rows, LANES), lambda i: (i // tpb, 0, 0, i % tpb, 0)),
            pl.BlockSpec((tm, e), lambda i: (i, 0)),
            pl.BlockSpec((tm, d), lambda i: (i, 0)),
            pl.BlockSpec((None, MOD_ROWS, 3 * d), lambda i: (layer, 0, 0)),
            pl.BlockSpec((e, d), lambda i: (0, 0)),
            pl.BlockSpec((1, d), lambda i: (0, 0)),
        ],
        out_specs=pl.BlockSpec((tm, d), lambda i: (i, 0)),
        out_shape=jax.ShapeDtypeStruct((rows, d), F32),
        scratch_shapes=[pltpu.VMEM((tm, e), BF16)],
        compiler_params=_cparams(("arbitrary",)),
        name="readout1",
    )(y_in, zs, x2d, mods, w_bf, final_g)


def kernel(x, c, ctx, c_ctx, ada_w, ada_b, norm_g, hg_w_in, hg_lb_logits, hg_norm_g, hg_w_out,
           ft_w_in, ft_w_out, final_g):
    batch, n_lat, d = x.shape
    n_ctx = ctx.shape[1]
    depth = ada_w.shape[0]
    e = hg_w_out.shape[1]
    assert depth == 2 and batch == 2 and batch + 1 <= MOD_ROWS

    cv = jnp.concatenate([c, c_ctx[None, :], jnp.zeros((MOD_ROWS - batch - 1, d), F32)], axis=0)
    mods = _ada_table(cv, ada_w, ada_b)
    ng = norm_g.reshape(depth, 1, d)
    x2d = x.reshape(batch * n_lat, d)
    ctx2d = ctx.reshape(batch * n_ctx, d)

    ctx_pad = jnp.pad(ctx2d, ((0, PROJ_TILE - batch * n_ctx), (0, 0)))
    a, g = _proj0(ctx_pad, x2d, mods, ng, hg_w_in[0].astype(BF16), hg_lb_logits,
                  n_lat=n_lat, lb_index=0)
    o_f, o_b = _scan(a, g, batch=batch, n_lat=n_lat, n_ctx=n_ctx, e=e)
    x1, _ctx1 = _out0(o_f, o_b, a, ctx2d, x2d, mods, hg_norm_g[0:1], hg_w_out[0].astype(BF16),
                      n_lat=n_lat)

    u, zs = _proj1(x1, mods, ng, ft_w_in[0].astype(BF16), batch=batch, n_lat=n_lat, layer=1)
    stage1, stage2, chan = _dft_tables(n_lat, e // FT_GROUPS)
    n1 = stage1.shape[2]
    tr, ti = _dft1(u, stage1)
    y = _dft2(tr, ti, stage2, chan, n1=n1)
    out = _out1(y, zs, x1, mods, ft_w_out[0].astype(BF16), final_g.reshape(1, d),
                n_lat=n_lat, n1=n1, layer=1)
    return out.reshape(batch, n_lat, d)
```

```python
import functools

import numpy as np
import jax
import jax.numpy as jnp
from jax import lax
from jax.experimental import pallas as pl
from jax.experimental.pallas import tpu as pltpu

F32 = jnp.float32
BF16 = jnp.bfloat16

EPS = 1e-6
HEAD_DIM = 128
HG_STREAMS = 5
FT_GROUPS = 8
SCAN_CHUNK = 64
SCAN_HEADS = 8
ROW_TILE = 512
PROJ_TILE = 1024
MOD_ROWS = 8
SUBLANES = 8
LANES = 128
VMEM_LIMIT = 56 * 1024 * 1024


def _cparams(sem):
    return pltpu.CompilerParams(dimension_semantics=sem, vmem_limit_bytes=VMEM_LIMIT)


def _sigmoid(x):
    return 0.5 * jnp.tanh(0.5 * x) + 0.5


def _silu(x):
    return _silu_half(0.5 * x)


def _silu_half(h):
    return h + h * jnp.tanh(h)


def _modulated_norm(x, g, shift, scale):
    var = jnp.mean(x * x, axis=-1, keepdims=True)
    return (x * lax.rsqrt(var + EPS)) * g * (1.0 + scale) + shift


def _ada_kernel(cv_ref, w_ref, b_ref, o_ref):
    a = _silu(cv_ref[...])
    o_ref[...] = jnp.dot(a, w_ref[...], preferred_element_type=F32,
                         precision=lax.Precision.HIGHEST) + b_ref[...]


def _ada_table(cv, ada_w, ada_b):
    depth, d, d3 = ada_w.shape
    tn = 1024
    return pl.pallas_call(
        _ada_kernel,
        grid=(depth, d3 // tn),
        in_specs=[
            pl.BlockSpec((MOD_ROWS, d), lambda l, j: (0, 0)),
            pl.BlockSpec((None, d, tn), lambda l, j: (l, 0, j)),
            pl.BlockSpec((None, 1, tn), lambda l, j: (l, 0, j)),
        ],
        out_specs=pl.BlockSpec((None, MOD_ROWS, tn), lambda l, j: (l, 0, j)),
        out_shape=jax.ShapeDtypeStruct((depth, MOD_ROWS, d3), F32),
        compiler_params=_cparams(("arbitrary", "arbitrary")),
        name="ada_table",
    )(cv, ada_w, ada_b.reshape(depth, 1, d3))


def _proj0_kernel(ctx_ref, x_ref, mod_ref, ng_ref, w_ref, lbl_ref, a_ref, h_ref,
                  *, d, e, tiles_per_batch, lb_index, sub):
    i = pl.program_id(0)
    j = pl.program_id(1)
    tm = h_ref.shape[0]

    def fill(src_ref, row):
        m = mod_ref[pl.ds(row, 1), :]
        h_ref[...] = _modulated_norm(src_ref[...], ng_ref[...], m[:, 0:d], m[:, d:2 * d]).astype(BF16)

    @pl.when(j == 0)
    def _():
        @pl.when(i == 0)
        def _():
            fill(ctx_ref, 2)

        @pl.when(i > 0)
        def _():
            fill(x_ref, (i - 1) // tiles_per_batch)

    def acc(c):
        return jnp.dot(h_ref[...], w_ref[:, c * sub:(c + 1) * sub], preferred_element_type=F32)

    nsub = e // sub

    @pl.when(j == 0)
    def _():
        for c in range(nsub):
            a_ref[:, c * sub:(c + 1) * sub] = _silu_half(acc(c)).astype(BF16)

    @pl.when((j == 1) | (j == 2))
    def _():
        dirn = j - 1
        logits = lbl_ref[:, pl.ds(dirn, 1), :]
        mx = jnp.max(logits, axis=0, keepdims=True)
        ex = jnp.exp(logits - mx)
        p = ex / jnp.sum(ex, axis=0, keepdims=True)
        lb_full = jnp.sum(p[0:lb_index + 1], axis=0)
        half_full = 0.5 * (1.0 - lb_full)
        for c in range(nsub):
            half = half_full[:, c * sub:(c + 1) * sub]
            a_ref[:, c * sub:(c + 1) * sub] = (half * (1.0 - jnp.tanh(acc(c)))).astype(BF16)

    @pl.when(j == 3)
    def _():
        for c in range(nsub):
            a_ref[:, c * sub:(c + 1) * sub] = acc(c).astype(BF16)

    @pl.when(j == 4)
    def _():
        for c in range(nsub):
            a_ref[:, c * sub:(c + 1) * sub] = _silu_half(acc(c)).astype(BF16)


def _proj0(ctx2d, x2d, mods, norm_g, w_bf, lb_logits, *, n_lat, lb_index):
    rows_ctx, d = ctx2d.shape
    rows_lat = x2d.shape[0]
    e = w_bf.shape[1] // HG_STREAMS
    tm = PROJ_TILE
    assert rows_ctx == tm and n_lat % tm == 0
    n_tiles = 1 + rows_lat // tm
    rows = tm + rows_lat
    nl = lb_logits.shape[0]
    kern = functools.partial(_proj0_kernel, d=d, e=e, tiles_per_batch=n_lat // tm,
                             lb_index=lb_index, sub=512)
    return pl.pallas_call(
        kern,
        grid=(n_tiles, HG_STREAMS),
        in_specs=[
            pl.BlockSpec((tm, d), lambda i, j: (0, 0)),
            pl.BlockSpec((tm, d), lambda i, j: (jnp.maximum(i - 1, 0), 0)),
            pl.BlockSpec((None, MOD_ROWS, 3 * d), lambda i, j: (0, 0, 0)),
            pl.BlockSpec((None, 1, d), lambda i, j: (0, 0, 0)),
            pl.BlockSpec((d, e), lambda i, j: (0, j)),
            pl.BlockSpec((nl, 2, e), lambda i, j: (0, 0, 0)),
        ],
        out_specs=pl.BlockSpec((tm, e), lambda i, j: (i, j)),
        out_shape=jax.ShapeDtypeStruct((rows, HG_STREAMS * e), BF16),
        scratch_shapes=[pltpu.VMEM((tm, d), BF16)],
        compiler_params=_cparams(("arbitrary", "arbitrary")),
        name="proj0",
    )(ctx2d, x2d, mods, norm_g, w_bf, lb_logits)


def _scan_kernel(qf_ref, kf_ref, vf_ref, qb_ref, kb_ref, vb_ref,
                 of_ref, ob_ref, st_ref, *, n_chunks):
    s = pl.program_id(2)
    c_len = SCAN_CHUNK
    hd = HEAD_DIM

    @pl.when(s == 0)
    def _():
        st_ref[...] = jnp.zeros_like(st_ref)

    row = lax.broadcasted_iota(jnp.int32, (c_len, c_len), 0)
    col = lax.broadcasted_iota(jnp.int32, (c_len, c_len), 1)
    causal = (col <= row, col >= row)
    tri = tuple(m.astype(F32).astype(BF16) for m in causal)
    end_row = (c_len - 1, 0)
    mid = c_len // 2
    refs = ((qf_ref, kf_ref, vf_ref, of_ref), (qb_ref, kb_ref, vb_ref, ob_ref))
    nt = (((1,), (1,)), ((), ()))
    tn = (((0,), (0,)), ((), ()))

    pre = {}
    for dirn in range(2):
        q_ref, k_ref, v_ref, _ = refs[dirn]
        for c in range(n_chunks):
            rows = slice(c * c_len, (c + 1) * c_len)
            q = q_ref[rows, :].astype(F32)
            k = k_ref[rows, :].astype(F32)
            b = jnp.dot(tri[dirn], jnp.log(1.0 - k).astype(BF16), preferred_element_type=F32)
            tot = b[end_row[dirn]:end_row[dirn] + 1, :]
            ref = b[mid:mid + 1, :]
            pre[dirn, c] = dict(
                qt=(q * jnp.exp(b - ref)).astype(BF16),
                kt=(k * jnp.exp(ref - b)).astype(BF16),
                qh=(q * jnp.exp(b)).astype(BF16),
                kh=(k * jnp.exp(tot - b)).astype(BF16),
                dec=jnp.exp(tot),
                v=v_ref[rows, :],
            )

    prob, upd = {}, {}
    for dirn in range(2):
        for c in range(n_chunks):
            w = pre[dirn, c]
            for hh in range(SCAN_HEADS):
                lanes = slice(hh * hd, (hh + 1) * hd)
                sc = lax.dot_general(w["qt"][:, lanes], w["kt"][:, lanes], nt,
                                     preferred_element_type=F32)
                prob[dirn, c, hh] = jnp.where(causal[dirn], sc, 0.0).astype(BF16)
                upd[dirn, c, hh] = lax.dot_general(w["v"][:, lanes], w["kh"][:, lanes], tn,
                                                   preferred_element_type=F32)

    for dirn in range(2):
        o_ref = refs[dirn][3]
        order = range(n_chunks) if dirn == 0 else range(n_chunks - 1, -1, -1)
        for hh in range(SCAN_HEADS):
            lanes = slice(hh * hd, (hh + 1) * hd)
            st = st_ref[dirn, hh]
            for c in order:
                w = pre[dirn, c]
                rows = slice(c * c_len, (c + 1) * c_len)
                o = (jnp.dot(prob[dirn, c, hh], w["v"][:, lanes], preferred_element_type=F32)
                     + lax.dot_general(w["qh"][:, lanes], st.astype(BF16), nt,
                                       preferred_element_type=F32))
                o_ref[rows, lanes] = o.astype(BF16)
                st = st * w["dec"][:, lanes] + upd[dirn, c, hh]
            st_ref[dirn, hh] = st


def _scan(a, *, batch, n_lat, n_ctx, e):
    ts = n_ctx
    assert ts % SCAN_CHUNK == 0 and n_lat % ts == 0
    lat_steps = n_lat // ts
    steps = 1 + lat_steps
    hw = SCAN_HEADS * HEAD_DIM
    groups = e // hw
    rows = batch * (n_ctx + n_lat)
    assert batch * n_ctx <= PROJ_TILE and PROJ_TILE % ts == 0

    def row_f(lat0):
        return lambda b, s: jnp.where(s == 0, b, lat0 + lat_steps * b + s - 1)

    def row_b(lat0):
        return lambda b, s: jnp.where(s == 0, b, lat0 + lat_steps * b + lat_steps - s)

    in_f, in_b = row_f(PROJ_TILE // ts), row_b(PROJ_TILE // ts)
    out_f, out_b = row_f(batch), row_b(batch)

    def spec(stream, rfn):
        return pl.BlockSpec((ts, hw), lambda b, hg, s: (rfn(b, s), stream * groups + hg))

    kern = functools.partial(_scan_kernel, n_chunks=ts // SCAN_CHUNK)
    return pl.pallas_call(
        kern,
        grid=(batch, groups, steps),
        in_specs=[spec(0, in_f), spec(1, in_f), spec(3, in_f),
                  spec(0, in_b), spec(2, in_b), spec(3, in_b)],
        out_specs=[pl.BlockSpec((ts, hw), lambda b, hg, s: (out_f(b, s), hg)),
                   pl.BlockSpec((ts, hw), lambda b, hg, s: (out_b(b, s), hg))],
        out_shape=[jax.ShapeDtypeStruct((rows, e), BF16), jax.ShapeDtypeStruct((rows, e), BF16)],
        scratch_shapes=[pltpu.VMEM((2, SCAN_HEADS, HEAD_DIM, HEAD_DIM), F32)],
        compiler_params=_cparams(("arbitrary", "arbitrary", "arbitrary")),
        name="hgrn_scan",
    )(a, a, a, a, a, a)


def _out0_kernel(of_ref, ob_ref, z_ref, ctx_ref, x_ref, mod_ref, hg_ref, w_ref,
                 xo_ref, co_ref, y_ref, *, d, e, tiles_per_batch):
    i = pl.program_id(0)
    heads = e // HEAD_DIM
    for h in range(heads):
        lanes = slice(h * HEAD_DIM, (h + 1) * HEAD_DIM)
        o = of_ref[:, lanes].astype(F32) + ob_ref[:, lanes].astype(F32)
        var = jnp.mean(o * o, axis=-1, keepdims=True)
        yn = (o * lax.rsqrt(var + EPS)) * hg_ref[...]
        y_ref[:, lanes] = (yn * z_ref[:, lanes].astype(F32)).astype(BF16)
    mix = jnp.dot(y_ref[...], w_ref[...], preferred_element_type=F32)

    @pl.when(i == 0)
    def _():
        gate = mod_ref[2:3, 2 * d:3 * d]
        co_ref[...] = ctx_ref[...] + gate * mix

    @pl.when(i > 0)
    def _():
        gate = mod_ref[pl.ds((i - 1) // tiles_per_batch, 1), 2 * d:3 * d]
        xo_ref[...] = x_ref[...] + gate * mix


def _out0(o_f, o_b, a, ctx2d, x2d, mods, hg_norm_g, w_bf, *, n_lat):
    rows_ctx, d = ctx2d.shape
    rows_lat = x2d.shape[0]
    e = o_f.shape[1]
    tm = ROW_TILE
    assert rows_ctx == tm and PROJ_TILE % tm == 0
    n_tiles = 1 + rows_lat // tm
    kern = functools.partial(_out0_kernel, d=d, e=e, tiles_per_batch=n_lat // tm)
    lat_idx = lambda i: (jnp.maximum(i - 1, 0), 0)
    skip = PROJ_TILE // tm - 1
    src_row = lambda i: jnp.where(i == 0, 0, i + skip)
    return pl.pallas_call(
        kern,
        grid=(n_tiles,),
        in_specs=[
            pl.BlockSpec((tm, e), lambda i: (i, 0)),
            pl.BlockSpec((tm, e), lambda i: (i, 0)),
            pl.BlockSpec((tm, e), lambda i: (src_row(i), HG_STREAMS - 1)),
            pl.BlockSpec((tm, d), lambda i: (0, 0)),
            pl.BlockSpec((tm, d), lat_idx),
            pl.BlockSpec((None, MOD_ROWS, 3 * d), lambda i: (0, 0, 0)),
            pl.BlockSpec((1, HEAD_DIM), lambda i: (0, 0)),
            pl.BlockSpec((e, d), lambda i: (0, 0)),
        ],
        out_specs=[pl.BlockSpec((tm, d), lat_idx), pl.BlockSpec((tm, d), lambda i: (0, 0))],
        out_shape=[jax.ShapeDtypeStruct((rows_lat, d), F32), jax.ShapeDtypeStruct((rows_ctx, d), F32)],
        scratch_shapes=[pltpu.VMEM((tm, e), BF16)],
        compiler_params=_cparams(("arbitrary",)),
        name="readout0",
    )(o_f, o_b, a, ctx2d, x2d, mods, hg_norm_g, w_bf)


def _proj1_kernel(x_ref, mod_ref, ng_ref, w_ref, u_ref, z_ref, h_ref,
                  *, d, e, tiles_per_batch, sub, n2):
    i = pl.program_id(0)
    j = pl.program_id(1)
    tm = h_ref.shape[0]

    @pl.when(j == 0)
    def _():
        m = mod_ref[pl.ds(i // tiles_per_batch, 1), :]
        h_ref[...] = _modulated_norm(x_ref[...], ng_ref[...], m[:, 0:d], m[:, d:2 * d]).astype(BF16)

    def acc(c):
        return jnp.dot(h_ref[...], w_ref[:, c * sub:(c + 1) * sub], preferred_element_type=F32)

    nsub = e // sub

    @pl.when(j == 0)
    def _():
        for c in range(nsub):
            a = acc(c)
            for n1l in range(tm // n2):
                for m in range(n2 // SUBLANES):
                    src = n1l * n2 + m * SUBLANES
                    for cl in range(sub // LANES):
                        u_ref[m, c * (sub // LANES) + cl, n1l * SUBLANES:(n1l + 1) * SUBLANES, :] = (
                            a[src:src + SUBLANES, cl * LANES:(cl + 1) * LANES])

    @pl.when(j == 1)
    def _():
        for c in range(nsub):
            z_ref[:, c * sub:(c + 1) * sub] = _silu_half(acc(c)).astype(BF16)


def _proj1(x2d, mods, norm_g, w_bf, *, batch, n_lat, layer):
    rows, d = x2d.shape
    e = w_bf.shape[1] // 2
    tm = PROJ_TILE
    n1, n2 = _dft_factors(n_lat)
    q = n2 // SUBLANES
    tpb = n_lat // tm
    rows_blk = (tm // n2) * SUBLANES
    kern = functools.partial(_proj1_kernel, d=d, e=e, tiles_per_batch=tpb, sub=512, n2=n2)
    return pl.pallas_call(
        kern,
        grid=(rows // tm, 2),
        in_specs=[
            pl.BlockSpec((tm, d), lambda i, j: (i, 0)),
            pl.BlockSpec((None, MOD_ROWS, 3 * d), lambda i, j: (layer, 0, 0)),
            pl.BlockSpec((None, 1, d), lambda i, j: (layer, 0, 0)),
            pl.BlockSpec((d, e), lambda i, j: (0, j)),
        ],
        out_specs=[pl.BlockSpec((None, q, e // LANES, rows_blk, LANES),
                                lambda i, j: (i // tpb, 0, 0, i % tpb, 0)),
                   pl.BlockSpec((tm, e), lambda i, j: (i, 0))],
        out_shape=[jax.ShapeDtypeStruct((batch, q, e // LANES, n1 * SUBLANES, LANES), F32),
                   jax.ShapeDtypeStruct((rows, e), BF16)],
        scratch_shapes=[pltpu.VMEM((tm, d), BF16)],
        compiler_params=_cparams(("arbitrary", "arbitrary")),
        name="proj1",
    )(x2d, mods, norm_g, w_bf)


def _dft_factors(n):
    n1 = 128 if n % 128 == 0 else n
    return n1, n // n1


def _dft_tables(n, group_dim):
    n1, n2 = _dft_factors(n)
    a2 = np.arange(n2, dtype=np.float64)[:, None, None]
    k1 = np.arange(n1, dtype=np.float64)[None, :, None]
    m1 = np.arange(n1, dtype=np.float64)[None, None, :]
    ang = -2.0 * np.pi * (a2 * k1 / n + m1 * k1 / n1)
    stage1 = np.concatenate([np.cos(ang), np.sin(ang)], axis=1) / np.sqrt(n1)
    kk = np.arange(n2, dtype=np.float64)
    ang2 = -2.0 * np.pi * np.outer(kk, kk) / n2
    fr, fi = np.cos(ang2) / np.sqrt(n2), np.sin(ang2) / np.sqrt(n2)
    stage2 = np.stack([np.concatenate([fr, fi], axis=0),
                       np.concatenate([-fi, fr], axis=0)])
    cc = np.arange(group_dim, dtype=np.float64)
    ang3 = 2.0 * np.pi * np.outer(cc, cc) / group_dim
    chan = np.concatenate([np.cos(ang3), np.sin(ang3)], axis=0) / np.sqrt(group_dim)
    to = lambda t: jnp.asarray(t, dtype=F32).astype(BF16)
    return to(stage1), to(stage2), to(chan)


def _dft1_kernel(m_ref, x_ref, tr_ref, ti_ref, *, n1):
    nlb = x_ref.shape[0]
    for r in range(SUBLANES):
        rows = pl.ds(r, n1, stride=SUBLANES)
        xr = jnp.concatenate([x_ref[lb, rows, :] for lb in range(nlb)], axis=1).astype(BF16)
        t = jnp.dot(m_ref[r], xr, preferred_element_type=F32)
        for lb in range(nlb):
            lanes = slice(lb * LANES, (lb + 1) * LANES)
            tr_ref[lb, rows, :] = t[0:n1, lanes]
            ti_ref[lb, rows, :] = t[n1:2 * n1, lanes]


def _dft1(u, stage1):
    batch, q, nlb, rows, _ = u.shape
    n2, two_n1, n1 = stage1.shape
    lbs = min(nlb, 8)
    m4 = stage1.reshape(q, SUBLANES, two_n1, n1)
    spec = pl.BlockSpec((None, None, lbs, rows, LANES), lambda b, m, c: (b, m, c, 0, 0))
    out = jax.ShapeDtypeStruct(u.shape, F32)
    return pl.pallas_call(
        functools.partial(_dft1_kernel, n1=n1),
        grid=(batch, q, nlb // lbs),
        in_specs=[pl.BlockSpec((None, SUBLANES, two_n1, n1), lambda b, m, c: (m, 0, 0, 0)), spec],
        out_specs=[spec, spec],
        out_shape=[out, out],
        compiler_params=_cparams(("arbitrary", "arbitrary", "arbitrary")),
        name="dft_stage1",
    )(m4, u)


def _dft2_kernel(f_ref, cs_ref, tr_ref, ti_ref, y_ref, *, n2, gd):
    q, nlb = tr_ref.shape[0], tr_ref.shape[1]
    per_group = gd // LANES

    def gather(ref, tile):
        return jnp.concatenate(
            [ref[:, lb, tile, :].reshape(n2, LANES) for lb in range(nlb)], axis=1).astype(BF16)

    for j in range(SUBLANES):
        tile = slice(j * SUBLANES, (j + 1) * SUBLANES)
        g = (jnp.dot(f_ref[0], gather(tr_ref, tile), preferred_element_type=F32)
             + jnp.dot(f_ref[1], gather(ti_ref, tile), preferred_element_type=F32))
        gr = g[0:n2].astype(BF16)
        gi = g[n2:2 * n2].astype(BF16)
        rows = pl.ds(j, n2, stride=SUBLANES)
        for gq in range(nlb // per_group):
            lanes = slice(gq * gd, (gq + 1) * gd)
            yg = (jnp.dot(gr[:, lanes], cs_ref[0:gd, :], preferred_element_type=F32)
                  + jnp.dot(gi[:, lanes], cs_ref[gd:2 * gd, :], preferred_element_type=F32))
            for cl in range(per_group):
                y_ref[gq * per_group + cl, rows, :] = yg[:, cl * LANES:(cl + 1) * LANES]


def _dft2(tr, ti, stage2, chan, *, n1):
    batch, q, nlb, rows, _ = tr.shape
    n2 = q * SUBLANES
    kt = n1 // SUBLANES
    gd = chan.shape[1]
    tile_rows = SUBLANES * SUBLANES
    shape_in = (batch, q, nlb, kt, tile_rows, LANES)
    in_spec = pl.BlockSpec((None, q, nlb, None, tile_rows, LANES), lambda b, k: (b, 0, 0, k, 0, 0))
    return pl.pallas_call(
        functools.partial(_dft2_kernel, n2=n2, gd=gd),
        grid=(batch, kt),
        in_specs=[pl.BlockSpec((2, 2 * n2, n2), lambda b, k: (0, 0, 0)),
                  pl.BlockSpec((2 * gd, gd), lambda b, k: (0, 0)),
                  in_spec, in_spec],
        out_specs=pl.BlockSpec((None, None, nlb, n2 * SUBLANES, LANES), lambda b, k: (b, k, 0, 0, 0)),
        out_shape=jax.ShapeDtypeStruct((batch, kt, nlb, n2 * SUBLANES, LANES), F32),
        compiler_params=_cparams(("arbitrary", "arbitrary")),
        name="dft_stage2",
    )(stage2, chan, tr.reshape(shape_in), ti.reshape(shape_in))


def _out1_kernel(y_in_ref, z_ref, x_ref, mod_ref, w_ref, fg_ref, o_ref, y_ref,
                 *, d, n1, tiles_per_batch):
    i = pl.program_id(0)
    kt, nlb = y_in_ref.shape[0], y_in_ref.shape[1]
    k2_per_tile = y_in_ref.shape[2] // SUBLANES
    pair = 2 * SUBLANES
    for k2l in range(k2_per_tile):
        src = slice(k2l * SUBLANES, (k2l + 1) * SUBLANES)
        for kp in range(kt // 2):
            dst = slice(k2l * n1 + kp * pair, k2l * n1 + (kp + 1) * pair)
            yv = jnp.concatenate(
                [jnp.concatenate([y_in_ref[2 * kp + a, lb, src, :] for lb in range(nlb)], axis=1)
                 for a in range(2)], axis=0)
            y_ref[dst, :] = (yv * z_ref[dst, :].astype(F32)).astype(BF16)
    mix = jnp.dot(y_ref[...], w_ref[...], preferred_element_type=F32)
    gate = mod_ref[pl.ds(i // tiles_per_batch, 1), 2 * d:3 * d]
    x = x_ref[...] + gate * mix
    var = jnp.mean(x * x, axis=-1, keepdims=True)
    o_ref[...] = (x * lax.rsqrt(var + EPS)) * fg_ref[...]


def _out1(y_in, zs, x2d, mods, w_bf, final_g, *, n_lat, n1, layer):
    rows, d = x2d.shape
    batch, kt, nlb, _, _ = y_in.shape
    e = nlb * LANES
    tm = ROW_TILE
    assert tm % n1 == 0
    tpb = n_lat // tm
    blk_rows = (tm // n1) * SUBLANES
    kern = functools.partial(_out1_kernel, d=d, n1=n1, tiles_per_batch=tpb)
    return pl.pallas_call(
        kern,
        grid=(rows // tm,),
        in_specs=[
            pl.BlockSpec((None, kt, nlb, blk_rows, LANES), lambda i: (i // tpb, 0, 0, i % tpb, 0)),
            pl.BlockSpec((tm, e), lambda i: (i, 0)),
            pl.BlockSpec((tm, d), lambda i: (i, 0)),
            pl.BlockSpec((None, MOD_ROWS, 3 * d), lambda i: (layer, 0, 0)),
            pl.BlockSpec((e, d), lambda i: (0, 0)),
            pl.BlockSpec((1, d), lambda i: (0, 0)),
        ],
        out_specs=pl.BlockSpec((tm, d), lambda i: (i, 0)),
        out_shape=jax.ShapeDtypeStruct((rows, d), F32),
        scratch_shapes=[pltpu.VMEM((tm, e), BF16)],
        compiler_params=_cparams(("arbitrary",)),
        name="readout1",
    )(y_in, zs, x2d, mods, w_bf, final_g)


def kernel(x, c, ctx, c_ctx, ada_w, ada_b, norm_g, hg_w_in, hg_lb_logits, hg_norm_g, hg_w_out,
           ft_w_in, ft_w_out, final_g):
    batch, n_lat, d = x.shape
    n_ctx = ctx.shape[1]
    depth = ada_w.shape[0]
    e = hg_w_out.shape[1]
    assert depth == 2 and batch == 2 and batch + 1 <= MOD_ROWS

    cv = jnp.concatenate([c, c_ctx[None, :], jnp.zeros((MOD_ROWS - batch - 1, d), F32)], axis=0)
    mods = _ada_table(cv, ada_w, ada_b)
    ng = norm_g.reshape(depth, 1, d)
    x2d = x.reshape(batch * n_lat, d)
    ctx2d = ctx.reshape(batch * n_ctx, d)

    ctx_pad = jnp.pad(ctx2d, ((0, PROJ_TILE - batch * n_ctx), (0, 0)))
    half_cols = jnp.where(jnp.arange(HG_STREAMS * e) // e == 3, 1.0, 0.5).astype(F32)
    a = _proj0(ctx_pad, x2d, mods, ng, (hg_w_in[0] * half_cols).astype(BF16), hg_lb_logits,
                  n_lat=n_lat, lb_index=0)
    o_f, o_b = _scan(a, batch=batch, n_lat=n_lat, n_ctx=n_ctx, e=e)
    x1, _ctx1 = _out0(o_f, o_b, a, ctx2d, x2d, mods, hg_norm_g[0:1], hg_w_out[0].astype(BF16),
                      n_lat=n_lat)

    z_half = jnp.where(jnp.arange(2 * e) < e, 1.0, 0.5).astype(F32)
    u, zs = _proj1(x1, mods, ng, (ft_w_in[0] * z_half).astype(BF16), batch=batch, n_lat=n_lat,
                   layer=1)
    stage1, stage2, chan = _dft_tables(n_lat, e // FT_GROUPS)
    n1 = stage1.shape[2]
    tr, ti = _dft1(u, stage1)
    y = _dft2(tr, ti, stage2, chan, n1=n1)
    out = _out1(y, zs, x1, mods, ft_w_out[0].astype(BF16), final_g.reshape(1, d),
                n_lat=n_lat, n1=n1, layer=1)
    return out.reshape(batch, n_lat, d)
```

```python
import functools

import numpy as np
import jax
import jax.numpy as jnp
from jax import lax
from jax.experimental import pallas as pl
from jax.experimental.pallas import tpu as pltpu

F32 = jnp.float32
BF16 = jnp.bfloat16

EPS = 1e-6
HEAD_DIM = 128
HG_STREAMS = 5
FT_GROUPS = 8
SCAN_CHUNK = 64
SCAN_HEADS = 8
ROW_TILE = 512
PROJ_TILE = 1024
MOD_ROWS = 8
SUBLANES = 8
LANES = 128
VMEM_LIMIT = 56 * 1024 * 1024


def _cparams(sem):
    return pltpu.CompilerParams(dimension_semantics=sem, vmem_limit_bytes=VMEM_LIMIT)


def _sigmoid(x):
    return 0.5 * jnp.tanh(0.5 * x) + 0.5


def _silu(x):
    return _silu_half(0.5 * x)


def _silu_half(h):
    return h + h * jnp.tanh(h)


def _modulated_norm(x, g, shift, scale):
    var = jnp.mean(x * x, axis=-1, keepdims=True)
    return (x * lax.rsqrt(var + EPS)) * g * (1.0 + scale) + shift


def _ada_kernel(cv_ref, w_ref, b_ref, o_ref):
    a = _silu(cv_ref[...])
    o_ref[...] = jnp.dot(a, w_ref[...], preferred_element_type=F32,
                         precision=lax.Precision.HIGHEST) + b_ref[...]


def _ada_table(cv, ada_w, ada_b):
    depth, d, d3 = ada_w.shape
    tn = 1024
    return pl.pallas_call(
        _ada_kernel,
        grid=(depth, d3 // tn),
        in_specs=[
            pl.BlockSpec((MOD_ROWS, d), lambda l, j: (0, 0)),
            pl.BlockSpec((None, d, tn), lambda l, j: (l, 0, j)),
            pl.BlockSpec((None, 1, tn), lambda l, j: (l, 0, j)),
        ],
        out_specs=pl.BlockSpec((None, MOD_ROWS, tn), lambda l, j: (l, 0, j)),
        out_shape=jax.ShapeDtypeStruct((depth, MOD_ROWS, d3), F32),
        compiler_params=_cparams(("arbitrary", "arbitrary")),
        name="ada_table",
    )(cv, ada_w, ada_b.reshape(depth, 1, d3))


def _proj0_kernel(ctx_ref, x_ref, mod_ref, ng_ref, w_ref, lbl_ref, a_ref, h_ref,
                  *, d, e, tiles_per_batch, lb_index, sub):
    i = pl.program_id(0)
    j = pl.program_id(1)
    tm = h_ref.shape[0]

    def fill(src_ref, row):
        m = mod_ref[pl.ds(row, 1), :]
        h_ref[...] = _modulated_norm(src_ref[...], ng_ref[...], m[:, 0:d], m[:, d:2 * d]).astype(BF16)

    @pl.when(j == 0)
    def _():
        @pl.when(i == 0)
        def _():
            fill(ctx_ref, 2)

        @pl.when(i > 0)
        def _():
            fill(x_ref, (i - 1) // tiles_per_batch)

    def acc(c):
        return jnp.dot(h_ref[...], w_ref[:, c * sub:(c + 1) * sub], preferred_element_type=F32)

    nsub = e // sub

    @pl.when(j == 0)
    def _():
        for c in range(nsub):
            a_ref[:, c * sub:(c + 1) * sub] = _silu_half(acc(c)).astype(BF16)

    @pl.when((j == 1) | (j == 2))
    def _():
        dirn = j - 1
        logits = lbl_ref[:, pl.ds(dirn, 1), :]
        mx = jnp.max(logits, axis=0, keepdims=True)
        ex = jnp.exp(logits - mx)
        p = ex / jnp.sum(ex, axis=0, keepdims=True)
        lb_full = jnp.sum(p[0:lb_index + 1], axis=0)
        half_full = 0.5 * (1.0 - lb_full)
        for c in range(nsub):
            half = half_full[:, c * sub:(c + 1) * sub]
            a_ref[:, c * sub:(c + 1) * sub] = (half * (1.0 - jnp.tanh(acc(c)))).astype(BF16)

    @pl.when(j == 3)
    def _():
        for c in range(nsub):
            a_ref[:, c * sub:(c + 1) * sub] = acc(c).astype(BF16)

    @pl.when(j == 4)
    def _():
        for c in range(nsub):
            a_ref[:, c * sub:(c + 1) * sub] = _silu_half(acc(c)).astype(BF16)


def _proj0(ctx2d, x2d, mods, norm_g, w_bf, lb_logits, *, n_lat, lb_index):
    rows_ctx, d = ctx2d.shape
    rows_lat = x2d.shape[0]
    e = w_bf.shape[1] // HG_STREAMS
    tm = PROJ_TILE
    assert rows_ctx == tm and n_lat % tm == 0
    n_tiles = 1 + rows_lat // tm
    rows = tm + rows_lat
    nl = lb_logits.shape[0]
    kern = functools.partial(_proj0_kernel, d=d, e=e, tiles_per_batch=n_lat // tm,
                             lb_index=lb_index, sub=512)
    return pl.pallas_call(
        kern,
        grid=(n_tiles, HG_STREAMS),
        in_specs=[
            pl.BlockSpec((tm, d), lambda i, j: (0, 0)),
            pl.BlockSpec((tm, d), lambda i, j: (jnp.maximum(i - 1, 0), 0)),
            pl.BlockSpec((None, MOD_ROWS, 3 * d), lambda i, j: (0, 0, 0)),
            pl.BlockSpec((None, 1, d), lambda i, j: (0, 0, 0)),
            pl.BlockSpec((d, e), lambda i, j: (0, j)),
            pl.BlockSpec((nl, 2, e), lambda i, j: (0, 0, 0)),
        ],
        out_specs=pl.BlockSpec((tm, e), lambda i, j: (i, j)),
        out_shape=jax.ShapeDtypeStruct((rows, HG_STREAMS * e), BF16),
        scratch_shapes=[pltpu.VMEM((tm, d), BF16)],
        compiler_params=_cparams(("arbitrary", "arbitrary")),
        name="proj0",
    )(ctx2d, x2d, mods, norm_g, w_bf, lb_logits)


def _scan_kernel(qf_ref, kf_ref, vf_ref, qb_ref, kb_ref, vb_ref, of_ref, ob_ref,
                 st_ref, pa_ref, pb_ref, da_ref, db_ref, *, n_chunks):
    s = pl.program_id(2)
    c_len = SCAN_CHUNK
    hd = HEAD_DIM

    @pl.when(s == 0)
    def _():
        st_ref[...] = jnp.zeros_like(st_ref)
        pb_ref[...] = jnp.zeros_like(pb_ref)
        db_ref[...] = jnp.zeros_like(db_ref)

    row = lax.broadcasted_iota(jnp.int32, (c_len, c_len), 0)
    col = lax.broadcasted_iota(jnp.int32, (c_len, c_len), 1)
    causal = (col <= row, col >= row)
    tri = tuple(m.astype(F32).astype(BF16) for m in causal)
    end_row = (c_len - 1, 0)
    mid = c_len // 2
    qk_refs = ((qf_ref, kf_ref), (qb_ref, kb_ref))
    vo_refs = ((vf_ref, of_ref), (vb_ref, ob_ref))
    nt = (((1,), (1,)), ((), ()))
    tn = (((0,), (0,)), ((), ()))
    QT, KT, QH, KH = range(4)

    def chunk_of(dirn, r):
        return r if dirn == 0 else n_chunks - 1 - r

    def prepare(p_ref, d_ref, dirn, r, lanes):
        q_ref, k_ref = qk_refs[dirn]
        c = chunk_of(dirn, r)
        rows = slice(c * c_len, (c + 1) * c_len)
        q = q_ref[rows, lanes].astype(F32)
        k = k_ref[rows, lanes].astype(F32)
        b = jnp.dot(tri[dirn], jnp.log(1.0 - k).astype(BF16), preferred_element_type=F32)
        tot = b[end_row[dirn]:end_row[dirn] + 1, :]
        ref = b[mid:mid + 1, :]
        dl = b - ref
        qt = q * jnp.exp(dl)
        kt = k * jnp.exp(-dl)
        p_ref[dirn, QT, rows, lanes] = qt.astype(BF16)
        p_ref[dirn, KT, rows, lanes] = kt.astype(BF16)
        p_ref[dirn, QH, rows, lanes] = (qt * jnp.exp(ref)).astype(BF16)
        p_ref[dirn, KH, rows, lanes] = (kt * jnp.exp(tot - ref)).astype(BF16)
        d_ref[dirn, c, :, lanes] = jnp.exp(tot)

    def scores(p_ref, dirn, r, lanes):
        v_ref = vo_refs[dirn][0]
        c = chunk_of(dirn, r)
        rows = slice(c * c_len, (c + 1) * c_len)
        sc = lax.dot_general(p_ref[dirn, QT, rows, lanes], p_ref[dirn, KT, rows, lanes],
                             nt, preferred_element_type=F32)
        prob = jnp.where(causal[dirn], sc, 0.0).astype(BF16)
        upd = lax.dot_general(v_ref[rows, lanes], p_ref[dirn, KH, rows, lanes],
                              tn, preferred_element_type=F32)
        return prob, upd

    def outputs(p_ref, d_ref, dirn, r, hh, lanes, prob, upd):
        v_ref, o_ref = vo_refs[dirn]
        c = chunk_of(dirn, r)
        rows = slice(c * c_len, (c + 1) * c_len)
        st = st_ref[dirn, hh]
        o = (jnp.dot(prob, v_ref[rows, lanes], preferred_element_type=F32)
             + lax.dot_general(p_ref[dirn, QH, rows, lanes], st.astype(BF16), nt,
                               preferred_element_type=F32))
        o_ref[rows, lanes] = o.astype(BF16)
        st_ref[dirn, hh] = st * d_ref[dirn, c, :, lanes] + upd

    def step(p_new, d_new, p_old, d_old):
        heads = [(hh, slice(hh * hd, (hh + 1) * hd)) for hh in range(SCAN_HEADS)]
        pu = {(dirn, hh): scores(p_old, dirn, 0, lanes) for hh, lanes in heads for dirn in range(2)}
        for r in range(n_chunks):
            pu_next = {}
            for hh, lanes in heads:
                for dirn in range(2):
                    if r + 1 < n_chunks:
                        pu_next[dirn, hh] = scores(p_old, dirn, r + 1, lanes)
                    outputs(p_old, d_old, dirn, r, hh, lanes, *pu[dirn, hh])
                    prepare(p_new, d_new, dirn, r, lanes)
            pu = pu_next

    @pl.when(s % 2 == 0)
    def _():
        step(pa_ref, da_ref, pb_ref, db_ref)

    @pl.when(s % 2 == 1)
    def _():
        step(pb_ref, db_ref, pa_ref, da_ref)


def _scan(a, *, batch, n_lat, n_ctx, e):
    ts = n_ctx
    assert ts % SCAN_CHUNK == 0 and n_lat % ts == 0
    lat_steps = n_lat // ts
    steps = 1 + lat_steps
    n_chunks = ts // SCAN_CHUNK
    hw = SCAN_HEADS * HEAD_DIM
    groups = e // hw
    rows = batch * (n_ctx + n_lat)
    assert batch * n_ctx <= PROJ_TILE and PROJ_TILE % ts == 0

    def row_f(lat0):
        return lambda b, s: jnp.where(s == 0, b, lat0 + lat_steps * b + s - 1)

    def row_b(lat0):
        return lambda b, s: jnp.where(s == 0, b, lat0 + lat_steps * b + lat_steps - s)

    in_f, in_b = row_f(PROJ_TILE // ts), row_b(PROJ_TILE // ts)
    out_f, out_b = row_f(batch), row_b(batch)

    ahead = lambda g: jnp.minimum(g, steps - 1)
    behind = lambda g: jnp.maximum(g - 1, 0)

    def spec(stream, rfn, when):
        return pl.BlockSpec((ts, hw), lambda b, hg, g: (rfn(b, when(g)), stream * groups + hg))

    kern = functools.partial(_scan_kernel, n_chunks=n_chunks)
    operands = pltpu.VMEM((2, 4, ts, hw), BF16)
    decays = pltpu.VMEM((2, n_chunks, 1, hw), F32)
    return pl.pallas_call(
        kern,
        grid=(batch, groups, steps + 1),
        in_specs=[spec(0, in_f, ahead), spec(1, in_f, ahead), spec(3, in_f, behind),
                  spec(0, in_b, ahead), spec(2, in_b, ahead), spec(3, in_b, behind)],
        out_specs=[pl.BlockSpec((ts, hw), lambda b, hg, g: (out_f(b, behind(g)), hg)),
                   pl.BlockSpec((ts, hw), lambda b, hg, g: (out_b(b, behind(g)), hg))],
        out_shape=[jax.ShapeDtypeStruct((rows, e), BF16), jax.ShapeDtypeStruct((rows, e), BF16)],
        scratch_shapes=[pltpu.VMEM((2, SCAN_HEADS, HEAD_DIM, HEAD_DIM), F32),
                        operands, operands, decays, decays],
        compiler_params=_cparams(("arbitrary", "arbitrary", "arbitrary")),
        name="hgrn_scan",
    )(a, a, a, a, a, a)


def _out0_kernel(of_ref, ob_ref, z_ref, ctx_ref, x_ref, mod_ref, hg_ref, w_ref,
                 xo_ref, co_ref, y_ref, *, d, e, tiles_per_batch):
    i = pl.program_id(0)
    heads = e // HEAD_DIM
    for h in range(heads):
        lanes = slice(h * HEAD_DIM, (h + 1) * HEAD_DIM)
        o = of_ref[:, lanes].astype(F32) + ob_ref[:, lanes].astype(F32)
        var = jnp.mean(o * o, axis=-1, keepdims=True)
        yn = (o * lax.rsqrt(var + EPS)) * hg_ref[...]
        y_ref[:, lanes] = (yn * z_ref[:, lanes].astype(F32)).astype(BF16)
    mix = jnp.dot(y_ref[...], w_ref[...], preferred_element_type=F32)

    @pl.when(i == 0)
    def _():
        gate = mod_ref[2:3, 2 * d:3 * d]
        co_ref[...] = ctx_ref[...] + gate * mix

    @pl.when(i > 0)
    def _():
        gate = mod_ref[pl.ds((i - 1) // tiles_per_batch, 1), 2 * d:3 * d]
        xo_ref[...] = x_ref[...] + gate * mix


def _out0(o_f, o_b, a, ctx2d, x2d, mods, hg_norm_g, w_bf, *, n_lat):
    rows_ctx, d = ctx2d.shape
    rows_lat = x2d.shape[0]
    e = o_f.shape[1]
    tm = ROW_TILE
    assert rows_ctx == tm and PROJ_TILE % tm == 0
    n_tiles = 1 + rows_lat // tm
    kern = functools.partial(_out0_kernel, d=d, e=e, tiles_per_batch=n_lat // tm)
    lat_idx = lambda i: (jnp.maximum(i - 1, 0), 0)
    skip = PROJ_TILE // tm - 1
    src_row = lambda i: jnp.where(i == 0, 0, i + skip)
    return pl.pallas_call(
        kern,
        grid=(n_tiles,),
        in_specs=[
            pl.BlockSpec((tm, e), lambda i: (i, 0)),
            pl.BlockSpec((tm, e), lambda i: (i, 0)),
            pl.BlockSpec((tm, e), lambda i: (src_row(i), HG_STREAMS - 1)),
            pl.BlockSpec((tm, d), lambda i: (0, 0)),
            pl.BlockSpec((tm, d), lat_idx),
            pl.BlockSpec((None, MOD_ROWS, 3 * d), lambda i: (0, 0, 0)),
            pl.BlockSpec((1, HEAD_DIM), lambda i: (0, 0)),
            pl.BlockSpec((e, d), lambda i: (0, 0)),
        ],
        out_specs=[pl.BlockSpec((tm, d), lat_idx), pl.BlockSpec((tm, d), lambda i: (0, 0))],
        out_shape=[jax.ShapeDtypeStruct((rows_lat, d), F32), jax.ShapeDtypeStruct((rows_ctx, d), F32)],
        scratch_shapes=[pltpu.VMEM((tm, e), BF16)],
        compiler_params=_cparams(("arbitrary",)),
        name="readout0",
    )(o_f, o_b, a, ctx2d, x2d, mods, hg_norm_g, w_bf)


def _proj1_kernel(x_ref, mod_ref, ng_ref, w_ref, u_ref, z_ref, h_ref,
                  *, d, e, tiles_per_batch, sub, n2):
    i = pl.program_id(0)
    j = pl.program_id(1)
    tm = h_ref.shape[0]

    @pl.when(j == 0)
    def _():
        m = mod_ref[pl.ds(i // tiles_per_batch, 1), :]
        h_ref[...] = _modulated_norm(x_ref[...], ng_ref[...], m[:, 0:d], m[:, d:2 * d]).astype(BF16)

    def acc(c):
        return jnp.dot(h_ref[...], w_ref[:, c * sub:(c + 1) * sub], preferred_element_type=F32)

    nsub = e // sub

    @pl.when(j == 0)
    def _():
        for c in range(nsub):
            a = acc(c)
            for n1l in range(tm // n2):
                for m in range(n2 // SUBLANES):
                    src = n1l * n2 + m * SUBLANES
                    for cl in range(sub // LANES):
                        u_ref[m, c * (sub // LANES) + cl, n1l * SUBLANES:(n1l + 1) * SUBLANES, :] = (
                            a[src:src + SUBLANES, cl * LANES:(cl + 1) * LANES])

    @pl.when(j == 1)
    def _():
        for c in range(nsub):
            z_ref[:, c * sub:(c + 1) * sub] = _silu_half(acc(c)).astype(BF16)


def _proj1(x2d, mods, norm_g, w_bf, *, batch, n_lat, layer):
    rows, d = x2d.shape
    e = w_bf.shape[1] // 2
    tm = PROJ_TILE
    n1, n2 = _dft_factors(n_lat)
    q = n2 // SUBLANES
    tpb = n_lat // tm
    rows_blk = (tm // n2) * SUBLANES
    kern = functools.partial(_proj1_kernel, d=d, e=e, tiles_per_batch=tpb, sub=512, n2=n2)
    return pl.pallas_call(
        kern,
        grid=(rows // tm, 2),
        in_specs=[
            pl.BlockSpec((tm, d), lambda i, j: (i, 0)),
            pl.BlockSpec((None, MOD_ROWS, 3 * d), lambda i, j: (layer, 0, 0)),
            pl.BlockSpec((None, 1, d), lambda i, j: (layer, 0, 0)),
            pl.BlockSpec((d, e), lambda i, j: (0, j)),
        ],
        out_specs=[pl.BlockSpec((None, q, e // LANES, rows_blk, LANES),
                                lambda i, j: (i // tpb, 0, 0, i % tpb, 0)),
                   pl.BlockSpec((tm, e), lambda i, j: (i, 0))],
        out_shape=[jax.ShapeDtypeStruct((batch, q, e // LANES, n1 * SUBLANES, LANES), F32),
                   jax.ShapeDtypeStruct((rows, e), BF16)],
        scratch_shapes=[pltpu.VMEM((tm, d), BF16)],
        compiler_params=_cparams(("arbitrary", "arbitrary")),
        name="proj1",
    )(x2d, mods, norm_g, w_bf)


def _dft_factors(n):
    n1 = 128 if n % 128 == 0 else n
    return n1, n // n1


def _dft_tables(n, group_dim):
    n1, n2 = _dft_factors(n)
    a2 = np.arange(n2, dtype=np.float64)[:, None, None]
    k1 = np.arange(n1, dtype=np.float64)[None, :, None]
    m1 = np.arange(n1, dtype=np.float64)[None, None, :]
    ang = -2.0 * np.pi * (a2 * k1 / n + m1 * k1 / n1)
    stage1 = np.concatenate([np.cos(ang), np.sin(ang)], axis=1) / np.sqrt(n1)
    kk = np.arange(n2, dtype=np.float64)
    ang2 = -2.0 * np.pi * np.outer(kk, kk) / n2
    fr, fi = np.cos(ang2) / np.sqrt(n2), np.sin(ang2) / np.sqrt(n2)
    stage2 = np.stack([np.concatenate([fr, fi], axis=0),
                       np.concatenate([-fi, fr], axis=0)])
    cc = np.arange(group_dim, dtype=np.float64)
    ang3 = 2.0 * np.pi * np.outer(cc, cc) / group_dim
    chan = np.concatenate([np.cos(ang3), np.sin(ang3)], axis=0) / np.sqrt(group_dim)
    to = lambda t: jnp.asarray(t, dtype=F32).astype(BF16)
    return to(stage1), to(stage2), to(chan)


def _dft1_kernel(m_ref, x_ref, tr_ref, ti_ref, *, n1):
    nlb = x_ref.shape[0]
    for r in range(SUBLANES):
        rows = pl.ds(r, n1, stride=SUBLANES)
        xr = jnp.concatenate([x_ref[lb, rows, :] for lb in range(nlb)], axis=1).astype(BF16)
        t = jnp.dot(m_ref[r], xr, preferred_element_type=F32)
        for lb in range(nlb):
            lanes = slice(lb * LANES, (lb + 1) * LANES)
            tr_ref[lb, rows, :] = t[0:n1, lanes]
            ti_ref[lb, rows, :] = t[n1:2 * n1, lanes]


def _dft1(u, stage1):
    batch, q, nlb, rows, _ = u.shape
    n2, two_n1, n1 = stage1.shape
    lbs = min(nlb, 8)
    m4 = stage1.reshape(q, SUBLANES, two_n1, n1)
    spec = pl.BlockSpec((None, None, lbs, rows, LANES), lambda b, m, c: (b, m, c, 0, 0))
    out = jax.ShapeDtypeStruct(u.shape, F32)
    return pl.pallas_call(
        functools.partial(_dft1_kernel, n1=n1),
        grid=(batch, q, nlb // lbs),
        in_specs=[pl.BlockSpec((None, SUBLANES, two_n1, n1), lambda b, m, c: (m, 0, 0, 0)), spec],
        out_specs=[spec, spec],
        out_shape=[out, out],
        compiler_params=_cparams(("arbitrary", "arbitrary", "arbitrary")),
        name="dft_stage1",
    )(m4, u)


def _dft2_kernel(f_ref, cs_ref, tr_ref, ti_ref, y_ref, *, n2, gd):
    q, nlb = tr_ref.shape[0], tr_ref.shape[1]
    per_group = gd // LANES

    def gather(ref, tile):
        return jnp.concatenate(
            [ref[:, lb, tile, :].reshape(n2, LANES) for lb in range(nlb)], axis=1).astype(BF16)

    for j in range(SUBLANES):
        tile = slice(j * SUBLANES, (j + 1) * SUBLANES)
        g = (jnp.dot(f_ref[0], gather(tr_ref, tile), preferred_element_type=F32)
             + jnp.dot(f_ref[1], gather(ti_ref, tile), preferred_element_type=F32))
        gr = g[0:n2].astype(BF16)
        gi = g[n2:2 * n2].astype(BF16)
        rows = pl.ds(j, n2, stride=SUBLANES)
        for gq in range(nlb // per_group):
            lanes = slice(gq * gd, (gq + 1) * gd)
            yg = (jnp.dot(gr[:, lanes], cs_ref[0:gd, :], preferred_element_type=F32)
                  + jnp.dot(gi[:, lanes], cs_ref[gd:2 * gd, :], preferred_element_type=F32))
            for cl in range(per_group):
                y_ref[gq * per_group + cl, rows, :] = yg[:, cl * LANES:(cl + 1) * LANES]


def _dft2(tr, ti, stage2, chan, *, n1):
    batch, q, nlb, rows, _ = tr.shape
    n2 = q * SUBLANES
    kt = n1 // SUBLANES
    gd = chan.shape[1]
    tile_rows = SUBLANES * SUBLANES
    shape_in = (batch, q, nlb, kt, tile_rows, LANES)
    in_spec = pl.BlockSpec((None, q, nlb, None, tile_rows, LANES), lambda b, k: (b, 0, 0, k, 0, 0))
    return pl.pallas_call(
        functools.partial(_dft2_kernel, n2=n2, gd=gd),
        grid=(batch, kt),
        in_specs=[pl.BlockSpec((2, 2 * n2, n2), lambda b, k: (0, 0, 0)),
                  pl.BlockSpec((2 * gd, gd), lambda b, k: (0, 0)),
                  in_spec, in_spec],
        out_specs=pl.BlockSpec((None, None, nlb, n2 * SUBLANES, LANES), lambda b, k: (b, k, 0, 0, 0)),
        out_shape=jax.ShapeDtypeStruct((batch, kt, nlb, n2 * SUBLANES, LANES), F32),
        compiler_params=_cparams(("arbitrary", "arbitrary")),
        name="dft_stage2",
    )(stage2, chan, tr.reshape(shape_in), ti.reshape(shape_in))


def _out1_kernel(y_in_ref, z_ref, x_ref, mod_ref, w_ref, fg_ref, o_ref, y_ref,
                 *, d, n1, tiles_per_batch):
    i = pl.program_id(0)
    kt, nlb = y_in_ref.shape[0], y_in_ref.shape[1]
    k2_per_tile = y_in_ref.shape[2] // SUBLANES
    pair = 2 * SUBLANES
    for k2l in range(k2_per_tile):
        src = slice(k2l * SUBLANES, (k2l + 1) * SUBLANES)
        for kp in range(kt // 2):
            dst = slice(k2l * n1 + kp * pair, k2l * n1 + (kp + 1) * pair)
            yv = jnp.concatenate(
                [jnp.concatenate([y_in_ref[2 * kp + a, lb, src, :] for lb in range(nlb)], axis=1)
                 for a in range(2)], axis=0)
            y_ref[dst, :] = (yv * z_ref[dst, :].astype(F32)).astype(BF16)
    mix = jnp.dot(y_ref[...], w_ref[...], preferred_element_type=F32)
    gate = mod_ref[pl.ds(i // tiles_per_batch, 1), 2 * d:3 * d]
    x = x_ref[...] + gate * mix
    var = jnp.mean(x * x, axis=-1, keepdims=True)
    o_ref[...] = (x * lax.rsqrt(var + EPS)) * fg_ref[...]


def _out1(y_in, zs, x2d, mods, w_bf, final_g, *, n_lat, n1, layer):
    rows, d = x2d.shape
    batch, kt, nlb, _, _ = y_in.shape
    e = nlb * LANES
    tm = ROW_TILE
    assert tm % n1 == 0
    tpb = n_lat // tm
    blk_rows = (tm // n1) * SUBLANES
    kern = functools.partial(_out1_kernel, d=d, n1=n1, tiles_per_batch=tpb)
    return pl.pallas_call(
        kern,
        grid=(rows // tm,),
        in_specs=[
            pl.BlockSpec((None, kt, nlb, blk_rows, LANES), lambda i: (i // tpb, 0, 0, i % tpb, 0)),
            pl.BlockSpec((tm, e), lambda i: (i, 0)),
            pl.BlockSpec((tm, d), lambda i: (i, 0)),
            pl.BlockSpec((None, MOD_ROWS, 3 * d), lambda i: (layer, 0, 0)),
            pl.BlockSpec((e, d), lambda i: (0, 0)),
            pl.BlockSpec((1, d), lambda i: (0, 0)),
        ],
        out_specs=pl.BlockSpec((tm, d), lambda i: (i, 0)),
        out_shape=jax.ShapeDtypeStruct((rows, d), F32),
        scratch_shapes=[pltpu.VMEM((tm, e), BF16)],
        compiler_params=_cparams(("arbitrary",)),
        name="readout1",
    )(y_in, zs, x2d, mods, w_bf, final_g)


def kernel(x, c, ctx, c_ctx, ada_w, ada_b, norm_g, hg_w_in, hg_lb_logits, hg_norm_g, hg_w_out,
           ft_w_in, ft_w_out, final_g):
    batch, n_lat, d = x.shape
    n_ctx = ctx.shape[1]
    depth = ada_w.shape[0]
    e = hg_w_out.shape[1]
    assert depth == 2 and batch == 2 and batch + 1 <= MOD_ROWS

    cv = jnp.concatenate([c, c_ctx[None, :], jnp.zeros((MOD_ROWS - batch - 1, d), F32)], axis=0)
    mods = _ada_table(cv, ada_w, ada_b)
    ng = norm_g.reshape(depth, 1, d)
    x2d = x.reshape(batch * n_lat, d)
    ctx2d = ctx.reshape(batch * n_ctx, d)

    ctx_pad = jnp.pad(ctx2d, ((0, PROJ_TILE - batch * n_ctx), (0, 0)))
    half_cols = jnp.where(jnp.arange(HG_STREAMS * e) // e == 3, 1.0, 0.5).astype(F32)
    a = _proj0(ctx_pad, x2d, mods, ng, (hg_w_in[0] * half_cols).astype(BF16), hg_lb_logits,
                  n_lat=n_lat, lb_index=0)
    o_f, o_b = _scan(a, batch=batch, n_lat=n_lat, n_ctx=n_ctx, e=e)
    x1, _ctx1 = _out0(o_f, o_b, a, ctx2d, x2d, mods, hg_norm_g[0:1], hg_w_out[0].astype(BF16),
                      n_lat=n_lat)

    z_half = jnp.where(jnp.arange(2 * e) < e, 1.0, 0.5).astype(F32)
    u, zs = _proj1(x1, mods, ng, (ft_w_in[0] * z_half).astype(BF16), batch=batch, n_lat=n_lat,
                   layer=1)
    stage1, stage2, chan = _dft_tables(n_lat, e // FT_GROUPS)
    n1 = stage1.shape[2]
    tr, ti = _dft1(u, stage1)
    y = _dft2(tr, ti, stage2, chan, n1=n1)
    out = _out1(y, zs, x1, mods, ft_w_out[0].astype(BF16), final_g.reshape(1, d),
                n_lat=n_lat, n1=n1, layer=1)
    return out.reshape(batch, n_lat, d)
```

```python
import functools

import numpy as np
import jax
import jax.numpy as jnp
from jax import lax
from jax.experimental import pallas as pl
from jax.experimental.pallas import tpu as pltpu

F32 = jnp.float32
BF16 = jnp.bfloat16

EPS = 1e-6
HEAD_DIM = 128
HG_STREAMS = 5
FT_GROUPS = 8
SCAN_CHUNK = 64
SCAN_HEADS = 16
ROW_TILE = 512
PROJ_TILE = 1024
MOD_ROWS = 8
SUBLANES = 8
LANES = 128
VMEM_LIMIT = 56 * 1024 * 1024


def _cparams(sem):
    return pltpu.CompilerParams(dimension_semantics=sem, vmem_limit_bytes=VMEM_LIMIT)


def _sigmoid(x):
    return 0.5 * jnp.tanh(0.5 * x) + 0.5


def _silu(x):
    return _silu_half(0.5 * x)


def _silu_half(h):
    return h + h * jnp.tanh(h)


def _modulated_norm(x, g, shift, scale):
    var = jnp.mean(x * x, axis=-1, keepdims=True)
    return (x * lax.rsqrt(var + EPS)) * g * (1.0 + scale) + shift


def _ada_kernel(cv_ref, w_ref, b_ref, o_ref):
    a = _silu(cv_ref[...])
    o_ref[...] = jnp.dot(a, w_ref[...], preferred_element_type=F32,
                         precision=lax.Precision.HIGHEST) + b_ref[...]


def _ada_table(cv, ada_w, ada_b):
    depth, d, d3 = ada_w.shape
    tn = 1024
    return pl.pallas_call(
        _ada_kernel,
        grid=(depth, d3 // tn),
        in_specs=[
            pl.BlockSpec((MOD_ROWS, d), lambda l, j: (0, 0)),
            pl.BlockSpec((None, d, tn), lambda l, j: (l, 0, j)),
            pl.BlockSpec((None, 1, tn), lambda l, j: (l, 0, j)),
        ],
        out_specs=pl.BlockSpec((None, MOD_ROWS, tn), lambda l, j: (l, 0, j)),
        out_shape=jax.ShapeDtypeStruct((depth, MOD_ROWS, d3), F32),
        compiler_params=_cparams(("arbitrary", "arbitrary")),
        name="ada_table",
    )(cv, ada_w, ada_b.reshape(depth, 1, d3))


def _proj0_kernel(ctx_ref, x_ref, mod_ref, ng_ref, w_ref, lbl_ref, a_ref, h_ref,
                  *, d, e, tiles_per_batch, lb_index, sub):
    i = pl.program_id(0)
    j = pl.program_id(1)
    tm = h_ref.shape[0]

    def fill(src_ref, row):
        m = mod_ref[pl.ds(row, 1), :]
        h_ref[...] = _modulated_norm(src_ref[...], ng_ref[...], m[:, 0:d], m[:, d:2 * d]).astype(BF16)

    @pl.when(j == 0)
    def _():
        @pl.when(i == 0)
        def _():
            fill(ctx_ref, 2)

        @pl.when(i > 0)
        def _():
            fill(x_ref, (i - 1) // tiles_per_batch)

    def acc(c):
        return jnp.dot(h_ref[...], w_ref[:, c * sub:(c + 1) * sub], preferred_element_type=F32)

    nsub = e // sub
    per_sub = sub // LANES

    def put(c, val):
        for cl in range(per_sub):
            a_ref[c * per_sub + cl] = val[:, cl * LANES:(cl + 1) * LANES]

    @pl.when(j == 0)
    def _():
        for c in range(nsub):
            put(c, _silu_half(acc(c)).astype(BF16))

    @pl.when((j == 1) | (j == 2))
    def _():
        dirn = j - 1
        logits = lbl_ref[:, pl.ds(dirn, 1), :]
        mx = jnp.max(logits, axis=0, keepdims=True)
        ex = jnp.exp(logits - mx)
        p = ex / jnp.sum(ex, axis=0, keepdims=True)
        lb_full = jnp.sum(p[0:lb_index + 1], axis=0)
        half_full = 0.5 * (1.0 - lb_full)
        for c in range(nsub):
            half = half_full[:, c * sub:(c + 1) * sub]
            put(c, (half * (1.0 - jnp.tanh(acc(c)))).astype(BF16))

    @pl.when(j == 3)
    def _():
        for c in range(nsub):
            put(c, acc(c).astype(BF16))

    @pl.when(j == 4)
    def _():
        for c in range(nsub):
            put(c, _silu_half(acc(c)).astype(BF16))


def _proj0(ctx2d, x2d, mods, norm_g, w_bf, lb_logits, *, n_lat, lb_index):
    rows_ctx, d = ctx2d.shape
    rows_lat = x2d.shape[0]
    e = w_bf.shape[1] // HG_STREAMS
    tm = PROJ_TILE
    assert rows_ctx == tm and n_lat % tm == 0
    n_tiles = 1 + rows_lat // tm
    rows = tm + rows_lat
    nl = lb_logits.shape[0]
    kern = functools.partial(_proj0_kernel, d=d, e=e, tiles_per_batch=n_lat // tm,
                             lb_index=lb_index, sub=512)
    return pl.pallas_call(
        kern,
        grid=(n_tiles, HG_STREAMS),
        in_specs=[
            pl.BlockSpec((tm, d), lambda i, j: (0, 0)),
            pl.BlockSpec((tm, d), lambda i, j: (jnp.maximum(i - 1, 0), 0)),
            pl.BlockSpec((None, MOD_ROWS, 3 * d), lambda i, j: (0, 0, 0)),
            pl.BlockSpec((None, 1, d), lambda i, j: (0, 0, 0)),
            pl.BlockSpec((d, e), lambda i, j: (0, j)),
            pl.BlockSpec((nl, 2, e), lambda i, j: (0, 0, 0)),
        ],
        out_specs=pl.BlockSpec((e // LANES, tm, LANES), lambda i, j: (j, i, 0)),
        out_shape=jax.ShapeDtypeStruct((HG_STREAMS * (e // LANES), rows, LANES), BF16),
        scratch_shapes=[pltpu.VMEM((tm, d), BF16)],
        compiler_params=_cparams(("arbitrary", "arbitrary")),
        name="proj0",
    )(ctx2d, x2d, mods, norm_g, w_bf, lb_logits)


def _scan_kernel(qf_ref, kf_ref, vf_ref, qb_ref, kb_ref, vb_ref, of_ref, ob_ref,
                 st_ref, pa_ref, pb_ref, da_ref, db_ref, *, n_chunks):
    s = pl.program_id(2)
    c_len = SCAN_CHUNK

    @pl.when(s == 0)
    def _():
        st_ref[...] = jnp.zeros_like(st_ref)
        pb_ref[...] = jnp.zeros_like(pb_ref)
        db_ref[...] = jnp.zeros_like(db_ref)

    row = lax.broadcasted_iota(jnp.int32, (c_len, c_len), 0)
    col = lax.broadcasted_iota(jnp.int32, (c_len, c_len), 1)
    causal = (col <= row, col >= row)
    tri = tuple(m.astype(F32).astype(BF16) for m in causal)
    end_row = (c_len - 1, 0)
    mid = c_len // 2
    qk_refs = ((qf_ref, kf_ref), (qb_ref, kb_ref))
    vo_refs = ((vf_ref, of_ref), (vb_ref, ob_ref))
    nt = (((1,), (1,)), ((), ()))
    tn = (((0,), (0,)), ((), ()))
    QT, KT, QH, KH = range(4)

    def chunk_of(dirn, r):
        return r if dirn == 0 else n_chunks - 1 - r

    def prepare(p_ref, d_ref, dirn, r, hh):
        q_ref, k_ref = qk_refs[dirn]
        c = chunk_of(dirn, r)
        rows = slice(c * c_len, (c + 1) * c_len)
        q = q_ref[hh, rows, :].astype(F32)
        k = k_ref[hh, rows, :].astype(F32)
        b = jnp.dot(tri[dirn], jnp.log(1.0 - k).astype(BF16), preferred_element_type=F32)
        tot = b[end_row[dirn]:end_row[dirn] + 1, :]
        ref = b[mid:mid + 1, :]
        dl = b - ref
        qt = q * jnp.exp(dl)
        kt = k * jnp.exp(-dl)
        p_ref[dirn, QT, hh, rows, :] = qt.astype(BF16)
        p_ref[dirn, KT, hh, rows, :] = kt.astype(BF16)
        p_ref[dirn, QH, hh, rows, :] = (qt * jnp.exp(ref)).astype(BF16)
        p_ref[dirn, KH, hh, rows, :] = (kt * jnp.exp(tot - ref)).astype(BF16)
        d_ref[dirn, c, hh] = jnp.exp(tot)

    def scores(p_ref, dirn, r, hh):
        v_ref = vo_refs[dirn][0]
        c = chunk_of(dirn, r)
        rows = slice(c * c_len, (c + 1) * c_len)
        sc = lax.dot_general(p_ref[dirn, QT, hh, rows, :], p_ref[dirn, KT, hh, rows, :],
                             nt, preferred_element_type=F32)
        prob = jnp.where(causal[dirn], sc, 0.0).astype(BF16)
        upd = lax.dot_general(v_ref[hh, rows, :], p_ref[dirn, KH, hh, rows, :],
                              tn, preferred_element_type=F32)
        return prob, upd

    def outputs(p_ref, d_ref, dirn, r, hh, prob, upd):
        v_ref, o_ref = vo_refs[dirn]
        c = chunk_of(dirn, r)
        rows = slice(c * c_len, (c + 1) * c_len)
        st = st_ref[dirn, hh]
        o = (jnp.dot(prob, v_ref[hh, rows, :], preferred_element_type=F32)
             + lax.dot_general(p_ref[dirn, QH, hh, rows, :], st.astype(BF16), nt,
                               preferred_element_type=F32))
        o_ref[hh, rows, :] = o.astype(BF16)
        st_ref[dirn, hh] = st * d_ref[dirn, c, hh] + upd

    def step(p_new, d_new, p_old, d_old):
        heads = range(SCAN_HEADS)
        pu = {(dirn, hh): scores(p_old, dirn, 0, hh) for hh in heads for dirn in range(2)}
        for r in range(n_chunks):
            pu_next = {}
            for hh in heads:
                for dirn in range(2):
                    if r + 1 < n_chunks:
                        pu_next[dirn, hh] = scores(p_old, dirn, r + 1, hh)
                    outputs(p_old, d_old, dirn, r, hh, *pu[dirn, hh])
                    prepare(p_new, d_new, dirn, r, hh)
            pu = pu_next

    @pl.when(s % 2 == 0)
    def _():
        step(pa_ref, da_ref, pb_ref, db_ref)

    @pl.when(s % 2 == 1)
    def _():
        step(pb_ref, db_ref, pa_ref, da_ref)


def _scan(a, *, batch, n_lat, n_ctx, e):
    ts = n_ctx
    assert ts % SCAN_CHUNK == 0 and n_lat % ts == 0
    lat_steps = n_lat // ts
    steps = 1 + lat_steps
    n_chunks = ts // SCAN_CHUNK
    assert HEAD_DIM == LANES
    groups = e // (SCAN_HEADS * HEAD_DIM)
    rows = batch * (n_ctx + n_lat)
    assert batch * n_ctx <= PROJ_TILE and PROJ_TILE % ts == 0

    def row_f(lat0):
        return lambda b, s: jnp.where(s == 0, b, lat0 + lat_steps * b + s - 1)

    def row_b(lat0):
        return lambda b, s: jnp.where(s == 0, b, lat0 + lat_steps * b + lat_steps - s)

    in_f, in_b = row_f(PROJ_TILE // ts), row_b(PROJ_TILE // ts)
    out_f, out_b = row_f(batch), row_b(batch)

    ahead = lambda g: jnp.minimum(g, steps - 1)
    behind = lambda g: jnp.maximum(g - 1, 0)

    def spec(stream, rfn, when):
        return pl.BlockSpec((SCAN_HEADS, ts, LANES),
                            lambda b, hg, g: (stream * groups + hg, rfn(b, when(g)), 0))

    kern = functools.partial(_scan_kernel, n_chunks=n_chunks)
    operands = pltpu.VMEM((2, 4, SCAN_HEADS, ts, LANES), BF16)
    decays = pltpu.VMEM((2, n_chunks, SCAN_HEADS, 1, LANES), F32)
    return pl.pallas_call(
        kern,
        grid=(batch, groups, steps + 1),
        in_specs=[spec(0, in_f, ahead), spec(1, in_f, ahead), spec(3, in_f, behind),
                  spec(0, in_b, ahead), spec(2, in_b, ahead), spec(3, in_b, behind)],
        out_specs=[pl.BlockSpec((SCAN_HEADS, ts, LANES), lambda b, hg, g: (hg, out_f(b, behind(g)), 0)),
                   pl.BlockSpec((SCAN_HEADS, ts, LANES), lambda b, hg, g: (hg, out_b(b, behind(g)), 0))],
        out_shape=[jax.ShapeDtypeStruct((e // LANES, rows, LANES), BF16)] * 2,
        scratch_shapes=[pltpu.VMEM((2, SCAN_HEADS, HEAD_DIM, HEAD_DIM), F32),
                        operands, operands, decays, decays],
        compiler_params=_cparams(("arbitrary", "arbitrary", "arbitrary")),
        name="hgrn_scan",
    )(a, a, a, a, a, a)


def _out0_kernel(of_ref, ob_ref, z_ref, ctx_ref, x_ref, mod_ref, hg_ref, w_ref,
                 xo_ref, co_ref, y_ref, *, d, e, tiles_per_batch):
    i = pl.program_id(0)
    heads = e // HEAD_DIM
    for h in range(heads):
        lanes = slice(h * HEAD_DIM, (h + 1) * HEAD_DIM)
        o = of_ref[h].astype(F32) + ob_ref[h].astype(F32)
        var = jnp.mean(o * o, axis=-1, keepdims=True)
        yn = (o * lax.rsqrt(var + EPS)) * hg_ref[...]
        y_ref[:, lanes] = (yn * z_ref[h].astype(F32)).astype(BF16)
    mix = jnp.dot(y_ref[...], w_ref[...], preferred_element_type=F32)

    @pl.when(i == 0)
    def _():
        gate = mod_ref[2:3, 2 * d:3 * d]
        co_ref[...] = ctx_ref[...] + gate * mix

    @pl.when(i > 0)
    def _():
        gate = mod_ref[pl.ds((i - 1) // tiles_per_batch, 1), 2 * d:3 * d]
        xo_ref[...] = x_ref[...] + gate * mix


def _out0(o_f, o_b, a, ctx2d, x2d, mods, hg_norm_g, w_bf, *, n_lat):
    rows_ctx, d = ctx2d.shape
    rows_lat = x2d.shape[0]
    heads = o_f.shape[0]
    e = heads * HEAD_DIM
    tm = ROW_TILE
    assert rows_ctx == tm and PROJ_TILE % tm == 0
    n_tiles = 1 + rows_lat // tm
    kern = functools.partial(_out0_kernel, d=d, e=e, tiles_per_batch=n_lat // tm)
    lat_idx = lambda i: (jnp.maximum(i - 1, 0), 0)
    skip = PROJ_TILE // tm - 1
    src_row = lambda i: jnp.where(i == 0, 0, i + skip)
    return pl.pallas_call(
        kern,
        grid=(n_tiles,),
        in_specs=[
            pl.BlockSpec((heads, tm, HEAD_DIM), lambda i: (0, i, 0)),
            pl.BlockSpec((heads, tm, HEAD_DIM), lambda i: (0, i, 0)),
            pl.BlockSpec((heads, tm, HEAD_DIM), lambda i: (HG_STREAMS - 1, src_row(i), 0)),
            pl.BlockSpec((tm, d), lambda i: (0, 0)),
            pl.BlockSpec((tm, d), lat_idx),
            pl.BlockSpec((None, MOD_ROWS, 3 * d), lambda i: (0, 0, 0)),
            pl.BlockSpec((1, HEAD_DIM), lambda i: (0, 0)),
            pl.BlockSpec((e, d), lambda i: (0, 0)),
        ],
        out_specs=[pl.BlockSpec((tm, d), lat_idx), pl.BlockSpec((tm, d), lambda i: (0, 0))],
        out_shape=[jax.ShapeDtypeStruct((rows_lat, d), F32), jax.ShapeDtypeStruct((rows_ctx, d), F32)],
        scratch_shapes=[pltpu.VMEM((tm, e), BF16)],
        compiler_params=_cparams(("arbitrary",)),
        name="readout0",
    )(o_f, o_b, a, ctx2d, x2d, mods, hg_norm_g, w_bf)


def _proj1_kernel(x_ref, mod_ref, ng_ref, w_ref, u_ref, z_ref, h_ref,
                  *, d, e, tiles_per_batch, sub, n2):
    i = pl.program_id(0)
    j = pl.program_id(1)
    tm = h_ref.shape[0]

    @pl.when(j == 0)
    def _():
        m = mod_ref[pl.ds(i // tiles_per_batch, 1), :]
        h_ref[...] = _modulated_norm(x_ref[...], ng_ref[...], m[:, 0:d], m[:, d:2 * d]).astype(BF16)

    def acc(c):
        return jnp.dot(h_ref[...], w_ref[:, c * sub:(c + 1) * sub], preferred_element_type=F32)

    nsub = e // sub

    @pl.when(j == 0)
    def _():
        for c in range(nsub):
            a = acc(c)
            for n1l in range(tm // n2):
                for m in range(n2 // SUBLANES):
                    src = n1l * n2 + m * SUBLANES
                    for cl in range(sub // LANES):
                        u_ref[m, c * (sub // LANES) + cl, n1l * SUBLANES:(n1l + 1) * SUBLANES, :] = (
                            a[src:src + SUBLANES, cl * LANES:(cl + 1) * LANES])

    @pl.when(j == 1)
    def _():
        for c in range(nsub):
            z_ref[:, c * sub:(c + 1) * sub] = _silu_half(acc(c)).astype(BF16)


def _proj1(x2d, mods, norm_g, w_bf, *, batch, n_lat, layer):
    rows, d = x2d.shape
    e = w_bf.shape[1] // 2
    tm = PROJ_TILE
    n1, n2 = _dft_factors(n_lat)
    q = n2 // SUBLANES
    tpb = n_lat // tm
    rows_blk = (tm // n2) * SUBLANES
    kern = functools.partial(_proj1_kernel, d=d, e=e, tiles_per_batch=tpb, sub=512, n2=n2)
    return pl.pallas_call(
        kern,
        grid=(rows // tm, 2),
        in_specs=[
            pl.BlockSpec((tm, d), lambda i, j: (i, 0)),
            pl.BlockSpec((None, MOD_ROWS, 3 * d), lambda i, j: (layer, 0, 0)),
            pl.BlockSpec((None, 1, d), lambda i, j: (layer, 0, 0)),
            pl.BlockSpec((d, e), lambda i, j: (0, j)),
        ],
        out_specs=[pl.BlockSpec((None, q, e // LANES, rows_blk, LANES),
                                lambda i, j: (i // tpb, 0, 0, i % tpb, 0)),
                   pl.BlockSpec((tm, e), lambda i, j: (i, 0))],
        out_shape=[jax.ShapeDtypeStruct((batch, q, e // LANES, n1 * SUBLANES, LANES), F32),
                   jax.ShapeDtypeStruct((rows, e), BF16)],
        scratch_shapes=[pltpu.VMEM((tm, d), BF16)],
        compiler_params=_cparams(("arbitrary", "arbitrary")),
        name="proj1",
    )(x2d, mods, norm_g, w_bf)


def _dft_factors(n):
    n1 = 128 if n % 128 == 0 else n
    return n1, n // n1


def _dft_tables(n, group_dim):
    n1, n2 = _dft_factors(n)
    a2 = np.arange(n2, dtype=np.float64)[:, None, None]
    k1 = np.arange(n1, dtype=np.float64)[None, :, None]
    m1 = np.arange(n1, dtype=np.float64)[None, None, :]
    ang = -2.0 * np.pi * (a2 * k1 / n + m1 * k1 / n1)
    stage1 = np.concatenate([np.cos(ang), np.sin(ang)], axis=1) / np.sqrt(n1)
    kk = np.arange(n2, dtype=np.float64)
    ang2 = -2.0 * np.pi * np.outer(kk, kk) / n2
    fr, fi = np.cos(ang2) / np.sqrt(n2), np.sin(ang2) / np.sqrt(n2)
    stage2 = np.stack([np.concatenate([fr, fi], axis=0),
                       np.concatenate([-fi, fr], axis=0)])
    cc = np.arange(group_dim, dtype=np.float64)
    ang3 = 2.0 * np.pi * np.outer(cc, cc) / group_dim
    chan = np.concatenate([np.cos(ang3), np.sin(ang3)], axis=0) / np.sqrt(group_dim)
    to = lambda t: jnp.asarray(t, dtype=F32).astype(BF16)
    return to(stage1), to(stage2), to(chan)


def _dft1_kernel(m_ref, x_ref, tr_ref, ti_ref, *, n1):
    nlb = x_ref.shape[0]
    for r in range(SUBLANES):
        rows = pl.ds(r, n1, stride=SUBLANES)
        xr = jnp.concatenate([x_ref[lb, rows, :] for lb in range(nlb)], axis=1).astype(BF16)
        t = jnp.dot(m_ref[r], xr, preferred_element_type=F32)
        for lb in range(nlb):
            lanes = slice(lb * LANES, (lb + 1) * LANES)
            tr_ref[lb, rows, :] = t[0:n1, lanes]
            ti_ref[lb, rows, :] = t[n1:2 * n1, lanes]


def _dft1(u, stage1):
    batch, q, nlb, rows, _ = u.shape
    n2, two_n1, n1 = stage1.shape
    lbs = min(nlb, 8)
    m4 = stage1.reshape(q, SUBLANES, two_n1, n1)
    spec = pl.BlockSpec((None, None, lbs, rows, LANES), lambda b, m, c: (b, m, c, 0, 0))
    out = jax.ShapeDtypeStruct(u.shape, F32)
    return pl.pallas_call(
        functools.partial(_dft1_kernel, n1=n1),
        grid=(batch, q, nlb // lbs),
        in_specs=[pl.BlockSpec((None, SUBLANES, two_n1, n1), lambda b, m, c: (m, 0, 0, 0)), spec],
        out_specs=[spec, spec],
        out_shape=[out, out],
        compiler_params=_cparams(("arbitrary", "arbitrary", "arbitrary")),
        name="dft_stage1",
    )(m4, u)


def _dft2_kernel(f_ref, cs_ref, tr_ref, ti_ref, y_ref, *, n2, gd):
    q, nlb = tr_ref.shape[0], tr_ref.shape[1]
    per_group = gd // LANES

    def gather(ref, tile):
        return jnp.concatenate(
            [ref[:, lb, tile, :].reshape(n2, LANES) for lb in range(nlb)], axis=1).astype(BF16)

    for j in range(SUBLANES):
        tile = slice(j * SUBLANES, (j + 1) * SUBLANES)
        g = (jnp.dot(f_ref[0], gather(tr_ref, tile), preferred_element_type=F32)
             + jnp.dot(f_ref[1], gather(ti_ref, tile), preferred_element_type=F32))
        gr = g[0:n2].astype(BF16)
        gi = g[n2:2 * n2].astype(BF16)
        rows = pl.ds(j, n2, stride=SUBLANES)
        for gq in range(nlb // per_group):
            lanes = slice(gq * gd, (gq + 1) * gd)
            yg = (jnp.dot(gr[:, lanes], cs_ref[0:gd, :], preferred_element_type=F32)
                  + jnp.dot(gi[:, lanes], cs_ref[gd:2 * gd, :], preferred_element_type=F32))
            for cl in range(per_group):
                y_ref[gq * per_group + cl, rows, :] = yg[:, cl * LANES:(cl + 1) * LANES]


def _dft2(tr, ti, stage2, chan, *, n1):
    batch, q, nlb, rows, _ = tr.shape
    n2 = q * SUBLANES
    kt = n1 // SUBLANES
    gd = chan.shape[1]
    tile_rows = SUBLANES * SUBLANES
    shape_in = (batch, q, nlb, kt, tile_rows, LANES)
    in_spec = pl.BlockSpec((None, q, nlb, None, tile_rows, LANES), lambda b, k: (b, 0, 0, k, 0, 0))
    return pl.pallas_call(
        functools.partial(_dft2_kernel, n2=n2, gd=gd),
        grid=(batch, kt),
        in_specs=[pl.BlockSpec((2, 2 * n2, n2), lambda b, k: (0, 0, 0)),
                  pl.BlockSpec((2 * gd, gd), lambda b, k: (0, 0)),
                  in_spec, in_spec],
        out_specs=pl.BlockSpec((None, None, nlb, n2 * SUBLANES, LANES), lambda b, k: (b, k, 0, 0, 0)),
        out_shape=jax.ShapeDtypeStruct((batch, kt, nlb, n2 * SUBLANES, LANES), F32),
        compiler_params=_cparams(("arbitrary", "arbitrary")),
        name="dft_stage2",
    )(stage2, chan, tr.reshape(shape_in), ti.reshape(shape_in))


def _out1_kernel(y_in_ref, z_ref, x_ref, mod_ref, w_ref, fg_ref, o_ref, y_ref,
                 *, d, n1, tiles_per_batch):
    i = pl.program_id(0)
    kt, nlb = y_in_ref.shape[0], y_in_ref.shape[1]
    k2_per_tile = y_in_ref.shape[2] // SUBLANES
    pair = 2 * SUBLANES
    for k2l in range(k2_per_tile):
        src = slice(k2l * SUBLANES, (k2l + 1) * SUBLANES)
        for kp in range(kt // 2):
            dst = slice(k2l * n1 + kp * pair, k2l * n1 + (kp + 1) * pair)
            yv = jnp.concatenate(
                [jnp.concatenate([y_in_ref[2 * kp + a, lb, src, :] for lb in range(nlb)], axis=1)
                 for a in range(2)], axis=0)
            y_ref[dst, :] = (yv * z_ref[dst, :].astype(F32)).astype(BF16)
    mix = jnp.dot(y_ref[...], w_ref[...], preferred_element_type=F32)
    gate = mod_ref[pl.ds(i // tiles_per_batch, 1), 2 * d:3 * d]
    x = x_ref[...] + gate * mix
    var = jnp.mean(x * x, axis=-1, keepdims=True)
    o_ref[...] = (x * lax.rsqrt(var + EPS)) * fg_ref[...]


def _out1(y_in, zs, x2d, mods, w_bf, final_g, *, n_lat, n1, layer):
    rows, d = x2d.shape
    batch, kt, nlb, _, _ = y_in.shape
    e = nlb * LANES
    tm = ROW_TILE
    assert tm % n1 == 0
    tpb = n_lat // tm
    blk_rows = (tm // n1) * SUBLANES
    kern = functools.partial(_out1_kernel, d=d, n1=n1, tiles_per_batch=tpb)
    return pl.pallas_call(
        kern,
        grid=(rows // tm,),
        in_specs=[
            pl.BlockSpec((None, kt, nlb, blk_rows, LANES), lambda i: (i // tpb, 0, 0, i % tpb, 0)),
            pl.BlockSpec((tm, e), lambda i: (i, 0)),
            pl.BlockSpec((tm, d), lambda i: (i, 0)),
            pl.BlockSpec((None, MOD_ROWS, 3 * d), lambda i: (layer, 0, 0)),
            pl.BlockSpec((e, d), lambda i: (0, 0)),
            pl.BlockSpec((1, d), lambda i: (0, 0)),
        ],
        out_specs=pl.BlockSpec((tm, d), lambda i: (i, 0)),
        out_shape=jax.ShapeDtypeStruct((rows, d), F32),
        scratch_shapes=[pltpu.VMEM((tm, e), BF16)],
        compiler_params=_cparams(("arbitrary",)),
        name="readout1",
    )(y_in, zs, x2d, mods, w_bf, final_g)


def kernel(x, c, ctx, c_ctx, ada_w, ada_b, norm_g, hg_w_in, hg_lb_logits, hg_norm_g, hg_w_out,
           ft_w_in, ft_w_out, final_g):
    batch, n_lat, d = x.shape
    n_ctx = ctx.shape[1]
    depth = ada_w.shape[0]
    e = hg_w_out.shape[1]
    assert depth == 2 and batch == 2 and batch + 1 <= MOD_ROWS

    cv = jnp.concatenate([c, c_ctx[None, :], jnp.zeros((MOD_ROWS - batch - 1, d), F32)], axis=0)
    mods = _ada_table(cv, ada_w, ada_b)
    ng = norm_g.reshape(depth, 1, d)
    x2d = x.reshape(batch * n_lat, d)
    ctx2d = ctx.reshape(batch * n_ctx, d)

    ctx_pad = jnp.pad(ctx2d, ((0, PROJ_TILE - batch * n_ctx), (0, 0)))
    half_cols = jnp.where(jnp.arange(HG_STREAMS * e) // e == 3, 1.0, 0.5).astype(F32)
    a = _proj0(ctx_pad, x2d, mods, ng, (hg_w_in[0] * half_cols).astype(BF16), hg_lb_logits,
                  n_lat=n_lat, lb_index=0)
    o_f, o_b = _scan(a, batch=batch, n_lat=n_lat, n_ctx=n_ctx, e=e)
    x1, _ctx1 = _out0(o_f, o_b, a, ctx2d, x2d, mods, hg_norm_g[0:1], hg_w_out[0].astype(BF16),
                      n_lat=n_lat)

    z_half = jnp.where(jnp.arange(2 * e) < e, 1.0, 0.5).astype(F32)
    u, zs = _proj1(x1, mods, ng, (ft_w_in[0] * z_half).astype(BF16), batch=batch, n_lat=n_lat,
                   layer=1)
    stage1, stage2, chan = _dft_tables(n_lat, e // FT_GROUPS)
    n1 = stage1.shape[2]
    tr, ti = _dft1(u, stage1)
    y = _dft2(tr, ti, stage2, chan, n1=n1)
    out = _out1(y, zs, x1, mods, ft_w_out[0].astype(BF16), final_g.reshape(1, d),
                n_lat=n_lat, n1=n1, layer=1)
    return out.reshape(batch, n_lat, d)
```

```python
import functools

import numpy as np
import jax
import jax.numpy as jnp
from jax import lax
from jax.experimental import pallas as pl
from jax.experimental.pallas import tpu as pltpu

F32 = jnp.float32
BF16 = jnp.bfloat16

EPS = 1e-6
HEAD_DIM = 128
HG_STREAMS = 5
FT_GROUPS = 8
SCAN_CHUNK = 64
SCAN_HEADS = 16
ROW_TILE = 512
PROJ_TILE = 1024
MOD_ROWS = 8
SUBLANES = 8
LANES = 128
VMEM_LIMIT = 56 * 1024 * 1024


def _cparams(sem):
    return pltpu.CompilerParams(dimension_semantics=sem, vmem_limit_bytes=VMEM_LIMIT)


def _sigmoid(x):
    return 0.5 * jnp.tanh(0.5 * x) + 0.5


def _silu(x):
    return _silu_half(0.5 * x)


def _silu_half(h):
    return h + h * jnp.tanh(h)


def _modulated_norm(x, g, shift, scale):
    var = jnp.mean(x * x, axis=-1, keepdims=True)
    return (x * lax.rsqrt(var + EPS)) * g * (1.0 + scale) + shift


def _ada_kernel(cv_ref, w_ref, b_ref, o_ref):
    a = _silu(cv_ref[...])
    o_ref[...] = jnp.dot(a, w_ref[...], preferred_element_type=F32,
                         precision=lax.Precision.HIGHEST) + b_ref[...]


def _ada_table(cv, ada_w, ada_b):
    depth, d, d3 = ada_w.shape
    tn = 1024
    return pl.pallas_call(
        _ada_kernel,
        grid=(depth, d3 // tn),
        in_specs=[
            pl.BlockSpec((MOD_ROWS, d), lambda l, j: (0, 0)),
            pl.BlockSpec((None, d, tn), lambda l, j: (l, 0, j)),
            pl.BlockSpec((None, 1, tn), lambda l, j: (l, 0, j)),
        ],
        out_specs=pl.BlockSpec((None, MOD_ROWS, tn), lambda l, j: (l, 0, j)),
        out_shape=jax.ShapeDtypeStruct((depth, MOD_ROWS, d3), F32),
        compiler_params=_cparams(("arbitrary", "arbitrary")),
        name="ada_table",
    )(cv, ada_w, ada_b.reshape(depth, 1, d3))


def _proj0_kernel(ctx_ref, x_ref, mod_ref, ng_ref, w_ref, lbl_ref, a_ref, h_ref,
                  *, d, e, tiles_per_batch, lb_index, sub):
    i = pl.program_id(0)
    j = pl.program_id(1)
    tm = h_ref.shape[1]
    nsub = e // sub
    per_sub = sub // LANES
    slot = i % 2

    def fill(src_ref, mod_row, dst_slot, part, parts):
        rows = slice(part * (tm // parts), (part + 1) * (tm // parts))
        m = mod_ref[pl.ds(mod_row, 1), :]
        h_ref[dst_slot, rows, :] = _modulated_norm(
            src_ref[rows, :], ng_ref[...], m[:, 0:d], m[:, d:2 * d]).astype(BF16)

    @pl.when((i == 0) & (j == 0))
    def _():
        fill(ctx_ref, 2, 0, 0, 1)

    def acc(c):
        return jnp.dot(h_ref[slot], w_ref[:, c * sub:(c + 1) * sub], preferred_element_type=F32)

    def put(c, val):
        for cl in range(per_sub):
            a_ref[c * per_sub + cl] = val[:, cl * LANES:(cl + 1) * LANES]

    @pl.when(j == 0)
    def _():
        for c in range(nsub):
            put(c, _silu_half(acc(c)).astype(BF16))

    @pl.when((j == 1) | (j == 2))
    def _():
        dirn = j - 1
        logits = lbl_ref[:, pl.ds(dirn, 1), :]
        mx = jnp.max(logits, axis=0, keepdims=True)
        ex = jnp.exp(logits - mx)
        p = ex / jnp.sum(ex, axis=0, keepdims=True)
        lb_full = jnp.sum(p[0:lb_index + 1], axis=0)
        half_full = 0.5 * (1.0 - lb_full)
        for c in range(nsub):
            half = half_full[:, c * sub:(c + 1) * sub]
            put(c, (half * (1.0 - jnp.tanh(acc(c)))).astype(BF16))

    @pl.when(j == 3)
    def _():
        for c in range(nsub):
            put(c, acc(c).astype(BF16))

    @pl.when((j == 4) & (i + 1 < pl.num_programs(0)))
    def _():
        for c in range(nsub):
            put(c, _silu_half(acc(c)).astype(BF16))
            fill(x_ref, i // tiles_per_batch, 1 - slot, c, nsub)

    @pl.when((j == 4) & (i + 1 == pl.num_programs(0)))
    def _():
        for c in range(nsub):
            put(c, _silu_half(acc(c)).astype(BF16))


def _proj0(ctx2d, x2d, mods, norm_g, w_bf, lb_logits, *, n_lat, lb_index):
    rows_ctx, d = ctx2d.shape
    rows_lat = x2d.shape[0]
    e = w_bf.shape[1] // HG_STREAMS
    tm = PROJ_TILE
    assert rows_ctx == tm and n_lat % tm == 0
    n_tiles = 1 + rows_lat // tm
    rows = tm + rows_lat
    nl = lb_logits.shape[0]
    kern = functools.partial(_proj0_kernel, d=d, e=e, tiles_per_batch=n_lat // tm,
                             lb_index=lb_index, sub=512)
    return pl.pallas_call(
        kern,
        grid=(n_tiles, HG_STREAMS),
        in_specs=[
            pl.BlockSpec((tm, d), lambda i, j: (0, 0)),
            pl.BlockSpec((tm, d), lambda i, j: (
                jnp.clip(i - 1 + (j == HG_STREAMS - 1).astype(jnp.int32), 0, n_tiles - 2), 0)),
            pl.BlockSpec((None, MOD_ROWS, 3 * d), lambda i, j: (0, 0, 0)),
            pl.BlockSpec((None, 1, d), lambda i, j: (0, 0, 0)),
            pl.BlockSpec((d, e), lambda i, j: (0, j)),
            pl.BlockSpec((nl, 2, e), lambda i, j: (0, 0, 0)),
        ],
        out_specs=pl.BlockSpec((e // LANES, tm, LANES), lambda i, j: (j, i, 0)),
        out_shape=jax.ShapeDtypeStruct((HG_STREAMS * (e // LANES), rows, LANES), BF16),
        scratch_shapes=[pltpu.VMEM((2, tm, d), BF16)],
        compiler_params=_cparams(("arbitrary", "arbitrary")),
        name="proj0",
    )(ctx2d, x2d, mods, norm_g, w_bf, lb_logits)


def _scan_kernel(qf_ref, kf_ref, vf_ref, qb_ref, kb_ref, vb_ref, of_ref, ob_ref,
                 st_ref, pa_ref, pb_ref, da_ref, db_ref, *, n_chunks):
    s = pl.program_id(2)
    c_len = SCAN_CHUNK

    @pl.when(s == 0)
    def _():
        st_ref[...] = jnp.zeros_like(st_ref)
        pb_ref[...] = jnp.zeros_like(pb_ref)
        db_ref[...] = jnp.zeros_like(db_ref)

    row = lax.broadcasted_iota(jnp.int32, (c_len, c_len), 0)
    col = lax.broadcasted_iota(jnp.int32, (c_len, c_len), 1)
    causal = (col <= row, col >= row)
    tri = tuple(m.astype(F32).astype(BF16) for m in causal)
    end_row = (c_len - 1, 0)
    mid = c_len // 2
    qk_refs = ((qf_ref, kf_ref), (qb_ref, kb_ref))
    vo_refs = ((vf_ref, of_ref), (vb_ref, ob_ref))
    nt = (((1,), (1,)), ((), ()))
    tn = (((0,), (0,)), ((), ()))
    QT, KT, QH, KH = range(4)

    def chunk_of(dirn, r):
        return r if dirn == 0 else n_chunks - 1 - r

    def prepare(p_ref, d_ref, dirn, r, hh):
        q_ref, k_ref = qk_refs[dirn]
        c = chunk_of(dirn, r)
        rows = slice(c * c_len, (c + 1) * c_len)
        q = q_ref[hh, rows, :].astype(F32)
        k = k_ref[hh, rows, :].astype(F32)
        b = jnp.dot(tri[dirn], jnp.log(1.0 - k).astype(BF16), preferred_element_type=F32)
        tot = b[end_row[dirn]:end_row[dirn] + 1, :]
        ref = b[mid:mid + 1, :]
        dl = b - ref
        qt = q * jnp.exp(dl)
        kt = k * jnp.exp(-dl)
        p_ref[dirn, QT, hh, rows, :] = qt.astype(BF16)
        p_ref[dirn, KT, hh, rows, :] = kt.astype(BF16)
        p_ref[dirn, QH, hh, rows, :] = (qt * jnp.exp(ref)).astype(BF16)
        p_ref[dirn, KH, hh, rows, :] = (kt * jnp.exp(tot - ref)).astype(BF16)
        d_ref[dirn, c, hh] = jnp.exp(tot)

    def scores(p_ref, dirn, r, hh):
        v_ref = vo_refs[dirn][0]
        c = chunk_of(dirn, r)
        rows = slice(c * c_len, (c + 1) * c_len)
        sc = lax.dot_general(p_ref[dirn, QT, hh, rows, :], p_ref[dirn, KT, hh, rows, :],
                             nt, preferred_element_type=F32)
        prob = jnp.where(causal[dirn], sc, 0.0).astype(BF16)
        upd = lax.dot_general(v_ref[hh, rows, :], p_ref[dirn, KH, hh, rows, :],
                              tn, preferred_element_type=F32)
        return prob, upd

    def outputs(p_ref, d_ref, dirn, r, hh, prob, upd):
        v_ref, o_ref = vo_refs[dirn]
        c = chunk_of(dirn, r)
        rows = slice(c * c_len, (c + 1) * c_len)
        st = st_ref[dirn, hh]
        o = (jnp.dot(prob, v_ref[hh, rows, :], preferred_element_type=F32)
             + lax.dot_general(p_ref[dirn, QH, hh, rows, :], st.astype(BF16), nt,
                               preferred_element_type=F32))
        o_ref[hh, rows, :] = o.astype(BF16)
        st_ref[dirn, hh] = st * d_ref[dirn, c, hh] + upd

    def step(p_new, d_new, p_old, d_old):
        heads = range(SCAN_HEADS)
        pu = {(dirn, hh): scores(p_old, dirn, 0, hh) for hh in heads for dirn in range(2)}
        for r in range(n_chunks):
            pu_next = {}
            for hh in heads:
                for dirn in range(2):
                    if r + 1 < n_chunks:
                        pu_next[dirn, hh] = scores(p_old, dirn, r + 1, hh)
                    outputs(p_old, d_old, dirn, r, hh, *pu[dirn, hh])
                    prepare(p_new, d_new, dirn, r, hh)
            pu = pu_next

    @pl.when(s % 2 == 0)
    def _():
        step(pa_ref, da_ref, pb_ref, db_ref)

    @pl.when(s % 2 == 1)
    def _():
        step(pb_ref, db_ref, pa_ref, da_ref)


def _scan(a, *, batch, n_lat, n_ctx, e):
    ts = n_ctx
    assert ts % SCAN_CHUNK == 0 and n_lat % ts == 0
    lat_steps = n_lat // ts
    steps = 1 + lat_steps
    n_chunks = ts // SCAN_CHUNK
    assert HEAD_DIM == LANES
    groups = e // (SCAN_HEADS * HEAD_DIM)
    rows = batch * (n_ctx + n_lat)
    assert batch * n_ctx <= PROJ_TILE and PROJ_TILE % ts == 0

    def row_f(lat0):
        return lambda b, s: jnp.where(s == 0, b, lat0 + lat_steps * b + s - 1)

    def row_b(lat0):
        return lambda b, s: jnp.where(s == 0, b, lat0 + lat_steps * b + lat_steps - s)

    in_f, in_b = row_f(PROJ_TILE // ts), row_b(PROJ_TILE // ts)
    out_f, out_b = row_f(batch), row_b(batch)

    ahead = lambda g: jnp.minimum(g, steps - 1)
    behind = lambda g: jnp.maximum(g - 1, 0)

    def spec(stream, rfn, when):
        return pl.BlockSpec((SCAN_HEADS, ts, LANES),
                            lambda b, hg, g: (stream * groups + hg, rfn(b, when(g)), 0))

    kern = functools.partial(_scan_kernel, n_chunks=n_chunks)
    operands = pltpu.VMEM((2, 4, SCAN_HEADS, ts, LANES), BF16)
    decays = pltpu.VMEM((2, n_chunks, SCAN_HEADS, 1, LANES), F32)
    return pl.pallas_call(
        kern,
        grid=(batch, groups, steps + 1),
        in_specs=[spec(0, in_f, ahead), spec(1, in_f, ahead), spec(3, in_f, behind),
                  spec(0, in_b, ahead), spec(2, in_b, ahead), spec(3, in_b, behind)],
        out_specs=[pl.BlockSpec((SCAN_HEADS, ts, LANES), lambda b, hg, g: (hg, out_f(b, behind(g)), 0)),
                   pl.BlockSpec((SCAN_HEADS, ts, LANES), lambda b, hg, g: (hg, out_b(b, behind(g)), 0))],
        out_shape=[jax.ShapeDtypeStruct((e // LANES, rows, LANES), BF16)] * 2,
        scratch_shapes=[pltpu.VMEM((2, SCAN_HEADS, HEAD_DIM, HEAD_DIM), F32),
                        operands, operands, decays, decays],
        compiler_params=_cparams(("arbitrary", "arbitrary", "arbitrary")),
        name="hgrn_scan",
    )(a, a, a, a, a, a)


def _out0_kernel(of_ref, ob_ref, z_ref, ctx_ref, x_ref, mod_ref, hg_ref, w_ref,
                 xo_ref, co_ref, y_ref, *, d, e, tiles_per_batch):
    i = pl.program_id(0)
    heads = e // HEAD_DIM
    for h in range(heads):
        lanes = slice(h * HEAD_DIM, (h + 1) * HEAD_DIM)
        o = of_ref[h].astype(F32) + ob_ref[h].astype(F32)
        var = jnp.mean(o * o, axis=-1, keepdims=True)
        yn = (o * lax.rsqrt(var + EPS)) * hg_ref[...]
        y_ref[:, lanes] = (yn * z_ref[h].astype(F32)).astype(BF16)
    mix = jnp.dot(y_ref[...], w_ref[...], preferred_element_type=F32)

    @pl.when(i == 0)
    def _():
        gate = mod_ref[2:3, 2 * d:3 * d]
        co_ref[...] = ctx_ref[...] + gate * mix

    @pl.when(i > 0)
    def _():
        gate = mod_ref[pl.ds((i - 1) // tiles_per_batch, 1), 2 * d:3 * d]
        xo_ref[...] = x_ref[...] + gate * mix


def _out0(o_f, o_b, a, ctx2d, x2d, mods, hg_norm_g, w_bf, *, n_lat):
    rows_ctx, d = ctx2d.shape
    rows_lat = x2d.shape[0]
    heads = o_f.shape[0]
    e = heads * HEAD_DIM
    tm = ROW_TILE
    assert rows_ctx == tm and PROJ_TILE % tm == 0
    n_tiles = 1 + rows_lat // tm
    kern = functools.partial(_out0_kernel, d=d, e=e, tiles_per_batch=n_lat // tm)
    lat_idx = lambda i: (jnp.maximum(i - 1, 0), 0)
    skip = PROJ_TILE // tm - 1
    src_row = lambda i: jnp.where(i == 0, 0, i + skip)
    return pl.pallas_call(
        kern,
        grid=(n_tiles,),
        in_specs=[
            pl.BlockSpec((heads, tm, HEAD_DIM), lambda i: (0, i, 0)),
            pl.BlockSpec((heads, tm, HEAD_DIM), lambda i: (0, i, 0)),
            pl.BlockSpec((heads, tm, HEAD_DIM), lambda i: (HG_STREAMS - 1, src_row(i), 0)),
            pl.BlockSpec((tm, d), lambda i: (0, 0)),
            pl.BlockSpec((tm, d), lat_idx),
            pl.BlockSpec((None, MOD_ROWS, 3 * d), lambda i: (0, 0, 0)),
            pl.BlockSpec((1, HEAD_DIM), lambda i: (0, 0)),
            pl.BlockSpec((e, d), lambda i: (0, 0)),
        ],
        out_specs=[pl.BlockSpec((tm, d), lat_idx), pl.BlockSpec((tm, d), lambda i: (0, 0))],
        out_shape=[jax.ShapeDtypeStruct((rows_lat, d), F32), jax.ShapeDtypeStruct((rows_ctx, d), F32)],
        scratch_shapes=[pltpu.VMEM((tm, e), BF16)],
        compiler_params=_cparams(("arbitrary",)),
        name="readout0",
    )(o_f, o_b, a, ctx2d, x2d, mods, hg_norm_g, w_bf)


def _proj1_kernel(x_ref, mod_ref, ng_ref, w_ref, u_ref, z_ref, h_ref,
                  *, d, e, tiles_per_batch, sub, n2):
    i = pl.program_id(0)
    j = pl.program_id(1)
    tm = h_ref.shape[1]
    nsub = e // sub
    slot = i % 2

    def fill(tile, dst_slot, part, parts):
        rows = slice(part * (tm // parts), (part + 1) * (tm // parts))
        m = mod_ref[pl.ds(tile // tiles_per_batch, 1), :]
        h_ref[dst_slot, rows, :] = _modulated_norm(
            x_ref[rows, :], ng_ref[...], m[:, 0:d], m[:, d:2 * d]).astype(BF16)

    @pl.when((i == 0) & (j == 0))
    def _():
        fill(0, 0, 0, 1)

    def acc(c):
        return jnp.dot(h_ref[slot], w_ref[:, c * sub:(c + 1) * sub], preferred_element_type=F32)

    @pl.when(j == 0)
    def _():
        for c in range(nsub):
            a = acc(c)
            for n1l in range(tm // n2):
                for m in range(n2 // SUBLANES):
                    src = n1l * n2 + m * SUBLANES
                    for cl in range(sub // LANES):
                        u_ref[m, c * (sub // LANES) + cl, n1l * SUBLANES:(n1l + 1) * SUBLANES, :] = (
                            a[src:src + SUBLANES, cl * LANES:(cl + 1) * LANES])

    @pl.when((j == 1) & (i + 1 < pl.num_programs(0)))
    def _():
        for c in range(nsub):
            z_ref[:, c * sub:(c + 1) * sub] = _silu_half(acc(c)).astype(BF16)
            fill(i + 1, 1 - slot, c, nsub)

    @pl.when((j == 1) & (i + 1 == pl.num_programs(0)))
    def _():
        for c in range(nsub):
            z_ref[:, c * sub:(c + 1) * sub] = _silu_half(acc(c)).astype(BF16)


def _proj1(x2d, mods, norm_g, w_bf, *, batch, n_lat, layer):
    rows, d = x2d.shape
    e = w_bf.shape[1] // 2
    tm = PROJ_TILE
    n1, n2 = _dft_factors(n_lat)
    q = n2 // SUBLANES
    tpb = n_lat // tm
    rows_blk = (tm // n2) * SUBLANES
    kern = functools.partial(_proj1_kernel, d=d, e=e, tiles_per_batch=tpb, sub=512, n2=n2)
    return pl.pallas_call(
        kern,
        grid=(rows // tm, 2),
        in_specs=[
            pl.BlockSpec((tm, d), lambda i, j: (jnp.minimum(i + j, rows // tm - 1), 0)),
            pl.BlockSpec((None, MOD_ROWS, 3 * d), lambda i, j: (layer, 0, 0)),
            pl.BlockSpec((None, 1, d), lambda i, j: (layer, 0, 0)),
            pl.BlockSpec((d, e), lambda i, j: (0, j)),
        ],
        out_specs=[pl.BlockSpec((None, q, e // LANES, rows_blk, LANES),
                                lambda i, j: (i // tpb, 0, 0, i % tpb, 0)),
                   pl.BlockSpec((tm, e), lambda i, j: (i, 0))],
        out_shape=[jax.ShapeDtypeStruct((batch, q, e // LANES, n1 * SUBLANES, LANES), F32),
                   jax.ShapeDtypeStruct((rows, e), BF16)],
        scratch_shapes=[pltpu.VMEM((2, tm, d), BF16)],
        compiler_params=_cparams(("arbitrary", "arbitrary")),
        name="proj1",
    )(x2d, mods, norm_g, w_bf)


def _dft_factors(n):
    n1 = 128 if n % 128 == 0 else n
    return n1, n // n1


def _dft_tables(n, group_dim):
    n1, n2 = _dft_factors(n)
    a2 = np.arange(n2, dtype=np.float64)[:, None, None]
    k1 = np.arange(n1, dtype=np.float64)[None, :, None]
    m1 = np.arange(n1, dtype=np.float64)[None, None, :]
    ang = -2.0 * np.pi * (a2 * k1 / n + m1 * k1 / n1)
    stage1 = np.concatenate([np.cos(ang), np.sin(ang)], axis=1) / np.sqrt(n1)
    kk = np.arange(n2, dtype=np.float64)
    ang2 = -2.0 * np.pi * np.outer(kk, kk) / n2
    fr, fi = np.cos(ang2) / np.sqrt(n2), np.sin(ang2) / np.sqrt(n2)
    stage2 = np.stack([np.concatenate([fr, fi], axis=0),
                       np.concatenate([-fi, fr], axis=0)])
    cc = np.arange(group_dim, dtype=np.float64)
    ang3 = 2.0 * np.pi * np.outer(cc, cc) / group_dim
    chan = np.concatenate([np.cos(ang3), np.sin(ang3)], axis=0) / np.sqrt(group_dim)
    to = lambda t: jnp.asarray(t, dtype=F32).astype(BF16)
    return to(stage1), to(stage2), to(chan)


def _dft1_kernel(m_ref, x_ref, tr_ref, ti_ref, *, n1):
    nlb = x_ref.shape[0]
    for r in range(SUBLANES):
        rows = pl.ds(r, n1, stride=SUBLANES)
        xr = jnp.concatenate([x_ref[lb, rows, :] for lb in range(nlb)], axis=1).astype(BF16)
        t = jnp.dot(m_ref[r], xr, preferred_element_type=F32)
        for lb in range(nlb):
            lanes = slice(lb * LANES, (lb + 1) * LANES)
            tr_ref[lb, rows, :] = t[0:n1, lanes]
            ti_ref[lb, rows, :] = t[n1:2 * n1, lanes]


def _dft1(u, stage1):
    batch, q, nlb, rows, _ = u.shape
    n2, two_n1, n1 = stage1.shape
    lbs = min(nlb, 8)
    m4 = stage1.reshape(q, SUBLANES, two_n1, n1)
    spec = pl.BlockSpec((None, None, lbs, rows, LANES), lambda b, m, c: (b, m, c, 0, 0))
    out = jax.ShapeDtypeStruct(u.shape, F32)
    return pl.pallas_call(
        functools.partial(_dft1_kernel, n1=n1),
        grid=(batch, q, nlb // lbs),
        in_specs=[pl.BlockSpec((None, SUBLANES, two_n1, n1), lambda b, m, c: (m, 0, 0, 0)), spec],
        out_specs=[spec, spec],
        out_shape=[out, out],
        compiler_params=_cparams(("arbitrary", "arbitrary", "arbitrary")),
        name="dft_stage1",
    )(m4, u)


def _dft2_kernel(f_ref, cs_ref, tr_ref, ti_ref, y_ref, *, n2, gd):
    q, nlb = tr_ref.shape[0], tr_ref.shape[1]
    per_group = gd // LANES

    def gather(ref, tile):
        return jnp.concatenate(
            [ref[:, lb, tile, :].reshape(n2, LANES) for lb in range(nlb)], axis=1).astype(BF16)

    for j in range(SUBLANES):
        tile = slice(j * SUBLANES, (j + 1) * SUBLANES)
        g = (jnp.dot(f_ref[0], gather(tr_ref, tile), preferred_element_type=F32)
             + jnp.dot(f_ref[1], gather(ti_ref, tile), preferred_element_type=F32))
        gr = g[0:n2].astype(BF16)
        gi = g[n2:2 * n2].astype(BF16)
        rows = pl.ds(j, n2, stride=SUBLANES)
        for gq in range(nlb // per_group):
            lanes = slice(gq * gd, (gq + 1) * gd)
            yg = (jnp.dot(gr[:, lanes], cs_ref[0:gd, :], preferred_element_type=F32)
                  + jnp.dot(gi[:, lanes], cs_ref[gd:2 * gd, :], preferred_element_type=F32))
            for cl in range(per_group):
                y_ref[gq * per_group + cl, rows, :] = yg[:, cl * LANES:(cl + 1) * LANES]


def _dft2(tr, ti, stage2, chan, *, n1):
    batch, q, nlb, rows, _ = tr.shape
    n2 = q * SUBLANES
    kt = n1 // SUBLANES
    gd = chan.shape[1]
    tile_rows = SUBLANES * SUBLANES
    shape_in = (batch, q, nlb, kt, tile_rows, LANES)
    in_spec = pl.BlockSpec((None, q, nlb, None, tile_rows, LANES), lambda b, k: (b, 0, 0, k, 0, 0))
    return pl.pallas_call(
        functools.partial(_dft2_kernel, n2=n2, gd=gd),
        grid=(batch, kt),
        in_specs=[pl.BlockSpec((2, 2 * n2, n2), lambda b, k: (0, 0, 0)),
                  pl.BlockSpec((2 * gd, gd), lambda b, k: (0, 0)),
                  in_spec, in_spec],
        out_specs=pl.BlockSpec((None, None, nlb, n2 * SUBLANES, LANES), lambda b, k: (b, k, 0, 0, 0)),
        out_shape=jax.ShapeDtypeStruct((batch, kt, nlb, n2 * SUBLANES, LANES), F32),
        compiler_params=_cparams(("arbitrary", "arbitrary")),
        name="dft_stage2",
    )(stage2, chan, tr.reshape(shape_in), ti.reshape(shape_in))


def _out1_kernel(y_in_ref, z_ref, x_ref, mod_ref, w_ref, fg_ref, o_ref, y_ref,
                 *, d, n1, tiles_per_batch):
    i = pl.program_id(0)
    kt, nlb = y_in_ref.shape[0], y_in_ref.shape[1]
    k2_per_tile = y_in_ref.shape[2] // SUBLANES
    pair = 2 * SUBLANES
    for k2l in range(k2_per_tile):
        src = slice(k2l * SUBLANES, (k2l + 1) * SUBLANES)
        for kp in range(kt // 2):
            dst = slice(k2l * n1 + kp * pair, k2l * n1 + (kp + 1) * pair)
            yv = jnp.concatenate(
                [jnp.concatenate([y_in_ref[2 * kp + a, lb, src, :] for lb in range(nlb)], axis=1)
                 for a in range(2)], axis=0)
            y_ref[dst, :] = (yv * z_ref[dst, :].astype(F32)).astype(BF16)
    mix = jnp.dot(y_ref[...], w_ref[...], preferred_element_type=F32)
    gate = mod_ref[pl.ds(i // tiles_per_batch, 1), 2 * d:3 * d]
    x = x_ref[...] + gate * mix
    var = jnp.mean(x * x, axis=-1, keepdims=True)
    o_ref[...] = (x * lax.rsqrt(var + EPS)) * fg_ref[...]


def _out1(y_in, zs, x2d, mods, w_bf, final_g, *, n_lat, n1, layer):
    rows, d = x2d.shape
    batch, kt, nlb, _, _ = y_in.shape
    e = nlb * LANES
    tm = ROW_TILE
    assert tm % n1 == 0
    tpb = n_lat // tm
    blk_rows = (tm // n1) * SUBLANES
    kern = functools.partial(_out1_kernel, d=d, n1=n1, tiles_per_batch=tpb)
    return pl.pallas_call(
        kern,
        grid=(rows // tm,),
        in_specs=[
            pl.BlockSpec((None, kt, nlb, blk_rows, LANES), lambda i: (i // tpb, 0, 0, i % tpb, 0)),
            pl.BlockSpec((tm, e), lambda i: (i, 0)),
            pl.BlockSpec((tm, d), lambda i: (i, 0)),
            pl.BlockSpec((None, MOD_ROWS, 3 * d), lambda i: (layer, 0, 0)),
            pl.BlockSpec((e, d), lambda i: (0, 0)),
            pl.BlockSpec((1, d), lambda i: (0, 0)),
        ],
        out_specs=pl.BlockSpec((tm, d), lambda i: (i, 0)),
        out_shape=jax.ShapeDtypeStruct((rows, d), F32),
        scratch_shapes=[pltpu.VMEM((tm, e), BF16)],
        compiler_params=_cparams(("arbitrary",)),
        name="readout1",
    )(y_in, zs, x2d, mods, w_bf, final_g)


def kernel(x, c, ctx, c_ctx, ada_w, ada_b, norm_g, hg_w_in, hg_lb_logits, hg_norm_g, hg_w_out,
           ft_w_in, ft_w_out, final_g):
    batch, n_lat, d = x.shape
    n_ctx = ctx.shape[1]
    depth = ada_w.shape[0]
    e = hg_w_out.shape[1]
    assert depth == 2 and batch == 2 and batch + 1 <= MOD_ROWS

    cv = jnp.concatenate([c, c_ctx[None, :], jnp.zeros((MOD_ROWS - batch - 1, d), F32)], axis=0)
    mods = _ada_table(cv, ada_w, ada_b)
    ng = norm_g.reshape(depth, 1, d)
    x2d = x.reshape(batch * n_lat, d)
    ctx2d = ctx.reshape(batch * n_ctx, d)

    ctx_pad = jnp.pad(ctx2d, ((0, PROJ_TILE - batch * n_ctx), (0, 0)))
    half_cols = jnp.where(jnp.arange(HG_STREAMS * e) // e == 3, 1.0, 0.5).astype(F32)
    a = _proj0(ctx_pad, x2d, mods, ng, (hg_w_in[0] * half_cols).astype(BF16), hg_lb_logits,
                  n_lat=n_lat, lb_index=0)
    o_f, o_b = _scan(a, batch=batch, n_lat=n_lat, n_ctx=n_ctx, e=e)
    x1, _ctx1 = _out0(o_f, o_b, a, ctx2d, x2d, mods, hg_norm_g[0:1], hg_w_out[0].astype(BF16),
                      n_lat=n_lat)

    z_half = jnp.where(jnp.arange(2 * e) < e, 1.0, 0.5).astype(F32)
    u, zs = _proj1(x1, mods, ng, (ft_w_in[0] * z_half).astype(BF16), batch=batch, n_lat=n_lat,
                   layer=1)
    stage1, stage2, chan = _dft_tables(n_lat, e // FT_GROUPS)
    n1 = stage1.shape[2]
    tr, ti = _dft1(u, stage1)
    y = _dft2(tr, ti, stage2, chan, n1=n1)
    out = _out1(y, zs, x1, mods, ft_w_out[0].astype(BF16), final_g.reshape(1, d),
                n_lat=n_lat, n1=n1, layer=1)
    return out.reshape(batch, n_lat, d)
```

```python
import functools

import numpy as np
import jax
import jax.numpy as jnp
from jax import lax
from jax.experimental import pallas as pl
from jax.experimental.pallas import tpu as pltpu

F32 = jnp.float32
BF16 = jnp.bfloat16

EPS = 1e-6
LOG2E = 1.4426950408889634
HEAD_DIM = 128
HG_STREAMS = 5
FT_GROUPS = 8
SCAN_CHUNK = 64
SCAN_HEADS = 16
ROW_TILE = 512
PROJ_TILE = 1024
MOD_ROWS = 8
SUBLANES = 8
LANES = 128
VMEM_LIMIT = 56 * 1024 * 1024


def _cparams(sem):
    return pltpu.CompilerParams(dimension_semantics=sem, vmem_limit_bytes=VMEM_LIMIT)


def _sigmoid(x):
    return 0.5 * jnp.tanh(0.5 * x) + 0.5


def _silu(x):
    return _silu_half(0.5 * x)


def _silu_half(h):
    return h + h * jnp.tanh(h)


def _modulated_norm(x, g, shift, scale):
    var = jnp.mean(x * x, axis=-1, keepdims=True)
    return (x * lax.rsqrt(var + EPS)) * (g * (1.0 + scale)) + shift


def _ada_kernel(cv_ref, w_ref, b_ref, o_ref):
    a = _silu(cv_ref[...])
    o_ref[...] = jnp.dot(a, w_ref[...], preferred_element_type=F32,
                         precision=lax.Precision.HIGHEST) + b_ref[...]


def _ada_table(cv, ada_w, ada_b):
    depth, d, d3 = ada_w.shape
    tn = 1024
    return pl.pallas_call(
        _ada_kernel,
        grid=(depth, d3 // tn),
        in_specs=[
            pl.BlockSpec((MOD_ROWS, d), lambda l, j: (0, 0)),
            pl.BlockSpec((None, d, tn), lambda l, j: (l, 0, j)),
            pl.BlockSpec((None, 1, tn), lambda l, j: (l, 0, j)),
        ],
        out_specs=pl.BlockSpec((None, MOD_ROWS, tn), lambda l, j: (l, 0, j)),
        out_shape=jax.ShapeDtypeStruct((depth, MOD_ROWS, d3), F32),
        compiler_params=_cparams(("arbitrary", "arbitrary")),
        name="ada_table",
    )(cv, ada_w, ada_b.reshape(depth, 1, d3))


def _proj0_kernel(ctx_ref, x_ref, mod_ref, ng_ref, w_ref, lbl_ref, a_ref, h_ref,
                  *, d, e, tiles_per_batch, lb_index, sub):
    i = pl.program_id(0)
    j = pl.program_id(1)
    tm = h_ref.shape[0]

    def fill(src_ref, row):
        m = mod_ref[pl.ds(row, 1), :]
        h_ref[...] = _modulated_norm(src_ref[...], ng_ref[...], m[:, 0:d], m[:, d:2 * d]).astype(BF16)

    @pl.when(j == 0)
    def _():
        @pl.when(i == 0)
        def _():
            fill(ctx_ref, 2)

        @pl.when(i > 0)
        def _():
            fill(x_ref, (i - 1) // tiles_per_batch)

    def acc(c):
        return jnp.dot(h_ref[...], w_ref[:, c * sub:(c + 1) * sub], preferred_element_type=F32)

    nsub = e // sub
    per_sub = sub // LANES

    def put(c, val):
        for cl in range(per_sub):
            a_ref[c * per_sub + cl] = val[:, cl * LANES:(cl + 1) * LANES]

    @pl.when(j == 0)
    def _():
        for c in range(nsub):
            put(c, _silu_half(acc(c)).astype(BF16))

    @pl.when((j == 1) | (j == 2))
    def _():
        dirn = j - 1
        logits = lbl_ref[:, pl.ds(dirn, 1), :]
        mx = jnp.max(logits, axis=0, keepdims=True)
        ex = jnp.exp(logits - mx)
        p = ex / jnp.sum(ex, axis=0, keepdims=True)
        lb_full = jnp.sum(p[0:lb_index + 1], axis=0)
        half_full = 0.5 * (1.0 - lb_full)
        for c in range(nsub):
            half = half_full[:, c * sub:(c + 1) * sub]
            put(c, (half * (1.0 - jnp.tanh(acc(c)))).astype(BF16))

    @pl.when(j == 3)
    def _():
        for c in range(nsub):
            put(c, acc(c).astype(BF16))

    @pl.when(j == 4)
    def _():
        for c in range(nsub):
            put(c, _silu_half(acc(c)).astype(BF16))


def _proj0(ctx2d, x2d, mods, norm_g, w_bf, lb_logits, *, n_lat, lb_index):
    rows_ctx, d = ctx2d.shape
    rows_lat = x2d.shape[0]
    e = w_bf.shape[1] // HG_STREAMS
    tm = PROJ_TILE
    assert rows_ctx == tm and n_lat % tm == 0
    n_tiles = 1 + rows_lat // tm
    rows = tm + rows_lat
    nl = lb_logits.shape[0]
    kern = functools.partial(_proj0_kernel, d=d, e=e, tiles_per_batch=n_lat // tm,
                             lb_index=lb_index, sub=512)
    return pl.pallas_call(
        kern,
        grid=(n_tiles, HG_STREAMS),
        in_specs=[
            pl.BlockSpec((tm, d), lambda i, j: (0, 0)),
            pl.BlockSpec((tm, d), lambda i, j: (jnp.maximum(i - 1, 0), 0)),
            pl.BlockSpec((None, MOD_ROWS, 3 * d), lambda i, j: (0, 0, 0)),
            pl.BlockSpec((None, 1, d), lambda i, j: (0, 0, 0)),
            pl.BlockSpec((d, e), lambda i, j: (0, j)),
            pl.BlockSpec((nl, 2, e), lambda i, j: (0, 0, 0)),
        ],
        out_specs=pl.BlockSpec((e // LANES, tm, LANES), lambda i, j: (j, i, 0)),
        out_shape=jax.ShapeDtypeStruct((HG_STREAMS * (e // LANES), rows, LANES), BF16),
        scratch_shapes=[pltpu.VMEM((tm, d), BF16)],
        compiler_params=_cparams(("arbitrary", "arbitrary")),
        name="proj0",
    )(ctx2d, x2d, mods, norm_g, w_bf, lb_logits)


def _scan_kernel(qf_ref, kf_ref, vf_ref, qb_ref, kb_ref, vb_ref, of_ref, ob_ref,
                 st_ref, pa_ref, pb_ref, da_ref, db_ref, *, n_chunks):
    s = pl.program_id(2)
    c_len = SCAN_CHUNK

    @pl.when(s == 0)
    def _():
        st_ref[...] = jnp.zeros_like(st_ref)
        pb_ref[...] = jnp.zeros_like(pb_ref)
        db_ref[...] = jnp.zeros_like(db_ref)

    row = lax.broadcasted_iota(jnp.int32, (c_len, c_len), 0)
    col = lax.broadcasted_iota(jnp.int32, (c_len, c_len), 1)
    causal = (col <= row, col >= row)
    tri = tuple(m.astype(F32).astype(BF16) for m in causal)
    end_row = (c_len - 1, 0)
    mid = c_len // 2
    qk_refs = ((qf_ref, kf_ref), (qb_ref, kb_ref))
    vo_refs = ((vf_ref, of_ref), (vb_ref, ob_ref))
    nt = (((1,), (1,)), ((), ()))
    tn = (((0,), (0,)), ((), ()))
    QT, KT, QH, KH = range(4)

    def chunk_of(dirn, r):
        return r if dirn == 0 else n_chunks - 1 - r

    def prepare(p_ref, d_ref, dirn, r, hh):
        q_ref, k_ref = qk_refs[dirn]
        c = chunk_of(dirn, r)
        rows = slice(c * c_len, (c + 1) * c_len)
        q = q_ref[hh, rows, :].astype(F32)
        k = k_ref[hh, rows, :].astype(F32)
        b = jnp.dot(tri[dirn], jnp.log(1.0 - k).astype(BF16), preferred_element_type=F32)
        tot = b[end_row[dirn]:end_row[dirn] + 1, :]
        ref = b[mid:mid + 1, :]
        dl = (b - ref) * LOG2E
        qt = q * jnp.exp2(dl)
        kt = k * jnp.exp2(-dl)
        p_ref[dirn, QT, hh, rows, :] = qt.astype(BF16)
        p_ref[dirn, KT, hh, rows, :] = kt.astype(BF16)
        p_ref[dirn, QH, hh, rows, :] = (qt * jnp.exp(ref)).astype(BF16)
        p_ref[dirn, KH, hh, rows, :] = (kt * jnp.exp(tot - ref)).astype(BF16)
        d_ref[dirn, c, hh] = jnp.exp(tot)

    def scores(p_ref, dirn, r, hh):
        v_ref = vo_refs[dirn][0]
        c = chunk_of(dirn, r)
        rows = slice(c * c_len, (c + 1) * c_len)
        sc = lax.dot_general(p_ref[dirn, QT, hh, rows, :], p_ref[dirn, KT, hh, rows, :],
                             nt, preferred_element_type=F32)
        prob = jnp.where(causal[dirn], sc, 0.0).astype(BF16)
        upd = lax.dot_general(v_ref[hh, rows, :], p_ref[dirn, KH, hh, rows, :],
                              tn, preferred_element_type=F32)
        return prob, upd

    def outputs(p_ref, d_ref, dirn, r, hh, prob, upd):
        v_ref, o_ref = vo_refs[dirn]
        c = chunk_of(dirn, r)
        rows = slice(c * c_len, (c + 1) * c_len)
        st = st_ref[dirn, hh]
        o = (jnp.dot(prob, v_ref[hh, rows, :], preferred_element_type=F32)
             + lax.dot_general(p_ref[dirn, QH, hh, rows, :], st.astype(BF16), nt,
                               preferred_element_type=F32))
        o_ref[hh, rows, :] = o.astype(BF16)
        st_ref[dirn, hh] = st * d_ref[dirn, c, hh] + upd

    def step(p_new, d_new, p_old, d_old):
        heads = range(SCAN_HEADS)
        pu = {(dirn, hh): scores(p_old, dirn, 0, hh) for hh in heads for dirn in range(2)}
        for r in range(n_chunks):
            pu_next = {}
            for hh in heads:
                for dirn in range(2):
                    if r + 1 < n_chunks:
                        pu_next[dirn, hh] = scores(p_old, dirn, r + 1, hh)
                    outputs(p_old, d_old, dirn, r, hh, *pu[dirn, hh])
                    prepare(p_new, d_new, dirn, r, hh)
            pu = pu_next

    @pl.when(s % 2 == 0)
    def _():
        step(pa_ref, da_ref, pb_ref, db_ref)

    @pl.when(s % 2 == 1)
    def _():
        step(pb_ref, db_ref, pa_ref, da_ref)


def _scan(a, *, batch, n_lat, n_ctx, e):
    ts = n_ctx
    assert ts % SCAN_CHUNK == 0 and n_lat % ts == 0
    lat_steps = n_lat // ts
    steps = 1 + lat_steps
    n_chunks = ts // SCAN_CHUNK
    assert HEAD_DIM == LANES
    groups = e // (SCAN_HEADS * HEAD_DIM)
    rows = batch * (n_ctx + n_lat)
    assert batch * n_ctx <= PROJ_TILE and PROJ_TILE % ts == 0

    def row_f(lat0):
        return lambda b, s: jnp.where(s == 0, b, lat0 + lat_steps * b + s - 1)

    def row_b(lat0):
        return lambda b, s: jnp.where(s == 0, b, lat0 + lat_steps * b + lat_steps - s)

    in_f, in_b = row_f(PROJ_TILE // ts), row_b(PROJ_TILE // ts)
    out_f, out_b = row_f(batch), row_b(batch)

    ahead = lambda g: jnp.minimum(g, steps - 1)
    behind = lambda g: jnp.maximum(g - 1, 0)

    def spec(stream, rfn, when):
        return pl.BlockSpec((SCAN_HEADS, ts, LANES),
                            lambda b, hg, g: (stream * groups + hg, rfn(b, when(g)), 0))

    kern = functools.partial(_scan_kernel, n_chunks=n_chunks)
    operands = pltpu.VMEM((2, 4, SCAN_HEADS, ts, LANES), BF16)
    decays = pltpu.VMEM((2, n_chunks, SCAN_HEADS, 1, LANES), F32)
    return pl.pallas_call(
        kern,
        grid=(batch, groups, steps + 1),
        in_specs=[spec(0, in_f, ahead), spec(1, in_f, ahead), spec(3, in_f, behind),
                  spec(0, in_b, ahead), spec(2, in_b, ahead), spec(3, in_b, behind)],
        out_specs=[pl.BlockSpec((SCAN_HEADS, ts, LANES), lambda b, hg, g: (hg, out_f(b, behind(g)), 0)),
                   pl.BlockSpec((SCAN_HEADS, ts, LANES), lambda b, hg, g: (hg, out_b(b, behind(g)), 0))],
        out_shape=[jax.ShapeDtypeStruct((e // LANES, rows, LANES), BF16)] * 2,
        scratch_shapes=[pltpu.VMEM((2, SCAN_HEADS, HEAD_DIM, HEAD_DIM), F32),
                        operands, operands, decays, decays],
        compiler_params=_cparams(("arbitrary", "arbitrary", "arbitrary")),
        name="hgrn_scan",
    )(a, a, a, a, a, a)


def _out0_kernel(of_ref, ob_ref, z_ref, ctx_ref, x_ref, mod_ref, hg_ref, w_ref,
                 xo_ref, co_ref, y_ref, *, d, e, tiles_per_batch):
    i = pl.program_id(0)
    heads = e // HEAD_DIM
    group = 4
    mix = None
    for h0 in range(0, heads, group):
        for h in range(h0, h0 + group):
            lanes = slice(h * HEAD_DIM, (h + 1) * HEAD_DIM)
            o = of_ref[h].astype(F32) + ob_ref[h].astype(F32)
            var = jnp.mean(o * o, axis=-1, keepdims=True)
            yn = (o * lax.rsqrt(var + EPS)) * hg_ref[...]
            y_ref[:, lanes] = (yn * z_ref[h].astype(F32)).astype(BF16)
        cols = slice(h0 * HEAD_DIM, (h0 + group) * HEAD_DIM)
        part = jnp.dot(y_ref[:, cols], w_ref[cols, :], preferred_element_type=F32)
        mix = part if mix is None else mix + part

    @pl.when(i == 0)
    def _():
        gate = mod_ref[2:3, 2 * d:3 * d]
        co_ref[...] = ctx_ref[...] + gate * mix

    @pl.when(i > 0)
    def _():
        gate = mod_ref[pl.ds((i - 1) // tiles_per_batch, 1), 2 * d:3 * d]
        xo_ref[...] = x_ref[...] + gate * mix


def _out0(o_f, o_b, a, ctx2d, x2d, mods, hg_norm_g, w_bf, *, n_lat):
    rows_ctx, d = ctx2d.shape
    rows_lat = x2d.shape[0]
    heads = o_f.shape[0]
    e = heads * HEAD_DIM
    tm = ROW_TILE
    assert rows_ctx == tm and PROJ_TILE % tm == 0
    n_tiles = 1 + rows_lat // tm
    kern = functools.partial(_out0_kernel, d=d, e=e, tiles_per_batch=n_lat // tm)
    lat_idx = lambda i: (jnp.maximum(i - 1, 0), 0)
    skip = PROJ_TILE // tm - 1
    src_row = lambda i: jnp.where(i == 0, 0, i + skip)
    return pl.pallas_call(
        kern,
        grid=(n_tiles,),
        in_specs=[
            pl.BlockSpec((heads, tm, HEAD_DIM), lambda i: (0, i, 0)),
            pl.BlockSpec((heads, tm, HEAD_DIM), lambda i: (0, i, 0)),
            pl.BlockSpec((heads, tm, HEAD_DIM), lambda i: (HG_STREAMS - 1, src_row(i), 0)),
            pl.BlockSpec((tm, d), lambda i: (0, 0)),
            pl.BlockSpec((tm, d), lat_idx),
            pl.BlockSpec((None, MOD_ROWS, 3 * d), lambda i: (0, 0, 0)),
            pl.BlockSpec((1, HEAD_DIM), lambda i: (0, 0)),
            pl.BlockSpec((e, d), lambda i: (0, 0)),
        ],
        out_specs=[pl.BlockSpec((tm, d), lat_idx), pl.BlockSpec((tm, d), lambda i: (0, 0))],
        out_shape=[jax.ShapeDtypeStruct((rows_lat, d), F32), jax.ShapeDtypeStruct((rows_ctx, d), F32)],
        scratch_shapes=[pltpu.VMEM((tm, e), BF16)],
        compiler_params=_cparams(("arbitrary",)),
        name="readout0",
    )(o_f, o_b, a, ctx2d, x2d, mods, hg_norm_g, w_bf)


def _proj1_kernel(x_ref, mod_ref, ng_ref, w_ref, u_ref, z_ref, h_ref,
                  *, d, e, tiles_per_batch, sub, n2):
    i = pl.program_id(0)
    j = pl.program_id(1)
    tm = h_ref.shape[0]

    @pl.when(j == 0)
    def _():
        m = mod_ref[pl.ds(i // tiles_per_batch, 1), :]
        h_ref[...] = _modulated_norm(x_ref[...], ng_ref[...], m[:, 0:d], m[:, d:2 * d]).astype(BF16)

    def acc(c):
        return jnp.dot(h_ref[...], w_ref[:, c * sub:(c + 1) * sub], preferred_element_type=F32)

    nsub = e // sub

    @pl.when(j == 0)
    def _():
        for c in range(nsub):
            a = acc(c)
            for n1l in range(tm // n2):
                for m in range(n2 // SUBLANES):
                    src = n1l * n2 + m * SUBLANES
                    for cl in range(sub // LANES):
                        u_ref[m, c * (sub // LANES) + cl, n1l * SUBLANES:(n1l + 1) * SUBLANES, :] = (
                            a[src:src + SUBLANES, cl * LANES:(cl + 1) * LANES])

    @pl.when(j == 1)
    def _():
        for c in range(nsub):
            z_ref[:, c * sub:(c + 1) * sub] = _silu_half(acc(c)).astype(BF16)


def _proj1(x2d, mods, norm_g, w_bf, *, batch, n_lat, layer):
    rows, d = x2d.shape
    e = w_bf.shape[1] // 2
    tm = PROJ_TILE
    n1, n2 = _dft_factors(n_lat)
    q = n2 // SUBLANES
    tpb = n_lat // tm
    rows_blk = (tm // n2) * SUBLANES
    kern = functools.partial(_proj1_kernel, d=d, e=e, tiles_per_batch=tpb, sub=512, n2=n2)
    return pl.pallas_call(
        kern,
        grid=(rows // tm, 2),
        in_specs=[
            pl.BlockSpec((tm, d), lambda i, j: (i, 0)),
            pl.BlockSpec((None, MOD_ROWS, 3 * d), lambda i, j: (layer, 0, 0)),
            pl.BlockSpec((None, 1, d), lambda i, j: (layer, 0, 0)),
            pl.BlockSpec((d, e), lambda i, j: (0, j)),
        ],
        out_specs=[pl.BlockSpec((None, q, e // LANES, rows_blk, LANES),
                                lambda i, j: (i // tpb, 0, 0, i % tpb, 0)),
                   pl.BlockSpec((tm, e), lambda i, j: (i, 0))],
        out_shape=[jax.ShapeDtypeStruct((batch, q, e // LANES, n1 * SUBLANES, LANES), F32),
                   jax.ShapeDtypeStruct((rows, e), BF16)],
        scratch_shapes=[pltpu.VMEM((tm, d), BF16)],
        compiler_params=_cparams(("arbitrary", "arbitrary")),
        name="proj1",
    )(x2d, mods, norm_g, w_bf)


def _dft_factors(n):
    n1 = 128 if n % 128 == 0 else n
    return n1, n // n1


def _dft_tables(n, group_dim):
    n1, n2 = _dft_factors(n)
    a2 = np.arange(n2, dtype=np.float64)[:, None, None]
    k1 = np.arange(n1, dtype=np.float64)[None, :, None]
    m1 = np.arange(n1, dtype=np.float64)[None, None, :]
    ang = -2.0 * np.pi * (a2 * k1 / n + m1 * k1 / n1)
    stage1 = np.concatenate([np.cos(ang), np.sin(ang)], axis=1) / np.sqrt(n1)
    kk = np.arange(n2, dtype=np.float64)
    ang2 = -2.0 * np.pi * np.outer(kk, kk) / n2
    fr, fi = np.cos(ang2) / np.sqrt(n2), np.sin(ang2) / np.sqrt(n2)
    stage2 = np.stack([np.concatenate([fr, fi], axis=0),
                       np.concatenate([-fi, fr], axis=0)])
    cc = np.arange(group_dim, dtype=np.float64)
    ang3 = 2.0 * np.pi * np.outer(cc, cc) / group_dim
    chan = np.concatenate([np.cos(ang3), np.sin(ang3)], axis=0) / np.sqrt(group_dim)
    to = lambda t: jnp.asarray(t, dtype=F32).astype(BF16)
    return to(stage1), to(stage2), to(chan)


def _dft_kernel(m_ref, f_ref, cs_ref, x_ref, y_ref, tr_ref, ti_ref, *, n1, n2, gd, kt_step):
    half = pl.program_id(2)
    q, nlb = x_ref.shape[0], x_ref.shape[1]

    @pl.when(half == 0)
    def _():
        def stage1(m, carry):
            for r in range(SUBLANES):
                rows = pl.ds(r, n1, stride=SUBLANES)
                xr = jnp.concatenate([x_ref[m, lb, rows, :] for lb in range(nlb)], axis=1)
                t = jnp.dot(m_ref[m, r], xr.astype(BF16), preferred_element_type=F32)
                for lb in range(nlb):
                    lanes = slice(lb * LANES, (lb + 1) * LANES)
                    tr_ref[m, lb, rows, :] = t[0:n1, lanes]
                    ti_ref[m, lb, rows, :] = t[n1:2 * n1, lanes]
            return carry

        lax.fori_loop(0, q, stage1, 0)

    width = nlb * LANES

    def gather(ref, kt):
        cols = []
        for j in range(SUBLANES):
            start = pl.multiple_of((kt * SUBLANES + j) * SUBLANES, SUBLANES)
            cols += [ref[:, lb, pl.ds(start, SUBLANES), :].reshape(n2, LANES) for lb in range(nlb)]
        return jnp.concatenate(cols, axis=1).astype(BF16)

    def by_k1(g):
        return jnp.concatenate([g[:, j * width:(j + 1) * width] for j in range(SUBLANES)], axis=0)

    def stage2(kl, carry):
        kt = half * kt_step + kl
        g = (jnp.dot(f_ref[0], gather(tr_ref, kt), preferred_element_type=F32)
             + jnp.dot(f_ref[1], gather(ti_ref, kt), preferred_element_type=F32))
        gr = by_k1(g[0:n2]).astype(BF16)
        gi = by_k1(g[n2:2 * n2]).astype(BF16)
        yg = (jnp.dot(gr, cs_ref[0:gd, :], preferred_element_type=F32)
              + jnp.dot(gi, cs_ref[gd:2 * gd, :], preferred_element_type=F32))
        for j in range(SUBLANES):
            rows = pl.ds(j, n2, stride=SUBLANES)
            for lb in range(nlb):
                y_ref[kl, lb, rows, :] = yg[j * n2:(j + 1) * n2, lb * LANES:(lb + 1) * LANES]
        return carry

    lax.fori_loop(0, kt_step, stage2, 0)


def _dft(u, stage1, stage2, chan):
    batch, q, nlb, rows, _ = u.shape
    n2, two_n1, n1 = stage1.shape
    gd = chan.shape[1]
    glb = gd // LANES
    kt = n1 // SUBLANES
    kt_step = max(kt // 2, 1)
    m4 = stage1.reshape(q, SUBLANES, two_n1, n1)
    scratch = pltpu.VMEM((q, glb, rows, LANES), F32)
    return pl.pallas_call(
        functools.partial(_dft_kernel, n1=n1, n2=n2, gd=gd, kt_step=kt_step),
        grid=(batch, nlb // glb, kt // kt_step),
        in_specs=[pl.BlockSpec((q, SUBLANES, two_n1, n1), lambda b, g, h: (0, 0, 0, 0)),
                  pl.BlockSpec((2, 2 * n2, n2), lambda b, g, h: (0, 0, 0)),
                  pl.BlockSpec((2 * gd, gd), lambda b, g, h: (0, 0)),
                  pl.BlockSpec((None, q, glb, rows, LANES), lambda b, g, h: (b, 0, g, 0, 0))],
        out_specs=pl.BlockSpec((None, kt_step, glb, n2 * SUBLANES, LANES),
                               lambda b, g, h: (b, h, g, 0, 0)),
        out_shape=jax.ShapeDtypeStruct((batch, kt, nlb, n2 * SUBLANES, LANES), F32),
        scratch_shapes=[scratch, scratch],
        compiler_params=_cparams(("arbitrary", "arbitrary", "arbitrary")),
        name="dft",
    )(m4, stage2, chan, u)


def _out1_kernel(y_in_ref, z_ref, x_ref, mod_ref, w_ref, fg_ref, o_ref, y_ref,
                 *, d, n1, tiles_per_batch):
    i = pl.program_id(0)
    kt, nlb = y_in_ref.shape[0], y_in_ref.shape[1]
    k2_per_tile = y_in_ref.shape[2] // SUBLANES
    pair = 2 * SUBLANES
    for k2l in range(k2_per_tile):
        src = slice(k2l * SUBLANES, (k2l + 1) * SUBLANES)
        for kp in range(kt // 2):
            dst = slice(k2l * n1 + kp * pair, k2l * n1 + (kp + 1) * pair)
            yv = jnp.concatenate(
                [jnp.concatenate([y_in_ref[2 * kp + a, lb, src, :] for lb in range(nlb)], axis=1)
                 for a in range(2)], axis=0)
            y_ref[dst, :] = (yv * z_ref[dst, :].astype(F32)).astype(BF16)
    mix = jnp.dot(y_ref[...], w_ref[...], preferred_element_type=F32)
    gate = mod_ref[pl.ds(i // tiles_per_batch, 1), 2 * d:3 * d]
    x = x_ref[...] + gate * mix
    var = jnp.mean(x * x, axis=-1, keepdims=True)
    o_ref[...] = (x * lax.rsqrt(var + EPS)) * fg_ref[...]


def _out1(y_in, zs, x2d, mods, w_bf, final_g, *, n_lat, n1, layer):
    rows, d = x2d.shape
    batch, kt, nlb, _, _ = y_in.shape
    e = nlb * LANES
    tm = ROW_TILE
    assert tm % n1 == 0
    tpb = n_lat // tm
    blk_rows = (tm // n1) * SUBLANES
    kern = functools.partial(_out1_kernel, d=d, n1=n1, tiles_per_batch=tpb)
    return pl.pallas_call(
        kern,
        grid=(rows // tm,),
        in_specs=[
            pl.BlockSpec((None, kt, nlb, blk_rows, LANES), lambda i: (i // tpb, 0, 0, i % tpb, 0)),
            pl.BlockSpec((tm, e), lambda i: (i, 0)),
            pl.BlockSpec((tm, d), lambda i: (i, 0)),
            pl.BlockSpec((None, MOD_ROWS, 3 * d), lambda i: (layer, 0, 0)),
            pl.BlockSpec((e, d), lambda i: (0, 0)),
            pl.BlockSpec((1, d), lambda i: (0, 0)),
        ],
        out_specs=pl.BlockSpec((tm, d), lambda i: (i, 0)),
        out_shape=jax.ShapeDtypeStruct((rows, d), F32),
        scratch_shapes=[pltpu.VMEM((tm, e), BF16)],
        compiler_params=_cparams(("arbitrary",)),
        name="readout1",
    )(y_in, zs, x2d, mods, w_bf, final_g)


def kernel(x, c, ctx, c_ctx, ada_w, ada_b, norm_g, hg_w_in, hg_lb_logits, hg_norm_g, hg_w_out,
           ft_w_in, ft_w_out, final_g):
    batch, n_lat, d = x.shape
    n_ctx = ctx.shape[1]
    depth = ada_w.shape[0]
    e = hg_w_out.shape[1]
    assert depth == 2 and batch == 2 and batch + 1 <= MOD_ROWS

    cv = jnp.concatenate([c, c_ctx[None, :], jnp.zeros((MOD_ROWS - batch - 1, d), F32)], axis=0)
    mods = _ada_table(cv, ada_w, ada_b)
    ng = norm_g.reshape(depth, 1, d)
    x2d = x.reshape(batch * n_lat, d)
    ctx2d = ctx.reshape(batch * n_ctx, d)

    ctx_pad = jnp.pad(ctx2d, ((0, PROJ_TILE - batch * n_ctx), (0, 0)))
    half_cols = jnp.where(jnp.arange(HG_STREAMS * e) // e == 3, 1.0, 0.5).astype(F32)
    a = _proj0(ctx_pad, x2d, mods, ng, (hg_w_in[0] * half_cols).astype(BF16), hg_lb_logits,
                  n_lat=n_lat, lb_index=0)
    o_f, o_b = _scan(a, batch=batch, n_lat=n_lat, n_ctx=n_ctx, e=e)
    x1, _ctx1 = _out0(o_f, o_b, a, ctx2d, x2d, mods, hg_norm_g[0:1], hg_w_out[0].astype(BF16),
                      n_lat=n_lat)

    z_half = jnp.where(jnp.arange(2 * e) < e, 1.0, 0.5).astype(F32)
    u, zs = _proj1(x1, mods, ng, (ft_w_in[0] * z_half).astype(BF16), batch=batch, n_lat=n_lat,
                   layer=1)
    stage1, stage2, chan = _dft_tables(n_lat, e // FT_GROUPS)
    n1 = stage1.shape[2]
    y = _dft(u, stage1, stage2, chan)
    out = _out1(y, zs, x1, mods, ft_w_out[0].astype(BF16), final_g.reshape(1, d),
                n_lat=n_lat, n1=n1, layer=1)
    return out.reshape(batch, n_lat, d)
```

```python
import functools

import numpy as np
import jax
import jax.numpy as jnp
from jax import lax
from jax.experimental import pallas as pl
from jax.experimental.pallas import tpu as pltpu

F32 = jnp.float32
BF16 = jnp.bfloat16

EPS = 1e-6
LOG2E = 1.4426950408889634
HEAD_DIM = 128
HG_STREAMS = 5
FT_GROUPS = 8
SCAN_CHUNK = 64
SCAN_HEADS = 16
ROW_TILE = 512
PROJ_TILE = 1024
MOD_ROWS = 8
SUBLANES = 8
LANES = 128
VMEM_LIMIT = 56 * 1024 * 1024


def _cparams(sem):
    return pltpu.CompilerParams(dimension_semantics=sem, vmem_limit_bytes=VMEM_LIMIT)


def _sigmoid(x):
    return 0.5 * jnp.tanh(0.5 * x) + 0.5


def _silu(x):
    return _silu_half(0.5 * x)


def _silu_half(h):
    return h + h * jnp.tanh(h)


def _modulated_norm(x, g, shift, scale):
    var = jnp.mean(x * x, axis=-1, keepdims=True)
    return (x * lax.rsqrt(var + EPS)) * (g * (1.0 + scale)) + shift


def _ada_kernel(cv_ref, w_ref, b_ref, o_ref):
    a = _silu(cv_ref[...])
    o_ref[...] = jnp.dot(a, w_ref[...], preferred_element_type=F32,
                         precision=lax.Precision.HIGHEST) + b_ref[...]


def _ada_table(cv, ada_w, ada_b):
    depth, d, d3 = ada_w.shape
    tn = 1024
    return pl.pallas_call(
        _ada_kernel,
        grid=(depth, d3 // tn),
        in_specs=[
            pl.BlockSpec((MOD_ROWS, d), lambda l, j: (0, 0)),
            pl.BlockSpec((None, d, tn), lambda l, j: (l, 0, j)),
            pl.BlockSpec((None, 1, tn), lambda l, j: (l, 0, j)),
        ],
        out_specs=pl.BlockSpec((None, MOD_ROWS, tn), lambda l, j: (l, 0, j)),
        out_shape=jax.ShapeDtypeStruct((depth, MOD_ROWS, d3), F32),
        compiler_params=_cparams(("arbitrary", "arbitrary")),
        name="ada_table",
    )(cv, ada_w, ada_b.reshape(depth, 1, d3))


def _proj0_kernel(ctx_ref, x_ref, mod_ref, ng_ref, w_ref, lbl_ref, a_ref, h_ref,
                  *, d, e, tiles_per_batch, lb_index, sub):
    i = pl.program_id(0)
    j = pl.program_id(1)
    tm = h_ref.shape[0]

    @pl.when(j == 0)
    def _():
        is_ctx = i == 0
        m = mod_ref[pl.ds(jnp.where(is_ctx, 2, (i - 1) // tiles_per_batch), 1), :]
        src = jnp.where(is_ctx, ctx_ref[...], x_ref[...])
        h_ref[...] = _modulated_norm(src, ng_ref[...], m[:, 0:d], m[:, d:2 * d]).astype(BF16)

    def acc(c):
        return jnp.dot(h_ref[...], w_ref[:, c * sub:(c + 1) * sub], preferred_element_type=F32)

    nsub = e // sub
    per_sub = sub // LANES

    def put(c, val):
        for cl in range(per_sub):
            a_ref[c * per_sub + cl] = val[:, cl * LANES:(cl + 1) * LANES]

    @pl.when(j == 0)
    def _():
        for c in range(nsub):
            put(c, _silu_half(acc(c)).astype(BF16))

    @pl.when((j == 1) | (j == 2))
    def _():
        dirn = j - 1
        logits = lbl_ref[:, pl.ds(dirn, 1), :]
        mx = jnp.max(logits, axis=0, keepdims=True)
        ex = jnp.exp(logits - mx)
        p = ex / jnp.sum(ex, axis=0, keepdims=True)
        lb_full = jnp.sum(p[0:lb_index + 1], axis=0)
        half_full = 0.5 * (1.0 - lb_full)
        for c in range(nsub):
            half = half_full[:, c * sub:(c + 1) * sub]
            put(c, (half * (1.0 - jnp.tanh(acc(c)))).astype(BF16))

    @pl.when(j == 3)
    def _():
        for c in range(nsub):
            put(c, acc(c).astype(BF16))

    @pl.when(j == 4)
    def _():
        for c in range(nsub):
            put(c, _silu_half(acc(c)).astype(BF16))


def _proj0(ctx2d, x2d, mods, norm_g, w_bf, lb_logits, *, n_lat, lb_index):
    rows_ctx, d = ctx2d.shape
    rows_lat = x2d.shape[0]
    e = w_bf.shape[1] // HG_STREAMS
    tm = PROJ_TILE
    assert rows_ctx == tm and n_lat % tm == 0
    n_tiles = 1 + rows_lat // tm
    rows = tm + rows_lat
    nl = lb_logits.shape[0]
    kern = functools.partial(_proj0_kernel, d=d, e=e, tiles_per_batch=n_lat // tm,
                             lb_index=lb_index, sub=512)
    return pl.pallas_call(
        kern,
        grid=(n_tiles, HG_STREAMS),
        in_specs=[
            pl.BlockSpec((tm, d), lambda i, j: (0, 0)),
            pl.BlockSpec((tm, d), lambda i, j: (jnp.maximum(i - 1, 0), 0)),
            pl.BlockSpec((None, MOD_ROWS, 3 * d), lambda i, j: (0, 0, 0)),
            pl.BlockSpec((None, 1, d), lambda i, j: (0, 0, 0)),
            pl.BlockSpec((d, e), lambda i, j: (0, j)),
            pl.BlockSpec((nl, 2, e), lambda i, j: (0, 0, 0)),
        ],
        out_specs=pl.BlockSpec((e // LANES, tm, LANES), lambda i, j: (j, i, 0)),
        out_shape=jax.ShapeDtypeStruct((HG_STREAMS * (e // LANES), rows, LANES), BF16),
        scratch_shapes=[pltpu.VMEM((tm, d), BF16)],
        compiler_params=_cparams(("arbitrary", "arbitrary")),
        name="proj0",
    )(ctx2d, x2d, mods, norm_g, w_bf, lb_logits)


def _scan_kernel(qf_ref, kf_ref, vf_ref, qb_ref, kb_ref, vb_ref, of_ref, ob_ref,
                 st_ref, pa_ref, pb_ref, da_ref, db_ref, *, n_chunks):
    s = pl.program_id(2)
    c_len = SCAN_CHUNK

    @pl.when(s == 0)
    def _():
        st_ref[...] = jnp.zeros_like(st_ref)
        pb_ref[...] = jnp.zeros_like(pb_ref)
        db_ref[...] = jnp.zeros_like(db_ref)

    row = lax.broadcasted_iota(jnp.int32, (c_len, c_len), 0)
    col = lax.broadcasted_iota(jnp.int32, (c_len, c_len), 1)
    causal = (col <= row, col >= row)
    tri = tuple(m.astype(F32).astype(BF16) for m in causal)
    end_row = (c_len - 1, 0)
    mid = c_len // 2
    qk_refs = ((qf_ref, kf_ref), (qb_ref, kb_ref))
    vo_refs = ((vf_ref, of_ref), (vb_ref, ob_ref))
    nt = (((1,), (1,)), ((), ()))
    tn = (((0,), (0,)), ((), ()))
    QT, KT, QH, KH = range(4)

    def chunk_of(dirn, r):
        return r if dirn == 0 else n_chunks - 1 - r

    def prepare(p_ref, d_ref, dirn, r, hh):
        q_ref, k_ref = qk_refs[dirn]
        c = chunk_of(dirn, r)
        rows = slice(c * c_len, (c + 1) * c_len)
        q = q_ref[hh, rows, :].astype(F32)
        k = k_ref[hh, rows, :].astype(F32)
        b = jnp.dot(tri[dirn], jnp.log(1.0 - k).astype(BF16), preferred_element_type=F32)
        tot = b[end_row[dirn]:end_row[dirn] + 1, :]
        ref = b[mid:mid + 1, :]
        dl = (b - ref) * LOG2E
        qt = q * jnp.exp2(dl)
        kt = k * jnp.exp2(-dl)
        p_ref[dirn, QT, hh, rows, :] = qt.astype(BF16)
        p_ref[dirn, KT, hh, rows, :] = kt.astype(BF16)
        p_ref[dirn, QH, hh, rows, :] = (qt * jnp.exp(ref)).astype(BF16)
        p_ref[dirn, KH, hh, rows, :] = (kt * jnp.exp(tot - ref)).astype(BF16)
        d_ref[dirn, c, hh] = jnp.exp(tot)

    def scores(p_ref, dirn, r, hh):
        v_ref = vo_refs[dirn][0]
        c = chunk_of(dirn, r)
        rows = slice(c * c_len, (c + 1) * c_len)
        sc = lax.dot_general(p_ref[dirn, QT, hh, rows, :], p_ref[dirn, KT, hh, rows, :],
                             nt, preferred_element_type=F32)
        prob = jnp.where(causal[dirn], sc, 0.0).astype(BF16)
        upd = lax.dot_general(v_ref[hh, rows, :], p_ref[dirn, KH, hh, rows, :],
                              tn, preferred_element_type=F32)
        return prob, upd

    def outputs(p_ref, d_ref, dirn, r, hh, prob, upd):
        v_ref, o_ref = vo_refs[dirn]
        c = chunk_of(dirn, r)
        rows = slice(c * c_len, (c + 1) * c_len)
        st = st_ref[dirn, hh]
        o = (jnp.dot(prob, v_ref[hh, rows, :], preferred_element_type=F32)
             + lax.dot_general(p_ref[dirn, QH, hh, rows, :], st.astype(BF16), nt,
                               preferred_element_type=F32))
        o_ref[hh, rows, :] = o.astype(BF16)
        st_ref[dirn, hh] = st * d_ref[dirn, c, hh] + upd

    def step(p_new, d_new, p_old, d_old):
        heads = range(SCAN_HEADS)
        pu = {(dirn, hh): scores(p_old, dirn, 0, hh) for hh in heads for dirn in range(2)}
        for r in range(n_chunks):
            pu_next = {}
            for hh in heads:
                for dirn in range(2):
                    if r + 1 < n_chunks:
                        pu_next[dirn, hh] = scores(p_old, dirn, r + 1, hh)
                    outputs(p_old, d_old, dirn, r, hh, *pu[dirn, hh])
                    prepare(p_new, d_new, dirn, r, hh)
            pu = pu_next

    @pl.when(s % 2 == 0)
    def _():
        step(pa_ref, da_ref, pb_ref, db_ref)

    @pl.when(s % 2 == 1)
    def _():
        step(pb_ref, db_ref, pa_ref, da_ref)


def _scan(a, *, batch, n_lat, n_ctx, e):
    ts = n_ctx
    assert ts % SCAN_CHUNK == 0 and n_lat % ts == 0
    lat_steps = n_lat // ts
    steps = 1 + lat_steps
    n_chunks = ts // SCAN_CHUNK
    assert HEAD_DIM == LANES
    groups = e // (SCAN_HEADS * HEAD_DIM)
    rows = batch * (n_ctx + n_lat)
    assert batch * n_ctx <= PROJ_TILE and PROJ_TILE % ts == 0

    def row_f(lat0):
        return lambda b, s: jnp.where(s == 0, b, lat0 + lat_steps * b + s - 1)

    def row_b(lat0):
        return lambda b, s: jnp.where(s == 0, b, lat0 + lat_steps * b + lat_steps - s)

    in_f, in_b = row_f(PROJ_TILE // ts), row_b(PROJ_TILE // ts)
    out_f, out_b = row_f(batch), row_b(batch)

    ahead = lambda g: jnp.minimum(g, steps - 1)
    behind = lambda g: jnp.maximum(g - 1, 0)

    def spec(stream, rfn, when):
        return pl.BlockSpec((SCAN_HEADS, ts, LANES),
                            lambda b, hg, g: (stream * groups + hg, rfn(b, when(g)), 0))

    kern = functools.partial(_scan_kernel, n_chunks=n_chunks)
    operands = pltpu.VMEM((2, 4, SCAN_HEADS, ts, LANES), BF16)
    decays = pltpu.VMEM((2, n_chunks, SCAN_HEADS, 1, LANES), F32)
    return pl.pallas_call(
        kern,
        grid=(batch, groups, steps + 1),
        in_specs=[spec(0, in_f, ahead), spec(1, in_f, ahead), spec(3, in_f, behind),
                  spec(0, in_b, ahead), spec(2, in_b, ahead), spec(3, in_b, behind)],
        out_specs=[pl.BlockSpec((SCAN_HEADS, ts, LANES), lambda b, hg, g: (hg, out_f(b, behind(g)), 0)),
                   pl.BlockSpec((SCAN_HEADS, ts, LANES), lambda b, hg, g: (hg, out_b(b, behind(g)), 0))],
        out_shape=[jax.ShapeDtypeStruct((e // LANES, rows, LANES), BF16)] * 2,
        scratch_shapes=[pltpu.VMEM((2, SCAN_HEADS, HEAD_DIM, HEAD_DIM), F32),
                        operands, operands, decays, decays],
        compiler_params=_cparams(("arbitrary", "arbitrary", "arbitrary")),
        name="hgrn_scan",
    )(a, a, a, a, a, a)


def _out0_kernel(of_ref, ob_ref, z_ref, ctx_ref, x_ref, mod_ref, hg_ref, w_ref, mod1_ref, ng1_ref,
                 xo_ref, co_ref, ho_ref, y_ref, *, d, e, tiles_per_batch):
    i = pl.program_id(0)
    heads = e // HEAD_DIM
    group = 4
    mix = None
    for h0 in range(0, heads, group):
        for h in range(h0, h0 + group):
            lanes = slice(h * HEAD_DIM, (h + 1) * HEAD_DIM)
            o = of_ref[h].astype(F32) + ob_ref[h].astype(F32)
            var = jnp.mean(o * o, axis=-1, keepdims=True)
            yn = (o * lax.rsqrt(var + EPS)) * hg_ref[...]
            y_ref[:, lanes] = (yn * z_ref[h].astype(F32)).astype(BF16)
        cols = slice(h0 * HEAD_DIM, (h0 + group) * HEAD_DIM)
        part = jnp.dot(y_ref[:, cols], w_ref[cols, :], preferred_element_type=F32)
        mix = part if mix is None else mix + part

    @pl.when(i == 0)
    def _():
        gate = mod_ref[2:3, 2 * d:3 * d]
        co_ref[...] = ctx_ref[...] + gate * mix

    @pl.when(i > 0)
    def _():
        row = (i - 1) // tiles_per_batch
        x1 = x_ref[...] + mod_ref[pl.ds(row, 1), 2 * d:3 * d] * mix
        xo_ref[...] = x1
        m1 = mod1_ref[pl.ds(row, 1), :]
        ho_ref[...] = _modulated_norm(x1, ng1_ref[...], m1[:, 0:d], m1[:, d:2 * d]).astype(BF16)


def _out0(o_f, o_b, a, ctx2d, x2d, mods, hg_norm_g, w_bf, norm_g, *, n_lat):
    rows_ctx, d = ctx2d.shape
    rows_lat = x2d.shape[0]
    heads = o_f.shape[0]
    e = heads * HEAD_DIM
    tm = ROW_TILE
    assert rows_ctx == tm and PROJ_TILE % tm == 0
    n_tiles = 1 + rows_lat // tm
    kern = functools.partial(_out0_kernel, d=d, e=e, tiles_per_batch=n_lat // tm)
    lat_idx = lambda i: (jnp.maximum(i - 1, 0), 0)
    skip = PROJ_TILE // tm - 1
    src_row = lambda i: jnp.where(i == 0, 0, i + skip)
    return pl.pallas_call(
        kern,
        grid=(n_tiles,),
        in_specs=[
            pl.BlockSpec((heads, tm, HEAD_DIM), lambda i: (0, i, 0)),
            pl.BlockSpec((heads, tm, HEAD_DIM), lambda i: (0, i, 0)),
            pl.BlockSpec((heads, tm, HEAD_DIM), lambda i: (HG_STREAMS - 1, src_row(i), 0)),
            pl.BlockSpec((tm, d), lambda i: (0, 0)),
            pl.BlockSpec((tm, d), lat_idx),
            pl.BlockSpec((None, MOD_ROWS, 3 * d), lambda i: (0, 0, 0)),
            pl.BlockSpec((1, HEAD_DIM), lambda i: (0, 0)),
            pl.BlockSpec((e, d), lambda i: (0, 0)),
            pl.BlockSpec((None, MOD_ROWS, 3 * d), lambda i: (1, 0, 0)),
            pl.BlockSpec((None, 1, d), lambda i: (1, 0, 0)),
        ],
        out_specs=[pl.BlockSpec((tm, d), lat_idx), pl.BlockSpec((tm, d), lambda i: (0, 0)),
                   pl.BlockSpec((tm, d), lat_idx)],
        out_shape=[jax.ShapeDtypeStruct((rows_lat, d), F32), jax.ShapeDtypeStruct((rows_ctx, d), F32),
                   jax.ShapeDtypeStruct((rows_lat, d), BF16)],
        scratch_shapes=[pltpu.VMEM((tm, e), BF16)],
        compiler_params=_cparams(("arbitrary",)),
        name="readout0",
    )(o_f, o_b, a, ctx2d, x2d, mods, hg_norm_g, w_bf, mods, norm_g)


def _proj1_kernel(h_ref, w_ref, u_ref, z_ref, *, e, sub, n2):
    j = pl.program_id(1)
    tm = h_ref.shape[0]

    def acc(c):
        return jnp.dot(h_ref[...], w_ref[:, c * sub:(c + 1) * sub], preferred_element_type=F32)

    nsub = e // sub

    @pl.when(j == 0)
    def _():
        for c in range(nsub):
            a = acc(c)
            for n1l in range(tm // n2):
                for m in range(n2 // SUBLANES):
                    src = n1l * n2 + m * SUBLANES
                    for cl in range(sub // LANES):
                        u_ref[m, c * (sub // LANES) + cl, n1l * SUBLANES:(n1l + 1) * SUBLANES, :] = (
                            a[src:src + SUBLANES, cl * LANES:(cl + 1) * LANES])

    @pl.when(j == 1)
    def _():
        for c in range(nsub):
            z_ref[:, c * sub:(c + 1) * sub] = _silu_half(acc(c)).astype(BF16)


def _proj1(h2d, w_bf, *, batch, n_lat):
    rows, d = h2d.shape
    e = w_bf.shape[1] // 2
    tm = PROJ_TILE
    n1, n2 = _dft_factors(n_lat)
    q = n2 // SUBLANES
    tpb = n_lat // tm
    rows_blk = (tm // n2) * SUBLANES
    kern = functools.partial(_proj1_kernel, e=e, sub=512, n2=n2)
    return pl.pallas_call(
        kern,
        grid=(rows // tm, 2),
        in_specs=[
            pl.BlockSpec((tm, d), lambda i, j: (i, 0)),
            pl.BlockSpec((d, e), lambda i, j: (0, j)),
        ],
        out_specs=[pl.BlockSpec((None, q, e // LANES, rows_blk, LANES),
                                lambda i, j: (i // tpb, 0, 0, i % tpb, 0)),
                   pl.BlockSpec((tm, e), lambda i, j: (i, 0))],
        out_shape=[jax.ShapeDtypeStruct((batch, q, e // LANES, n1 * SUBLANES, LANES), F32),
                   jax.ShapeDtypeStruct((rows, e), BF16)],
        compiler_params=_cparams(("arbitrary", "arbitrary")),
        name="proj1",
    )(h2d, w_bf)


def _dft_factors(n):
    n1 = 128 if n % 128 == 0 else n
    return n1, n // n1


def _dft_tables(n, group_dim):
    n1, n2 = _dft_factors(n)
    a2 = np.arange(n2, dtype=np.float64)[:, None, None]
    k1 = np.arange(n1, dtype=np.float64)[None, :, None]
    m1 = np.arange(n1, dtype=np.float64)[None, None, :]
    ang = -2.0 * np.pi * (a2 * k1 / n + m1 * k1 / n1)
    stage1 = np.concatenate([np.cos(ang), np.sin(ang)], axis=1) / np.sqrt(n1)
    kk = np.arange(n2, dtype=np.float64)
    ang2 = -2.0 * np.pi * np.outer(kk, kk) / n2
    fr, fi = np.cos(ang2) / np.sqrt(n2), np.sin(ang2) / np.sqrt(n2)
    stage2 = np.stack([np.concatenate([fr, fi], axis=0),
                       np.concatenate([-fi, fr], axis=0)])
    cc = np.arange(group_dim, dtype=np.float64)
    ang3 = 2.0 * np.pi * np.outer(cc, cc) / group_dim
    chan = np.concatenate([np.cos(ang3), np.sin(ang3)], axis=0) / np.sqrt(group_dim)
    to = lambda t: jnp.asarray(t, dtype=F32).astype(BF16)
    return to(stage1), to(stage2), to(chan)


def _dft_kernel(m_ref, f_ref, cs_ref, x_ref, y_ref, tr_ref, ti_ref, *, n1, n2, gd, kt_step):
    half = pl.program_id(2)
    q, nlb = x_ref.shape[0], x_ref.shape[1]

    @pl.when(half == 0)
    def _():
        def stage1(m, carry):
            for r in range(SUBLANES):
                rows = pl.ds(r, n1, stride=SUBLANES)
                xr = jnp.concatenate([x_ref[m, lb, rows, :] for lb in range(nlb)], axis=1)
                t = jnp.dot(m_ref[m, r], xr.astype(BF16), preferred_element_type=F32)
                for lb in range(nlb):
                    lanes = slice(lb * LANES, (lb + 1) * LANES)
                    tr_ref[m, lb, rows, :] = t[0:n1, lanes]
                    ti_ref[m, lb, rows, :] = t[n1:2 * n1, lanes]
            return carry

        lax.fori_loop(0, q, stage1, 0)

    width = nlb * LANES

    def gather(ref, kt):
        cols = []
        for j in range(SUBLANES):
            start = pl.multiple_of((kt * SUBLANES + j) * SUBLANES, SUBLANES)
            cols += [ref[:, lb, pl.ds(start, SUBLANES), :].reshape(n2, LANES) for lb in range(nlb)]
        return jnp.concatenate(cols, axis=1).astype(BF16)

    def by_k1(g):
        return jnp.concatenate([g[:, j * width:(j + 1) * width] for j in range(SUBLANES)], axis=0)

    def stage2(kl, carry):
        kt = half * kt_step + kl
        g = (jnp.dot(f_ref[0], gather(tr_ref, kt), preferred_element_type=F32)
             + jnp.dot(f_ref[1], gather(ti_ref, kt), preferred_element_type=F32))
        gr = by_k1(g[0:n2]).astype(BF16)
        gi = by_k1(g[n2:2 * n2]).astype(BF16)
        yg = (jnp.dot(gr, cs_ref[0:gd, :], preferred_element_type=F32)
              + jnp.dot(gi, cs_ref[gd:2 * gd, :], preferred_element_type=F32))
        for j in range(SUBLANES):
            rows = pl.ds(j, n2, stride=SUBLANES)
            for lb in range(nlb):
                y_ref[kl, lb, rows, :] = yg[j * n2:(j + 1) * n2, lb * LANES:(lb + 1) * LANES]
        return carry

    lax.fori_loop(0, kt_step, stage2, 0)


def _dft(u, stage1, stage2, chan):
    batch, q, nlb, rows, _ = u.shape
    n2, two_n1, n1 = stage1.shape
    gd = chan.shape[1]
    glb = gd // LANES
    kt = n1 // SUBLANES
    kt_step = max(kt // 2, 1)
    m4 = stage1.reshape(q, SUBLANES, two_n1, n1)
    scratch = pltpu.VMEM((q, glb, rows, LANES), F32)
    return pl.pallas_call(
        functools.partial(_dft_kernel, n1=n1, n2=n2, gd=gd, kt_step=kt_step),
        grid=(batch, nlb // glb, kt // kt_step),
        in_specs=[pl.BlockSpec((q, SUBLANES, two_n1, n1), lambda b, g, h: (0, 0, 0, 0)),
                  pl.BlockSpec((2, 2 * n2, n2), lambda b, g, h: (0, 0, 0)),
                  pl.BlockSpec((2 * gd, gd), lambda b, g, h: (0, 0)),
                  pl.BlockSpec((None, q, glb, rows, LANES), lambda b, g, h: (b, 0, g, 0, 0))],
        out_specs=pl.BlockSpec((None, kt_step, glb, n2 * SUBLANES, LANES),
                               lambda b, g, h: (b, h, g, 0, 0)),
        out_shape=jax.ShapeDtypeStruct((batch, kt, nlb, n2 * SUBLANES, LANES), F32),
        scratch_shapes=[scratch, scratch],
        compiler_params=_cparams(("arbitrary", "arbitrary", "arbitrary")),
        name="dft",
    )(m4, stage2, chan, u)


def _out1_kernel(y_in_ref, z_ref, x_ref, mod_ref, w_ref, fg_ref, o_ref, y_ref,
                 *, d, n1, tiles_per_batch):
    i = pl.program_id(0)
    kt, nlb = y_in_ref.shape[0], y_in_ref.shape[1]
    k2_per_tile = y_in_ref.shape[2] // SUBLANES
    pair = 2 * SUBLANES
    for k2l in range(k2_per_tile):
        src = slice(k2l * SUBLANES, (k2l + 1) * SUBLANES)
        for kp in range(kt // 2):
            dst = slice(k2l * n1 + kp * pair, k2l * n1 + (kp + 1) * pair)
            yv = jnp.concatenate(
                [jnp.concatenate([y_in_ref[2 * kp + a, lb, src, :] for lb in range(nlb)], axis=1)
                 for a in range(2)], axis=0)
            y_ref[dst, :] = (yv * z_ref[dst, :].astype(F32)).astype(BF16)
    mix = jnp.dot(y_ref[...], w_ref[...], preferred_element_type=F32)
    gate = mod_ref[pl.ds(i // tiles_per_batch, 1), 2 * d:3 * d]
    x = x_ref[...] + gate * mix
    var = jnp.mean(x * x, axis=-1, keepdims=True)
    o_ref[...] = (x * lax.rsqrt(var + EPS)) * fg_ref[...]


def _out1(y_in, zs, x2d, mods, w_bf, final_g, *, n_lat, n1, layer):
    rows, d = x2d.shape
    batch, kt, nlb, _, _ = y_in.shape
    e = nlb * LANES
    tm = ROW_TILE
    assert tm % n1 == 0
    tpb = n_lat // tm
    blk_rows = (tm // n1) * SUBLANES
    kern = functools.partial(_out1_kernel, d=d, n1=n1, tiles_per_batch=tpb)
    return pl.pallas_call(
        kern,
        grid=(rows // tm,),
        in_specs=[
            pl.BlockSpec((None, kt, nlb, blk_rows, LANES), lambda i: (i // tpb, 0, 0, i % tpb, 0)),
            pl.BlockSpec((tm, e), lambda i: (i, 0)),
            pl.BlockSpec((tm, d), lambda i: (i, 0)),
            pl.BlockSpec((None, MOD_ROWS, 3 * d), lambda i: (layer, 0, 0)),
            pl.BlockSpec((e, d), lambda i: (0, 0)),
            pl.BlockSpec((1, d), lambda i: (0, 0)),
        ],
        out_specs=pl.BlockSpec((tm, d), lambda i: (i, 0)),
        out_shape=jax.ShapeDtypeStruct((rows, d), F32),
        scratch_shapes=[pltpu.VMEM((tm, e), BF16)],
        compiler_params=_cparams(("arbitrary",)),
        name="readout1",
    )(y_in, zs, x2d, mods, w_bf, final_g)


def kernel(x, c, ctx, c_ctx, ada_w, ada_b, norm_g, hg_w_in, hg_lb_logits, hg_norm_g, hg_w_out,
           ft_w_in, ft_w_out, final_g):
    batch, n_lat, d = x.shape
    n_ctx = ctx.shape[1]
    depth = ada_w.shape[0]
    e = hg_w_out.shape[1]
    assert depth == 2 and batch == 2 and batch + 1 <= MOD_ROWS

    cv = jnp.concatenate([c, c_ctx[None, :], jnp.zeros((MOD_ROWS - batch - 1, d), F32)], axis=0)
    mods = _ada_table(cv, ada_w, ada_b)
    ng = norm_g.reshape(depth, 1, d)
    x2d = x.reshape(batch * n_lat, d)
    ctx2d = ctx.reshape(batch * n_ctx, d)

    ctx_pad = jnp.pad(ctx2d, ((0, PROJ_TILE - batch * n_ctx), (0, 0)))
    half_cols = jnp.where(jnp.arange(HG_STREAMS * e) // e == 3, 1.0, 0.5).astype(F32)
    a = _proj0(ctx_pad, x2d, mods, ng, (hg_w_in[0] * half_cols).astype(BF16), hg_lb_logits,
                  n_lat=n_lat, lb_index=0)
    o_f, o_b = _scan(a, batch=batch, n_lat=n_lat, n_ctx=n_ctx, e=e)
    x1, _ctx1, h1 = _out0(o_f, o_b, a, ctx2d, x2d, mods, hg_norm_g[0:1], hg_w_out[0].astype(BF16), ng,
                          n_lat=n_lat)

    z_half = jnp.where(jnp.arange(2 * e) < e, 1.0, 0.5).astype(F32)
    u, zs = _proj1(h1, (ft_w_in[0] * z_half).astype(BF16), batch=batch, n_lat=n_lat)
    stage1, stage2, chan = _dft_tables(n_lat, e // FT_GROUPS)
    n1 = stage1.shape[2]
    y = _dft(u, stage1, stage2, chan)
    out = _out1(y, zs, x1, mods, ft_w_out[0].astype(BF16), final_g.reshape(1, d),
                n_lat=n_lat, n1=n1, layer=1)
    return out.reshape(batch, n_lat, d)
```

```python
import functools

import numpy as np
import jax
import jax.numpy as jnp
from jax import lax
from jax.experimental import pallas as pl
from jax.experimental.pallas import tpu as pltpu

F32 = jnp.float32
BF16 = jnp.bfloat16

EPS = 1e-6
LOG2E = 1.4426950408889634
SPAN_LIMIT = 100.0
HEAD_DIM = 128
HG_STREAMS = 5
FT_GROUPS = 8
SCAN_CHUNK = 64
SCAN_HEADS = 16
ROW_TILE = 512
PROJ_TILE = 1024
MOD_ROWS = 8
SUBLANES = 8
LANES = 128
VMEM_LIMIT = 56 * 1024 * 1024


def _cparams(sem):
    return pltpu.CompilerParams(dimension_semantics=sem, vmem_limit_bytes=VMEM_LIMIT)


def _sigmoid(x):
    return 0.5 * jnp.tanh(0.5 * x) + 0.5


def _silu(x):
    return _silu_half(0.5 * x)


def _silu_half(h):
    return h + h * jnp.tanh(h)


def _modulated_norm(x, g, shift, scale):
    var = jnp.mean(x * x, axis=-1, keepdims=True)
    return (x * lax.rsqrt(var + EPS)) * (g * (1.0 + scale)) + shift


def _ada_kernel(cv_ref, w_ref, b_ref, o_ref):
    a = _silu(cv_ref[...])
    o_ref[...] = jnp.dot(a, w_ref[...], preferred_element_type=F32,
                         precision=lax.Precision.HIGHEST) + b_ref[...]


def _ada_table(cv, ada_w, ada_b):
    depth, d, d3 = ada_w.shape
    tn = 1024
    return pl.pallas_call(
        _ada_kernel,
        grid=(depth, d3 // tn),
        in_specs=[
            pl.BlockSpec((MOD_ROWS, d), lambda l, j: (0, 0)),
            pl.BlockSpec((None, d, tn), lambda l, j: (l, 0, j)),
            pl.BlockSpec((None, 1, tn), lambda l, j: (l, 0, j)),
        ],
        out_specs=pl.BlockSpec((None, MOD_ROWS, tn), lambda l, j: (l, 0, j)),
        out_shape=jax.ShapeDtypeStruct((depth, MOD_ROWS, d3), F32),
        compiler_params=_cparams(("arbitrary", "arbitrary")),
        name="ada_table",
    )(cv, ada_w, ada_b.reshape(depth, 1, d3))


def _proj0_kernel(ctx_ref, x_ref, mod_ref, ng_ref, w_ref, lbl_ref, a_ref, h_ref,
                  *, d, e, tiles_per_batch, lb_index, sub):
    i = pl.program_id(0)
    j = pl.program_id(1)
    tm = h_ref.shape[0]

    @pl.when(j == 0)
    def _():
        is_ctx = i == 0
        m = mod_ref[pl.ds(jnp.where(is_ctx, 2, (i - 1) // tiles_per_batch), 1), :]
        src = jnp.where(is_ctx, ctx_ref[...], x_ref[...])
        h_ref[...] = _modulated_norm(src, ng_ref[...], m[:, 0:d], m[:, d:2 * d]).astype(BF16)

    def acc(c):
        return jnp.dot(h_ref[...], w_ref[:, c * sub:(c + 1) * sub], preferred_element_type=F32)

    nsub = e // sub
    per_sub = sub // LANES

    def put(c, val):
        for cl in range(per_sub):
            a_ref[c * per_sub + cl] = val[:, cl * LANES:(cl + 1) * LANES]

    @pl.when(j == 0)
    def _():
        for c in range(nsub):
            put(c, _silu_half(acc(c)).astype(BF16))

    @pl.when((j == 1) | (j == 2))
    def _():
        dirn = j - 1
        logits = lbl_ref[:, pl.ds(dirn, 1), :]
        mx = jnp.max(logits, axis=0, keepdims=True)
        ex = jnp.exp(logits - mx)
        p = ex / jnp.sum(ex, axis=0, keepdims=True)
        lb_full = jnp.sum(p[0:lb_index + 1], axis=0)
        half_full = 0.5 * (1.0 - lb_full)
        for c in range(nsub):
            half = half_full[:, c * sub:(c + 1) * sub]
            put(c, (half * (1.0 - jnp.tanh(acc(c)))).astype(BF16))

    @pl.when(j == 3)
    def _():
        for c in range(nsub):
            put(c, acc(c).astype(BF16))

    @pl.when(j == 4)
    def _():
        for c in range(nsub):
            put(c, _silu_half(acc(c)).astype(BF16))


def _proj0(ctx2d, x2d, mods, norm_g, w_bf, lb_logits, *, n_lat, lb_index):
    rows_ctx, d = ctx2d.shape
    rows_lat = x2d.shape[0]
    e = w_bf.shape[1] // HG_STREAMS
    tm = PROJ_TILE
    assert rows_ctx == tm and n_lat % tm == 0
    n_tiles = 1 + rows_lat // tm
    rows = tm + rows_lat
    nl = lb_logits.shape[0]
    kern = functools.partial(_proj0_kernel, d=d, e=e, tiles_per_batch=n_lat // tm,
                             lb_index=lb_index, sub=512)
    return pl.pallas_call(
        kern,
        grid=(n_tiles, HG_STREAMS),
        in_specs=[
            pl.BlockSpec((tm, d), lambda i, j: (0, 0)),
            pl.BlockSpec((tm, d), lambda i, j: (jnp.maximum(i - 1, 0), 0)),
            pl.BlockSpec((None, MOD_ROWS, 3 * d), lambda i, j: (0, 0, 0)),
            pl.BlockSpec((None, 1, d), lambda i, j: (0, 0, 0)),
            pl.BlockSpec((d, e), lambda i, j: (0, j)),
            pl.BlockSpec((nl, 2, e), lambda i, j: (0, 0, 0)),
        ],
        out_specs=pl.BlockSpec((e // LANES, tm, LANES), lambda i, j: (j, i, 0)),
        out_shape=jax.ShapeDtypeStruct((HG_STREAMS * (e // LANES), rows, LANES), BF16),
        scratch_shapes=[pltpu.VMEM((tm, d), BF16)],
        compiler_params=_cparams(("arbitrary", "arbitrary")),
        name="proj0",
    )(ctx2d, x2d, mods, norm_g, w_bf, lb_logits)


def _scan_kernel(qf_ref, kf_ref, vf_ref, qb_ref, kb_ref, vb_ref, of_ref, ob_ref, span_ref,
                 st_ref, pa_ref, pb_ref, da_ref, db_ref, sp_ref, *, n_chunks):
    s = pl.program_id(2)
    c_len = SCAN_CHUNK

    @pl.when(s == 0)
    def _():
        st_ref[...] = jnp.zeros_like(st_ref)
        pb_ref[...] = jnp.zeros_like(pb_ref)
        db_ref[...] = jnp.zeros_like(db_ref)
        sp_ref[...] = jnp.zeros_like(sp_ref)

    row = lax.broadcasted_iota(jnp.int32, (c_len, c_len), 0)
    col = lax.broadcasted_iota(jnp.int32, (c_len, c_len), 1)
    causal = (col <= row, col >= row)
    tri = tuple(m.astype(F32).astype(BF16) for m in causal)
    end_row = (c_len - 1, 0)
    mid = c_len // 2
    qk_refs = ((qf_ref, kf_ref), (qb_ref, kb_ref))
    vo_refs = ((vf_ref, of_ref), (vb_ref, ob_ref))
    nt = (((1,), (1,)), ((), ()))
    tn = (((0,), (0,)), ((), ()))
    QT, KT, QH, KH = range(4)

    def chunk_of(dirn, r):
        return r if dirn == 0 else n_chunks - 1 - r

    def prepare(p_ref, d_ref, dirn, r, hh):
        q_ref, k_ref = qk_refs[dirn]
        c = chunk_of(dirn, r)
        rows = slice(c * c_len, (c + 1) * c_len)
        q = q_ref[hh, rows, :].astype(F32)
        k = k_ref[hh, rows, :].astype(F32)
        b = jnp.dot(tri[dirn], jnp.log(1.0 - k).astype(BF16), preferred_element_type=F32)
        tot = b[end_row[dirn]:end_row[dirn] + 1, :]
        ref = b[mid:mid + 1, :]
        dl = (b - ref) * LOG2E
        ends = jnp.maximum(jnp.abs(dl[0:1, :]), jnp.abs(dl[c_len - 1:c_len, :]))
        sp_ref[dirn, hh] = jnp.maximum(sp_ref[dirn, hh], ends)
        qt = q * jnp.exp2(dl)
        kt = k * jnp.exp2(-dl)
        p_ref[dirn, QT, hh, rows, :] = qt.astype(BF16)
        p_ref[dirn, KT, hh, rows, :] = kt.astype(BF16)
        p_ref[dirn, QH, hh, rows, :] = (qt * jnp.exp(ref)).astype(BF16)
        p_ref[dirn, KH, hh, rows, :] = (kt * jnp.exp(tot - ref)).astype(BF16)
        d_ref[dirn, c, hh] = jnp.exp(tot)

    def scores(p_ref, dirn, r, hh):
        v_ref = vo_refs[dirn][0]
        c = chunk_of(dirn, r)
        rows = slice(c * c_len, (c + 1) * c_len)
        sc = lax.dot_general(p_ref[dirn, QT, hh, rows, :], p_ref[dirn, KT, hh, rows, :],
                             nt, preferred_element_type=F32)
        prob = jnp.where(causal[dirn], sc, 0.0).astype(BF16)
        upd = lax.dot_general(v_ref[hh, rows, :], p_ref[dirn, KH, hh, rows, :],
                              tn, preferred_element_type=F32)
        return prob, upd

    def outputs(p_ref, d_ref, dirn, r, hh, prob, upd):
        v_ref, o_ref = vo_refs[dirn]
        c = chunk_of(dirn, r)
        rows = slice(c * c_len, (c + 1) * c_len)
        st = st_ref[dirn, hh]
        o = (jnp.dot(prob, v_ref[hh, rows, :], preferred_element_type=F32)
             + lax.dot_general(p_ref[dirn, QH, hh, rows, :], st.astype(BF16), nt,
                               preferred_element_type=F32))
        o_ref[hh, rows, :] = o.astype(BF16)
        st_ref[dirn, hh] = st * d_ref[dirn, c, hh] + upd

    def step(p_new, d_new, p_old, d_old):
        heads = range(SCAN_HEADS)
        pu = {(dirn, hh): scores(p_old, dirn, 0, hh) for hh in heads for dirn in range(2)}
        for r in range(n_chunks):
            pu_next = {}
            for hh in heads:
                for dirn in range(2):
                    if r + 1 < n_chunks:
                        pu_next[dirn, hh] = scores(p_old, dirn, r + 1, hh)
                    outputs(p_old, d_old, dirn, r, hh, *pu[dirn, hh])
                    prepare(p_new, d_new, dirn, r, hh)
            pu = pu_next

    @pl.when(s % 2 == 0)
    def _():
        step(pa_ref, da_ref, pb_ref, db_ref)

    @pl.when(s % 2 == 1)
    def _():
        step(pb_ref, db_ref, pa_ref, da_ref)

    @pl.when(s == pl.num_programs(2) - 1)
    def _():
        span_ref[...] = sp_ref[...]


def _scan(a, *, batch, n_lat, n_ctx, e):
    ts = n_ctx
    assert ts % SCAN_CHUNK == 0 and n_lat % ts == 0
    lat_steps = n_lat // ts
    steps = 1 + lat_steps
    n_chunks = ts // SCAN_CHUNK
    assert HEAD_DIM == LANES
    groups = e // (SCAN_HEADS * HEAD_DIM)
    rows = batch * (n_ctx + n_lat)
    assert batch * n_ctx <= PROJ_TILE and PROJ_TILE % ts == 0

    def row_f(lat0):
        return lambda b, s: jnp.where(s == 0, b, lat0 + lat_steps * b + s - 1)

    def row_b(lat0):
        return lambda b, s: jnp.where(s == 0, b, lat0 + lat_steps * b + lat_steps - s)

    in_f, in_b = row_f(PROJ_TILE // ts), row_b(PROJ_TILE // ts)
    out_f, out_b = row_f(batch), row_b(batch)

    ahead = lambda g: jnp.minimum(g, steps - 1)
    behind = lambda g: jnp.maximum(g - 1, 0)

    def spec(stream, rfn, when):
        return pl.BlockSpec((SCAN_HEADS, ts, LANES),
                            lambda b, hg, g: (stream * groups + hg, rfn(b, when(g)), 0))

    kern = functools.partial(_scan_kernel, n_chunks=n_chunks)
    operands = pltpu.VMEM((2, 4, SCAN_HEADS, ts, LANES), BF16)
    decays = pltpu.VMEM((2, n_chunks, SCAN_HEADS, 1, LANES), F32)
    return pl.pallas_call(
        kern,
        grid=(batch, groups, steps + 1),
        in_specs=[spec(0, in_f, ahead), spec(1, in_f, ahead), spec(3, in_f, behind),
                  spec(0, in_b, ahead), spec(2, in_b, ahead), spec(3, in_b, behind)],
        out_specs=[pl.BlockSpec((SCAN_HEADS, ts, LANES), lambda b, hg, g: (hg, out_f(b, behind(g)), 0)),
                   pl.BlockSpec((SCAN_HEADS, ts, LANES), lambda b, hg, g: (hg, out_b(b, behind(g)), 0)),
                   pl.BlockSpec((None, None, 2, SCAN_HEADS, 1, LANES), lambda b, hg, g: (b, hg, 0, 0, 0, 0))],
        out_shape=[jax.ShapeDtypeStruct((e // LANES, rows, LANES), BF16)] * 2
        + [jax.ShapeDtypeStruct((batch, groups, 2, SCAN_HEADS, 1, LANES), F32)],
        scratch_shapes=[pltpu.VMEM((2, SCAN_HEADS, HEAD_DIM, HEAD_DIM), F32),
                        operands, operands, decays, decays,
                        pltpu.VMEM((2, SCAN_HEADS, 1, LANES), F32)],
        compiler_params=_cparams(("arbitrary", "arbitrary", "arbitrary")),
        name="hgrn_scan",
    )(a, a, a, a, a, a)


def _scan_exact_kernel(q_ref, k_ref, v_ref, o_ref, st_ref, *, n_chunks, reverse):
    c_len = SCAN_CHUNK

    @pl.when(pl.program_id(2) == 0)
    def _():
        st_ref[...] = jnp.zeros_like(st_ref)

    lane = lax.broadcasted_iota(jnp.int32, (1, c_len), 1)

    def chunk(ci, carry):
        c = (n_chunks - 1 - ci) if reverse else ci
        rows = pl.ds(pl.multiple_of(c * c_len, c_len), c_len)
        q = q_ref[rows, :].astype(F32)
        k = k_ref[rows, :].astype(F32)
        f = 1.0 - k
        vt = v_ref[rows, :].astype(F32).T
        st = st_ref[...]
        ot = jnp.zeros((HEAD_DIM, c_len), F32)
        for step in range(c_len):
            t = c_len - 1 - step if reverse else step
            st = st * f[t:t + 1, :] + vt[:, t:t + 1] * k[t:t + 1, :]
            o_col = jnp.sum(st * q[t:t + 1, :], axis=-1, keepdims=True)
            ot = ot + o_col * (lane == t).astype(F32)
        st_ref[...] = st
        o_ref[rows, :] = ot.T.astype(BF16)
        return carry

    lax.fori_loop(0, n_chunks, chunk, 0)


def _scan_exact(a, *, batch, n_lat, n_ctx, e):
    ts = n_ctx
    lat_steps = n_lat // ts
    steps = 1 + lat_steps
    heads = e // HEAD_DIM
    rows = batch * (n_ctx + n_lat)
    outs = []
    for reverse, k_stream in ((False, 1), (True, 2)):
        def row(lat0, reverse=reverse):
            if reverse:
                return lambda b, s: jnp.where(s == 0, b, lat0 + lat_steps * b + lat_steps - s)
            return lambda b, s: jnp.where(s == 0, b, lat0 + lat_steps * b + s - 1)

        rin, rout = row(PROJ_TILE // ts), row(batch)

        def spec(stream, rin=rin):
            return pl.BlockSpec((None, ts, LANES), lambda b, h, s: (stream * heads + h, rin(b, s), 0))

        outs.append(pl.pallas_call(
            functools.partial(_scan_exact_kernel, n_chunks=ts // SCAN_CHUNK, reverse=reverse),
            grid=(batch, heads, steps),
            in_specs=[spec(0), spec(k_stream), spec(3)],
            out_specs=pl.BlockSpec((None, ts, LANES), lambda b, h, s, rout=rout: (h, rout(b, s), 0)),
            out_shape=jax.ShapeDtypeStruct((heads, rows, LANES), BF16),
            scratch_shapes=[pltpu.VMEM((HEAD_DIM, HEAD_DIM), F32)],
            compiler_params=_cparams(("arbitrary", "arbitrary", "arbitrary")),
            name="hgrn_scan_exact_bwd" if reverse else "hgrn_scan_exact_fwd",
        )(a, a, a))
    return outs


def _out0_kernel(of_ref, ob_ref, z_ref, ctx_ref, x_ref, mod_ref, hg_ref, w_ref,
                 xo_ref, co_ref, y_ref, *, d, e, tiles_per_batch):
    i = pl.program_id(0)
    heads = e // HEAD_DIM
    group = 4
    mix = None
    for h0 in range(0, heads, group):
        for h in range(h0, h0 + group):
            lanes = slice(h * HEAD_DIM, (h + 1) * HEAD_DIM)
            o = of_ref[h].astype(F32) + ob_ref[h].astype(F32)
            var = jnp.mean(o * o, axis=-1, keepdims=True)
            yn = (o * lax.rsqrt(var + EPS)) * hg_ref[...]
            y_ref[:, lanes] = (yn * z_ref[h].astype(F32)).astype(BF16)
        cols = slice(h0 * HEAD_DIM, (h0 + group) * HEAD_DIM)
        part = jnp.dot(y_ref[:, cols], w_ref[cols, :], preferred_element_type=F32)
        mix = part if mix is None else mix + part

    @pl.when(i == 0)
    def _():
        gate = mod_ref[2:3, 2 * d:3 * d]
        co_ref[...] = ctx_ref[...] + gate * mix

    @pl.when(i > 0)
    def _():
        gate = mod_ref[pl.ds((i - 1) // tiles_per_batch, 1), 2 * d:3 * d]
        xo_ref[...] = x_ref[...] + gate * mix


def _out0(o_f, o_b, a, ctx2d, x2d, mods, hg_norm_g, w_bf, *, n_lat):
    rows_ctx, d = ctx2d.shape
    rows_lat = x2d.shape[0]
    heads = o_f.shape[0]
    e = heads * HEAD_DIM
    tm = ROW_TILE
    assert rows_ctx == tm and PROJ_TILE % tm == 0
    n_tiles = 1 + rows_lat // tm
    kern = functools.partial(_out0_kernel, d=d, e=e, tiles_per_batch=n_lat // tm)
    lat_idx = lambda i: (jnp.maximum(i - 1, 0), 0)
    skip = PROJ_TILE // tm - 1
    src_row = lambda i: jnp.where(i == 0, 0, i + skip)
    return pl.pallas_call(
        kern,
        grid=(n_tiles,),
        in_specs=[
            pl.BlockSpec((heads, tm, HEAD_DIM), lambda i: (0, i, 0)),
            pl.BlockSpec((heads, tm, HEAD_DIM), lambda i: (0, i, 0)),
            pl.BlockSpec((heads, tm, HEAD_DIM), lambda i: (HG_STREAMS - 1, src_row(i), 0)),
            pl.BlockSpec((tm, d), lambda i: (0, 0)),
            pl.BlockSpec((tm, d), lat_idx),
            pl.BlockSpec((None, MOD_ROWS, 3 * d), lambda i: (0, 0, 0)),
            pl.BlockSpec((1, HEAD_DIM), lambda i: (0, 0)),
            pl.BlockSpec((e, d), lambda i: (0, 0)),
        ],
        out_specs=[pl.BlockSpec((tm, d), lat_idx), pl.BlockSpec((tm, d), lambda i: (0, 0))],
        out_shape=[jax.ShapeDtypeStruct((rows_lat, d), F32), jax.ShapeDtypeStruct((rows_ctx, d), F32)],
        scratch_shapes=[pltpu.VMEM((tm, e), BF16)],
        compiler_params=_cparams(("arbitrary",)),
        name="readout0",
    )(o_f, o_b, a, ctx2d, x2d, mods, hg_norm_g, w_bf)


def _proj1_kernel(x_ref, mod_ref, ng_ref, w_ref, u_ref, z_ref, h_ref,
                  *, d, e, tiles_per_batch, sub, n2):
    i = pl.program_id(0)
    j = pl.program_id(1)
    tm = h_ref.shape[0]

    @pl.when(j == 0)
    def _():
        m = mod_ref[pl.ds(i // tiles_per_batch, 1), :]
        h_ref[...] = _modulated_norm(x_ref[...], ng_ref[...], m[:, 0:d], m[:, d:2 * d]).astype(BF16)

    def acc(c):
        return jnp.dot(h_ref[...], w_ref[:, c * sub:(c + 1) * sub], preferred_element_type=F32)

    nsub = e // sub

    @pl.when(j == 0)
    def _():
        for c in range(nsub):
            a = acc(c)
            for n1l in range(tm // n2):
                for m in range(n2 // SUBLANES):
                    src = n1l * n2 + m * SUBLANES
                    for cl in range(sub // LANES):
                        u_ref[m, c * (sub // LANES) + cl, n1l * SUBLANES:(n1l + 1) * SUBLANES, :] = (
                            a[src:src + SUBLANES, cl * LANES:(cl + 1) * LANES])

    @pl.when(j == 1)
    def _():
        for c in range(nsub):
            z_ref[:, c * sub:(c + 1) * sub] = _silu_half(acc(c)).astype(BF16)


def _proj1(x2d, mods, norm_g, w_bf, *, batch, n_lat, layer):
    rows, d = x2d.shape
    e = w_bf.shape[1] // 2
    tm = PROJ_TILE
    n1, n2 = _dft_factors(n_lat)
    q = n2 // SUBLANES
    tpb = n_lat // tm
    rows_blk = (tm // n2) * SUBLANES
    kern = functools.partial(_proj1_kernel, d=d, e=e, tiles_per_batch=tpb, sub=512, n2=n2)
    return pl.pallas_call(
        kern,
        grid=(rows // tm, 2),
        in_specs=[
            pl.BlockSpec((tm, d), lambda i, j: (i, 0)),
            pl.BlockSpec((None, MOD_ROWS, 3 * d), lambda i, j: (layer, 0, 0)),
            pl.BlockSpec((None, 1, d), lambda i, j: (layer, 0, 0)),
            pl.BlockSpec((d, e), lambda i, j: (0, j)),
        ],
        out_specs=[pl.BlockSpec((None, q, e // LANES, rows_blk, LANES),
                                lambda i, j: (i // tpb, 0, 0, i % tpb, 0)),
                   pl.BlockSpec((tm, e), lambda i, j: (i, 0))],
        out_shape=[jax.ShapeDtypeStruct((batch, q, e // LANES, n1 * SUBLANES, LANES), F32),
                   jax.ShapeDtypeStruct((rows, e), BF16)],
        scratch_shapes=[pltpu.VMEM((tm, d), BF16)],
        compiler_params=_cparams(("arbitrary", "arbitrary")),
        name="proj1",
    )(x2d, mods, norm_g, w_bf)


def _dft_factors(n):
    n1 = 128 if n % 128 == 0 else n
    return n1, n // n1


def _dft_tables(n, group_dim):
    n1, n2 = _dft_factors(n)
    a2 = np.arange(n2, dtype=np.float64)[:, None, None]
    k1 = np.arange(n1, dtype=np.float64)[None, :, None]
    m1 = np.arange(n1, dtype=np.float64)[None, None, :]
    ang = -2.0 * np.pi * (a2 * k1 / n + m1 * k1 / n1)
    stage1 = np.concatenate([np.cos(ang), np.sin(ang)], axis=1) / np.sqrt(n1)
    kk = np.arange(n2, dtype=np.float64)
    ang2 = -2.0 * np.pi * np.outer(kk, kk) / n2
    fr, fi = np.cos(ang2) / np.sqrt(n2), np.sin(ang2) / np.sqrt(n2)
    stage2 = np.stack([np.concatenate([fr, fi], axis=0),
                       np.concatenate([-fi, fr], axis=0)])
    cc = np.arange(group_dim, dtype=np.float64)
    ang3 = 2.0 * np.pi * np.outer(cc, cc) / group_dim
    chan = np.concatenate([np.cos(ang3), np.sin(ang3)], axis=0) / np.sqrt(group_dim)
    to = lambda t: jnp.asarray(t, dtype=F32).astype(BF16)
    return to(stage1), to(stage2), to(chan)


def _dft_kernel(m_ref, f_ref, cs_ref, x_ref, y_ref, tr_ref, ti_ref, ys_ref, *, n1, n2, gd, kt_step):
    half = pl.program_id(2)
    q, nlb = x_ref.shape[0], x_ref.shape[1]

    @pl.when(half == 0)
    def _():
        def stage1(m, carry):
            for r in range(SUBLANES):
                rows = pl.ds(r, n1, stride=SUBLANES)
                xr = jnp.concatenate([x_ref[m, lb, rows, :] for lb in range(nlb)], axis=1)
                t = jnp.dot(m_ref[m, r], xr.astype(BF16), preferred_element_type=F32)
                for lb in range(nlb):
                    lanes = slice(lb * LANES, (lb + 1) * LANES)
                    tr_ref[m, lb, rows, :] = t[0:n1, lanes]
                    ti_ref[m, lb, rows, :] = t[n1:2 * n1, lanes]
            return carry

        lax.fori_loop(0, q, stage1, 0)

    width = nlb * LANES

    def gather(ref, kt):
        cols = []
        for j in range(SUBLANES):
            start = pl.multiple_of((kt * SUBLANES + j) * SUBLANES, SUBLANES)
            cols += [ref[:, lb, pl.ds(start, SUBLANES), :].reshape(n2, LANES) for lb in range(nlb)]
        return jnp.concatenate(cols, axis=1).astype(BF16)

    def by_k1(g):
        return jnp.concatenate([g[:, j * width:(j + 1) * width] for j in range(SUBLANES)], axis=0)

    def stage2(kl, carry):
        kt = half * kt_step + kl
        g = (jnp.dot(f_ref[0], gather(tr_ref, kt), preferred_element_type=F32)
             + jnp.dot(f_ref[1], gather(ti_ref, kt), preferred_element_type=F32))
        gr = by_k1(g[0:n2]).astype(BF16)
        gi = by_k1(g[n2:2 * n2]).astype(BF16)
        yg = (jnp.dot(gr, cs_ref[0:gd, :], preferred_element_type=F32)
              + jnp.dot(gi, cs_ref[gd:2 * gd, :], preferred_element_type=F32))
        for j in range(SUBLANES):
            rows = pl.ds(j, n2, stride=SUBLANES)
            for lb in range(nlb):
                ys_ref[lb, rows, :] = yg[j * n2:(j + 1) * n2, lb * LANES:(lb + 1) * LANES]
        y_ref[kl] = ys_ref[...].astype(BF16)
        return carry

    lax.fori_loop(0, kt_step, stage2, 0)


def _dft(u, stage1, stage2, chan):
    batch, q, nlb, rows, _ = u.shape
    n2, two_n1, n1 = stage1.shape
    gd = chan.shape[1]
    glb = gd // LANES
    kt = n1 // SUBLANES
    kt_step = max(kt // 2, 1)
    m4 = stage1.reshape(q, SUBLANES, two_n1, n1)
    scratch = pltpu.VMEM((q, glb, rows, LANES), F32)
    return pl.pallas_call(
        functools.partial(_dft_kernel, n1=n1, n2=n2, gd=gd, kt_step=kt_step),
        grid=(batch, nlb // glb, kt // kt_step),
        in_specs=[pl.BlockSpec((q, SUBLANES, two_n1, n1), lambda b, g, h: (0, 0, 0, 0)),
                  pl.BlockSpec((2, 2 * n2, n2), lambda b, g, h: (0, 0, 0)),
                  pl.BlockSpec((2 * gd, gd), lambda b, g, h: (0, 0)),
                  pl.BlockSpec((None, q, glb, rows, LANES), lambda b, g, h: (b, 0, g, 0, 0))],
        out_specs=pl.BlockSpec((None, kt_step, glb, n2 * SUBLANES, LANES),
                               lambda b, g, h: (b, h, g, 0, 0)),
        out_shape=jax.ShapeDtypeStruct((batch, kt, nlb, n2 * SUBLANES, LANES), BF16),
        scratch_shapes=[scratch, scratch, pltpu.VMEM((glb, n2 * SUBLANES, LANES), F32)],
        compiler_params=_cparams(("arbitrary", "arbitrary", "arbitrary")),
        name="dft",
    )(m4, stage2, chan, u)


def _out1_kernel(y_in_ref, z_ref, x_ref, mod_ref, w_ref, fg_ref, o_ref, y_ref,
                 *, d, n1, tiles_per_batch):
    i = pl.program_id(0)
    kt, nlb = y_in_ref.shape[0], y_in_ref.shape[1]
    k2_per_tile = y_in_ref.shape[2] // SUBLANES
    pair = 2 * SUBLANES
    for t in range(k2_per_tile // 2):
        tile = slice(t * pair, (t + 1) * pair)
        for kp in range(kt // 2):
            blocks = [jnp.concatenate([y_in_ref[2 * kp + a, lb, tile, :] for lb in range(nlb)],
                                      axis=1).astype(F32) for a in range(2)]
            for s2 in range(2):
                k2l = 2 * t + s2
                dst = slice(k2l * n1 + kp * pair, k2l * n1 + (kp + 1) * pair)
                yv = jnp.concatenate([blk[s2 * SUBLANES:(s2 + 1) * SUBLANES] for blk in blocks], axis=0)
                y_ref[dst, :] = (yv * z_ref[dst, :].astype(F32)).astype(BF16)
    mix = jnp.dot(y_ref[...], w_ref[...], preferred_element_type=F32)
    gate = mod_ref[pl.ds(i // tiles_per_batch, 1), 2 * d:3 * d]
    x = x_ref[...] + gate * mix
    var = jnp.mean(x * x, axis=-1, keepdims=True)
    o_ref[...] = (x * lax.rsqrt(var + EPS)) * fg_ref[...]


def _out1(y_in, zs, x2d, mods, w_bf, final_g, *, n_lat, n1, layer):
    rows, d = x2d.shape
    batch, kt, nlb, _, _ = y_in.shape
    e = nlb * LANES
    tm = ROW_TILE
    assert tm % n1 == 0
    tpb = n_lat // tm
    blk_rows = (tm // n1) * SUBLANES
    kern = functools.partial(_out1_kernel, d=d, n1=n1, tiles_per_batch=tpb)
    return pl.pallas_call(
        kern,
        grid=(rows // tm,),
        in_specs=[
            pl.BlockSpec((None, kt, nlb, blk_rows, LANES), lambda i: (i // tpb, 0, 0, i % tpb, 0)),
            pl.BlockSpec((tm, e), lambda i: (i, 0)),
            pl.BlockSpec((tm, d), lambda i: (i, 0)),
            pl.BlockSpec((None, MOD_ROWS, 3 * d), lambda i: (layer, 0, 0)),
            pl.BlockSpec((e, d), lambda i: (0, 0)),
            pl.BlockSpec((1, d), lambda i: (0, 0)),
        ],
        out_specs=pl.BlockSpec((tm, d), lambda i: (i, 0)),
        out_shape=jax.ShapeDtypeStruct((rows, d), F32),
        scratch_shapes=[pltpu.VMEM((tm, e), BF16)],
        compiler_params=_cparams(("arbitrary",)),
        name="readout1",
    )(y_in, zs, x2d, mods, w_bf, final_g)


def kernel(x, c, ctx, c_ctx, ada_w, ada_b, norm_g, hg_w_in, hg_lb_logits, hg_norm_g, hg_w_out,
           ft_w_in, ft_w_out, final_g):
    batch, n_lat, d = x.shape
    n_ctx = ctx.shape[1]
    depth = ada_w.shape[0]
    e = hg_w_out.shape[1]
    assert depth == 2 and batch == 2 and batch + 1 <= MOD_ROWS

    cv = jnp.concatenate([c, c_ctx[None, :], jnp.zeros((MOD_ROWS - batch - 1, d), F32)], axis=0)
    mods = _ada_table(cv, ada_w, ada_b)
    ng = norm_g.reshape(depth, 1, d)
    x2d = x.reshape(batch * n_lat, d)
    ctx2d = ctx.reshape(batch * n_ctx, d)

    ctx_pad = jnp.pad(ctx2d, ((0, PROJ_TILE - batch * n_ctx), (0, 0)))
    half_cols = jnp.where(jnp.arange(HG_STREAMS * e) // e == 3, 1.0, 0.5).astype(F32)
    a = _proj0(ctx_pad, x2d, mods, ng, (hg_w_in[0] * half_cols).astype(BF16), hg_lb_logits,
                  n_lat=n_lat, lb_index=0)
    o_f, o_b, span = _scan(a, batch=batch, n_lat=n_lat, n_ctx=n_ctx, e=e)
    o_f, o_b = lax.cond(jnp.logical_not(jnp.max(span) <= SPAN_LIMIT),
                        lambda: tuple(_scan_exact(a, batch=batch, n_lat=n_lat, n_ctx=n_ctx, e=e)),
                        lambda: (o_f, o_b))
    x1, _ctx1 = _out0(o_f, o_b, a, ctx2d, x2d, mods, hg_norm_g[0:1], hg_w_out[0].astype(BF16),
                      n_lat=n_lat)

    z_half = jnp.where(jnp.arange(2 * e) < e, 1.0, 0.5).astype(F32)
    u, zs = _proj1(x1, mods, ng, (ft_w_in[0] * z_half).astype(BF16), batch=batch, n_lat=n_lat,
                   layer=1)
    stage1, stage2, chan = _dft_tables(n_lat, e // FT_GROUPS)
    n1 = stage1.shape[2]
    y = _dft(u, stage1, stage2, chan)
    out = _out1(y, zs, x1, mods, ft_w_out[0].astype(BF16), final_g.reshape(1, d),
                n_lat=n_lat, n1=n1, layer=1)
    return out.reshape(batch, n_lat, d)
```

```python
import functools

import numpy as np
import jax
import jax.numpy as jnp
from jax import lax
from jax.experimental import pallas as pl
from jax.experimental.pallas import tpu as pltpu

F32 = jnp.float32
BF16 = jnp.bfloat16

EPS = 1e-6
LOG2E = 1.4426950408889634
SPAN_LIMIT = 100.0
HEAD_DIM = 128
HG_STREAMS = 5
FT_GROUPS = 8
SCAN_CHUNK = 64
SCAN_HEADS = 16
ROW_TILE = 512
PROJ_TILE = 1024
MOD_ROWS = 8
SUBLANES = 8
LANES = 128
VMEM_LIMIT = 56 * 1024 * 1024


def _cparams(sem):
    return pltpu.CompilerParams(dimension_semantics=sem, vmem_limit_bytes=VMEM_LIMIT)


def _sigmoid(x):
    return 0.5 * jnp.tanh(0.5 * x) + 0.5


def _silu(x):
    return _silu_half(0.5 * x)


def _silu_half(h):
    return h + h * jnp.tanh(h)


def _modulated_norm(x, g, shift, scale):
    var = jnp.mean(x * x, axis=-1, keepdims=True)
    return (x * lax.rsqrt(var + EPS)) * (g * (1.0 + scale)) + shift


def _ada_kernel(cv_ref, w_ref, b_ref, o_ref):
    a = _silu(cv_ref[...])
    o_ref[...] = jnp.dot(a, w_ref[...], preferred_element_type=F32,
                         precision=lax.Precision.HIGHEST) + b_ref[...]


def _ada_table(cv, ada_w, ada_b):
    depth, d, d3 = ada_w.shape
    tn = 1024
    return pl.pallas_call(
        _ada_kernel,
        grid=(depth, d3 // tn),
        in_specs=[
            pl.BlockSpec((MOD_ROWS, d), lambda l, j: (0, 0)),
            pl.BlockSpec((None, d, tn), lambda l, j: (l, 0, j)),
            pl.BlockSpec((None, 1, tn), lambda l, j: (l, 0, j)),
        ],
        out_specs=pl.BlockSpec((None, MOD_ROWS, tn), lambda l, j: (l, 0, j)),
        out_shape=jax.ShapeDtypeStruct((depth, MOD_ROWS, d3), F32),
        compiler_params=_cparams(("arbitrary", "arbitrary")),
        name="ada_table",
    )(cv, ada_w, ada_b.reshape(depth, 1, d3))


def _proj0_kernel(ctx_ref, x_ref, mod_ref, ng_ref, w_ref, lbl_ref, a_ref, h_ref,
                  *, d, e, tiles_per_batch, lb_index, sub):
    i = pl.program_id(0)
    j = pl.program_id(1)
    tm = h_ref.shape[0]

    @pl.when(j == 0)
    def _():
        is_ctx = i == 0
        m = mod_ref[pl.ds(jnp.where(is_ctx, 2, (i - 1) // tiles_per_batch), 1), :]
        src = jnp.where(is_ctx, ctx_ref[...], x_ref[...])
        h_ref[...] = _modulated_norm(src, ng_ref[...], m[:, 0:d], m[:, d:2 * d]).astype(BF16)

    def acc(c):
        return jnp.dot(h_ref[...], w_ref[:, c * sub:(c + 1) * sub], preferred_element_type=F32)

    nsub = e // sub
    per_sub = sub // LANES

    def put(c, val):
        for cl in range(per_sub):
            a_ref[c * per_sub + cl] = val[:, cl * LANES:(cl + 1) * LANES]

    @pl.when(j == 0)
    def _():
        for c in range(nsub):
            put(c, _silu_half(acc(c)).astype(BF16))

    @pl.when((j == 1) | (j == 2))
    def _():
        dirn = j - 1
        logits = lbl_ref[:, pl.ds(dirn, 1), :]
        mx = jnp.max(logits, axis=0, keepdims=True)
        ex = jnp.exp(logits - mx)
        p = ex / jnp.sum(ex, axis=0, keepdims=True)
        lb_full = jnp.sum(p[0:lb_index + 1], axis=0)
        half_full = 0.5 * (1.0 - lb_full)
        for c in range(nsub):
            half = half_full[:, c * sub:(c + 1) * sub]
            put(c, (half * (1.0 - jnp.tanh(acc(c)))).astype(BF16))

    @pl.when(j == 3)
    def _():
        for c in range(nsub):
            put(c, acc(c).astype(BF16))

    @pl.when(j == 4)
    def _():
        for c in range(nsub):
            put(c, _silu_half(acc(c)).astype(BF16))


def _proj0(ctx2d, x2d, mods, norm_g, w_bf, lb_logits, *, n_lat, lb_index):
    rows_ctx, d = ctx2d.shape
    rows_lat = x2d.shape[0]
    e = w_bf.shape[1] // HG_STREAMS
    tm = PROJ_TILE
    assert rows_ctx == tm and n_lat % tm == 0
    n_tiles = 1 + rows_lat // tm
    rows = tm + rows_lat
    nl = lb_logits.shape[0]
    kern = functools.partial(_proj0_kernel, d=d, e=e, tiles_per_batch=n_lat // tm,
                             lb_index=lb_index, sub=512)
    return pl.pallas_call(
        kern,
        grid=(n_tiles, HG_STREAMS),
        in_specs=[
            pl.BlockSpec((tm, d), lambda i, j: (0, 0)),
            pl.BlockSpec((tm, d), lambda i, j: (jnp.maximum(i - 1, 0), 0)),
            pl.BlockSpec((None, MOD_ROWS, 3 * d), lambda i, j: (0, 0, 0)),
            pl.BlockSpec((None, 1, d), lambda i, j: (0, 0, 0)),
            pl.BlockSpec((d, e), lambda i, j: (0, j)),
            pl.BlockSpec((nl, 2, e), lambda i, j: (0, 0, 0)),
        ],
        out_specs=pl.BlockSpec((e // LANES, tm, LANES), lambda i, j: (j, i, 0)),
        out_shape=jax.ShapeDtypeStruct((HG_STREAMS * (e // LANES), rows, LANES), BF16),
        scratch_shapes=[pltpu.VMEM((tm, d), BF16)],
        compiler_params=_cparams(("arbitrary", "arbitrary")),
        name="proj0",
    )(ctx2d, x2d, mods, norm_g, w_bf, lb_logits)


def _scan_kernel(qf_ref, kf_ref, vf_ref, qb_ref, kb_ref, vb_ref, of_ref, ob_ref, span_ref,
                 st_ref, pa_ref, pb_ref, da_ref, db_ref, sp_ref, *, n_chunks):
    s = pl.program_id(2)
    c_len = SCAN_CHUNK

    @pl.when(s == 0)
    def _():
        st_ref[...] = jnp.zeros_like(st_ref)
        pb_ref[...] = jnp.zeros_like(pb_ref)
        db_ref[...] = jnp.zeros_like(db_ref)
        sp_ref[...] = jnp.zeros_like(sp_ref)

    row = lax.broadcasted_iota(jnp.int32, (c_len, c_len), 0)
    col = lax.broadcasted_iota(jnp.int32, (c_len, c_len), 1)
    causal = (col <= row, col >= row)
    tri = tuple(m.astype(F32).astype(BF16) for m in causal)
    end_row = (c_len - 1, 0)
    mid = c_len // 2
    qk_refs = ((qf_ref, kf_ref), (qb_ref, kb_ref))
    vo_refs = ((vf_ref, of_ref), (vb_ref, ob_ref))
    nt = (((1,), (1,)), ((), ()))
    tn = (((0,), (0,)), ((), ()))
    QT, KT, QH, KH = range(4)

    def chunk_of(dirn, r):
        return r if dirn == 0 else n_chunks - 1 - r

    def prepare(p_ref, d_ref, dirn, r, hh):
        q_ref, k_ref = qk_refs[dirn]
        c = chunk_of(dirn, r)
        rows = slice(c * c_len, (c + 1) * c_len)
        q = q_ref[hh, rows, :].astype(F32)
        k = k_ref[hh, rows, :].astype(F32)
        b = jnp.dot(tri[dirn], jnp.log(1.0 - k).astype(BF16), preferred_element_type=F32)
        tot = b[end_row[dirn]:end_row[dirn] + 1, :]
        ref = b[mid:mid + 1, :]
        dl = (b - ref) * LOG2E
        ends = jnp.maximum(jnp.abs(dl[0:1, :]), jnp.abs(dl[c_len - 1:c_len, :]))
        sp_ref[dirn, hh] = jnp.maximum(sp_ref[dirn, hh], ends)
        qt = q * jnp.exp2(dl)
        kt = k * jnp.exp2(-dl)
        p_ref[dirn, QT, hh, rows, :] = qt.astype(BF16)
        p_ref[dirn, KT, hh, rows, :] = kt.astype(BF16)
        p_ref[dirn, QH, hh, rows, :] = (qt * jnp.exp(ref)).astype(BF16)
        p_ref[dirn, KH, hh, rows, :] = (kt * jnp.exp(tot - ref)).astype(BF16)
        d_ref[dirn, c, hh] = jnp.exp(tot)

    def scores(p_ref, dirn, r, hh):
        v_ref = vo_refs[dirn][0]
        c = chunk_of(dirn, r)
        rows = slice(c * c_len, (c + 1) * c_len)
        sc = lax.dot_general(p_ref[dirn, QT, hh, rows, :], p_ref[dirn, KT, hh, rows, :],
                             nt, preferred_element_type=F32)
        prob = jnp.where(causal[dirn], sc, 0.0).astype(BF16)
        upd = lax.dot_general(v_ref[hh, rows, :], p_ref[dirn, KH, hh, rows, :],
                              tn, preferred_element_type=F32)
        return prob, upd

    def outputs(p_ref, d_ref, dirn, r, hh, prob, upd):
        v_ref, o_ref = vo_refs[dirn]
        c = chunk_of(dirn, r)
        rows = slice(c * c_len, (c + 1) * c_len)
        st = st_ref[dirn, hh]
        o = (jnp.dot(prob, v_ref[hh, rows, :], preferred_element_type=F32)
             + lax.dot_general(p_ref[dirn, QH, hh, rows, :], st.astype(BF16), nt,
                               preferred_element_type=F32))
        o_ref[hh, rows, :] = o.astype(BF16)
        st_ref[dirn, hh] = st * d_ref[dirn, c, hh] + upd

    def step(p_new, d_new, p_old, d_old):
        heads = range(SCAN_HEADS)
        pu = {(dirn, hh): scores(p_old, dirn, 0, hh) for hh in heads for dirn in range(2)}
        for r in range(n_chunks):
            pu_next = {}
            for hh in heads:
                for dirn in range(2):
                    if r + 1 < n_chunks:
                        pu_next[dirn, hh] = scores(p_old, dirn, r + 1, hh)
                    outputs(p_old, d_old, dirn, r, hh, *pu[dirn, hh])
                    prepare(p_new, d_new, dirn, r, hh)
            pu = pu_next

    @pl.when(s % 2 == 0)
    def _():
        step(pa_ref, da_ref, pb_ref, db_ref)

    @pl.when(s % 2 == 1)
    def _():
        step(pb_ref, db_ref, pa_ref, da_ref)

    @pl.when(s == pl.num_programs(2) - 1)
    def _():
        span_ref[...] = sp_ref[...]


def _scan(a, *, batch, n_lat, n_ctx, e):
    ts = n_ctx
    assert ts % SCAN_CHUNK == 0 and n_lat % ts == 0
    lat_steps = n_lat // ts
    steps = 1 + lat_steps
    n_chunks = ts // SCAN_CHUNK
    assert HEAD_DIM == LANES
    groups = e // (SCAN_HEADS * HEAD_DIM)
    rows = batch * (n_ctx + n_lat)
    assert batch * n_ctx <= PROJ_TILE and PROJ_TILE % ts == 0

    def row_f(lat0):
        return lambda b, s: jnp.where(s == 0, b, lat0 + lat_steps * b + s - 1)

    def row_b(lat0):
        return lambda b, s: jnp.where(s == 0, b, lat0 + lat_steps * b + lat_steps - s)

    in_f, in_b = row_f(PROJ_TILE // ts), row_b(PROJ_TILE // ts)
    out_f, out_b = row_f(batch), row_b(batch)

    ahead = lambda g: jnp.minimum(g, steps - 1)
    behind = lambda g: jnp.maximum(g - 1, 0)

    def spec(stream, rfn, when):
        return pl.BlockSpec((SCAN_HEADS, ts, LANES),
                            lambda b, hg, g: (stream * groups + hg, rfn(b, when(g)), 0))

    kern = functools.partial(_scan_kernel, n_chunks=n_chunks)
    operands = pltpu.VMEM((2, 4, SCAN_HEADS, ts, LANES), BF16)
    decays = pltpu.VMEM((2, n_chunks, SCAN_HEADS, 1, LANES), F32)
    return pl.pallas_call(
        kern,
        grid=(batch, groups, steps + 1),
        in_specs=[spec(0, in_f, ahead), spec(1, in_f, ahead), spec(3, in_f, behind),
                  spec(0, in_b, ahead), spec(2, in_b, ahead), spec(3, in_b, behind)],
        out_specs=[pl.BlockSpec((SCAN_HEADS, ts, LANES), lambda b, hg, g: (hg, out_f(b, behind(g)), 0)),
                   pl.BlockSpec((SCAN_HEADS, ts, LANES), lambda b, hg, g: (hg, out_b(b, behind(g)), 0)),
                   pl.BlockSpec((None, None, 2, SCAN_HEADS, 1, LANES), lambda b, hg, g: (b, hg, 0, 0, 0, 0))],
        out_shape=[jax.ShapeDtypeStruct((e // LANES, rows, LANES), BF16)] * 2
        + [jax.ShapeDtypeStruct((batch, groups, 2, SCAN_HEADS, 1, LANES), F32)],
        scratch_shapes=[pltpu.VMEM((2, SCAN_HEADS, HEAD_DIM, HEAD_DIM), F32),
                        operands, operands, decays, decays,
                        pltpu.VMEM((2, SCAN_HEADS, 1, LANES), F32)],
        compiler_params=_cparams(("arbitrary", "arbitrary", "arbitrary")),
        name="hgrn_scan",
    )(a, a, a, a, a, a)


def _scan_exact_kernel(q_ref, k_ref, v_ref, o_ref, st_ref, *, n_chunks, reverse):
    c_len = SCAN_CHUNK

    @pl.when(pl.program_id(2) == 0)
    def _():
        st_ref[...] = jnp.zeros_like(st_ref)

    lane = lax.broadcasted_iota(jnp.int32, (1, c_len), 1)

    def chunk(ci, carry):
        c = (n_chunks - 1 - ci) if reverse else ci
        rows = pl.ds(pl.multiple_of(c * c_len, c_len), c_len)
        q = q_ref[rows, :].astype(F32)
        k = k_ref[rows, :].astype(F32)
        f = 1.0 - k
        vt = v_ref[rows, :].astype(F32).T
        st = st_ref[...]
        ot = jnp.zeros((HEAD_DIM, c_len), F32)
        for step in range(c_len):
            t = c_len - 1 - step if reverse else step
            st = st * f[t:t + 1, :] + vt[:, t:t + 1] * k[t:t + 1, :]
            o_col = jnp.sum(st * q[t:t + 1, :], axis=-1, keepdims=True)
            ot = ot + o_col * (lane == t).astype(F32)
        st_ref[...] = st
        o_ref[rows, :] = ot.T.astype(BF16)
        return carry

    lax.fori_loop(0, n_chunks, chunk, 0)


def _scan_exact(a, *, batch, n_lat, n_ctx, e):
    ts = n_ctx
    lat_steps = n_lat // ts
    steps = 1 + lat_steps
    heads = e // HEAD_DIM
    rows = batch * (n_ctx + n_lat)
    outs = []
    for reverse, k_stream in ((False, 1), (True, 2)):
        def row(lat0, reverse=reverse):
            if reverse:
                return lambda b, s: jnp.where(s == 0, b, lat0 + lat_steps * b + lat_steps - s)
            return lambda b, s: jnp.where(s == 0, b, lat0 + lat_steps * b + s - 1)

        rin, rout = row(PROJ_TILE // ts), row(batch)

        def spec(stream, rin=rin):
            return pl.BlockSpec((None, ts, LANES), lambda b, h, s: (stream * heads + h, rin(b, s), 0))

        outs.append(pl.pallas_call(
            functools.partial(_scan_exact_kernel, n_chunks=ts // SCAN_CHUNK, reverse=reverse),
            grid=(batch, heads, steps),
            in_specs=[spec(0), spec(k_stream), spec(3)],
            out_specs=pl.BlockSpec((None, ts, LANES), lambda b, h, s, rout=rout: (h, rout(b, s), 0)),
            out_shape=jax.ShapeDtypeStruct((heads, rows, LANES), BF16),
            scratch_shapes=[pltpu.VMEM((HEAD_DIM, HEAD_DIM), F32)],
            compiler_params=_cparams(("arbitrary", "arbitrary", "arbitrary")),
            name="hgrn_scan_exact_bwd" if reverse else "hgrn_scan_exact_fwd",
        )(a, a, a))
    return outs


def _out0_kernel(of_ref, ob_ref, z_ref, ctx_ref, x_ref, mod_ref, hg_ref, w_ref,
                 xo_ref, co_ref, y_ref, *, d, e, tiles_per_batch):
    i = pl.program_id(0)
    heads = e // HEAD_DIM
    group = 4
    mix = None
    for h0 in range(0, heads, group):
        for h in range(h0, h0 + group):
            lanes = slice(h * HEAD_DIM, (h + 1) * HEAD_DIM)
            o = of_ref[h].astype(F32) + ob_ref[h].astype(F32)
            var = jnp.mean(o * o, axis=-1, keepdims=True)
            yn = (o * lax.rsqrt(var + EPS)) * hg_ref[...]
            y_ref[:, lanes] = (yn * z_ref[h].astype(F32)).astype(BF16)
        cols = slice(h0 * HEAD_DIM, (h0 + group) * HEAD_DIM)
        part = jnp.dot(y_ref[:, cols], w_ref[cols, :], preferred_element_type=F32)
        mix = part if mix is None else mix + part

    @pl.when(i == 0)
    def _():
        gate = mod_ref[2:3, 2 * d:3 * d]
        co_ref[...] = ctx_ref[...] + gate * mix

    @pl.when(i > 0)
    def _():
        gate = mod_ref[pl.ds((i - 1) // tiles_per_batch, 1), 2 * d:3 * d]
        xo_ref[...] = x_ref[...] + gate * mix


def _out0(o_f, o_b, a, ctx2d, x2d, mods, hg_norm_g, w_bf, *, n_lat):
    rows_ctx, d = ctx2d.shape
    rows_lat = x2d.shape[0]
    heads = o_f.shape[0]
    e = heads * HEAD_DIM
    tm = ROW_TILE
    assert rows_ctx == tm and PROJ_TILE % tm == 0
    n_tiles = 1 + rows_lat // tm
    kern = functools.partial(_out0_kernel, d=d, e=e, tiles_per_batch=n_lat // tm)
    lat_idx = lambda i: (jnp.maximum(i - 1, 0), 0)
    skip = PROJ_TILE // tm - 1
    src_row = lambda i: jnp.where(i == 0, 0, i + skip)
    return pl.pallas_call(
        kern,
        grid=(n_tiles,),
        in_specs=[
            pl.BlockSpec((heads, tm, HEAD_DIM), lambda i: (0, i, 0)),
            pl.BlockSpec((heads, tm, HEAD_DIM), lambda i: (0, i, 0)),
            pl.BlockSpec((heads, tm, HEAD_DIM), lambda i: (HG_STREAMS - 1, src_row(i), 0)),
            pl.BlockSpec((tm, d), lambda i: (0, 0)),
            pl.BlockSpec((tm, d), lat_idx),
            pl.BlockSpec((None, MOD_ROWS, 3 * d), lambda i: (0, 0, 0)),
            pl.BlockSpec((1, HEAD_DIM), lambda i: (0, 0)),
            pl.BlockSpec((e, d), lambda i: (0, 0)),
        ],
        out_specs=[pl.BlockSpec((tm, d), lat_idx), pl.BlockSpec((tm, d), lambda i: (0, 0))],
        out_shape=[jax.ShapeDtypeStruct((rows_lat, d), F32), jax.ShapeDtypeStruct((rows_ctx, d), F32)],
        scratch_shapes=[pltpu.VMEM((tm, e), BF16)],
        compiler_params=_cparams(("arbitrary",)),
        name="readout0",
    )(o_f, o_b, a, ctx2d, x2d, mods, hg_norm_g, w_bf)


def _proj1_kernel(x_ref, mod_ref, ng_ref, w_ref, u_ref, z_ref, h_ref,
                  *, d, e, tiles_per_batch, sub, n2):
    i = pl.program_id(0)
    j = pl.program_id(1)
    tm = h_ref.shape[0]

    @pl.when(j == 0)
    def _():
        m = mod_ref[pl.ds(i // tiles_per_batch, 1), :]
        h_ref[...] = _modulated_norm(x_ref[...], ng_ref[...], m[:, 0:d], m[:, d:2 * d]).astype(BF16)

    def acc(c):
        return jnp.dot(h_ref[...], w_ref[:, c * sub:(c + 1) * sub], preferred_element_type=F32)

    nsub = e // sub

    @pl.when(j == 0)
    def _():
        for c in range(nsub):
            a = acc(c)
            for n1l in range(tm // n2):
                for m in range(n2 // SUBLANES):
                    src = n1l * n2 + m * SUBLANES
                    for cl in range(sub // LANES):
                        u_ref[m, c * (sub // LANES) + cl, n1l * SUBLANES:(n1l + 1) * SUBLANES, :] = (
                            a[src:src + SUBLANES, cl * LANES:(cl + 1) * LANES])

    @pl.when(j == 1)
    def _():
        for c in range(nsub):
            z_ref[:, c * sub:(c + 1) * sub] = _silu_half(acc(c)).astype(BF16)


def _proj1(x2d, mods, norm_g, w_bf, *, batch, n_lat, layer):
    rows, d = x2d.shape
    e = w_bf.shape[1] // 2
    tm = PROJ_TILE
    n1, n2 = _dft_factors(n_lat)
    q = n2 // SUBLANES
    tpb = n_lat // tm
    rows_blk = (tm // n2) * SUBLANES
    kern = functools.partial(_proj1_kernel, d=d, e=e, tiles_per_batch=tpb, sub=512, n2=n2)
    return pl.pallas_call(
        kern,
        grid=(rows // tm, 2),
        in_specs=[
            pl.BlockSpec((tm, d), lambda i, j: (i, 0)),
            pl.BlockSpec((None, MOD_ROWS, 3 * d), lambda i, j: (layer, 0, 0)),
            pl.BlockSpec((None, 1, d), lambda i, j: (layer, 0, 0)),
            pl.BlockSpec((d, e), lambda i, j: (0, j)),
        ],
        out_specs=[pl.BlockSpec((None, q, e // LANES, rows_blk, LANES),
                                lambda i, j: (i // tpb, 0, 0, i % tpb, 0)),
                   pl.BlockSpec((tm, e), lambda i, j: (i, 0))],
        out_shape=[jax.ShapeDtypeStruct((batch, q, e // LANES, n1 * SUBLANES, LANES), F32),
                   jax.ShapeDtypeStruct((rows, e), BF16)],
        scratch_shapes=[pltpu.VMEM((tm, d), BF16)],
        compiler_params=_cparams(("arbitrary", "arbitrary")),
        name="proj1",
    )(x2d, mods, norm_g, w_bf)


def _dft_factors(n):
    n1 = 128 if n % 128 == 0 else n
    return n1, n // n1


def _dft_tables(n, group_dim):
    n1, n2 = _dft_factors(n)
    a2 = np.arange(n2, dtype=np.float64)[:, None, None]
    k1 = np.arange(n1, dtype=np.float64)[None, :, None]
    m1 = np.arange(n1, dtype=np.float64)[None, None, :]
    ang = -2.0 * np.pi * (a2 * k1 / n + m1 * k1 / n1)
    stage1 = np.concatenate([np.cos(ang), np.sin(ang)], axis=1) / np.sqrt(n1)
    kk = np.arange(n2, dtype=np.float64)
    ang2 = -2.0 * np.pi * np.outer(kk, kk) / n2
    fr, fi = np.cos(ang2) / np.sqrt(n2), np.sin(ang2) / np.sqrt(n2)
    stage2 = np.stack([np.concatenate([fr, fi], axis=0),
                       np.concatenate([-fi, fr], axis=0)])
    cc = np.arange(group_dim, dtype=np.float64)
    ang3 = 2.0 * np.pi * np.outer(cc, cc) / group_dim
    chan = np.concatenate([np.cos(ang3), np.sin(ang3)], axis=0) / np.sqrt(group_dim)
    to = lambda t: jnp.asarray(t, dtype=F32).astype(BF16)
    return to(stage1), to(stage2), to(chan)


def _dft_kernel(m_ref, f_ref, cs_ref, x_ref, y_ref, tr_ref, ti_ref, ys_ref, *, n1, n2, gd, kt_step):
    half = pl.program_id(2)
    q, nlb = x_ref.shape[0], x_ref.shape[1]

    @pl.when(half == 0)
    def _():
        def stage1(m, carry):
            for r in range(SUBLANES):
                rows = pl.ds(r, n1, stride=SUBLANES)
                xr = jnp.concatenate([x_ref[m, lb, rows, :] for lb in range(nlb)], axis=1)
                t = jnp.dot(m_ref[m, r], xr.astype(BF16), preferred_element_type=F32)
                for lb in range(nlb):
                    lanes = slice(lb * LANES, (lb + 1) * LANES)
                    tr_ref[m, lb, rows, :] = t[0:n1, lanes]
                    ti_ref[m, lb, rows, :] = t[n1:2 * n1, lanes]
            return carry

        lax.fori_loop(0, q, stage1, 0)

    width = nlb * LANES

    def gather(ref, kt):
        cols = []
        for j in range(SUBLANES):
            start = pl.multiple_of((kt * SUBLANES + j) * SUBLANES, SUBLANES)
            cols += [ref[:, lb, pl.ds(start, SUBLANES), :].reshape(n2, LANES) for lb in range(nlb)]
        return jnp.concatenate(cols, axis=1).astype(BF16)

    def by_k1(g):
        return jnp.concatenate([g[:, j * width:(j + 1) * width] for j in range(SUBLANES)], axis=0)

    def stage2(kl, carry):
        kt = half * kt_step + kl
        g = (jnp.dot(f_ref[0], gather(tr_ref, kt), preferred_element_type=F32)
             + jnp.dot(f_ref[1], gather(ti_ref, kt), preferred_element_type=F32))
        gr = by_k1(g[0:n2]).astype(BF16)
        gi = by_k1(g[n2:2 * n2]).astype(BF16)
        yg = (jnp.dot(gr, cs_ref[0:gd, :], preferred_element_type=F32)
              + jnp.dot(gi, cs_ref[gd:2 * gd, :], preferred_element_type=F32))
        for j in range(SUBLANES):
            rows = pl.ds(j, n2, stride=SUBLANES)
            for lb in range(nlb):
                ys_ref[lb, rows, :] = yg[j * n2:(j + 1) * n2, lb * LANES:(lb + 1) * LANES]
        y_ref[kl] = ys_ref[...].astype(BF16)
        return carry

    lax.fori_loop(0, kt_step, stage2, 0)


def _dft(u, stage1, stage2, chan):
    batch, q, nlb, rows, _ = u.shape
    n2, two_n1, n1 = stage1.shape
    gd = chan.shape[1]
    glb = gd // LANES
    kt = n1 // SUBLANES
    kt_step = max(kt // 2, 1)
    m4 = stage1.reshape(q, SUBLANES, two_n1, n1)
    scratch = pltpu.VMEM((q, glb, rows, LANES), F32)
    return pl.pallas_call(
        functools.partial(_dft_kernel, n1=n1, n2=n2, gd=gd, kt_step=kt_step),
        grid=(batch, nlb // glb, kt // kt_step),
        in_specs=[pl.BlockSpec((q, SUBLANES, two_n1, n1), lambda b, g, h: (0, 0, 0, 0)),
                  pl.BlockSpec((2, 2 * n2, n2), lambda b, g, h: (0, 0, 0)),
                  pl.BlockSpec((2 * gd, gd), lambda b, g, h: (0, 0)),
                  pl.BlockSpec((None, q, glb, rows, LANES), lambda b, g, h: (b, 0, g, 0, 0))],
        out_specs=pl.BlockSpec((None, kt_step, glb, n2 * SUBLANES, LANES),
                               lambda b, g, h: (b, h, g, 0, 0)),
        out_shape=jax.ShapeDtypeStruct((batch, kt, nlb, n2 * SUBLANES, LANES), BF16),
        scratch_shapes=[scratch, scratch, pltpu.VMEM((glb, n2 * SUBLANES, LANES), F32)],
        compiler_params=_cparams(("arbitrary", "arbitrary", "arbitrary")),
        name="dft",
    )(m4, stage2, chan, u)


def _out1_kernel(y_in_ref, z_ref, x_ref, mod_ref, w_ref, fg_ref, o_ref, y_ref,
                 *, d, n1, tiles_per_batch):
    i = pl.program_id(0)
    kt, nlb = y_in_ref.shape[0], y_in_ref.shape[1]
    k2_per_tile = y_in_ref.shape[2] // SUBLANES
    pair = 2 * SUBLANES
    for t in range(k2_per_tile // 2):
        tile = slice(t * pair, (t + 1) * pair)
        for kp in range(kt // 2):
            blocks = [jnp.concatenate([y_in_ref[2 * kp + a, lb, tile, :] for lb in range(nlb)],
                                      axis=1).astype(F32) for a in range(2)]
            for s2 in range(2):
                k2l = 2 * t + s2
                dst = slice(k2l * n1 + kp * pair, k2l * n1 + (kp + 1) * pair)
                yv = jnp.concatenate([blk[s2 * SUBLANES:(s2 + 1) * SUBLANES] for blk in blocks], axis=0)
                y_ref[dst, :] = (yv * z_ref[dst, :].astype(F32)).astype(BF16)
    mix = jnp.dot(y_ref[...], w_ref[...], preferred_element_type=F32)
    gate = mod_ref[pl.ds(i // tiles_per_batch, 1), 2 * d:3 * d]
    x = x_ref[...] + gate * mix
    var = jnp.mean(x * x, axis=-1, keepdims=True)
    o_ref[...] = (x * lax.rsqrt(var + EPS)) * fg_ref[...]


def _out1(y_in, zs, x2d, mods, w_bf, final_g, *, n_lat, n1, layer):
    rows, d = x2d.shape
    batch, kt, nlb, _, _ = y_in.shape
    e = nlb * LANES
    tm = PROJ_TILE
    assert tm % n1 == 0
    tpb = n_lat // tm
    blk_rows = (tm // n1) * SUBLANES
    kern = functools.partial(_out1_kernel, d=d, n1=n1, tiles_per_batch=tpb)
    return pl.pallas_call(
        kern,
        grid=(rows // tm,),
        in_specs=[
            pl.BlockSpec((None, kt, nlb, blk_rows, LANES), lambda i: (i // tpb, 0, 0, i % tpb, 0)),
            pl.BlockSpec((tm, e), lambda i: (i, 0)),
            pl.BlockSpec((tm, d), lambda i: (i, 0)),
            pl.BlockSpec((None, MOD_ROWS, 3 * d), lambda i: (layer, 0, 0)),
            pl.BlockSpec((e, d), lambda i: (0, 0)),
            pl.BlockSpec((1, d), lambda i: (0, 0)),
        ],
        out_specs=pl.BlockSpec((tm, d), lambda i: (i, 0)),
        out_shape=jax.ShapeDtypeStruct((rows, d), F32),
        scratch_shapes=[pltpu.VMEM((tm, e), BF16)],
        compiler_params=_cparams(("arbitrary",)),
        name="readout1",
    )(y_in, zs, x2d, mods, w_bf, final_g)


def kernel(x, c, ctx, c_ctx, ada_w, ada_b, norm_g, hg_w_in, hg_lb_logits, hg_norm_g, hg_w_out,
           ft_w_in, ft_w_out, final_g):
    batch, n_lat, d = x.shape
    n_ctx = ctx.shape[1]
    depth = ada_w.shape[0]
    e = hg_w_out.shape[1]
    assert depth == 2 and batch == 2 and batch + 1 <= MOD_ROWS

    cv = jnp.concatenate([c, c_ctx[None, :], jnp.zeros((MOD_ROWS - batch - 1, d), F32)], axis=0)
    mods = _ada_table(cv, ada_w, ada_b)
    ng = norm_g.reshape(depth, 1, d)
    x2d = x.reshape(batch * n_lat, d)
    ctx2d = ctx.reshape(batch * n_ctx, d)

    ctx_pad = jnp.pad(ctx2d, ((0, PROJ_TILE - batch * n_ctx), (0, 0)))
    half_cols = jnp.where(jnp.arange(HG_STREAMS * e) // e == 3, 1.0, 0.5).astype(F32)
    a = _proj0(ctx_pad, x2d, mods, ng, (hg_w_in[0] * half_cols).astype(BF16), hg_lb_logits,
                  n_lat=n_lat, lb_index=0)
    o_f, o_b, span = _scan(a, batch=batch, n_lat=n_lat, n_ctx=n_ctx, e=e)
    o_f, o_b = lax.cond(jnp.logical_not(jnp.max(span) <= SPAN_LIMIT),
                        lambda: tuple(_scan_exact(a, batch=batch, n_lat=n_lat, n_ctx=n_ctx, e=e)),
                        lambda: (o_f, o_b))
    x1, _ctx1 = _out0(o_f, o_b, a, ctx2d, x2d, mods, hg_norm_g[0:1], hg_w_out[0].astype(BF16),
                      n_lat=n_lat)

    z_half = jnp.where(jnp.arange(2 * e) < e, 1.0, 0.5).astype(F32)
    u, zs = _proj1(x1, mods, ng, (ft_w_in[0] * z_half).astype(BF16), batch=batch, n_lat=n_lat,
                   layer=1)
    stage1, stage2, chan = _dft_tables(n_lat, e // FT_GROUPS)
    n1 = stage1.shape[2]
    y = _dft(u, stage1, stage2, chan)
    out = _out1(y, zs, x1, mods, ft_w_out[0].astype(BF16), final_g.reshape(1, d),
                n_lat=n_lat, n1=n1, layer=1)
    return out.reshape(batch, n_lat, d)
```

```python
import functools

import numpy as np
import jax
import jax.numpy as jnp
from jax import lax
from jax.experimental import pallas as pl
from jax.experimental.pallas import tpu as pltpu

F32 = jnp.float32
BF16 = jnp.bfloat16

EPS = 1e-6
LOG2E = 1.4426950408889634
SPAN_LIMIT = 100.0
HEAD_DIM = 128
HG_STREAMS = 5
FT_GROUPS = 8
SCAN_CHUNK = 64
SCAN_HEADS = 16
ROW_TILE = 512
PROJ_TILE = 1024
MOD_ROWS = 8
SUBLANES = 8
LANES = 128
VMEM_LIMIT = 56 * 1024 * 1024


def _cparams(sem):
    return pltpu.CompilerParams(dimension_semantics=sem, vmem_limit_bytes=VMEM_LIMIT)


def _sigmoid(x):
    return 0.5 * jnp.tanh(0.5 * x) + 0.5


def _silu(x):
    return _silu_half(0.5 * x)


def _silu_half(h):
    return h + h * jnp.tanh(h)


def _modulated_norm(x, g, shift, scale):
    var = jnp.mean(x * x, axis=-1, keepdims=True)
    return (x * lax.rsqrt(var + EPS)) * (g * (1.0 + scale)) + shift


def _ada_kernel(cv_ref, w_ref, b_ref, o_ref):
    a = _silu(cv_ref[...])
    o_ref[...] = jnp.dot(a, w_ref[...], preferred_element_type=F32,
                         precision=lax.Precision.HIGHEST) + b_ref[...]


def _ada_table(cv, ada_w, ada_b):
    depth, d, d3 = ada_w.shape
    tn = 1024
    return pl.pallas_call(
        _ada_kernel,
        grid=(depth, d3 // tn),
        in_specs=[
            pl.BlockSpec((MOD_ROWS, d), lambda l, j: (0, 0)),
            pl.BlockSpec((None, d, tn), lambda l, j: (l, 0, j)),
            pl.BlockSpec((None, 1, tn), lambda l, j: (l, 0, j)),
        ],
        out_specs=pl.BlockSpec((None, MOD_ROWS, tn), lambda l, j: (l, 0, j)),
        out_shape=jax.ShapeDtypeStruct((depth, MOD_ROWS, d3), F32),
        compiler_params=_cparams(("arbitrary", "arbitrary")),
        name="ada_table",
    )(cv, ada_w, ada_b.reshape(depth, 1, d3))


def _proj0_kernel(ctx_ref, x_ref, mod_ref, ng_ref, w_ref, lbl_ref, a_ref, h_ref,
                  *, d, e, tiles_per_batch, lb_index, sub):
    i = pl.program_id(0)
    j = pl.program_id(1)
    tm = h_ref.shape[0]

    @pl.when(j == 0)
    def _():
        is_ctx = i == 0
        m = mod_ref[pl.ds(jnp.where(is_ctx, 2, (i - 1) // tiles_per_batch), 1), :]
        src = jnp.where(is_ctx, ctx_ref[...], x_ref[...])
        h_ref[...] = _modulated_norm(src, ng_ref[...], m[:, 0:d], m[:, d:2 * d]).astype(BF16)

    def acc(c):
        return jnp.dot(h_ref[...], w_ref[:, c * sub:(c + 1) * sub], preferred_element_type=F32)

    nsub = w_ref.shape[1] // sub
    per_sub = sub // LANES
    per_stream = e // sub

    def put(c, val):
        for cl in range(per_sub):
            a_ref[c * per_sub + cl] = val[:, cl * LANES:(cl + 1) * LANES]

    def half_gate(dirn, cols):
        logits = lbl_ref[:, dirn:dirn + 1, cols]
        mx = jnp.max(logits, axis=0, keepdims=True)
        ex = jnp.exp(logits - mx)
        p = ex / jnp.sum(ex, axis=0, keepdims=True)
        return 0.5 * (1.0 - jnp.sum(p[0:lb_index + 1], axis=0))

    for step in range(pl.cdiv(HG_STREAMS * e, w_ref.shape[1])):
        @pl.when(j == step)
        def _(step=step):
            for c in range(nsub):
                stream, cs = divmod(step * nsub + c, per_stream)
                if stream in (0, 4):
                    put(c, _silu_half(acc(c)).astype(BF16))
                elif stream == 3:
                    put(c, acc(c).astype(BF16))
                else:
                    half = half_gate(stream - 1, slice(cs * sub, (cs + 1) * sub))
                    put(c, (half * (1.0 - jnp.tanh(acc(c)))).astype(BF16))


def _proj0(ctx2d, x2d, mods, norm_g, w_bf, lb_logits, *, n_lat, lb_index):
    rows_ctx, d = ctx2d.shape
    rows_lat = x2d.shape[0]
    e = w_bf.shape[1] // HG_STREAMS
    tm = PROJ_TILE
    assert rows_ctx == tm and n_lat % tm == 0
    n_tiles = 1 + rows_lat // tm
    rows = tm + rows_lat
    nl = lb_logits.shape[0]
    sub = 512
    col_steps = HG_STREAMS - 1
    tn = HG_STREAMS * e // col_steps
    assert tn % sub == 0 and e % sub == 0
    kern = functools.partial(_proj0_kernel, d=d, e=e, tiles_per_batch=n_lat // tm,
                             lb_index=lb_index, sub=sub)
    return pl.pallas_call(
        kern,
        grid=(n_tiles, col_steps),
        in_specs=[
            pl.BlockSpec((tm, d), lambda i, j: (0, 0)),
            pl.BlockSpec((tm, d), lambda i, j: (jnp.maximum(i - 1, 0), 0)),
            pl.BlockSpec((None, MOD_ROWS, 3 * d), lambda i, j: (0, 0, 0)),
            pl.BlockSpec((None, 1, d), lambda i, j: (0, 0, 0)),
            pl.BlockSpec((d, tn), lambda i, j: (0, j)),
            pl.BlockSpec((nl, 2, e), lambda i, j: (0, 0, 0)),
        ],
        out_specs=pl.BlockSpec((tn // LANES, tm, LANES), lambda i, j: (j, i, 0)),
        out_shape=jax.ShapeDtypeStruct((HG_STREAMS * (e // LANES), rows, LANES), BF16),
        scratch_shapes=[pltpu.VMEM((tm, d), BF16)],
        compiler_params=_cparams(("arbitrary", "arbitrary")),
        name="proj0",
    )(ctx2d, x2d, mods, norm_g, w_bf, lb_logits)


def _scan_kernel(qf_ref, kf_ref, vf_ref, qb_ref, kb_ref, vb_ref, of_ref, ob_ref, span_ref,
                 st_ref, pa_ref, pb_ref, da_ref, db_ref, sp_ref, *, n_chunks):
    s = pl.program_id(2)
    c_len = SCAN_CHUNK

    @pl.when(s == 0)
    def _():
        st_ref[...] = jnp.zeros_like(st_ref)
        pb_ref[...] = jnp.zeros_like(pb_ref)
        db_ref[...] = jnp.zeros_like(db_ref)
        sp_ref[...] = jnp.zeros_like(sp_ref)

    row = lax.broadcasted_iota(jnp.int32, (c_len, c_len), 0)
    col = lax.broadcasted_iota(jnp.int32, (c_len, c_len), 1)
    causal = (col <= row, col >= row)
    tri = tuple(m.astype(F32).astype(BF16) for m in causal)
    end_row = (c_len - 1, 0)
    mid = c_len // 2
    qk_refs = ((qf_ref, kf_ref), (qb_ref, kb_ref))
    vo_refs = ((vf_ref, of_ref), (vb_ref, ob_ref))
    nt = (((1,), (1,)), ((), ()))
    tn = (((0,), (0,)), ((), ()))
    QT, KT, QH, KH = range(4)

    def chunk_of(dirn, r):
        return r if dirn == 0 else n_chunks - 1 - r

    def prepare(p_ref, d_ref, dirn, r, hh):
        q_ref, k_ref = qk_refs[dirn]
        c = chunk_of(dirn, r)
        rows = slice(c * c_len, (c + 1) * c_len)
        q = q_ref[hh, rows, :].astype(F32)
        k = k_ref[hh, rows, :].astype(F32)
        b = jnp.dot(tri[dirn], jnp.log(1.0 - k).astype(BF16), preferred_element_type=F32)
        tot = b[end_row[dirn]:end_row[dirn] + 1, :]
        ref = b[mid:mid + 1, :]
        dl = (b - ref) * LOG2E
        ends = jnp.maximum(jnp.abs(dl[0:1, :]), jnp.abs(dl[c_len - 1:c_len, :]))
        sp_ref[dirn, hh] = jnp.maximum(sp_ref[dirn, hh], ends)
        qt = q * jnp.exp2(dl)
        kt = k * jnp.exp2(-dl)
        p_ref[dirn, QT, hh, rows, :] = qt.astype(BF16)
        p_ref[dirn, KT, hh, rows, :] = kt.astype(BF16)
        p_ref[dirn, QH, hh, rows, :] = (qt * jnp.exp(ref)).astype(BF16)
        p_ref[dirn, KH, hh, rows, :] = (kt * jnp.exp(tot - ref)).astype(BF16)
        d_ref[dirn, c, hh] = jnp.exp(tot)

    def scores(p_ref, dirn, r, hh):
        v_ref = vo_refs[dirn][0]
        c = chunk_of(dirn, r)
        rows = slice(c * c_len, (c + 1) * c_len)
        sc = lax.dot_general(p_ref[dirn, QT, hh, rows, :], p_ref[dirn, KT, hh, rows, :],
                             nt, preferred_element_type=F32)
        prob = jnp.where(causal[dirn], sc, 0.0).astype(BF16)
        upd = lax.dot_general(v_ref[hh, rows, :], p_ref[dirn, KH, hh, rows, :],
                              tn, preferred_element_type=F32)
        return prob, upd

    def outputs(p_ref, d_ref, dirn, r, hh, prob, upd):
        v_ref, o_ref = vo_refs[dirn]
        c = chunk_of(dirn, r)
        rows = slice(c * c_len, (c + 1) * c_len)
        st = st_ref[dirn, hh]
        o = (jnp.dot(prob, v_ref[hh, rows, :], preferred_element_type=F32)
             + lax.dot_general(p_ref[dirn, QH, hh, rows, :], st.astype(BF16), nt,
                               preferred_element_type=F32))
        o_ref[hh, rows, :] = o.astype(BF16)
        st_ref[dirn, hh] = st * d_ref[dirn, c, hh] + upd

    def step(p_new, d_new, p_old, d_old):
        heads = range(SCAN_HEADS)
        pu = {(dirn, hh): scores(p_old, dirn, 0, hh) for hh in heads for dirn in range(2)}
        for r in range(n_chunks):
            pu_next = {}
            for hh in heads:
                for dirn in range(2):
                    if r + 1 < n_chunks:
                        pu_next[dirn, hh] = scores(p_old, dirn, r + 1, hh)
                    outputs(p_old, d_old, dirn, r, hh, *pu[dirn, hh])
                    prepare(p_new, d_new, dirn, r, hh)
            pu = pu_next

    @pl.when(s % 2 == 0)
    def _():
        step(pa_ref, da_ref, pb_ref, db_ref)

    @pl.when(s % 2 == 1)
    def _():
        step(pb_ref, db_ref, pa_ref, da_ref)

    @pl.when(s == pl.num_programs(2) - 1)
    def _():
        span_ref[...] = sp_ref[...]


def _scan(a, *, batch, n_lat, n_ctx, e):
    ts = n_ctx
    assert ts % SCAN_CHUNK == 0 and n_lat % ts == 0
    lat_steps = n_lat // ts
    steps = 1 + lat_steps
    n_chunks = ts // SCAN_CHUNK
    assert HEAD_DIM == LANES
    groups = e // (SCAN_HEADS * HEAD_DIM)
    rows = batch * (n_ctx + n_lat)
    assert batch * n_ctx <= PROJ_TILE and PROJ_TILE % ts == 0

    def row_f(lat0):
        return lambda b, s: jnp.where(s == 0, b, lat0 + lat_steps * b + s - 1)

    def row_b(lat0):
        return lambda b, s: jnp.where(s == 0, b, lat0 + lat_steps * b + lat_steps - s)

    in_f, in_b = row_f(PROJ_TILE // ts), row_b(PROJ_TILE // ts)
    out_f, out_b = row_f(batch), row_b(batch)

    ahead = lambda g: jnp.minimum(g, steps - 1)
    behind = lambda g: jnp.maximum(g - 1, 0)

    def spec(stream, rfn, when):
        return pl.BlockSpec((SCAN_HEADS, ts, LANES),
                            lambda b, hg, g: (stream * groups + hg, rfn(b, when(g)), 0))

    kern = functools.partial(_scan_kernel, n_chunks=n_chunks)
    operands = pltpu.VMEM((2, 4, SCAN_HEADS, ts, LANES), BF16)
    decays = pltpu.VMEM((2, n_chunks, SCAN_HEADS, 1, LANES), F32)
    return pl.pallas_call(
        kern,
        grid=(batch, groups, steps + 1),
        in_specs=[spec(0, in_f, ahead), spec(1, in_f, ahead), spec(3, in_f, behind),
                  spec(0, in_b, ahead), spec(2, in_b, ahead), spec(3, in_b, behind)],
        out_specs=[pl.BlockSpec((SCAN_HEADS, ts, LANES), lambda b, hg, g: (hg, out_f(b, behind(g)), 0)),
                   pl.BlockSpec((SCAN_HEADS, ts, LANES), lambda b, hg, g: (hg, out_b(b, behind(g)), 0)),
                   pl.BlockSpec((None, None, 2, SCAN_HEADS, 1, LANES), lambda b, hg, g: (b, hg, 0, 0, 0, 0))],
        out_shape=[jax.ShapeDtypeStruct((e // LANES, rows, LANES), BF16)] * 2
        + [jax.ShapeDtypeStruct((batch, groups, 2, SCAN_HEADS, 1, LANES), F32)],
        scratch_shapes=[pltpu.VMEM((2, SCAN_HEADS, HEAD_DIM, HEAD_DIM), F32),
                        operands, operands, decays, decays,
                        pltpu.VMEM((2, SCAN_HEADS, 1, LANES), F32)],
        compiler_params=_cparams(("arbitrary", "arbitrary", "arbitrary")),
        name="hgrn_scan",
    )(a, a, a, a, a, a)


def _scan_exact_kernel(q_ref, k_ref, v_ref, o_ref, st_ref, *, n_chunks, reverse):
    c_len = SCAN_CHUNK

    @pl.when(pl.program_id(2) == 0)
    def _():
        st_ref[...] = jnp.zeros_like(st_ref)

    lane = lax.broadcasted_iota(jnp.int32, (1, c_len), 1)

    def chunk(ci, carry):
        c = (n_chunks - 1 - ci) if reverse else ci
        rows = pl.ds(pl.multiple_of(c * c_len, c_len), c_len)
        q = q_ref[rows, :].astype(F32)
        k = k_ref[rows, :].astype(F32)
        f = 1.0 - k
        vt = v_ref[rows, :].astype(F32).T
        st = st_ref[...]
        ot = jnp.zeros((HEAD_DIM, c_len), F32)
        for step in range(c_len):
            t = c_len - 1 - step if reverse else step
            st = st * f[t:t + 1, :] + vt[:, t:t + 1] * k[t:t + 1, :]
            o_col = jnp.sum(st * q[t:t + 1, :], axis=-1, keepdims=True)
            ot = ot + o_col * (lane == t).astype(F32)
        st_ref[...] = st
        o_ref[rows, :] = ot.T.astype(BF16)
        return carry

    lax.fori_loop(0, n_chunks, chunk, 0)


def _scan_exact(a, *, batch, n_lat, n_ctx, e):
    ts = n_ctx
    lat_steps = n_lat // ts
    steps = 1 + lat_steps
    heads = e // HEAD_DIM
    rows = batch * (n_ctx + n_lat)
    outs = []
    for reverse, k_stream in ((False, 1), (True, 2)):
        def row(lat0, reverse=reverse):
            if reverse:
                return lambda b, s: jnp.where(s == 0, b, lat0 + lat_steps * b + lat_steps - s)
            return lambda b, s: jnp.where(s == 0, b, lat0 + lat_steps * b + s - 1)

        rin, rout = row(PROJ_TILE // ts), row(batch)

        def spec(stream, rin=rin):
            return pl.BlockSpec((None, ts, LANES), lambda b, h, s: (stream * heads + h, rin(b, s), 0))

        outs.append(pl.pallas_call(
            functools.partial(_scan_exact_kernel, n_chunks=ts // SCAN_CHUNK, reverse=reverse),
            grid=(batch, heads, steps),
            in_specs=[spec(0), spec(k_stream), spec(3)],
            out_specs=pl.BlockSpec((None, ts, LANES), lambda b, h, s, rout=rout: (h, rout(b, s), 0)),
            out_shape=jax.ShapeDtypeStruct((heads, rows, LANES), BF16),
            scratch_shapes=[pltpu.VMEM((HEAD_DIM, HEAD_DIM), F32)],
            compiler_params=_cparams(("arbitrary", "arbitrary", "arbitrary")),
            name="hgrn_scan_exact_bwd" if reverse else "hgrn_scan_exact_fwd",
        )(a, a, a))
    return outs


def _out0_kernel(of_ref, ob_ref, z_ref, ctx_ref, x_ref, mod_ref, hg_ref, w_ref,
                 xo_ref, co_ref, y_ref, *, d, e, tiles_per_batch):
    i = pl.program_id(0)
    heads = e // HEAD_DIM
    group = 4
    mix = None
    for h0 in range(0, heads, group):
        for h in range(h0, h0 + group):
            lanes = slice(h * HEAD_DIM, (h + 1) * HEAD_DIM)
            o = of_ref[h].astype(F32) + ob_ref[h].astype(F32)
            var = jnp.mean(o * o, axis=-1, keepdims=True)
            yn = (o * lax.rsqrt(var + EPS)) * hg_ref[...]
            y_ref[:, lanes] = (yn * z_ref[h].astype(F32)).astype(BF16)
        cols = slice(h0 * HEAD_DIM, (h0 + group) * HEAD_DIM)
        part = jnp.dot(y_ref[:, cols], w_ref[cols, :], preferred_element_type=F32)
        mix = part if mix is None else mix + part

    @pl.when(i == 0)
    def _():
        gate = mod_ref[2:3, 2 * d:3 * d]
        co_ref[...] = ctx_ref[...] + gate * mix

    @pl.when(i > 0)
    def _():
        gate = mod_ref[pl.ds((i - 1) // tiles_per_batch, 1), 2 * d:3 * d]
        xo_ref[...] = x_ref[...] + gate * mix


def _out0(o_f, o_b, a, ctx2d, x2d, mods, hg_norm_g, w_bf, *, n_lat):
    rows_ctx, d = ctx2d.shape
    rows_lat = x2d.shape[0]
    heads = o_f.shape[0]
    e = heads * HEAD_DIM
    tm = ROW_TILE
    assert rows_ctx == tm and PROJ_TILE % tm == 0
    n_tiles = 1 + rows_lat // tm
    kern = functools.partial(_out0_kernel, d=d, e=e, tiles_per_batch=n_lat // tm)
    lat_idx = lambda i: (jnp.maximum(i - 1, 0), 0)
    skip = PROJ_TILE // tm - 1
    src_row = lambda i: jnp.where(i == 0, 0, i + skip)
    return pl.pallas_call(
        kern,
        grid=(n_tiles,),
        in_specs=[
            pl.BlockSpec((heads, tm, HEAD_DIM), lambda i: (0, i, 0)),
            pl.BlockSpec((heads, tm, HEAD_DIM), lambda i: (0, i, 0)),
            pl.BlockSpec((heads, tm, HEAD_DIM), lambda i: (HG_STREAMS - 1, src_row(i), 0)),
            pl.BlockSpec((tm, d), lambda i: (0, 0)),
            pl.BlockSpec((tm, d), lat_idx),
            pl.BlockSpec((None, MOD_ROWS, 3 * d), lambda i: (0, 0, 0)),
            pl.BlockSpec((1, HEAD_DIM), lambda i: (0, 0)),
            pl.BlockSpec((e, d), lambda i: (0, 0)),
        ],
        out_specs=[pl.BlockSpec((tm, d), lat_idx), pl.BlockSpec((tm, d), lambda i: (0, 0))],
        out_shape=[jax.ShapeDtypeStruct((rows_lat, d), F32), jax.ShapeDtypeStruct((rows_ctx, d), F32)],
        scratch_shapes=[pltpu.VMEM((tm, e), BF16)],
        compiler_params=_cparams(("arbitrary",)),
        name="readout0",
    )(o_f, o_b, a, ctx2d, x2d, mods, hg_norm_g, w_bf)


def _proj1_kernel(x_ref, mod_ref, ng_ref, w_ref, u_ref, z_ref, h_ref,
                  *, d, e, tiles_per_batch, sub, n2):
    i = pl.program_id(0)
    j = pl.program_id(1)
    tm = h_ref.shape[0]

    @pl.when(j == 0)
    def _():
        m = mod_ref[pl.ds(i // tiles_per_batch, 1), :]
        h_ref[...] = _modulated_norm(x_ref[...], ng_ref[...], m[:, 0:d], m[:, d:2 * d]).astype(BF16)

    def acc(c):
        return jnp.dot(h_ref[...], w_ref[:, c * sub:(c + 1) * sub], preferred_element_type=F32)

    nsub = e // sub

    @pl.when(j == 0)
    def _():
        for c in range(nsub):
            a = acc(c)
            for n1l in range(tm // n2):
                for m in range(n2 // SUBLANES):
                    src = n1l * n2 + m * SUBLANES
                    for cl in range(sub // LANES):
                        u_ref[m, c * (sub // LANES) + cl, n1l * SUBLANES:(n1l + 1) * SUBLANES, :] = (
                            a[src:src + SUBLANES, cl * LANES:(cl + 1) * LANES])

    @pl.when(j == 1)
    def _():
        for c in range(nsub):
            z_ref[:, c * sub:(c + 1) * sub] = _silu_half(acc(c)).astype(BF16)


def _proj1(x2d, mods, norm_g, w_bf, *, batch, n_lat, layer):
    rows, d = x2d.shape
    e = w_bf.shape[1] // 2
    tm = PROJ_TILE
    n1, n2 = _dft_factors(n_lat)
    q = n2 // SUBLANES
    tpb = n_lat // tm
    rows_blk = (tm // n2) * SUBLANES
    kern = functools.partial(_proj1_kernel, d=d, e=e, tiles_per_batch=tpb, sub=512, n2=n2)
    return pl.pallas_call(
        kern,
        grid=(rows // tm, 2),
        in_specs=[
            pl.BlockSpec((tm, d), lambda i, j: (i, 0)),
            pl.BlockSpec((None, MOD_ROWS, 3 * d), lambda i, j: (layer, 0, 0)),
            pl.BlockSpec((None, 1, d), lambda i, j: (layer, 0, 0)),
            pl.BlockSpec((d, e), lambda i, j: (0, j)),
        ],
        out_specs=[pl.BlockSpec((None, q, e // LANES, rows_blk, LANES),
                                lambda i, j: (i // tpb, 0, 0, i % tpb, 0)),
                   pl.BlockSpec((tm, e), lambda i, j: (i, 0))],
        out_shape=[jax.ShapeDtypeStruct((batch, q, e // LANES, n1 * SUBLANES, LANES), F32),
                   jax.ShapeDtypeStruct((rows, e), BF16)],
        scratch_shapes=[pltpu.VMEM((tm, d), BF16)],
        compiler_params=_cparams(("arbitrary", "arbitrary")),
        name="proj1",
    )(x2d, mods, norm_g, w_bf)


def _dft_factors(n):
    n1 = 128 if n % 128 == 0 else n
    return n1, n // n1


def _dft_tables(n, group_dim):
    n1, n2 = _dft_factors(n)
    a2 = np.arange(n2, dtype=np.float64)[:, None, None]
    k1 = np.arange(n1, dtype=np.float64)[None, :, None]
    m1 = np.arange(n1, dtype=np.float64)[None, None, :]
    ang = -2.0 * np.pi * (a2 * k1 / n + m1 * k1 / n1)
    stage1 = np.concatenate([np.cos(ang), np.sin(ang)], axis=1) / np.sqrt(n1)
    kk = np.arange(n2, dtype=np.float64)
    ang2 = -2.0 * np.pi * np.outer(kk, kk) / n2
    fr, fi = np.cos(ang2) / np.sqrt(n2), np.sin(ang2) / np.sqrt(n2)
    stage2 = np.stack([np.concatenate([fr, fi], axis=0),
                       np.concatenate([-fi, fr], axis=0)])
    cc = np.arange(group_dim, dtype=np.float64)
    ang3 = 2.0 * np.pi * np.outer(cc, cc) / group_dim
    chan = np.concatenate([np.cos(ang3), np.sin(ang3)], axis=0) / np.sqrt(group_dim)
    to = lambda t: jnp.asarray(t, dtype=F32).astype(BF16)
    return to(stage1), to(stage2), to(chan)


def _dft_kernel(m_ref, f_ref, cs_ref, x_ref, y_ref, tr_ref, ti_ref, ys_ref, *, n1, n2, gd, kt_step):
    half = pl.program_id(2)
    q, nlb = x_ref.shape[0], x_ref.shape[1]

    @pl.when(half == 0)
    def _():
        def stage1(m, carry):
            for r in range(SUBLANES):
                rows = pl.ds(r, n1, stride=SUBLANES)
                xr = jnp.concatenate([x_ref[m, lb, rows, :] for lb in range(nlb)], axis=1)
                t = jnp.dot(m_ref[m, r], xr.astype(BF16), preferred_element_type=F32)
                for lb in range(nlb):
                    lanes = slice(lb * LANES, (lb + 1) * LANES)
                    tr_ref[m, lb, rows, :] = t[0:n1, lanes]
                    ti_ref[m, lb, rows, :] = t[n1:2 * n1, lanes]
            return carry

        lax.fori_loop(0, q, stage1, 0)

    width = nlb * LANES

    def gather(ref, kt):
        cols = []
        for j in range(SUBLANES):
            start = pl.multiple_of((kt * SUBLANES + j) * SUBLANES, SUBLANES)
            cols += [ref[:, lb, pl.ds(start, SUBLANES), :].reshape(n2, LANES) for lb in range(nlb)]
        return jnp.concatenate(cols, axis=1).astype(BF16)

    def by_k1(g):
        return jnp.concatenate([g[:, j * width:(j + 1) * width] for j in range(SUBLANES)], axis=0)

    def stage2(kl, carry):
        kt = half * kt_step + kl
        g = (jnp.dot(f_ref[0], gather(tr_ref, kt), preferred_element_type=F32)
             + jnp.dot(f_ref[1], gather(ti_ref, kt), preferred_element_type=F32))
        gr = by_k1(g[0:n2]).astype(BF16)
        gi = by_k1(g[n2:2 * n2]).astype(BF16)
        yg = (jnp.dot(gr, cs_ref[0:gd, :], preferred_element_type=F32)
              + jnp.dot(gi, cs_ref[gd:2 * gd, :], preferred_element_type=F32))
        for j in range(SUBLANES):
            rows = pl.ds(j, n2, stride=SUBLANES)
            for lb in range(nlb):
                ys_ref[lb, rows, :] = yg[j * n2:(j + 1) * n2, lb * LANES:(lb + 1) * LANES]
        y_ref[kl] = ys_ref[...].astype(BF16)
        return carry

    lax.fori_loop(0, kt_step, stage2, 0)


def _dft(u, stage1, stage2, chan):
    batch, q, nlb, rows, _ = u.shape
    n2, two_n1, n1 = stage1.shape
    gd = chan.shape[1]
    glb = gd // LANES
    kt = n1 // SUBLANES
    kt_step = max(kt // 2, 1)
    m4 = stage1.reshape(q, SUBLANES, two_n1, n1)
    scratch = pltpu.VMEM((q, glb, rows, LANES), F32)
    return pl.pallas_call(
        functools.partial(_dft_kernel, n1=n1, n2=n2, gd=gd, kt_step=kt_step),
        grid=(batch, nlb // glb, kt // kt_step),
        in_specs=[pl.BlockSpec((q, SUBLANES, two_n1, n1), lambda b, g, h: (0, 0, 0, 0)),
                  pl.BlockSpec((2, 2 * n2, n2), lambda b, g, h: (0, 0, 0)),
                  pl.BlockSpec((2 * gd, gd), lambda b, g, h: (0, 0)),
                  pl.BlockSpec((None, q, glb, rows, LANES), lambda b, g, h: (b, 0, g, 0, 0))],
        out_specs=pl.BlockSpec((None, kt_step, glb, n2 * SUBLANES, LANES),
                               lambda b, g, h: (b, h, g, 0, 0)),
        out_shape=jax.ShapeDtypeStruct((batch, kt, nlb, n2 * SUBLANES, LANES), BF16),
        scratch_shapes=[scratch, scratch, pltpu.VMEM((glb, n2 * SUBLANES, LANES), F32)],
        compiler_params=_cparams(("arbitrary", "arbitrary", "arbitrary")),
        name="dft",
    )(m4, stage2, chan, u)


def _out1_kernel(y_in_ref, z_ref, x_ref, mod_ref, w_ref, fg_ref, o_ref, y_ref,
                 *, d, n1, tiles_per_batch):
    i = pl.program_id(0)
    kt, nlb = y_in_ref.shape[0], y_in_ref.shape[1]
    k2_per_tile = y_in_ref.shape[2] // SUBLANES
    pair = 2 * SUBLANES
    for t in range(k2_per_tile // 2):
        tile = slice(t * pair, (t + 1) * pair)
        for kp in range(kt // 2):
            blocks = [jnp.concatenate([y_in_ref[2 * kp + a, lb, tile, :] for lb in range(nlb)],
                                      axis=1).astype(F32) for a in range(2)]
            for s2 in range(2):
                k2l = 2 * t + s2
                dst = slice(k2l * n1 + kp * pair, k2l * n1 + (kp + 1) * pair)
                yv = jnp.concatenate([blk[s2 * SUBLANES:(s2 + 1) * SUBLANES] for blk in blocks], axis=0)
                y_ref[dst, :] = (yv * z_ref[dst, :].astype(F32)).astype(BF16)
    mix = jnp.dot(y_ref[...], w_ref[...], preferred_element_type=F32)
    gate = mod_ref[pl.ds(i // tiles_per_batch, 1), 2 * d:3 * d]
    x = x_ref[...] + gate * mix
    var = jnp.mean(x * x, axis=-1, keepdims=True)
    o_ref[...] = (x * lax.rsqrt(var + EPS)) * fg_ref[...]


def _out1(y_in, zs, x2d, mods, w_bf, final_g, *, n_lat, n1, layer):
    rows, d = x2d.shape
    batch, kt, nlb, _, _ = y_in.shape
    e = nlb * LANES
    tm = PROJ_TILE
    assert tm % n1 == 0
    tpb = n_lat // tm
    blk_rows = (tm // n1) * SUBLANES
    kern = functools.partial(_out1_kernel, d=d, n1=n1, tiles_per_batch=tpb)
    return pl.pallas_call(
        kern,
        grid=(rows // tm,),
        in_specs=[
            pl.BlockSpec((None, kt, nlb, blk_rows, LANES), lambda i: (i // tpb, 0, 0, i % tpb, 0)),
            pl.BlockSpec((tm, e), lambda i: (i, 0)),
            pl.BlockSpec((tm, d), lambda i: (i, 0)),
            pl.BlockSpec((None, MOD_ROWS, 3 * d), lambda i: (layer, 0, 0)),
            pl.BlockSpec((e, d), lambda i: (0, 0)),
            pl.BlockSpec((1, d), lambda i: (0, 0)),
        ],
        out_specs=pl.BlockSpec((tm, d), lambda i: (i, 0)),
        out_shape=jax.ShapeDtypeStruct((rows, d), F32),
        scratch_shapes=[pltpu.VMEM((tm, e), BF16)],
        compiler_params=_cparams(("arbitrary",)),
        name="readout1",
    )(y_in, zs, x2d, mods, w_bf, final_g)


def kernel(x, c, ctx, c_ctx, ada_w, ada_b, norm_g, hg_w_in, hg_lb_logits, hg_norm_g, hg_w_out,
           ft_w_in, ft_w_out, final_g):
    batch, n_lat, d = x.shape
    n_ctx = ctx.shape[1]
    depth = ada_w.shape[0]
    e = hg_w_out.shape[1]
    assert depth == 2 and batch == 2 and batch + 1 <= MOD_ROWS

    cv = jnp.concatenate([c, c_ctx[None, :], jnp.zeros((MOD_ROWS - batch - 1, d), F32)], axis=0)
    mods = _ada_table(cv, ada_w, ada_b)
    ng = norm_g.reshape(depth, 1, d)
    x2d = x.reshape(batch * n_lat, d)
    ctx2d = ctx.reshape(batch * n_ctx, d)

    ctx_pad = jnp.pad(ctx2d, ((0, PROJ_TILE - batch * n_ctx), (0, 0)))
    half_cols = jnp.where(jnp.arange(HG_STREAMS * e) // e == 3, 1.0, 0.5).astype(F32)
    a = _proj0(ctx_pad, x2d, mods, ng, (hg_w_in[0] * half_cols).astype(BF16), hg_lb_logits,
                  n_lat=n_lat, lb_index=0)
    o_f, o_b, span = _scan(a, batch=batch, n_lat=n_lat, n_ctx=n_ctx, e=e)
    o_f, o_b = lax.cond(jnp.logical_not(jnp.max(span) <= SPAN_LIMIT),
                        lambda: tuple(_scan_exact(a, batch=batch, n_lat=n_lat, n_ctx=n_ctx, e=e)),
                        lambda: (o_f, o_b))
    x1, _ctx1 = _out0(o_f, o_b, a, ctx2d, x2d, mods, hg_norm_g[0:1], hg_w_out[0].astype(BF16),
                      n_lat=n_lat)

    z_half = jnp.where(jnp.arange(2 * e) < e, 1.0, 0.5).astype(F32)
    u, zs = _proj1(x1, mods, ng, (ft_w_in[0] * z_half).astype(BF16), batch=batch, n_lat=n_lat,
                   layer=1)
    stage1, stage2, chan = _dft_tables(n_lat, e // FT_GROUPS)
    n1 = stage1.shape[2]
    y = _dft(u, stage1, stage2, chan)
    out = _out1(y, zs, x1, mods, ft_w_out[0].astype(BF16), final_g.reshape(1, d),
                n_lat=n_lat, n1=n1, layer=1)
    return out.reshape(batch, n_lat, d)
```

```python
import functools

import numpy as np
import jax
import jax.numpy as jnp
from jax import lax
from jax.experimental import pallas as pl
from jax.experimental.pallas import tpu as pltpu

F32 = jnp.float32
BF16 = jnp.bfloat16

EPS = 1e-6
LOG2E = 1.4426950408889634
SPAN_LIMIT = 100.0
HEAD_DIM = 128
HG_STREAMS = 5
FT_GROUPS = 8
SCAN_CHUNK = 64
SCAN_HEADS = 16
ROW_TILE = 512
PROJ_TILE = 1024
MOD_ROWS = 8
SUBLANES = 8
LANES = 128
VMEM_LIMIT = 56 * 1024 * 1024


def _cparams(sem):
    return pltpu.CompilerParams(dimension_semantics=sem, vmem_limit_bytes=VMEM_LIMIT)


def _sigmoid(x):
    return 0.5 * jnp.tanh(0.5 * x) + 0.5


def _silu(x):
    return _silu_half(0.5 * x)


def _silu_half(h):
    return h + h * jnp.tanh(h)


def _modulated_norm(x, g, shift, scale):
    var = jnp.mean(x * x, axis=-1, keepdims=True)
    return (x * lax.rsqrt(var + EPS)) * (g * (1.0 + scale)) + shift


def _ada_kernel(cv_ref, w_ref, b_ref, o_ref):
    a = _silu(cv_ref[...])
    o_ref[...] = jnp.dot(a, w_ref[...], preferred_element_type=F32,
                         precision=lax.Precision.HIGHEST) + b_ref[...]


def _ada_table(cv, ada_w, ada_b):
    depth, d, d3 = ada_w.shape
    tn = 1024
    return pl.pallas_call(
        _ada_kernel,
        grid=(depth, d3 // tn),
        in_specs=[
            pl.BlockSpec((MOD_ROWS, d), lambda l, j: (0, 0)),
            pl.BlockSpec((None, d, tn), lambda l, j: (l, 0, j)),
            pl.BlockSpec((None, 1, tn), lambda l, j: (l, 0, j)),
        ],
        out_specs=pl.BlockSpec((None, MOD_ROWS, tn), lambda l, j: (l, 0, j)),
        out_shape=jax.ShapeDtypeStruct((depth, MOD_ROWS, d3), F32),
        compiler_params=_cparams(("arbitrary", "arbitrary")),
        name="ada_table",
    )(cv, ada_w, ada_b.reshape(depth, 1, d3))


def _proj0_kernel(ctx_ref, x_ref, mod_ref, ng_ref, w_ref, lbl_ref, a_ref, h_ref,
                  *, d, e, tiles_per_batch, lb_index, sub):
    i = pl.program_id(0)
    j = pl.program_id(1)
    tm = h_ref.shape[0]

    @pl.when(j == 0)
    def _():
        is_ctx = i == 0
        m = mod_ref[pl.ds(jnp.where(is_ctx, 2, (i - 1) // tiles_per_batch), 1), :]
        src = jnp.where(is_ctx, ctx_ref[...], x_ref[...])
        h_ref[...] = _modulated_norm(src, ng_ref[...], m[:, 0:d], m[:, d:2 * d]).astype(BF16)

    def acc(c):
        return jnp.dot(h_ref[...], w_ref[:, c * sub:(c + 1) * sub], preferred_element_type=F32)

    nsub = w_ref.shape[1] // sub
    per_sub = sub // LANES
    per_stream = e // sub

    def put(c, val):
        for cl in range(per_sub):
            a_ref[c * per_sub + cl] = val[:, cl * LANES:(cl + 1) * LANES]

    def half_gate(dirn, cols):
        logits = lbl_ref[:, dirn:dirn + 1, cols]
        mx = jnp.max(logits, axis=0, keepdims=True)
        ex = jnp.exp(logits - mx)
        p = ex / jnp.sum(ex, axis=0, keepdims=True)
        return 0.5 * (1.0 - jnp.sum(p[0:lb_index + 1], axis=0))

    for step in range(pl.cdiv(HG_STREAMS * e, w_ref.shape[1])):
        @pl.when(j == step)
        def _(step=step):
            for c in range(nsub):
                stream, cs = divmod(step * nsub + c, per_stream)
                if stream in (0, 4):
                    put(c, _silu_half(acc(c)).astype(BF16))
                elif stream == 3:
                    put(c, acc(c).astype(BF16))
                else:
                    half = half_gate(stream - 1, slice(cs * sub, (cs + 1) * sub))
                    put(c, (half * (1.0 - jnp.tanh(acc(c)))).astype(BF16))


def _proj0(ctx2d, x2d, mods, norm_g, w_bf, lb_logits, *, n_lat, lb_index):
    rows_ctx, d = ctx2d.shape
    rows_lat = x2d.shape[0]
    e = w_bf.shape[1] // HG_STREAMS
    tm = PROJ_TILE
    assert rows_ctx == tm and n_lat % tm == 0
    n_tiles = 1 + rows_lat // tm
    rows = tm + rows_lat
    nl = lb_logits.shape[0]
    sub = 512
    col_steps = HG_STREAMS - 1
    tn = HG_STREAMS * e // col_steps
    assert tn % sub == 0 and e % sub == 0
    kern = functools.partial(_proj0_kernel, d=d, e=e, tiles_per_batch=n_lat // tm,
                             lb_index=lb_index, sub=sub)
    return pl.pallas_call(
        kern,
        grid=(n_tiles, col_steps),
        in_specs=[
            pl.BlockSpec((tm, d), lambda i, j: (0, 0)),
            pl.BlockSpec((tm, d), lambda i, j: (jnp.maximum(i - 1, 0), 0)),
            pl.BlockSpec((None, MOD_ROWS, 3 * d), lambda i, j: (0, 0, 0)),
            pl.BlockSpec((None, 1, d), lambda i, j: (0, 0, 0)),
            pl.BlockSpec((d, tn), lambda i, j: (0, j)),
            pl.BlockSpec((nl, 2, e), lambda i, j: (0, 0, 0)),
        ],
        out_specs=pl.BlockSpec((tn // LANES, tm, LANES), lambda i, j: (j, i, 0)),
        out_shape=jax.ShapeDtypeStruct((HG_STREAMS * (e // LANES), rows, LANES), BF16),
        scratch_shapes=[pltpu.VMEM((tm, d), BF16)],
        compiler_params=_cparams(("arbitrary", "arbitrary")),
        name="proj0",
    )(ctx2d, x2d, mods, norm_g, w_bf, lb_logits)


def _scan_kernel(qf_ref, kf_ref, vf_ref, qb_ref, kb_ref, vb_ref, of_ref, ob_ref, span_ref,
                 st_ref, pa_ref, pb_ref, da_ref, db_ref, sp_ref, *, n_chunks):
    s = pl.program_id(2)
    c_len = SCAN_CHUNK

    @pl.when(s == 0)
    def _():
        st_ref[...] = jnp.zeros_like(st_ref)
        pb_ref[...] = jnp.zeros_like(pb_ref)
        db_ref[...] = jnp.zeros_like(db_ref)
        sp_ref[...] = jnp.zeros_like(sp_ref)

    row = lax.broadcasted_iota(jnp.int32, (c_len, c_len), 0)
    col = lax.broadcasted_iota(jnp.int32, (c_len, c_len), 1)
    causal = (col <= row, col >= row)
    tri = tuple(m.astype(F32).astype(BF16) for m in causal)
    end_row = (c_len - 1, 0)
    mid = c_len // 2
    qk_refs = ((qf_ref, kf_ref), (qb_ref, kb_ref))
    vo_refs = ((vf_ref, of_ref), (vb_ref, ob_ref))
    nt = (((1,), (1,)), ((), ()))
    tn = (((0,), (0,)), ((), ()))
    QT, KT, QH, KH = range(4)

    def chunk_of(dirn, r):
        return r if dirn == 0 else n_chunks - 1 - r

    def prepare(p_ref, d_ref, dirn, r, hh):
        q_ref, k_ref = qk_refs[dirn]
        c = chunk_of(dirn, r)
        rows = slice(c * c_len, (c + 1) * c_len)
        q = q_ref[hh, rows, :].astype(F32)
        k = k_ref[hh, rows, :].astype(F32)
        b = jnp.dot(tri[dirn], jnp.log(1.0 - k).astype(BF16), preferred_element_type=F32)
        tot = b[end_row[dirn]:end_row[dirn] + 1, :]
        ref = b[mid:mid + 1, :]
        dl = (b - ref) * LOG2E
        ends = jnp.maximum(jnp.abs(dl[0:1, :]), jnp.abs(dl[c_len - 1:c_len, :]))
        sp_ref[dirn, hh] = jnp.maximum(sp_ref[dirn, hh], ends)
        qt = q * jnp.exp2(dl)
        kt = k * jnp.exp2(-dl)
        p_ref[dirn, QT, hh, rows, :] = qt.astype(BF16)
        p_ref[dirn, KT, hh, rows, :] = kt.astype(BF16)
        p_ref[dirn, QH, hh, rows, :] = (qt * jnp.exp(ref)).astype(BF16)
        p_ref[dirn, KH, hh, rows, :] = (kt * jnp.exp(tot - ref)).astype(BF16)
        d_ref[dirn, c, hh] = jnp.exp(tot)

    def scores(p_ref, dirn, r, hh):
        v_ref = vo_refs[dirn][0]
        c = chunk_of(dirn, r)
        rows = slice(c * c_len, (c + 1) * c_len)
        sc = lax.dot_general(p_ref[dirn, QT, hh, rows, :], p_ref[dirn, KT, hh, rows, :],
                             nt, preferred_element_type=F32)
        prob = jnp.where(causal[dirn], sc, 0.0).astype(BF16)
        upd = lax.dot_general(v_ref[hh, rows, :], p_ref[dirn, KH, hh, rows, :],
                              tn, preferred_element_type=F32)
        return prob, upd

    def outputs(p_ref, d_ref, dirn, r, hh, prob, upd):
        v_ref, o_ref = vo_refs[dirn]
        c = chunk_of(dirn, r)
        rows = slice(c * c_len, (c + 1) * c_len)
        st = st_ref[dirn, hh]
        o = (jnp.dot(prob, v_ref[hh, rows, :], preferred_element_type=F32)
             + lax.dot_general(p_ref[dirn, QH, hh, rows, :], st.astype(BF16), nt,
                               preferred_element_type=F32))
        o_ref[hh, rows, :] = o.astype(BF16)
        st_ref[dirn, hh] = st * d_ref[dirn, c, hh] + upd

    def step(p_new, d_new, p_old, d_old):
        heads = range(SCAN_HEADS)
        pu = {(dirn, hh): scores(p_old, dirn, 0, hh) for hh in heads for dirn in range(2)}
        for r in range(n_chunks):
            pu_next = {}
            for hh in heads:
                for dirn in range(2):
                    if r + 1 < n_chunks:
                        pu_next[dirn, hh] = scores(p_old, dirn, r + 1, hh)
                    outputs(p_old, d_old, dirn, r, hh, *pu[dirn, hh])
                    prepare(p_new, d_new, dirn, r, hh)
            pu = pu_next

    @pl.when(s % 2 == 0)
    def _():
        step(pa_ref, da_ref, pb_ref, db_ref)

    @pl.when(s % 2 == 1)
    def _():
        step(pb_ref, db_ref, pa_ref, da_ref)

    @pl.when(s == pl.num_programs(2) - 1)
    def _():
        span_ref[...] = sp_ref[...]


def _scan(a, *, batch, n_lat, n_ctx, e):
    ts = n_ctx
    assert ts % SCAN_CHUNK == 0 and n_lat % ts == 0
    lat_steps = n_lat // ts
    steps = 1 + lat_steps
    n_chunks = ts // SCAN_CHUNK
    assert HEAD_DIM == LANES
    groups = e // (SCAN_HEADS * HEAD_DIM)
    rows = batch * (n_ctx + n_lat)
    assert batch * n_ctx <= PROJ_TILE and PROJ_TILE % ts == 0

    def row_f(lat0):
        return lambda b, s: jnp.where(s == 0, b, lat0 + lat_steps * b + s - 1)

    def row_b(lat0):
        return lambda b, s: jnp.where(s == 0, b, lat0 + lat_steps * b + lat_steps - s)

    in_f, in_b = row_f(PROJ_TILE // ts), row_b(PROJ_TILE // ts)
    out_f, out_b = row_f(batch), row_b(batch)

    ahead = lambda g: jnp.minimum(g, steps - 1)
    behind = lambda g: jnp.maximum(g - 1, 0)

    def spec(stream, rfn, when):
        return pl.BlockSpec((SCAN_HEADS, ts, LANES),
                            lambda b, hg, g: (stream * groups + hg, rfn(b, when(g)), 0))

    kern = functools.partial(_scan_kernel, n_chunks=n_chunks)
    operands = pltpu.VMEM((2, 4, SCAN_HEADS, ts, LANES), BF16)
    decays = pltpu.VMEM((2, n_chunks, SCAN_HEADS, 1, LANES), F32)
    return pl.pallas_call(
        kern,
        grid=(batch, groups, steps + 1),
        in_specs=[spec(0, in_f, ahead), spec(1, in_f, ahead), spec(3, in_f, behind),
                  spec(0, in_b, ahead), spec(2, in_b, ahead), spec(3, in_b, behind)],
        out_specs=[pl.BlockSpec((SCAN_HEADS, ts, LANES), lambda b, hg, g: (hg, out_f(b, behind(g)), 0)),
                   pl.BlockSpec((SCAN_HEADS, ts, LANES), lambda b, hg, g: (hg, out_b(b, behind(g)), 0)),
                   pl.BlockSpec((None, None, 2, SCAN_HEADS, 1, LANES), lambda b, hg, g: (b, hg, 0, 0, 0, 0))],
        out_shape=[jax.ShapeDtypeStruct((e // LANES, rows, LANES), BF16)] * 2
        + [jax.ShapeDtypeStruct((batch, groups, 2, SCAN_HEADS, 1, LANES), F32)],
        scratch_shapes=[pltpu.VMEM((2, SCAN_HEADS, HEAD_DIM, HEAD_DIM), F32),
                        operands, operands, decays, decays,
                        pltpu.VMEM((2, SCAN_HEADS, 1, LANES), F32)],
        compiler_params=_cparams(("arbitrary", "arbitrary", "arbitrary")),
        name="hgrn_scan",
    )(a, a, a, a, a, a)


def _scan_exact_kernel(q_ref, k_ref, v_ref, o_ref, st_ref, *, n_chunks, reverse):
    c_len = SCAN_CHUNK

    @pl.when(pl.program_id(2) == 0)
    def _():
        st_ref[...] = jnp.zeros_like(st_ref)

    lane = lax.broadcasted_iota(jnp.int32, (1, c_len), 1)

    def chunk(ci, carry):
        c = (n_chunks - 1 - ci) if reverse else ci
        rows = pl.ds(pl.multiple_of(c * c_len, c_len), c_len)
        q = q_ref[rows, :].astype(F32)
        k = k_ref[rows, :].astype(F32)
        f = 1.0 - k
        vt = v_ref[rows, :].astype(F32).T
        st = st_ref[...]
        ot = jnp.zeros((HEAD_DIM, c_len), F32)
        for step in range(c_len):
            t = c_len - 1 - step if reverse else step
            st = st * f[t:t + 1, :] + vt[:, t:t + 1] * k[t:t + 1, :]
            o_col = jnp.sum(st * q[t:t + 1, :], axis=-1, keepdims=True)
            ot = ot + o_col * (lane == t).astype(F32)
        st_ref[...] = st
        o_ref[rows, :] = ot.T.astype(BF16)
        return carry

    lax.fori_loop(0, n_chunks, chunk, 0)


def _scan_exact(a, *, batch, n_lat, n_ctx, e):
    ts = n_ctx
    lat_steps = n_lat // ts
    steps = 1 + lat_steps
    heads = e // HEAD_DIM
    rows = batch * (n_ctx + n_lat)
    outs = []
    for reverse, k_stream in ((False, 1), (True, 2)):
        def row(lat0, reverse=reverse):
            if reverse:
                return lambda b, s: jnp.where(s == 0, b, lat0 + lat_steps * b + lat_steps - s)
            return lambda b, s: jnp.where(s == 0, b, lat0 + lat_steps * b + s - 1)

        rin, rout = row(PROJ_TILE // ts), row(batch)

        def spec(stream, rin=rin):
            return pl.BlockSpec((None, ts, LANES), lambda b, h, s: (stream * heads + h, rin(b, s), 0))

        outs.append(pl.pallas_call(
            functools.partial(_scan_exact_kernel, n_chunks=ts // SCAN_CHUNK, reverse=reverse),
            grid=(batch, heads, steps),
            in_specs=[spec(0), spec(k_stream), spec(3)],
            out_specs=pl.BlockSpec((None, ts, LANES), lambda b, h, s, rout=rout: (h, rout(b, s), 0)),
            out_shape=jax.ShapeDtypeStruct((heads, rows, LANES), BF16),
            scratch_shapes=[pltpu.VMEM((HEAD_DIM, HEAD_DIM), F32)],
            compiler_params=_cparams(("arbitrary", "arbitrary", "arbitrary")),
            name="hgrn_scan_exact_bwd" if reverse else "hgrn_scan_exact_fwd",
        )(a, a, a))
    return outs


def _out0_kernel(of_ref, ob_ref, z_ref, ctx_ref, x_ref, mod_ref, hg_ref, w_ref,
                 xo_ref, co_ref, y_ref, *, d, e, tiles_per_batch):
    i = pl.program_id(0)
    heads = e // HEAD_DIM
    group = 4
    mix = None
    for h0 in range(0, heads, group):
        for h in range(h0, h0 + group):
            lanes = slice(h * HEAD_DIM, (h + 1) * HEAD_DIM)
            o = of_ref[h].astype(F32) + ob_ref[h].astype(F32)
            var = jnp.mean(o * o, axis=-1, keepdims=True)
            yn = (o * lax.rsqrt(var + EPS)) * hg_ref[...]
            y_ref[:, lanes] = (yn * z_ref[h].astype(F32)).astype(BF16)
        cols = slice(h0 * HEAD_DIM, (h0 + group) * HEAD_DIM)
        part = jnp.dot(y_ref[:, cols], w_ref[cols, :], preferred_element_type=F32)
        mix = part if mix is None else mix + part

    @pl.when(i == 0)
    def _():
        gate = mod_ref[2:3, 2 * d:3 * d]
        co_ref[...] = ctx_ref[...] + gate * mix

    @pl.when(i > 0)
    def _():
        gate = mod_ref[pl.ds((i - 1) // tiles_per_batch, 1), 2 * d:3 * d]
        xo_ref[...] = x_ref[...] + gate * mix


def _out0(o_f, o_b, a, ctx2d, x2d, mods, hg_norm_g, w_bf, *, n_lat):
    rows_ctx, d = ctx2d.shape
    rows_lat = x2d.shape[0]
    heads = o_f.shape[0]
    e = heads * HEAD_DIM
    tm = ROW_TILE
    assert rows_ctx == tm and PROJ_TILE % tm == 0
    n_tiles = 1 + rows_lat // tm
    kern = functools.partial(_out0_kernel, d=d, e=e, tiles_per_batch=n_lat // tm)
    lat_idx = lambda i: (jnp.maximum(i - 1, 0), 0)
    skip = PROJ_TILE // tm - 1
    src_row = lambda i: jnp.where(i == 0, 0, i + skip)
    return pl.pallas_call(
        kern,
        grid=(n_tiles,),
        in_specs=[
            pl.BlockSpec((heads, tm, HEAD_DIM), lambda i: (0, i, 0)),
            pl.BlockSpec((heads, tm, HEAD_DIM), lambda i: (0, i, 0)),
            pl.BlockSpec((heads, tm, HEAD_DIM), lambda i: (HG_STREAMS - 1, src_row(i), 0)),
            pl.BlockSpec((tm, d), lambda i: (0, 0)),
            pl.BlockSpec((tm, d), lat_idx),
            pl.BlockSpec((None, MOD_ROWS, 3 * d), lambda i: (0, 0, 0)),
            pl.BlockSpec((1, HEAD_DIM), lambda i: (0, 0)),
            pl.BlockSpec((e, d), lambda i: (0, 0)),
        ],
        out_specs=[pl.BlockSpec((tm, d), lat_idx), pl.BlockSpec((tm, d), lambda i: (0, 0))],
        out_shape=[jax.ShapeDtypeStruct((rows_lat, d), F32), jax.ShapeDtypeStruct((rows_ctx, d), F32)],
        scratch_shapes=[pltpu.VMEM((tm, e), BF16)],
        compiler_params=_cparams(("arbitrary",)),
        name="readout0",
    )(o_f, o_b, a, ctx2d, x2d, mods, hg_norm_g, w_bf)


def _proj1_kernel(x_ref, mod_ref, ng_ref, w_ref, u_ref, z_ref, h_ref,
                  *, d, e, tiles_per_batch, sub, n2):
    i = pl.program_id(0)
    tm = h_ref.shape[0]
    m = mod_ref[pl.ds(i // tiles_per_batch, 1), :]
    h_ref[...] = _modulated_norm(x_ref[...], ng_ref[...], m[:, 0:d], m[:, d:2 * d]).astype(BF16)

    def acc(c):
        return jnp.dot(h_ref[...], w_ref[:, c * sub:(c + 1) * sub], preferred_element_type=F32)

    nsub = e // sub

    for c in range(nsub):
        a = acc(c)
        for n1l in range(tm // n2):
            for m8 in range(n2 // SUBLANES):
                src = n1l * n2 + m8 * SUBLANES
                for cl in range(sub // LANES):
                    u_ref[m8, c * (sub // LANES) + cl, n1l * SUBLANES:(n1l + 1) * SUBLANES, :] = (
                        a[src:src + SUBLANES, cl * LANES:(cl + 1) * LANES])

    for c in range(nsub):
        z_ref[:, c * sub:(c + 1) * sub] = _silu_half(acc(nsub + c)).astype(BF16)


def _proj1(x2d, mods, norm_g, w_bf, *, batch, n_lat, layer):
    rows, d = x2d.shape
    e = w_bf.shape[1] // 2
    tm = PROJ_TILE
    n1, n2 = _dft_factors(n_lat)
    q = n2 // SUBLANES
    tpb = n_lat // tm
    rows_blk = (tm // n2) * SUBLANES
    kern = functools.partial(_proj1_kernel, d=d, e=e, tiles_per_batch=tpb, sub=512, n2=n2)
    return pl.pallas_call(
        kern,
        grid=(rows // tm,),
        in_specs=[
            pl.BlockSpec((tm, d), lambda i: (i, 0)),
            pl.BlockSpec((None, MOD_ROWS, 3 * d), lambda i: (layer, 0, 0)),
            pl.BlockSpec((None, 1, d), lambda i: (layer, 0, 0)),
            pl.BlockSpec((d, 2 * e), lambda i: (0, 0), pipeline_mode=pl.Buffered(1)),
        ],
        out_specs=[pl.BlockSpec((None, q, e // LANES, rows_blk, LANES),
                                lambda i: (i // tpb, 0, 0, i % tpb, 0)),
                   pl.BlockSpec((tm, e), lambda i: (i, 0))],
        out_shape=[jax.ShapeDtypeStruct((batch, q, e // LANES, n1 * SUBLANES, LANES), F32),
                   jax.ShapeDtypeStruct((rows, e), BF16)],
        scratch_shapes=[pltpu.VMEM((tm, d), BF16)],
        compiler_params=_cparams(("arbitrary",)),
        name="proj1",
    )(x2d, mods, norm_g, w_bf)


def _dft_factors(n):
    n1 = 128 if n % 128 == 0 else n
    return n1, n // n1


def _dft_tables(n, group_dim):
    n1, n2 = _dft_factors(n)
    a2 = np.arange(n2, dtype=np.float64)[:, None, None]
    k1 = np.arange(n1, dtype=np.float64)[None, :, None]
    m1 = np.arange(n1, dtype=np.float64)[None, None, :]
    ang = -2.0 * np.pi * (a2 * k1 / n + m1 * k1 / n1)
    stage1 = np.concatenate([np.cos(ang), np.sin(ang)], axis=1) / np.sqrt(n1)
    kk = np.arange(n2, dtype=np.float64)
    ang2 = -2.0 * np.pi * np.outer(kk, kk) / n2
    fr, fi = np.cos(ang2) / np.sqrt(n2), np.sin(ang2) / np.sqrt(n2)
    stage2 = np.stack([np.concatenate([fr, fi], axis=0),
                       np.concatenate([-fi, fr], axis=0)])
    cc = np.arange(group_dim, dtype=np.float64)
    ang3 = 2.0 * np.pi * np.outer(cc, cc) / group_dim
    chan = np.concatenate([np.cos(ang3), np.sin(ang3)], axis=0) / np.sqrt(group_dim)
    to = lambda t: jnp.asarray(t, dtype=F32).astype(BF16)
    return to(stage1), to(stage2), to(chan)


def _dft_kernel(m_ref, f_ref, cs_ref, x_ref, y_ref, tr_ref, ti_ref, ys_ref, *, n1, n2, gd, kt_step):
    half = pl.program_id(2)
    q, nlb = x_ref.shape[0], x_ref.shape[1]

    @pl.when(half == 0)
    def _():
        def stage1(m, carry):
            for r in range(SUBLANES):
                rows = pl.ds(r, n1, stride=SUBLANES)
                xr = jnp.concatenate([x_ref[m, lb, rows, :] for lb in range(nlb)], axis=1)
                t = jnp.dot(m_ref[m, r], xr.astype(BF16), preferred_element_type=F32)
                for lb in range(nlb):
                    lanes = slice(lb * LANES, (lb + 1) * LANES)
                    tr_ref[m, lb, rows, :] = t[0:n1, lanes]
                    ti_ref[m, lb, rows, :] = t[n1:2 * n1, lanes]
            return carry

        lax.fori_loop(0, q, stage1, 0)

    width = nlb * LANES

    def gather(ref, kt):
        cols = []
        for j in range(SUBLANES):
            start = pl.multiple_of((kt * SUBLANES + j) * SUBLANES, SUBLANES)
            cols += [ref[:, lb, pl.ds(start, SUBLANES), :].reshape(n2, LANES) for lb in range(nlb)]
        return jnp.concatenate(cols, axis=1).astype(BF16)

    def by_k1(g):
        return jnp.concatenate([g[:, j * width:(j + 1) * width] for j in range(SUBLANES)], axis=0)

    def stage2(kl, carry):
        kt = half * kt_step + kl
        g = (jnp.dot(f_ref[0], gather(tr_ref, kt), preferred_element_type=F32)
             + jnp.dot(f_ref[1], gather(ti_ref, kt), preferred_element_type=F32))
        gr = by_k1(g[0:n2]).astype(BF16)
        gi = by_k1(g[n2:2 * n2]).astype(BF16)
        yg = (jnp.dot(gr, cs_ref[0:gd, :], preferred_element_type=F32)
              + jnp.dot(gi, cs_ref[gd:2 * gd, :], preferred_element_type=F32))
        for j in range(SUBLANES):
            rows = pl.ds(j, n2, stride=SUBLANES)
            for lb in range(nlb):
                ys_ref[lb, rows, :] = yg[j * n2:(j + 1) * n2, lb * LANES:(lb + 1) * LANES]
        y_ref[kl] = ys_ref[...].astype(BF16)
        return carry

    lax.fori_loop(0, kt_step, stage2, 0)


def _dft(u, stage1, stage2, chan):
    batch, q, nlb, rows, _ = u.shape
    n2, two_n1, n1 = stage1.shape
    gd = chan.shape[1]
    glb = gd // LANES
    kt = n1 // SUBLANES
    kt_step = max(kt // 2, 1)
    m4 = stage1.reshape(q, SUBLANES, two_n1, n1)
    scratch = pltpu.VMEM((q, glb, rows, LANES), F32)
    return pl.pallas_call(
        functools.partial(_dft_kernel, n1=n1, n2=n2, gd=gd, kt_step=kt_step),
        grid=(batch, nlb // glb, kt // kt_step),
        in_specs=[pl.BlockSpec((q, SUBLANES, two_n1, n1), lambda b, g, h: (0, 0, 0, 0)),
                  pl.BlockSpec((2, 2 * n2, n2), lambda b, g, h: (0, 0, 0)),
                  pl.BlockSpec((2 * gd, gd), lambda b, g, h: (0, 0)),
                  pl.BlockSpec((None, q, glb, rows, LANES), lambda b, g, h: (b, 0, g, 0, 0))],
        out_specs=pl.BlockSpec((None, kt_step, glb, n2 * SUBLANES, LANES),
                               lambda b, g, h: (b, h, g, 0, 0)),
        out_shape=jax.ShapeDtypeStruct((batch, kt, nlb, n2 * SUBLANES, LANES), BF16),
        scratch_shapes=[scratch, scratch, pltpu.VMEM((glb, n2 * SUBLANES, LANES), F32)],
        compiler_params=_cparams(("arbitrary", "arbitrary", "arbitrary")),
        name="dft",
    )(m4, stage2, chan, u)


def _out1_kernel(y_in_ref, z_ref, x_ref, mod_ref, w_ref, fg_ref, o_ref, y_ref,
                 *, d, n1, tiles_per_batch):
    i = pl.program_id(0)
    kt, nlb = y_in_ref.shape[0], y_in_ref.shape[1]
    k2_per_tile = y_in_ref.shape[2] // SUBLANES
    pair = 2 * SUBLANES
    for t in range(k2_per_tile // 2):
        tile = slice(t * pair, (t + 1) * pair)
        for kp in range(kt // 2):
            blocks = [jnp.concatenate([y_in_ref[2 * kp + a, lb, tile, :] for lb in range(nlb)],
                                      axis=1).astype(F32) for a in range(2)]
            for s2 in range(2):
                k2l = 2 * t + s2
                dst = slice(k2l * n1 + kp * pair, k2l * n1 + (kp + 1) * pair)
                yv = jnp.concatenate([blk[s2 * SUBLANES:(s2 + 1) * SUBLANES] for blk in blocks], axis=0)
                y_ref[dst, :] = (yv * z_ref[dst, :].astype(F32)).astype(BF16)
    mix = jnp.dot(y_ref[...], w_ref[...], preferred_element_type=F32)
    gate = mod_ref[pl.ds(i // tiles_per_batch, 1), 2 * d:3 * d]
    x = x_ref[...] + gate * mix
    var = jnp.mean(x * x, axis=-1, keepdims=True)
    o_ref[...] = (x * lax.rsqrt(var + EPS)) * fg_ref[...]


def _out1(y_in, zs, x2d, mods, w_bf, final_g, *, n_lat, n1, layer):
    rows, d = x2d.shape
    batch, kt, nlb, _, _ = y_in.shape
    e = nlb * LANES
    tm = PROJ_TILE
    assert tm % n1 == 0
    tpb = n_lat // tm
    blk_rows = (tm // n1) * SUBLANES
    kern = functools.partial(_out1_kernel, d=d, n1=n1, tiles_per_batch=tpb)
    return pl.pallas_call(
        kern,
        grid=(rows // tm,),
        in_specs=[
            pl.BlockSpec((None, kt, nlb, blk_rows, LANES), lambda i: (i // tpb, 0, 0, i % tpb, 0)),
            pl.BlockSpec((tm, e), lambda i: (i, 0)),
            pl.BlockSpec((tm, d), lambda i: (i, 0)),
            pl.BlockSpec((None, MOD_ROWS, 3 * d), lambda i: (layer, 0, 0)),
            pl.BlockSpec((e, d), lambda i: (0, 0)),
            pl.BlockSpec((1, d), lambda i: (0, 0)),
        ],
        out_specs=pl.BlockSpec((tm, d), lambda i: (i, 0)),
        out_shape=jax.ShapeDtypeStruct((rows, d), F32),
        scratch_shapes=[pltpu.VMEM((tm, e), BF16)],
        compiler_params=_cparams(("arbitrary",)),
        name="readout1",
    )(y_in, zs, x2d, mods, w_bf, final_g)


def kernel(x, c, ctx, c_ctx, ada_w, ada_b, norm_g, hg_w_in, hg_lb_logits, hg_norm_g, hg_w_out,
           ft_w_in, ft_w_out, final_g):
    batch, n_lat, d = x.shape
    n_ctx = ctx.shape[1]
    depth = ada_w.shape[0]
    e = hg_w_out.shape[1]
    assert depth == 2 and batch == 2 and batch + 1 <= MOD_ROWS

    cv = jnp.concatenate([c, c_ctx[None, :], jnp.zeros((MOD_ROWS - batch - 1, d), F32)], axis=0)
    mods = _ada_table(cv, ada_w, ada_b)
    ng = norm_g.reshape(depth, 1, d)
    x2d = x.reshape(batch * n_lat, d)
    ctx2d = ctx.reshape(batch * n_ctx, d)

    ctx_pad = jnp.pad(ctx2d, ((0, PROJ_TILE - batch * n_ctx), (0, 0)))
    half_cols = jnp.where(jnp.arange(HG_STREAMS * e) // e == 3, 1.0, 0.5).astype(F32)
    a = _proj0(ctx_pad, x2d, mods, ng, (hg_w_in[0] * half_cols).astype(BF16), hg_lb_logits,
                  n_lat=n_lat, lb_index=0)
    o_f, o_b, span = _scan(a, batch=batch, n_lat=n_lat, n_ctx=n_ctx, e=e)
    o_f, o_b = lax.cond(jnp.logical_not(jnp.max(span) <= SPAN_LIMIT),
                        lambda: tuple(_scan_exact(a, batch=batch, n_lat=n_lat, n_ctx=n_ctx, e=e)),
                        lambda: (o_f, o_b))
    x1, _ctx1 = _out0(o_f, o_b, a, ctx2d, x2d, mods, hg_norm_g[0:1], hg_w_out[0].astype(BF16),
                      n_lat=n_lat)

    z_half = jnp.where(jnp.arange(2 * e) < e, 1.0, 0.5).astype(F32)
    u, zs = _proj1(x1, mods, ng, (ft_w_in[0] * z_half).astype(BF16), batch=batch, n_lat=n_lat,
                   layer=1)
    stage1, stage2, chan = _dft_tables(n_lat, e // FT_GROUPS)
    n1 = stage1.shape[2]
    y = _dft(u, stage1, stage2, chan)
    out = _out1(y, zs, x1, mods, ft_w_out[0].astype(BF16), final_g.reshape(1, d),
                n_lat=n_lat, n1=n1, layer=1)
    return out.reshape(batch, n_lat, d)
```

```python
import functools

import numpy as np
import jax
import jax.numpy as jnp
from jax import lax
from jax.experimental import pallas as pl
from jax.experimental.pallas import tpu as pltpu

F32 = jnp.float32
BF16 = jnp.bfloat16

EPS = 1e-6
LOG2E = 1.4426950408889634
SPAN_LIMIT = 100.0
HEAD_DIM = 128
HG_STREAMS = 5
FT_GROUPS = 8
SCAN_CHUNK = 64
SCAN_HEADS = 16
ROW_TILE = 512
PROJ_TILE = 1024
MOD_ROWS = 8
SUBLANES = 8
LANES = 128
VMEM_LIMIT = 56 * 1024 * 1024


def _cparams(sem):
    return pltpu.CompilerParams(dimension_semantics=sem, vmem_limit_bytes=VMEM_LIMIT)


def _sigmoid(x):
    return 0.5 * jnp.tanh(0.5 * x) + 0.5


def _silu(x):
    return _silu_half(0.5 * x)


def _silu_half(h):
    return h + h * jnp.tanh(h)


def _modulated_norm(x, g, shift, scale):
    var = jnp.mean(x * x, axis=-1, keepdims=True)
    return (x * lax.rsqrt(var + EPS)) * (g * (1.0 + scale)) + shift


def _ada_kernel(cv_ref, w_ref, b_ref, o_ref):
    a = _silu(cv_ref[...])
    o_ref[...] = jnp.dot(a, w_ref[...], preferred_element_type=F32,
                         precision=lax.Precision.HIGHEST) + b_ref[...]


def _ada_table(cv, ada_w, ada_b):
    depth, d, d3 = ada_w.shape
    tn = 1024
    return pl.pallas_call(
        _ada_kernel,
        grid=(depth, d3 // tn),
        in_specs=[
            pl.BlockSpec((MOD_ROWS, d), lambda l, j: (0, 0)),
            pl.BlockSpec((None, d, tn), lambda l, j: (l, 0, j)),
            pl.BlockSpec((None, 1, tn), lambda l, j: (l, 0, j)),
        ],
        out_specs=pl.BlockSpec((None, MOD_ROWS, tn), lambda l, j: (l, 0, j)),
        out_shape=jax.ShapeDtypeStruct((depth, MOD_ROWS, d3), F32),
        compiler_params=_cparams(("arbitrary", "arbitrary")),
        name="ada_table",
    )(cv, ada_w, ada_b.reshape(depth, 1, d3))


def _proj0_kernel(ctx_ref, x_ref, mod_ref, ng_ref, w_ref, lbl_ref, a_ref, h_ref,
                  *, d, e, tiles_per_batch, lb_index, sub):
    i = pl.program_id(0)
    j = pl.program_id(1)
    tm = h_ref.shape[0]

    @pl.when(j == 0)
    def _():
        is_ctx = i == 0
        m = mod_ref[pl.ds(jnp.where(is_ctx, 2, (i - 1) // tiles_per_batch), 1), :]
        src = jnp.where(is_ctx, ctx_ref[...], x_ref[...])
        h_ref[...] = _modulated_norm(src, ng_ref[...], m[:, 0:d], m[:, d:2 * d]).astype(BF16)

    def acc(c):
        return jnp.dot(h_ref[...], w_ref[:, c * sub:(c + 1) * sub], preferred_element_type=F32)

    nsub = w_ref.shape[1] // sub
    per_sub = sub // LANES
    per_stream = e // sub

    def put(c, val):
        for cl in range(per_sub):
            a_ref[c * per_sub + cl] = val[:, cl * LANES:(cl + 1) * LANES]

    def half_gate(dirn, cols):
        logits = lbl_ref[:, dirn:dirn + 1, cols]
        mx = jnp.max(logits, axis=0, keepdims=True)
        ex = jnp.exp(logits - mx)
        p = ex / jnp.sum(ex, axis=0, keepdims=True)
        return 0.5 * (1.0 - jnp.sum(p[0:lb_index + 1], axis=0))

    for step in range(pl.cdiv(HG_STREAMS * e, w_ref.shape[1])):
        @pl.when(j == step)
        def _(step=step):
            for c in range(nsub):
                stream, cs = divmod(step * nsub + c, per_stream)
                if stream in (0, 4):
                    put(c, _silu_half(acc(c)).astype(BF16))
                elif stream == 3:
                    put(c, acc(c).astype(BF16))
                else:
                    half = half_gate(stream - 1, slice(cs * sub, (cs + 1) * sub))
                    put(c, (half * (1.0 - jnp.tanh(acc(c)))).astype(BF16))


def _proj0(ctx2d, x2d, mods, norm_g, w_bf, lb_logits, *, n_lat, lb_index):
    rows_ctx, d = ctx2d.shape
    rows_lat = x2d.shape[0]
    e = w_bf.shape[1] // HG_STREAMS
    tm = PROJ_TILE
    assert rows_ctx == tm and n_lat % tm == 0
    n_tiles = 1 + rows_lat // tm
    rows = tm + rows_lat
    nl = lb_logits.shape[0]
    sub = 512
    col_steps = HG_STREAMS - 1
    tn = HG_STREAMS * e // col_steps
    assert tn % sub == 0 and e % sub == 0
    kern = functools.partial(_proj0_kernel, d=d, e=e, tiles_per_batch=n_lat // tm,
                             lb_index=lb_index, sub=sub)
    return pl.pallas_call(
        kern,
        grid=(n_tiles, col_steps),
        in_specs=[
            pl.BlockSpec((tm, d), lambda i, j: (0, 0)),
            pl.BlockSpec((tm, d), lambda i, j: (jnp.maximum(i - 1, 0), 0)),
            pl.BlockSpec((None, MOD_ROWS, 3 * d), lambda i, j: (0, 0, 0)),
            pl.BlockSpec((None, 1, d), lambda i, j: (0, 0, 0)),
            pl.BlockSpec((d, tn), lambda i, j: (0, j)),
            pl.BlockSpec((nl, 2, e), lambda i, j: (0, 0, 0)),
        ],
        out_specs=pl.BlockSpec((tn // LANES, tm, LANES), lambda i, j: (j, i, 0)),
        out_shape=jax.ShapeDtypeStruct((HG_STREAMS * (e // LANES), rows, LANES), BF16),
        scratch_shapes=[pltpu.VMEM((tm, d), BF16)],
        compiler_params=_cparams(("arbitrary", "arbitrary")),
        name="proj0",
    )(ctx2d, x2d, mods, norm_g, w_bf, lb_logits)


def _scan_kernel(qf_ref, kf_ref, vf_ref, qb_ref, kb_ref, vb_ref, of_ref, ob_ref, span_ref,
                 st_ref, pa_ref, pb_ref, da_ref, db_ref, sp_ref, *, n_chunks, last):
    s = pl.program_id(2)
    c_len = SCAN_CHUNK

    @pl.when(s == 0)
    def _():
        st_ref[...] = jnp.zeros_like(st_ref)
        sp_ref[...] = jnp.zeros_like(sp_ref)

    row = lax.broadcasted_iota(jnp.int32, (c_len, c_len), 0)
    col = lax.broadcasted_iota(jnp.int32, (c_len, c_len), 1)
    causal = (col <= row, col >= row)
    tri = tuple(m.astype(F32).astype(BF16) for m in causal)
    end_row = (c_len - 1, 0)
    mid = c_len // 2
    qk_refs = ((qf_ref, kf_ref), (qb_ref, kb_ref))
    vo_refs = ((vf_ref, of_ref), (vb_ref, ob_ref))
    nt = (((1,), (1,)), ((), ()))
    tn = (((0,), (0,)), ((), ()))
    QT, KT, QH, KH = range(4)

    def chunk_of(dirn, r):
        return r if dirn == 0 else n_chunks - 1 - r

    def prepare(p_ref, d_ref, dirn, r, hh):
        q_ref, k_ref = qk_refs[dirn]
        c = chunk_of(dirn, r)
        rows = slice(c * c_len, (c + 1) * c_len)
        q = q_ref[hh, rows, :].astype(F32)
        k = k_ref[hh, rows, :].astype(F32)
        b = jnp.dot(tri[dirn], jnp.log(1.0 - k).astype(BF16), preferred_element_type=F32)
        tot = b[end_row[dirn]:end_row[dirn] + 1, :]
        ref = b[mid:mid + 1, :]
        dl = (b - ref) * LOG2E
        ends = jnp.maximum(jnp.abs(dl[0:1, :]), jnp.abs(dl[c_len - 1:c_len, :]))
        sp_ref[dirn, hh] = jnp.maximum(sp_ref[dirn, hh], ends)
        qt = q * jnp.exp2(dl)
        kt = k * jnp.exp2(-dl)
        p_ref[dirn, QT, hh, rows, :] = qt.astype(BF16)
        p_ref[dirn, KT, hh, rows, :] = kt.astype(BF16)
        p_ref[dirn, QH, hh, rows, :] = (qt * jnp.exp(ref)).astype(BF16)
        p_ref[dirn, KH, hh, rows, :] = (kt * jnp.exp(tot - ref)).astype(BF16)
        d_ref[dirn, c, hh] = jnp.exp(tot)

    def scores(p_ref, dirn, r, hh):
        v_ref = vo_refs[dirn][0]
        c = chunk_of(dirn, r)
        rows = slice(c * c_len, (c + 1) * c_len)
        sc = lax.dot_general(p_ref[dirn, QT, hh, rows, :], p_ref[dirn, KT, hh, rows, :],
                             nt, preferred_element_type=F32)
        prob = jnp.where(causal[dirn], sc, 0.0).astype(BF16)
        upd = lax.dot_general(v_ref[hh, rows, :], p_ref[dirn, KH, hh, rows, :],
                              tn, preferred_element_type=F32)
        return prob, upd

    def outputs(p_ref, d_ref, dirn, r, hh, prob, upd):
        v_ref, o_ref = vo_refs[dirn]
        c = chunk_of(dirn, r)
        rows = slice(c * c_len, (c + 1) * c_len)
        st = st_ref[dirn, hh]
        o = (jnp.dot(prob, v_ref[hh, rows, :], preferred_element_type=F32)
             + lax.dot_general(p_ref[dirn, QH, hh, rows, :], st.astype(BF16), nt,
                               preferred_element_type=F32))
        o_ref[hh, rows, :] = o.astype(BF16)
        st_ref[dirn, hh] = st * d_ref[dirn, c, hh] + upd

    def step(new, old, do_prepare=True, do_finish=True):
        (p_new, d_new), (p_old, d_old) = new, old
        heads = range(SCAN_HEADS)
        pu = {}
        if do_finish:
            pu = {(dirn, hh): scores(p_old, dirn, 0, hh) for hh in heads for dirn in range(2)}
        for r in range(n_chunks):
            pu_next = {}
            for hh in heads:
                for dirn in range(2):
                    if do_finish:
                        if r + 1 < n_chunks:
                            pu_next[dirn, hh] = scores(p_old, dirn, r + 1, hh)
                        outputs(p_old, d_old, dirn, r, hh, *pu[dirn, hh])
                    if do_prepare:
                        prepare(p_new, d_new, dirn, r, hh)
            pu = pu_next

    bufs = ((pa_ref, da_ref), (pb_ref, db_ref))
    inner = (s > 0) & (s < last)

    @pl.when(s == 0)
    def _():
        step(bufs[0], bufs[1], do_finish=False)

    @pl.when(inner & (s % 2 == 0))
    def _():
        step(bufs[0], bufs[1])

    @pl.when(inner & (s % 2 == 1))
    def _():
        step(bufs[1], bufs[0])

    @pl.when(s == last)
    def _():
        step(bufs[last % 2], bufs[1 - last % 2], do_prepare=False)
        span_ref[...] = sp_ref[...]


def _scan(a, *, batch, n_lat, n_ctx, e):
    ts = n_ctx
    assert ts % SCAN_CHUNK == 0 and n_lat % ts == 0
    lat_steps = n_lat // ts
    steps = 1 + lat_steps
    n_chunks = ts // SCAN_CHUNK
    assert HEAD_DIM == LANES
    groups = e // (SCAN_HEADS * HEAD_DIM)
    rows = batch * (n_ctx + n_lat)
    assert batch * n_ctx <= PROJ_TILE and PROJ_TILE % ts == 0

    def row_f(lat0):
        return lambda b, s: jnp.where(s == 0, b, lat0 + lat_steps * b + s - 1)

    def row_b(lat0):
        return lambda b, s: jnp.where(s == 0, b, lat0 + lat_steps * b + lat_steps - s)

    in_f, in_b = row_f(PROJ_TILE // ts), row_b(PROJ_TILE // ts)
    out_f, out_b = row_f(batch), row_b(batch)

    ahead = lambda g: jnp.minimum(g, steps - 1)
    behind = lambda g: jnp.maximum(g - 1, 0)

    def spec(stream, rfn, when):
        return pl.BlockSpec((SCAN_HEADS, ts, LANES),
                            lambda b, hg, g: (stream * groups + hg, rfn(b, when(g)), 0))

    kern = functools.partial(_scan_kernel, n_chunks=n_chunks, last=steps)
    operands = pltpu.VMEM((2, 4, SCAN_HEADS, ts, LANES), BF16)
    decays = pltpu.VMEM((2, n_chunks, SCAN_HEADS, 1, LANES), F32)
    return pl.pallas_call(
        kern,
        grid=(batch, groups, steps + 1),
        in_specs=[spec(0, in_f, ahead), spec(1, in_f, ahead), spec(3, in_f, behind),
                  spec(0, in_b, ahead), spec(2, in_b, ahead), spec(3, in_b, behind)],
        out_specs=[pl.BlockSpec((SCAN_HEADS, ts, LANES), lambda b, hg, g: (hg, out_f(b, behind(g)), 0)),
                   pl.BlockSpec((SCAN_HEADS, ts, LANES), lambda b, hg, g: (hg, out_b(b, behind(g)), 0)),
                   pl.BlockSpec((None, None, 2, SCAN_HEADS, 1, LANES), lambda b, hg, g: (b, hg, 0, 0, 0, 0))],
        out_shape=[jax.ShapeDtypeStruct((e // LANES, rows, LANES), BF16)] * 2
        + [jax.ShapeDtypeStruct((batch, groups, 2, SCAN_HEADS, 1, LANES), F32)],
        scratch_shapes=[pltpu.VMEM((2, SCAN_HEADS, HEAD_DIM, HEAD_DIM), F32),
                        operands, operands, decays, decays,
                        pltpu.VMEM((2, SCAN_HEADS, 1, LANES), F32)],
        compiler_params=_cparams(("arbitrary", "arbitrary", "arbitrary")),
        name="hgrn_scan",
    )(a, a, a, a, a, a)


def _scan_exact_kernel(q_ref, k_ref, v_ref, o_ref, st_ref, *, n_chunks, reverse):
    c_len = SCAN_CHUNK

    @pl.when(pl.program_id(2) == 0)
    def _():
        st_ref[...] = jnp.zeros_like(st_ref)

    lane = lax.broadcasted_iota(jnp.int32, (1, c_len), 1)

    def chunk(ci, carry):
        c = (n_chunks - 1 - ci) if reverse else ci
        rows = pl.ds(pl.multiple_of(c * c_len, c_len), c_len)
        q = q_ref[rows, :].astype(F32)
        k = k_ref[rows, :].astype(F32)
        f = 1.0 - k
        vt = v_ref[rows, :].astype(F32).T
        st = st_ref[...]
        ot = jnp.zeros((HEAD_DIM, c_len), F32)
        for step in range(c_len):
            t = c_len - 1 - step if reverse else step
            st = st * f[t:t + 1, :] + vt[:, t:t + 1] * k[t:t + 1, :]
            o_col = jnp.sum(st * q[t:t + 1, :], axis=-1, keepdims=True)
            ot = ot + o_col * (lane == t).astype(F32)
        st_ref[...] = st
        o_ref[rows, :] = ot.T.astype(BF16)
        return carry

    lax.fori_loop(0, n_chunks, chunk, 0)


def _scan_exact(a, *, batch, n_lat, n_ctx, e):
    ts = n_ctx
    lat_steps = n_lat // ts
    steps = 1 + lat_steps
    heads = e // HEAD_DIM
    rows = batch * (n_ctx + n_lat)
    outs = []
    for reverse, k_stream in ((False, 1), (True, 2)):
        def row(lat0, reverse=reverse):
            if reverse:
                return lambda b, s: jnp.where(s == 0, b, lat0 + lat_steps * b + lat_steps - s)
            return lambda b, s: jnp.where(s == 0, b, lat0 + lat_steps * b + s - 1)

        rin, rout = row(PROJ_TILE // ts), row(batch)

        def spec(stream, rin=rin):
            return pl.BlockSpec((None, ts, LANES), lambda b, h, s: (stream * heads + h, rin(b, s), 0))

        outs.append(pl.pallas_call(
            functools.partial(_scan_exact_kernel, n_chunks=ts // SCAN_CHUNK, reverse=reverse),
            grid=(batch, heads, steps),
            in_specs=[spec(0), spec(k_stream), spec(3)],
            out_specs=pl.BlockSpec((None, ts, LANES), lambda b, h, s, rout=rout: (h, rout(b, s), 0)),
            out_shape=jax.ShapeDtypeStruct((heads, rows, LANES), BF16),
            scratch_shapes=[pltpu.VMEM((HEAD_DIM, HEAD_DIM), F32)],
            compiler_params=_cparams(("arbitrary", "arbitrary", "arbitrary")),
            name="hgrn_scan_exact_bwd" if reverse else "hgrn_scan_exact_fwd",
        )(a, a, a))
    return outs


def _out0_kernel(of_ref, ob_ref, z_ref, ctx_ref, x_ref, mod_ref, hg_ref, w_ref,
                 xo_ref, co_ref, y_ref, *, d, e, tiles_per_batch):
    i = pl.program_id(0)
    heads = e // HEAD_DIM
    group = 4
    mix = None
    for h0 in range(0, heads, group):
        for h in range(h0, h0 + group):
            lanes = slice(h * HEAD_DIM, (h + 1) * HEAD_DIM)
            o = of_ref[h].astype(F32) + ob_ref[h].astype(F32)
            var = jnp.mean(o * o, axis=-1, keepdims=True)
            yn = (o * lax.rsqrt(var + EPS)) * hg_ref[...]
            y_ref[:, lanes] = (yn * z_ref[h].astype(F32)).astype(BF16)
        cols = slice(h0 * HEAD_DIM, (h0 + group) * HEAD_DIM)
        part = jnp.dot(y_ref[:, cols], w_ref[cols, :], preferred_element_type=F32)
        mix = part if mix is None else mix + part

    @pl.when(i == 0)
    def _():
        gate = mod_ref[2:3, 2 * d:3 * d]
        co_ref[...] = ctx_ref[...] + gate * mix

    @pl.when(i > 0)
    def _():
        gate = mod_ref[pl.ds((i - 1) // tiles_per_batch, 1), 2 * d:3 * d]
        xo_ref[...] = x_ref[...] + gate * mix


def _out0(o_f, o_b, a, ctx2d, x2d, mods, hg_norm_g, w_bf, *, n_lat):
    rows_ctx, d = ctx2d.shape
    rows_lat = x2d.shape[0]
    heads = o_f.shape[0]
    e = heads * HEAD_DIM
    tm = ROW_TILE
    assert rows_ctx == tm and PROJ_TILE % tm == 0
    n_tiles = 1 + rows_lat // tm
    kern = functools.partial(_out0_kernel, d=d, e=e, tiles_per_batch=n_lat // tm)
    lat_idx = lambda i: (jnp.maximum(i - 1, 0), 0)
    skip = PROJ_TILE // tm - 1
    src_row = lambda i: jnp.where(i == 0, 0, i + skip)
    return pl.pallas_call(
        kern,
        grid=(n_tiles,),
        in_specs=[
            pl.BlockSpec((heads, tm, HEAD_DIM), lambda i: (0, i, 0)),
            pl.BlockSpec((heads, tm, HEAD_DIM), lambda i: (0, i, 0)),
            pl.BlockSpec((heads, tm, HEAD_DIM), lambda i: (HG_STREAMS - 1, src_row(i), 0)),
            pl.BlockSpec((tm, d), lambda i: (0, 0)),
            pl.BlockSpec((tm, d), lat_idx),
            pl.BlockSpec((None, MOD_ROWS, 3 * d), lambda i: (0, 0, 0)),
            pl.BlockSpec((1, HEAD_DIM), lambda i: (0, 0)),
            pl.BlockSpec((e, d), lambda i: (0, 0)),
        ],
        out_specs=[pl.BlockSpec((tm, d), lat_idx), pl.BlockSpec((tm, d), lambda i: (0, 0))],
        out_shape=[jax.ShapeDtypeStruct((rows_lat, d), F32), jax.ShapeDtypeStruct((rows_ctx, d), F32)],
        scratch_shapes=[pltpu.VMEM((tm, e), BF16)],
        compiler_params=_cparams(("arbitrary",)),
        name="readout0",
    )(o_f, o_b, a, ctx2d, x2d, mods, hg_norm_g, w_bf)


def _proj1_kernel(x_ref, mod_ref, ng_ref, w_ref, u_ref, z_ref, h_ref,
                  *, d, e, tiles_per_batch, sub, n2):
    i = pl.program_id(0)
    tm = h_ref.shape[0]
    m = mod_ref[pl.ds(i // tiles_per_batch, 1), :]
    h_ref[...] = _modulated_norm(x_ref[...], ng_ref[...], m[:, 0:d], m[:, d:2 * d]).astype(BF16)

    def acc(c):
        return jnp.dot(h_ref[...], w_ref[:, c * sub:(c + 1) * sub], preferred_element_type=F32)

    nsub = e // sub

    for c in range(nsub):
        a = acc(c)
        for n1l in range(tm // n2):
            for m8 in range(n2 // SUBLANES):
                src = n1l * n2 + m8 * SUBLANES
                for cl in range(sub // LANES):
                    u_ref[m8, c * (sub // LANES) + cl, n1l * SUBLANES:(n1l + 1) * SUBLANES, :] = (
                        a[src:src + SUBLANES, cl * LANES:(cl + 1) * LANES])

    for c in range(nsub):
        z_ref[:, c * sub:(c + 1) * sub] = _silu_half(acc(nsub + c)).astype(BF16)


def _proj1(x2d, mods, norm_g, w_bf, *, batch, n_lat, layer):
    rows, d = x2d.shape
    e = w_bf.shape[1] // 2
    tm = PROJ_TILE
    n1, n2 = _dft_factors(n_lat)
    q = n2 // SUBLANES
    tpb = n_lat // tm
    rows_blk = (tm // n2) * SUBLANES
    kern = functools.partial(_proj1_kernel, d=d, e=e, tiles_per_batch=tpb, sub=512, n2=n2)
    return pl.pallas_call(
        kern,
        grid=(rows // tm,),
        in_specs=[
            pl.BlockSpec((tm, d), lambda i: (i, 0)),
            pl.BlockSpec((None, MOD_ROWS, 3 * d), lambda i: (layer, 0, 0)),
            pl.BlockSpec((None, 1, d), lambda i: (layer, 0, 0)),
            pl.BlockSpec((d, 2 * e), lambda i: (0, 0), pipeline_mode=pl.Buffered(1)),
        ],
        out_specs=[pl.BlockSpec((None, q, e // LANES, rows_blk, LANES),
                                lambda i: (i // tpb, 0, 0, i % tpb, 0)),
                   pl.BlockSpec((tm, e), lambda i: (i, 0))],
        out_shape=[jax.ShapeDtypeStruct((batch, q, e // LANES, n1 * SUBLANES, LANES), F32),
                   jax.ShapeDtypeStruct((rows, e), BF16)],
        scratch_shapes=[pltpu.VMEM((tm, d), BF16)],
        compiler_params=_cparams(("arbitrary",)),
        name="proj1",
    )(x2d, mods, norm_g, w_bf)


def _dft_factors(n):
    n1 = 128 if n % 128 == 0 else n
    return n1, n // n1


def _dft_tables(n, group_dim):
    n1, n2 = _dft_factors(n)
    a2 = np.arange(n2, dtype=np.float64)[:, None, None]
    k1 = np.arange(n1, dtype=np.float64)[None, :, None]
    m1 = np.arange(n1, dtype=np.float64)[None, None, :]
    ang = -2.0 * np.pi * (a2 * k1 / n + m1 * k1 / n1)
    stage1 = np.concatenate([np.cos(ang), np.sin(ang)], axis=1) / np.sqrt(n1)
    kk = np.arange(n2, dtype=np.float64)
    ang2 = -2.0 * np.pi * np.outer(kk, kk) / n2
    fr, fi = np.cos(ang2) / np.sqrt(n2), np.sin(ang2) / np.sqrt(n2)
    stage2 = np.stack([np.concatenate([fr, fi], axis=0),
                       np.concatenate([-fi, fr], axis=0)])
    cc = np.arange(group_dim, dtype=np.float64)
    ang3 = 2.0 * np.pi * np.outer(cc, cc) / group_dim
    chan = np.concatenate([np.cos(ang3), np.sin(ang3)], axis=0) / np.sqrt(group_dim)
    to = lambda t: jnp.asarray(t, dtype=F32).astype(BF16)
    return to(stage1), to(stage2), to(chan)


def _dft_kernel(m_ref, f_ref, cs_ref, x_ref, y_ref, tr_ref, ti_ref, ys_ref, *, n1, n2, gd, kt_step):
    half = pl.program_id(2)
    q, nlb = x_ref.shape[0], x_ref.shape[1]

    @pl.when(half == 0)
    def _():
        def stage1(m, carry):
            for r in range(SUBLANES):
                rows = pl.ds(r, n1, stride=SUBLANES)
                xr = jnp.concatenate([x_ref[m, lb, rows, :] for lb in range(nlb)], axis=1)
                t = jnp.dot(m_ref[m, r], xr.astype(BF16), preferred_element_type=F32)
                for lb in range(nlb):
                    lanes = slice(lb * LANES, (lb + 1) * LANES)
                    tr_ref[m, lb, rows, :] = t[0:n1, lanes]
                    ti_ref[m, lb, rows, :] = t[n1:2 * n1, lanes]
            return carry

        lax.fori_loop(0, q, stage1, 0)

    width = nlb * LANES

    def gather(ref, kt):
        cols = []
        for j in range(SUBLANES):
            start = pl.multiple_of((kt * SUBLANES + j) * SUBLANES, SUBLANES)
            cols += [ref[:, lb, pl.ds(start, SUBLANES), :].reshape(n2, LANES) for lb in range(nlb)]
        return jnp.concatenate(cols, axis=1).astype(BF16)

    def by_k1(g):
        return jnp.concatenate([g[:, j * width:(j + 1) * width] for j in range(SUBLANES)], axis=0)

    def stage2(kl, carry):
        kt = half * kt_step + kl
        g = (jnp.dot(f_ref[0], gather(tr_ref, kt), preferred_element_type=F32)
             + jnp.dot(f_ref[1], gather(ti_ref, kt), preferred_element_type=F32))
        gr = by_k1(g[0:n2]).astype(BF16)
        gi = by_k1(g[n2:2 * n2]).astype(BF16)
        yg = (jnp.dot(gr, cs_ref[0:gd, :], preferred_element_type=F32)
              + jnp.dot(gi, cs_ref[gd:2 * gd, :], preferred_element_type=F32))
        for j in range(SUBLANES):
            rows = pl.ds(j, n2, stride=SUBLANES)
            for lb in range(nlb):
                ys_ref[lb, rows, :] = yg[j * n2:(j + 1) * n2, lb * LANES:(lb + 1) * LANES]
        y_ref[kl] = ys_ref[...].astype(BF16)
        return carry

    lax.fori_loop(0, kt_step, stage2, 0)


def _dft(u, stage1, stage2, chan):
    batch, q, nlb, rows, _ = u.shape
    n2, two_n1, n1 = stage1.shape
    gd = chan.shape[1]
    glb = gd // LANES
    kt = n1 // SUBLANES
    kt_step = max(kt // 2, 1)
    m4 = stage1.reshape(q, SUBLANES, two_n1, n1)
    scratch = pltpu.VMEM((q, glb, rows, LANES), F32)
    return pl.pallas_call(
        functools.partial(_dft_kernel, n1=n1, n2=n2, gd=gd, kt_step=kt_step),
        grid=(batch, nlb // glb, kt // kt_step),
        in_specs=[pl.BlockSpec((q, SUBLANES, two_n1, n1), lambda b, g, h: (0, 0, 0, 0)),
                  pl.BlockSpec((2, 2 * n2, n2), lambda b, g, h: (0, 0, 0)),
                  pl.BlockSpec((2 * gd, gd), lambda b, g, h: (0, 0)),
                  pl.BlockSpec((None, q, glb, rows, LANES), lambda b, g, h: (b, 0, g, 0, 0))],
        out_specs=pl.BlockSpec((None, kt_step, glb, n2 * SUBLANES, LANES),
                               lambda b, g, h: (b, h, g, 0, 0)),
        out_shape=jax.ShapeDtypeStruct((batch, kt, nlb, n2 * SUBLANES, LANES), BF16),
        scratch_shapes=[scratch, scratch, pltpu.VMEM((glb, n2 * SUBLANES, LANES), F32)],
        compiler_params=_cparams(("arbitrary", "arbitrary", "arbitrary")),
        name="dft",
    )(m4, stage2, chan, u)


def _out1_kernel(y_in_ref, z_ref, x_ref, mod_ref, w_ref, fg_ref, o_ref, y_ref,
                 *, d, n1, tiles_per_batch):
    i = pl.program_id(0)
    kt, nlb = y_in_ref.shape[0], y_in_ref.shape[1]
    k2_per_tile = y_in_ref.shape[2] // SUBLANES
    pair = 2 * SUBLANES
    for t in range(k2_per_tile // 2):
        tile = slice(t * pair, (t + 1) * pair)
        for kp in range(kt // 2):
            blocks = [jnp.concatenate([y_in_ref[2 * kp + a, lb, tile, :] for lb in range(nlb)],
                                      axis=1).astype(F32) for a in range(2)]
            for s2 in range(2):
                k2l = 2 * t + s2
                dst = slice(k2l * n1 + kp * pair, k2l * n1 + (kp + 1) * pair)
                yv = jnp.concatenate([blk[s2 * SUBLANES:(s2 + 1) * SUBLANES] for blk in blocks], axis=0)
                y_ref[dst, :] = (yv * z_ref[dst, :].astype(F32)).astype(BF16)
    mix = jnp.dot(y_ref[...], w_ref[...], preferred_element_type=F32)
    gate = mod_ref[pl.ds(i // tiles_per_batch, 1), 2 * d:3 * d]
    x = x_ref[...] + gate * mix
    var = jnp.mean(x * x, axis=-1, keepdims=True)
    o_ref[...] = (x * lax.rsqrt(var + EPS)) * fg_ref[...]


def _out1(y_in, zs, x2d, mods, w_bf, final_g, *, n_lat, n1, layer):
    rows, d = x2d.shape
    batch, kt, nlb, _, _ = y_in.shape
    e = nlb * LANES
    tm = PROJ_TILE
    assert tm % n1 == 0
    tpb = n_lat // tm
    blk_rows = (tm // n1) * SUBLANES
    kern = functools.partial(_out1_kernel, d=d, n1=n1, tiles_per_batch=tpb)
    return pl.pallas_call(
        kern,
        grid=(rows // tm,),
        in_specs=[
            pl.BlockSpec((None, kt, nlb, blk_rows, LANES), lambda i: (i // tpb, 0, 0, i % tpb, 0)),
            pl.BlockSpec((tm, e), lambda i: (i, 0)),
            pl.BlockSpec((tm, d), lambda i: (i, 0)),
            pl.BlockSpec((None, MOD_ROWS, 3 * d), lambda i: (layer, 0, 0)),
            pl.BlockSpec((e, d), lambda i: (0, 0)),
            pl.BlockSpec((1, d), lambda i: (0, 0)),
        ],
        out_specs=pl.BlockSpec((tm, d), lambda i: (i, 0)),
        out_shape=jax.ShapeDtypeStruct((rows, d), F32),
        scratch_shapes=[pltpu.VMEM((tm, e), BF16)],
        compiler_params=_cparams(("arbitrary",)),
        name="readout1",
    )(y_in, zs, x2d, mods, w_bf, final_g)


def kernel(x, c, ctx, c_ctx, ada_w, ada_b, norm_g, hg_w_in, hg_lb_logits, hg_norm_g, hg_w_out,
           ft_w_in, ft_w_out, final_g):
    batch, n_lat, d = x.shape
    n_ctx = ctx.shape[1]
    depth = ada_w.shape[0]
    e = hg_w_out.shape[1]
    assert depth == 2 and batch == 2 and batch + 1 <= MOD_ROWS

    cv = jnp.concatenate([c, c_ctx[None, :], jnp.zeros((MOD_ROWS - batch - 1, d), F32)], axis=0)
    mods = _ada_table(cv, ada_w, ada_b)
    ng = norm_g.reshape(depth, 1, d)
    x2d = x.reshape(batch * n_lat, d)
    ctx2d = ctx.reshape(batch * n_ctx, d)

    ctx_pad = jnp.pad(ctx2d, ((0, PROJ_TILE - batch * n_ctx), (0, 0)))
    half_cols = jnp.where(jnp.arange(HG_STREAMS * e) // e == 3, 1.0, 0.5).astype(F32)
    a = _proj0(ctx_pad, x2d, mods, ng, (hg_w_in[0] * half_cols).astype(BF16), hg_lb_logits,
                  n_lat=n_lat, lb_index=0)
    o_f, o_b, span = _scan(a, batch=batch, n_lat=n_lat, n_ctx=n_ctx, e=e)
    o_f, o_b = lax.cond(jnp.logical_not(jnp.max(span) <= SPAN_LIMIT),
                        lambda: tuple(_scan_exact(a, batch=batch, n_lat=n_lat, n_ctx=n_ctx, e=e)),
                        lambda: (o_f, o_b))
    x1, _ctx1 = _out0(o_f, o_b, a, ctx2d, x2d, mods, hg_norm_g[0:1], hg_w_out[0].astype(BF16),
                      n_lat=n_lat)

    z_half = jnp.where(jnp.arange(2 * e) < e, 1.0, 0.5).astype(F32)
    u, zs = _proj1(x1, mods, ng, (ft_w_in[0] * z_half).astype(BF16), batch=batch, n_lat=n_lat,
                   layer=1)
    stage1, stage2, chan = _dft_tables(n_lat, e // FT_GROUPS)
    n1 = stage1.shape[2]
    y = _dft(u, stage1, stage2, chan)
    out = _out1(y, zs, x1, mods, ft_w_out[0].astype(BF16), final_g.reshape(1, d),
                n_lat=n_lat, n1=n1, layer=1)
    return out.reshape(batch, n_lat, d)
```

```python
import functools

import numpy as np
import jax
import jax.numpy as jnp
from jax import lax
from jax.experimental import pallas as pl
from jax.experimental.pallas import tpu as pltpu

F32 = jnp.float32
BF16 = jnp.bfloat16

EPS = 1e-6
LOG2E = 1.4426950408889634
SPAN_LIMIT = 100.0
HEAD_DIM = 128
HG_STREAMS = 5
FT_GROUPS = 8
SCAN_CHUNK = 64
SCAN_HEADS = 16
ROW_TILE = 512
PROJ_TILE = 1024
MOD_ROWS = 8
SUBLANES = 8
LANES = 128
VMEM_LIMIT = 56 * 1024 * 1024


def _cparams(sem):
    return pltpu.CompilerParams(dimension_semantics=sem, vmem_limit_bytes=VMEM_LIMIT)


def _sigmoid(x):
    return 0.5 * jnp.tanh(0.5 * x) + 0.5


def _silu(x):
    return _silu_half(0.5 * x)


def _silu_half(h):
    return h + h * jnp.tanh(h)


def _modulated_norm(x, g, shift, scale):
    var = jnp.mean(x * x, axis=-1, keepdims=True)
    return (x * lax.rsqrt(var + EPS)) * (g * (1.0 + scale)) + shift


def _ada_kernel(cv_ref, w_ref, b_ref, o_ref):
    a = _silu(cv_ref[...])
    o_ref[...] = jnp.dot(a, w_ref[...], preferred_element_type=F32,
                         precision=lax.Precision.HIGHEST) + b_ref[...]


def _ada_table(cv, ada_w, ada_b):
    depth, d, d3 = ada_w.shape
    tn = 1024
    return pl.pallas_call(
        _ada_kernel,
        grid=(depth, d3 // tn),
        in_specs=[
            pl.BlockSpec((MOD_ROWS, d), lambda l, j: (0, 0)),
            pl.BlockSpec((None, d, tn), lambda l, j: (l, 0, j)),
            pl.BlockSpec((None, 1, tn), lambda l, j: (l, 0, j)),
        ],
        out_specs=pl.BlockSpec((None, MOD_ROWS, tn), lambda l, j: (l, 0, j)),
        out_shape=jax.ShapeDtypeStruct((depth, MOD_ROWS, d3), F32),
        compiler_params=_cparams(("arbitrary", "arbitrary")),
        name="ada_table",
    )(cv, ada_w, ada_b.reshape(depth, 1, d3))


def _proj0_kernel(ctx_ref, x_ref, mod_ref, ng_ref, w_ref, lbl_ref, a_ref, h_ref,
                  *, d, e, tiles_per_batch, lb_index, sub):
    i = pl.program_id(0)
    j = pl.program_id(1)
    tm = h_ref.shape[0]

    @pl.when(j == 0)
    def _():
        is_ctx = i == 0
        m = mod_ref[pl.ds(jnp.where(is_ctx, 2, (i - 1) // tiles_per_batch), 1), :]
        src = jnp.where(is_ctx, ctx_ref[...], x_ref[...])
        h_ref[...] = _modulated_norm(src, ng_ref[...], m[:, 0:d], m[:, d:2 * d]).astype(BF16)

    def acc(c):
        return jnp.dot(h_ref[...], w_ref[:, c * sub:(c + 1) * sub], preferred_element_type=F32)

    nsub = w_ref.shape[1] // sub
    per_sub = sub // LANES
    per_stream = e // sub

    def put(c, val):
        for cl in range(per_sub):
            a_ref[c * per_sub + cl] = val[:, cl * LANES:(cl + 1) * LANES]

    def half_gate(dirn, cols):
        logits = lbl_ref[:, dirn:dirn + 1, cols]
        mx = jnp.max(logits, axis=0, keepdims=True)
        ex = jnp.exp(logits - mx)
        p = ex / jnp.sum(ex, axis=0, keepdims=True)
        return 0.5 * (1.0 - jnp.sum(p[0:lb_index + 1], axis=0))

    for step in range(pl.cdiv(HG_STREAMS * e, w_ref.shape[1])):
        @pl.when(j == step)
        def _(step=step):
            for c in range(nsub):
                stream, cs = divmod(step * nsub + c, per_stream)
                if stream in (0, 4):
                    put(c, _silu_half(acc(c)).astype(BF16))
                elif stream == 3:
                    put(c, acc(c).astype(BF16))
                else:
                    half = half_gate(stream - 1, slice(cs * sub, (cs + 1) * sub))
                    put(c, (half * (1.0 - jnp.tanh(acc(c)))).astype(BF16))


def _proj0(ctx2d, x2d, mods, norm_g, w_bf, lb_logits, *, n_lat, lb_index):
    rows_ctx, d = ctx2d.shape
    rows_lat = x2d.shape[0]
    e = w_bf.shape[1] // HG_STREAMS
    tm = PROJ_TILE
    assert rows_ctx == tm and n_lat % tm == 0
    n_tiles = 1 + rows_lat // tm
    rows = tm + rows_lat
    nl = lb_logits.shape[0]
    sub = 512
    col_steps = HG_STREAMS - 1
    tn = HG_STREAMS * e // col_steps
    assert tn % sub == 0 and e % sub == 0
    kern = functools.partial(_proj0_kernel, d=d, e=e, tiles_per_batch=n_lat // tm,
                             lb_index=lb_index, sub=sub)
    return pl.pallas_call(
        kern,
        grid=(n_tiles, col_steps),
        in_specs=[
            pl.BlockSpec((tm, d), lambda i, j: (0, 0)),
            pl.BlockSpec((tm, d), lambda i, j: (jnp.maximum(i - 1, 0), 0)),
            pl.BlockSpec((None, MOD_ROWS, 3 * d), lambda i, j: (0, 0, 0)),
            pl.BlockSpec((None, 1, d), lambda i, j: (0, 0, 0)),
            pl.BlockSpec((d, tn), lambda i, j: (0, j)),
            pl.BlockSpec((nl, 2, e), lambda i, j: (0, 0, 0)),
        ],
        out_specs=pl.BlockSpec((tn // LANES, tm, LANES), lambda i, j: (j, i, 0)),
        out_shape=jax.ShapeDtypeStruct((HG_STREAMS * (e // LANES), rows, LANES), BF16),
        scratch_shapes=[pltpu.VMEM((tm, d), BF16)],
        compiler_params=_cparams(("arbitrary", "arbitrary")),
        name="proj0",
    )(ctx2d, x2d, mods, norm_g, w_bf, lb_logits)


def _scan_kernel(qf_ref, kf_ref, vf_ref, qb_ref, kb_ref, vb_ref, of_ref, ob_ref, span_ref,
                 st_ref, pa_ref, pb_ref, da_ref, db_ref, sp_ref, *, n_chunks):
    s = pl.program_id(2)
    c_len = SCAN_CHUNK

    @pl.when(s == 0)
    def _():
        st_ref[...] = jnp.zeros_like(st_ref)
        pb_ref[...] = jnp.zeros_like(pb_ref)
        db_ref[...] = jnp.zeros_like(db_ref)
        sp_ref[...] = jnp.zeros_like(sp_ref)

    row = lax.broadcasted_iota(jnp.int32, (c_len, c_len), 0)
    col = lax.broadcasted_iota(jnp.int32, (c_len, c_len), 1)
    causal = (col <= row, col >= row)
    tri = tuple(m.astype(F32).astype(BF16) for m in causal)
    end_row = (c_len - 1, 0)
    mid = c_len // 2
    qk_refs = ((qf_ref, kf_ref), (qb_ref, kb_ref))
    vo_refs = ((vf_ref, of_ref), (vb_ref, ob_ref))
    nt = (((1,), (1,)), ((), ()))
    tn = (((0,), (0,)), ((), ()))
    QT, KT, QH, KH = range(4)

    def chunk_of(dirn, r):
        return r if dirn == 0 else n_chunks - 1 - r

    def prepare(p_ref, d_ref, dirn, r, hh):
        q_ref, k_ref = qk_refs[dirn]
        c = chunk_of(dirn, r)
        rows = slice(c * c_len, (c + 1) * c_len)
        q = q_ref[hh, rows, :].astype(F32)
        k = k_ref[hh, rows, :].astype(F32)
        b = jnp.dot(tri[dirn], jnp.log(1.0 - k).astype(BF16), preferred_element_type=F32)
        tot = b[end_row[dirn]:end_row[dirn] + 1, :]
        ref = b[mid:mid + 1, :]
        dl = (b - ref) * LOG2E
        ends = jnp.maximum(jnp.abs(dl[0:1, :]), jnp.abs(dl[c_len - 1:c_len, :]))
        sp_ref[dirn, hh] = jnp.maximum(sp_ref[dirn, hh], ends)
        qt = q * jnp.exp2(dl)
        kt = k * jnp.exp2(-dl)
        p_ref[dirn, QT, hh, rows, :] = qt.astype(BF16)
        p_ref[dirn, KT, hh, rows, :] = kt.astype(BF16)
        p_ref[dirn, QH, hh, rows, :] = (qt * jnp.exp(ref)).astype(BF16)
        p_ref[dirn, KH, hh, rows, :] = (kt * jnp.exp(tot - ref)).astype(BF16)
        d_ref[dirn, c, hh] = jnp.exp(tot)

    def scores(p_ref, dirn, r, hh):
        v_ref = vo_refs[dirn][0]
        c = chunk_of(dirn, r)
        rows = slice(c * c_len, (c + 1) * c_len)
        sc = lax.dot_general(p_ref[dirn, QT, hh, rows, :], p_ref[dirn, KT, hh, rows, :],
                             nt, preferred_element_type=F32)
        prob = jnp.where(causal[dirn], sc, 0.0).astype(BF16)
        upd = lax.dot_general(v_ref[hh, rows, :], p_ref[dirn, KH, hh, rows, :],
                              tn, preferred_element_type=F32)
        return prob, upd

    def outputs(p_ref, d_ref, dirn, r, hh, prob, upd):
        v_ref, o_ref = vo_refs[dirn]
        c = chunk_of(dirn, r)
        rows = slice(c * c_len, (c + 1) * c_len)
        st = st_ref[dirn, hh]
        o = (jnp.dot(prob, v_ref[hh, rows, :], preferred_element_type=F32)
             + lax.dot_general(p_ref[dirn, QH, hh, rows, :], st.astype(BF16), nt,
                               preferred_element_type=F32))
        o_ref[hh, rows, :] = o.astype(BF16)
        st_ref[dirn, hh] = st * d_ref[dirn, c, hh] + upd

    def step(p_new, d_new, p_old, d_old):
        heads = range(SCAN_HEADS)
        pu = {(dirn, hh): scores(p_old, dirn, 0, hh) for hh in heads for dirn in range(2)}
        for r in range(n_chunks):
            pu_next = {}
            for hh in heads:
                for dirn in range(2):
                    if r + 1 < n_chunks:
                        pu_next[dirn, hh] = scores(p_old, dirn, r + 1, hh)
                    outputs(p_old, d_old, dirn, r, hh, *pu[dirn, hh])
                    prepare(p_new, d_new, dirn, r, hh)
            pu = pu_next

    @pl.when(s % 2 == 0)
    def _():
        step(pa_ref, da_ref, pb_ref, db_ref)

    @pl.when(s % 2 == 1)
    def _():
        step(pb_ref, db_ref, pa_ref, da_ref)

    @pl.when(s == pl.num_programs(2) - 1)
    def _():
        span_ref[...] = sp_ref[...]


def _scan(a, *, batch, n_lat, n_ctx, e):
    ts = n_ctx
    assert ts % SCAN_CHUNK == 0 and n_lat % ts == 0
    lat_steps = n_lat // ts
    steps = 1 + lat_steps
    n_chunks = ts // SCAN_CHUNK
    assert HEAD_DIM == LANES
    groups = e // (SCAN_HEADS * HEAD_DIM)
    rows = batch * (n_ctx + n_lat)
    assert batch * n_ctx <= PROJ_TILE and PROJ_TILE % ts == 0

    def row_f(lat0):
        return lambda b, s: jnp.where(s == 0, b, lat0 + lat_steps * b + s - 1)

    def row_b(lat0):
        return lambda b, s: jnp.where(s == 0, b, lat0 + lat_steps * b + lat_steps - s)

    in_f, in_b = row_f(PROJ_TILE // ts), row_b(PROJ_TILE // ts)
    out_f, out_b = row_f(batch), row_b(batch)

    ahead = lambda g: jnp.minimum(g, steps - 1)
    behind = lambda g: jnp.maximum(g - 1, 0)

    def spec(stream, rfn, when):
        return pl.BlockSpec((SCAN_HEADS, ts, LANES),
                            lambda b, hg, g: (stream * groups + hg, rfn(b, when(g)), 0))

    kern = functools.partial(_scan_kernel, n_chunks=n_chunks)
    operands = pltpu.VMEM((2, 4, SCAN_HEADS, ts, LANES), BF16)
    decays = pltpu.VMEM((2, n_chunks, SCAN_HEADS, 1, LANES), F32)
    return pl.pallas_call(
        kern,
        grid=(batch, groups, steps + 1),
        in_specs=[spec(0, in_f, ahead), spec(1, in_f, ahead), spec(3, in_f, behind),
                  spec(0, in_b, ahead), spec(2, in_b, ahead), spec(3, in_b, behind)],
        out_specs=[pl.BlockSpec((SCAN_HEADS, ts, LANES), lambda b, hg, g: (hg, out_f(b, behind(g)), 0)),
                   pl.BlockSpec((SCAN_HEADS, ts, LANES), lambda b, hg, g: (hg, out_b(b, behind(g)), 0)),
                   pl.BlockSpec((None, None, 2, SCAN_HEADS, 1, LANES), lambda b, hg, g: (b, hg, 0, 0, 0, 0))],
        out_shape=[jax.ShapeDtypeStruct((e // LANES, rows, LANES), BF16)] * 2
        + [jax.ShapeDtypeStruct((batch, groups, 2, SCAN_HEADS, 1, LANES), F32)],
        scratch_shapes=[pltpu.VMEM((2, SCAN_HEADS, HEAD_DIM, HEAD_DIM), F32),
                        operands, operands, decays, decays,
                        pltpu.VMEM((2, SCAN_HEADS, 1, LANES), F32)],
        compiler_params=_cparams(("arbitrary", "arbitrary", "arbitrary")),
        name="hgrn_scan",
    )(a, a, a, a, a, a)


def _scan_exact_kernel(q_ref, k_ref, v_ref, o_ref, st_ref, *, n_chunks, reverse):
    c_len = SCAN_CHUNK

    @pl.when(pl.program_id(2) == 0)
    def _():
        st_ref[...] = jnp.zeros_like(st_ref)

    lane = lax.broadcasted_iota(jnp.int32, (1, c_len), 1)

    def chunk(ci, carry):
        c = (n_chunks - 1 - ci) if reverse else ci
        rows = pl.ds(pl.multiple_of(c * c_len, c_len), c_len)
        q = q_ref[rows, :].astype(F32)
        k = k_ref[rows, :].astype(F32)
        f = 1.0 - k
        vt = v_ref[rows, :].astype(F32).T
        st = st_ref[...]
        ot = jnp.zeros((HEAD_DIM, c_len), F32)
        for step in range(c_len):
            t = c_len - 1 - step if reverse else step
            st = st * f[t:t + 1, :] + vt[:, t:t + 1] * k[t:t + 1, :]
            o_col = jnp.sum(st * q[t:t + 1, :], axis=-1, keepdims=True)
            ot = ot + o_col * (lane == t).astype(F32)
        st_ref[...] = st
        o_ref[rows, :] = ot.T.astype(BF16)
        return carry

    lax.fori_loop(0, n_chunks, chunk, 0)


def _scan_exact(a, *, batch, n_lat, n_ctx, e):
    ts = n_ctx
    lat_steps = n_lat // ts
    steps = 1 + lat_steps
    heads = e // HEAD_DIM
    rows = batch * (n_ctx + n_lat)
    outs = []
    for reverse, k_stream in ((False, 1), (True, 2)):
        def row(lat0, reverse=reverse):
            if reverse:
                return lambda b, s: jnp.where(s == 0, b, lat0 + lat_steps * b + lat_steps - s)
            return lambda b, s: jnp.where(s == 0, b, lat0 + lat_steps * b + s - 1)

        rin, rout = row(PROJ_TILE // ts), row(batch)

        def spec(stream, rin=rin):
            return pl.BlockSpec((None, ts, LANES), lambda b, h, s: (stream * heads + h, rin(b, s), 0))

        outs.append(pl.pallas_call(
            functools.partial(_scan_exact_kernel, n_chunks=ts // SCAN_CHUNK, reverse=reverse),
            grid=(batch, heads, steps),
            in_specs=[spec(0), spec(k_stream), spec(3)],
            out_specs=pl.BlockSpec((None, ts, LANES), lambda b, h, s, rout=rout: (h, rout(b, s), 0)),
            out_shape=jax.ShapeDtypeStruct((heads, rows, LANES), BF16),
            scratch_shapes=[pltpu.VMEM((HEAD_DIM, HEAD_DIM), F32)],
            compiler_params=_cparams(("arbitrary", "arbitrary", "arbitrary")),
            name="hgrn_scan_exact_bwd" if reverse else "hgrn_scan_exact_fwd",
        )(a, a, a))
    return outs


OUT0_SLOTS = 3


def _out0_kernel(of_hbm, ob_hbm, a_hbm, ctx_ref, x_ref, mod_ref, hg_ref, w_ref,
                 xo_ref, co_ref, of_buf, ob_buf, z_buf, sems, y_ref,
                 *, d, e, tiles_per_batch, tm, skip, z_slab0):
    i = pl.program_id(0)
    n = pl.num_programs(0)
    heads = e // HEAD_DIM

    def copies(step, slot):
        rows = pl.ds(pl.multiple_of(step * tm, tm), tm)
        z_rows = pl.ds(pl.multiple_of(jnp.where(step == 0, 0, step + skip) * tm, tm), tm)
        return (
            pltpu.make_async_copy(of_hbm.at[:, rows, :], of_buf.at[slot], sems.at[0, slot]),
            pltpu.make_async_copy(ob_hbm.at[:, rows, :], ob_buf.at[slot], sems.at[1, slot]),
            pltpu.make_async_copy(a_hbm.at[pl.ds(z_slab0, heads), z_rows, :], z_buf.at[slot],
                                  sems.at[2, slot]),
        )

    @pl.when(i == 0)
    def _():
        for first in range(OUT0_SLOTS - 1):
            for cp in copies(first, first):
                cp.start()

    ahead = i + OUT0_SLOTS - 1

    @pl.when(ahead < n)
    def _():
        for cp in copies(ahead, ahead % OUT0_SLOTS):
            cp.start()

    for cp in copies(i, i % OUT0_SLOTS):
        cp.wait()

    def body(of_ref, ob_ref, z_ref):
        group = 4
        mix = None
        for h0 in range(0, heads, group):
            for h in range(h0, h0 + group):
                lanes = slice(h * HEAD_DIM, (h + 1) * HEAD_DIM)
                o = of_ref[h].astype(F32) + ob_ref[h].astype(F32)
                var = jnp.mean(o * o, axis=-1, keepdims=True)
                yn = (o * lax.rsqrt(var + EPS)) * hg_ref[...]
                y_ref[:, lanes] = (yn * z_ref[h].astype(F32)).astype(BF16)
            cols = slice(h0 * HEAD_DIM, (h0 + group) * HEAD_DIM)
            part = jnp.dot(y_ref[:, cols], w_ref[cols, :], preferred_element_type=F32)
            mix = part if mix is None else mix + part

        @pl.when(i == 0)
        def _():
            gate = mod_ref[2:3, 2 * d:3 * d]
            co_ref[...] = ctx_ref[...] + gate * mix

        @pl.when(i > 0)
        def _():
            gate = mod_ref[pl.ds((i - 1) // tiles_per_batch, 1), 2 * d:3 * d]
            xo_ref[...] = x_ref[...] + gate * mix

    for slot in range(OUT0_SLOTS):
        @pl.when(i % OUT0_SLOTS == slot)
        def _(slot=slot):
            body(of_buf.at[slot], ob_buf.at[slot], z_buf.at[slot])


def _out0(o_f, o_b, a, ctx2d, x2d, mods, hg_norm_g, w_bf, *, n_lat):
    rows_ctx, d = ctx2d.shape
    rows_lat = x2d.shape[0]
    heads = o_f.shape[0]
    e = heads * HEAD_DIM
    tm = ROW_TILE
    assert rows_ctx == tm and PROJ_TILE % tm == 0
    n_tiles = 1 + rows_lat // tm
    skip = PROJ_TILE // tm - 1
    assert n_tiles >= OUT0_SLOTS
    kern = functools.partial(_out0_kernel, d=d, e=e, tiles_per_batch=n_lat // tm, tm=tm, skip=skip,
                             z_slab0=(HG_STREAMS - 1) * heads)
    lat_idx = lambda i: (jnp.maximum(i - 1, 0), 0)
    ring = pltpu.VMEM((OUT0_SLOTS, heads, tm, HEAD_DIM), BF16)
    return pl.pallas_call(
        kern,
        grid=(n_tiles,),
        in_specs=[
            pl.BlockSpec(memory_space=pl.ANY),
            pl.BlockSpec(memory_space=pl.ANY),
            pl.BlockSpec(memory_space=pl.ANY),
            pl.BlockSpec((tm, d), lambda i: (0, 0)),
            pl.BlockSpec((tm, d), lat_idx),
            pl.BlockSpec((None, MOD_ROWS, 3 * d), lambda i: (0, 0, 0)),
            pl.BlockSpec((1, HEAD_DIM), lambda i: (0, 0)),
            pl.BlockSpec((e, d), lambda i: (0, 0)),
        ],
        out_specs=[pl.BlockSpec((tm, d), lat_idx), pl.BlockSpec((tm, d), lambda i: (0, 0))],
        out_shape=[jax.ShapeDtypeStruct((rows_lat, d), F32), jax.ShapeDtypeStruct((rows_ctx, d), F32)],
        scratch_shapes=[ring, ring, ring, pltpu.SemaphoreType.DMA((3, OUT0_SLOTS)),
                        pltpu.VMEM((tm, e), BF16)],
        compiler_params=_cparams(("arbitrary",)),
        name="readout0",
    )(o_f, o_b, a, ctx2d, x2d, mods, hg_norm_g, w_bf)


def _proj1_kernel(x_ref, mod_ref, ng_ref, w_ref, u_ref, z_ref, h_ref,
                  *, d, e, tiles_per_batch, sub, n2):
    i = pl.program_id(0)
    tm = h_ref.shape[0]
    m = mod_ref[pl.ds(i // tiles_per_batch, 1), :]
    h_ref[...] = _modulated_norm(x_ref[...], ng_ref[...], m[:, 0:d], m[:, d:2 * d]).astype(BF16)

    def acc(c):
        return jnp.dot(h_ref[...], w_ref[:, c * sub:(c + 1) * sub], preferred_element_type=F32)

    nsub = e // sub

    for c in range(nsub):
        a = acc(c)
        for n1l in range(tm // n2):
            for m8 in range(n2 // SUBLANES):
                src = n1l * n2 + m8 * SUBLANES
                for cl in range(sub // LANES):
                    u_ref[m8, c * (sub // LANES) + cl, n1l * SUBLANES:(n1l + 1) * SUBLANES, :] = (
                        a[src:src + SUBLANES, cl * LANES:(cl + 1) * LANES])

    for c in range(nsub):
        z_ref[:, c * sub:(c + 1) * sub] = _silu_half(acc(nsub + c)).astype(BF16)


def _proj1(x2d, mods, norm_g, w_bf, *, batch, n_lat, layer):
    rows, d = x2d.shape
    e = w_bf.shape[1] // 2
    tm = PROJ_TILE
    n1, n2 = _dft_factors(n_lat)
    q = n2 // SUBLANES
    tpb = n_lat // tm
    rows_blk = (tm // n2) * SUBLANES
    kern = functools.partial(_proj1_kernel, d=d, e=e, tiles_per_batch=tpb, sub=512, n2=n2)
    return pl.pallas_call(
        kern,
        grid=(rows // tm,),
        in_specs=[
            pl.BlockSpec((tm, d), lambda i: (i, 0)),
            pl.BlockSpec((None, MOD_ROWS, 3 * d), lambda i: (layer, 0, 0)),
            pl.BlockSpec((None, 1, d), lambda i: (layer, 0, 0)),
            pl.BlockSpec((d, 2 * e), lambda i: (0, 0), pipeline_mode=pl.Buffered(1)),
        ],
        out_specs=[pl.BlockSpec((None, q, e // LANES, rows_blk, LANES),
                                lambda i: (i // tpb, 0, 0, i % tpb, 0)),
                   pl.BlockSpec((tm, e), lambda i: (i, 0))],
        out_shape=[jax.ShapeDtypeStruct((batch, q, e // LANES, n1 * SUBLANES, LANES), F32),
                   jax.ShapeDtypeStruct((rows, e), BF16)],
        scratch_shapes=[pltpu.VMEM((tm, d), BF16)],
        compiler_params=_cparams(("arbitrary",)),
        name="proj1",
    )(x2d, mods, norm_g, w_bf)


def _dft_factors(n):
    n1 = 128 if n % 128 == 0 else n
    return n1, n // n1


def _dft_tables(n, group_dim):
    n1, n2 = _dft_factors(n)
    a2 = np.arange(n2, dtype=np.float64)[:, None, None]
    k1 = np.arange(n1, dtype=np.float64)[None, :, None]
    m1 = np.arange(n1, dtype=np.float64)[None, None, :]
    ang = -2.0 * np.pi * (a2 * k1 / n + m1 * k1 / n1)
    stage1 = np.concatenate([np.cos(ang), np.sin(ang)], axis=1) / np.sqrt(n1)
    kk = np.arange(n2, dtype=np.float64)
    ang2 = -2.0 * np.pi * np.outer(kk, kk) / n2
    fr, fi = np.cos(ang2) / np.sqrt(n2), np.sin(ang2) / np.sqrt(n2)
    stage2 = np.stack([np.concatenate([fr, fi], axis=0),
                       np.concatenate([-fi, fr], axis=0)])
    cc = np.arange(group_dim, dtype=np.float64)
    ang3 = 2.0 * np.pi * np.outer(cc, cc) / group_dim
    chan = np.concatenate([np.cos(ang3), np.sin(ang3)], axis=0) / np.sqrt(group_dim)
    to = lambda t: jnp.asarray(t, dtype=F32).astype(BF16)
    return to(stage1), to(stage2), to(chan)


def _dft_kernel(m_ref, f_ref, cs_ref, x_ref, y_ref, tr_ref, ti_ref, ys_ref, *, n1, n2, gd, kt_step):
    half = pl.program_id(2)
    q, nlb = x_ref.shape[0], x_ref.shape[1]

    @pl.when(half == 0)
    def _():
        def stage1(m, carry):
            for r in range(SUBLANES):
                rows = pl.ds(r, n1, stride=SUBLANES)
                xr = jnp.concatenate([x_ref[m, lb, rows, :] for lb in range(nlb)], axis=1)
                t = jnp.dot(m_ref[m, r], xr.astype(BF16), preferred_element_type=F32)
                for lb in range(nlb):
                    lanes = slice(lb * LANES, (lb + 1) * LANES)
                    tr_ref[m, lb, rows, :] = t[0:n1, lanes]
                    ti_ref[m, lb, rows, :] = t[n1:2 * n1, lanes]
            return carry

        lax.fori_loop(0, q, stage1, 0)

    width = nlb * LANES

    def gather(ref, kt):
        cols = []
        for j in range(SUBLANES):
            start = pl.multiple_of((kt * SUBLANES + j) * SUBLANES, SUBLANES)
            cols += [ref[:, lb, pl.ds(start, SUBLANES), :].reshape(n2, LANES) for lb in range(nlb)]
        return jnp.concatenate(cols, axis=1).astype(BF16)

    def by_k1(g):
        return jnp.concatenate([g[:, j * width:(j + 1) * width] for j in range(SUBLANES)], axis=0)

    def stage2(kl, carry):
        kt = half * kt_step + kl
        g = (jnp.dot(f_ref[0], gather(tr_ref, kt), preferred_element_type=F32)
             + jnp.dot(f_ref[1], gather(ti_ref, kt), preferred_element_type=F32))
        gr = by_k1(g[0:n2]).astype(BF16)
        gi = by_k1(g[n2:2 * n2]).astype(BF16)
        yg = (jnp.dot(gr, cs_ref[0:gd, :], preferred_element_type=F32)
              + jnp.dot(gi, cs_ref[gd:2 * gd, :], preferred_element_type=F32))
        for j in range(SUBLANES):
            rows = pl.ds(j, n2, stride=SUBLANES)
            for lb in range(nlb):
                ys_ref[lb, rows, :] = yg[j * n2:(j + 1) * n2, lb * LANES:(lb + 1) * LANES]
        y_ref[kl] = ys_ref[...].astype(BF16)
        return carry

    lax.fori_loop(0, kt_step, stage2, 0)


def _dft(u, stage1, stage2, chan):
    batch, q, nlb, rows, _ = u.shape
    n2, two_n1, n1 = stage1.shape
    gd = chan.shape[1]
    glb = gd // LANES
    kt = n1 // SUBLANES
    kt_step = max(kt // 2, 1)
    m4 = stage1.reshape(q, SUBLANES, two_n1, n1)
    scratch = pltpu.VMEM((q, glb, rows, LANES), F32)
    return pl.pallas_call(
        functools.partial(_dft_kernel, n1=n1, n2=n2, gd=gd, kt_step=kt_step),
        grid=(batch, nlb // glb, kt // kt_step),
        in_specs=[pl.BlockSpec((q, SUBLANES, two_n1, n1), lambda b, g, h: (0, 0, 0, 0)),
                  pl.BlockSpec((2, 2 * n2, n2), lambda b, g, h: (0, 0, 0)),
                  pl.BlockSpec((2 * gd, gd), lambda b, g, h: (0, 0)),
                  pl.BlockSpec((None, q, glb, rows, LANES), lambda b, g, h: (b, 0, g, 0, 0))],
        out_specs=pl.BlockSpec((None, kt_step, glb, n2 * SUBLANES, LANES),
                               lambda b, g, h: (b, h, g, 0, 0)),
        out_shape=jax.ShapeDtypeStruct((batch, kt, nlb, n2 * SUBLANES, LANES), BF16),
        scratch_shapes=[scratch, scratch, pltpu.VMEM((glb, n2 * SUBLANES, LANES), F32)],
        compiler_params=_cparams(("arbitrary", "arbitrary", "arbitrary")),
        name="dft",
    )(m4, stage2, chan, u)


def _out1_kernel(y_in_ref, z_ref, x_ref, mod_ref, w_ref, fg_ref, o_ref, y_ref,
                 *, d, n1, tiles_per_batch):
    i = pl.program_id(0)
    kt, nlb = y_in_ref.shape[0], y_in_ref.shape[1]
    k2_per_tile = y_in_ref.shape[2] // SUBLANES
    pair = 2 * SUBLANES
    for t in range(k2_per_tile // 2):
        tile = slice(t * pair, (t + 1) * pair)
        for kp in range(kt // 2):
            blocks = [jnp.concatenate([y_in_ref[2 * kp + a, lb, tile, :] for lb in range(nlb)],
                                      axis=1).astype(F32) for a in range(2)]
            for s2 in range(2):
                k2l = 2 * t + s2
                dst = slice(k2l * n1 + kp * pair, k2l * n1 + (kp + 1) * pair)
                yv = jnp.concatenate([blk[s2 * SUBLANES:(s2 + 1) * SUBLANES] for blk in blocks], axis=0)
                y_ref[dst, :] = (yv * z_ref[dst, :].astype(F32)).astype(BF16)
    mix = jnp.dot(y_ref[...], w_ref[...], preferred_element_type=F32)
    gate = mod_ref[pl.ds(i // tiles_per_batch, 1), 2 * d:3 * d]
    x = x_ref[...] + gate * mix
    var = jnp.mean(x * x, axis=-1, keepdims=True)
    o_ref[...] = (x * lax.rsqrt(var + EPS)) * fg_ref[...]


def _out1(y_in, zs, x2d, mods, w_bf, final_g, *, n_lat, n1, layer):
    rows, d = x2d.shape
    batch, kt, nlb, _, _ = y_in.shape
    e = nlb * LANES
    tm = PROJ_TILE
    assert tm % n1 == 0
    tpb = n_lat // tm
    blk_rows = (tm // n1) * SUBLANES
    kern = functools.partial(_out1_kernel, d=d, n1=n1, tiles_per_batch=tpb)
    return pl.pallas_call(
        kern,
        grid=(rows // tm,),
        in_specs=[
            pl.BlockSpec((None, kt, nlb, blk_rows, LANES), lambda i: (i // tpb, 0, 0, i % tpb, 0)),
            pl.BlockSpec((tm, e), lambda i: (i, 0)),
            pl.BlockSpec((tm, d), lambda i: (i, 0)),
            pl.BlockSpec((None, MOD_ROWS, 3 * d), lambda i: (layer, 0, 0)),
            pl.BlockSpec((e, d), lambda i: (0, 0)),
            pl.BlockSpec((1, d), lambda i: (0, 0)),
        ],
        out_specs=pl.BlockSpec((tm, d), lambda i: (i, 0)),
        out_shape=jax.ShapeDtypeStruct((rows, d), F32),
        scratch_shapes=[pltpu.VMEM((tm, e), BF16)],
        compiler_params=_cparams(("arbitrary",)),
        name="readout1",
    )(y_in, zs, x2d, mods, w_bf, final_g)


def kernel(x, c, ctx, c_ctx, ada_w, ada_b, norm_g, hg_w_in, hg_lb_logits, hg_norm_g, hg_w_out,
           ft_w_in, ft_w_out, final_g):
    batch, n_lat, d = x.shape
    n_ctx = ctx.shape[1]
    depth = ada_w.shape[0]
    e = hg_w_out.shape[1]
    assert depth == 2 and batch == 2 and batch + 1 <= MOD_ROWS

    cv = jnp.concatenate([c, c_ctx[None, :], jnp.zeros((MOD_ROWS - batch - 1, d), F32)], axis=0)
    mods = _ada_table(cv, ada_w, ada_b)
    ng = norm_g.reshape(depth, 1, d)
    x2d = x.reshape(batch * n_lat, d)
    ctx2d = ctx.reshape(batch * n_ctx, d)

    ctx_pad = jnp.pad(ctx2d, ((0, PROJ_TILE - batch * n_ctx), (0, 0)))
    half_cols = jnp.where(jnp.arange(HG_STREAMS * e) // e == 3, 1.0, 0.5).astype(F32)
    a = _proj0(ctx_pad, x2d, mods, ng, (hg_w_in[0] * half_cols).astype(BF16), hg_lb_logits,
                  n_lat=n_lat, lb_index=0)
    o_f, o_b, span = _scan(a, batch=batch, n_lat=n_lat, n_ctx=n_ctx, e=e)
    o_f, o_b = lax.cond(jnp.logical_not(jnp.max(span) <= SPAN_LIMIT),
                        lambda: tuple(_scan_exact(a, batch=batch, n_lat=n_lat, n_ctx=n_ctx, e=e)),
                        lambda: (o_f, o_b))
    x1, _ctx1 = _out0(o_f, o_b, a, ctx2d, x2d, mods, hg_norm_g[0:1], hg_w_out[0].astype(BF16),
                      n_lat=n_lat)

    z_half = jnp.where(jnp.arange(2 * e) < e, 1.0, 0.5).astype(F32)
    u, zs = _proj1(x1, mods, ng, (ft_w_in[0] * z_half).astype(BF16), batch=batch, n_lat=n_lat,
                   layer=1)
    stage1, stage2, chan = _dft_tables(n_lat, e // FT_GROUPS)
    n1 = stage1.shape[2]
    y = _dft(u, stage1, stage2, chan)
    out = _out1(y, zs, x1, mods, ft_w_out[0].astype(BF16), final_g.reshape(1, d),
                n_lat=n_lat, n1=n1, layer=1)
    return out.reshape(batch, n_lat, d)
```

```python
import functools

import numpy as np
import jax
import jax.numpy as jnp
from jax import lax
from jax.experimental import pallas as pl
from jax.experimental.pallas import tpu as pltpu

F32 = jnp.float32
BF16 = jnp.bfloat16

EPS = 1e-6
LOG2E = 1.4426950408889634
SPAN_LIMIT = 100.0
HEAD_DIM = 128
HG_STREAMS = 5
FT_GROUPS = 8
SCAN_CHUNK = 64
SCAN_HEADS = 16
ROW_TILE = 512
PROJ_TILE = 1024
MOD_ROWS = 8
SUBLANES = 8
LANES = 128
VMEM_LIMIT = 56 * 1024 * 1024


def _cparams(sem):
    return pltpu.CompilerParams(dimension_semantics=sem, vmem_limit_bytes=VMEM_LIMIT)


def _sigmoid(x):
    return 0.5 * jnp.tanh(0.5 * x) + 0.5


def _silu(x):
    return _silu_half(0.5 * x)


def _silu_half(h):
    return h + h * jnp.tanh(h)


def _modulated_norm(x, g, shift, scale):
    var = jnp.mean(x * x, axis=-1, keepdims=True)
    return (x * lax.rsqrt(var + EPS)) * (g * (1.0 + scale)) + shift


def _ada_kernel(cv_ref, w_ref, b_ref, o_ref):
    a = _silu(cv_ref[...])
    o_ref[...] = jnp.dot(a, w_ref[...], preferred_element_type=F32,
                         precision=lax.Precision.HIGHEST) + b_ref[...]


def _ada_table(cv, ada_w, ada_b):
    depth, d, d3 = ada_w.shape
    tn = 1024
    return pl.pallas_call(
        _ada_kernel,
        grid=(depth, d3 // tn),
        in_specs=[
            pl.BlockSpec((MOD_ROWS, d), lambda l, j: (0, 0)),
            pl.BlockSpec((None, d, tn), lambda l, j: (l, 0, j)),
            pl.BlockSpec((None, 1, tn), lambda l, j: (l, 0, j)),
        ],
        out_specs=pl.BlockSpec((None, MOD_ROWS, tn), lambda l, j: (l, 0, j)),
        out_shape=jax.ShapeDtypeStruct((depth, MOD_ROWS, d3), F32),
        compiler_params=_cparams(("arbitrary", "arbitrary")),
        name="ada_table",
    )(cv, ada_w, ada_b.reshape(depth, 1, d3))


def _proj0_kernel(ctx_ref, x_ref, mod_ref, ng_ref, w_ref, lbl_ref, a_ref, h_ref,
                  *, d, e, tiles_per_batch, lb_index, sub):
    i = pl.program_id(0)
    j = pl.program_id(1)
    tm = h_ref.shape[0]

    @pl.when(j == 0)
    def _():
        is_ctx = i == 0
        m = mod_ref[pl.ds(jnp.where(is_ctx, 2, (i - 1) // tiles_per_batch), 1), :]
        src = jnp.where(is_ctx, ctx_ref[...], x_ref[...])
        h_ref[...] = _modulated_norm(src, ng_ref[...], m[:, 0:d], m[:, d:2 * d]).astype(BF16)

    def acc(c):
        return jnp.dot(h_ref[...], w_ref[:, c * sub:(c + 1) * sub], preferred_element_type=F32)

    nsub = w_ref.shape[1] // sub
    per_sub = sub // LANES
    per_stream = e // sub

    def put(c, val):
        for cl in range(per_sub):
            a_ref[c * per_sub + cl] = val[:, cl * LANES:(cl + 1) * LANES]

    def half_gate(dirn, cols):
        logits = lbl_ref[:, dirn:dirn + 1, cols]
        mx = jnp.max(logits, axis=0, keepdims=True)
        ex = jnp.exp(logits - mx)
        p = ex / jnp.sum(ex, axis=0, keepdims=True)
        return 0.5 * (1.0 - jnp.sum(p[0:lb_index + 1], axis=0))

    for step in range(pl.cdiv(HG_STREAMS * e, w_ref.shape[1])):
        @pl.when(j == step)
        def _(step=step):
            for c in range(nsub):
                stream, cs = divmod(step * nsub + c, per_stream)
                if stream in (0, 4):
                    put(c, _silu_half(acc(c)).astype(BF16))
                elif stream == 3:
                    put(c, acc(c).astype(BF16))
                else:
                    half = half_gate(stream - 1, slice(cs * sub, (cs + 1) * sub))
                    put(c, (half * (1.0 - jnp.tanh(acc(c)))).astype(BF16))


def _proj0(ctx2d, x2d, mods, norm_g, w_bf, lb_logits, *, n_lat, lb_index):
    rows_ctx, d = ctx2d.shape
    rows_lat = x2d.shape[0]
    e = w_bf.shape[1] // HG_STREAMS
    tm = PROJ_TILE
    assert rows_ctx == tm and n_lat % tm == 0
    n_tiles = 1 + rows_lat // tm
    rows = tm + rows_lat
    nl = lb_logits.shape[0]
    sub = 512
    col_steps = HG_STREAMS - 1
    tn = HG_STREAMS * e // col_steps
    assert tn % sub == 0 and e % sub == 0
    kern = functools.partial(_proj0_kernel, d=d, e=e, tiles_per_batch=n_lat // tm,
                             lb_index=lb_index, sub=sub)
    return pl.pallas_call(
        kern,
        grid=(n_tiles, col_steps),
        in_specs=[
            pl.BlockSpec((tm, d), lambda i, j: (0, 0)),
            pl.BlockSpec((tm, d), lambda i, j: (jnp.maximum(i - 1, 0), 0)),
            pl.BlockSpec((None, MOD_ROWS, 3 * d), lambda i, j: (0, 0, 0)),
            pl.BlockSpec((None, 1, d), lambda i, j: (0, 0, 0)),
            pl.BlockSpec((d, tn), lambda i, j: (0, j)),
            pl.BlockSpec((nl, 2, e), lambda i, j: (0, 0, 0)),
        ],
        out_specs=pl.BlockSpec((tn // LANES, tm, LANES), lambda i, j: (j, i, 0)),
        out_shape=jax.ShapeDtypeStruct((HG_STREAMS * (e // LANES), rows, LANES), BF16),
        scratch_shapes=[pltpu.VMEM((tm, d), BF16)],
        compiler_params=_cparams(("arbitrary", "arbitrary")),
        name="proj0",
    )(ctx2d, x2d, mods, norm_g, w_bf, lb_logits)


def _scan_kernel(qf_ref, kf_ref, vf_ref, qb_ref, kb_ref, vb_ref, of_ref, ob_ref, span_ref,
                 st_ref, pa_ref, pb_ref, da_ref, db_ref, sp_ref, *, n_chunks):
    s = pl.program_id(2)
    c_len = SCAN_CHUNK

    @pl.when(s == 0)
    def _():
        st_ref[...] = jnp.zeros_like(st_ref)
        pb_ref[...] = jnp.zeros_like(pb_ref)
        db_ref[...] = jnp.zeros_like(db_ref)
        sp_ref[...] = jnp.zeros_like(sp_ref)

    row = lax.broadcasted_iota(jnp.int32, (c_len, c_len), 0)
    col = lax.broadcasted_iota(jnp.int32, (c_len, c_len), 1)
    causal = (col <= row, col >= row)
    tri = tuple(m.astype(F32).astype(BF16) for m in causal)
    end_row = (c_len - 1, 0)
    mid = c_len // 2
    qk_refs = ((qf_ref, kf_ref), (qb_ref, kb_ref))
    vo_refs = ((vf_ref, of_ref), (vb_ref, ob_ref))
    nt = (((1,), (1,)), ((), ()))
    tn = (((0,), (0,)), ((), ()))
    QT, KT, QH, KH = range(4)

    def chunk_of(dirn, r):
        return r if dirn == 0 else n_chunks - 1 - r

    def prepare(p_ref, d_ref, dirn, r, hh):
        q_ref, k_ref = qk_refs[dirn]
        c = chunk_of(dirn, r)
        rows = slice(c * c_len, (c + 1) * c_len)
        q = q_ref[hh, rows, :].astype(F32)
        k = k_ref[hh, rows, :].astype(F32)
        b = jnp.dot(tri[dirn], jnp.log(1.0 - k).astype(BF16), preferred_element_type=F32)
        tot = b[end_row[dirn]:end_row[dirn] + 1, :]
        ref = b[mid:mid + 1, :]
        dl = (b - ref) * LOG2E
        ends = jnp.maximum(jnp.abs(dl[0:1, :]), jnp.abs(dl[c_len - 1:c_len, :]))
        sp_ref[dirn, hh] = jnp.maximum(sp_ref[dirn, hh], ends)
        qt = q * jnp.exp2(dl)
        kt = k * jnp.exp2(-dl)
        p_ref[dirn, QT, hh, rows, :] = qt.astype(BF16)
        p_ref[dirn, KT, hh, rows, :] = kt.astype(BF16)
        p_ref[dirn, QH, hh, rows, :] = (qt * jnp.exp(ref)).astype(BF16)
        p_ref[dirn, KH, hh, rows, :] = (kt * jnp.exp(tot - ref)).astype(BF16)
        d_ref[dirn, c, hh] = jnp.exp(tot)

    def scores(p_ref, dirn, r, hh):
        v_ref = vo_refs[dirn][0]
        c = chunk_of(dirn, r)
        rows = slice(c * c_len, (c + 1) * c_len)
        sc = lax.dot_general(p_ref[dirn, QT, hh, rows, :], p_ref[dirn, KT, hh, rows, :],
                             nt, preferred_element_type=F32)
        prob = jnp.where(causal[dirn], sc, 0.0).astype(BF16)
        upd = lax.dot_general(v_ref[hh, rows, :], p_ref[dirn, KH, hh, rows, :],
                              tn, preferred_element_type=F32)
        return prob, upd

    def outputs(p_ref, d_ref, dirn, r, hh, prob, upd):
        v_ref, o_ref = vo_refs[dirn]
        c = chunk_of(dirn, r)
        rows = slice(c * c_len, (c + 1) * c_len)
        st = st_ref[dirn, hh]
        o = (jnp.dot(prob, v_ref[hh, rows, :], preferred_element_type=F32)
             + lax.dot_general(p_ref[dirn, QH, hh, rows, :], st.astype(BF16), nt,
                               preferred_element_type=F32))
        o_ref[hh, rows, :] = o.astype(BF16)
        st_ref[dirn, hh] = st * d_ref[dirn, c, hh] + upd

    def step(p_new, d_new, p_old, d_old):
        heads = range(SCAN_HEADS)
        pu = {(dirn, hh): scores(p_old, dirn, 0, hh) for hh in heads for dirn in range(2)}
        for r in range(n_chunks):
            pu_next = {}
            for hh in heads:
                for dirn in range(2):
                    if r + 1 < n_chunks:
                        pu_next[dirn, hh] = scores(p_old, dirn, r + 1, hh)
                    outputs(p_old, d_old, dirn, r, hh, *pu[dirn, hh])
                    prepare(p_new, d_new, dirn, r, hh)
            pu = pu_next

    @pl.when(s % 2 == 0)
    def _():
        step(pa_ref, da_ref, pb_ref, db_ref)

    @pl.when(s % 2 == 1)
    def _():
        step(pb_ref, db_ref, pa_ref, da_ref)

    @pl.when(s == pl.num_programs(2) - 1)
    def _():
        span_ref[...] = sp_ref[...]


def _scan(a, *, batch, n_lat, n_ctx, e):
    ts = n_ctx
    assert ts % SCAN_CHUNK == 0 and n_lat % ts == 0
    lat_steps = n_lat // ts
    steps = 1 + lat_steps
    n_chunks = ts // SCAN_CHUNK
    assert HEAD_DIM == LANES
    groups = e // (SCAN_HEADS * HEAD_DIM)
    rows = batch * (n_ctx + n_lat)
    assert batch * n_ctx <= PROJ_TILE and PROJ_TILE % ts == 0

    def row_f(lat0):
        return lambda b, s: jnp.where(s == 0, b, lat0 + lat_steps * b + s - 1)

    def row_b(lat0):
        return lambda b, s: jnp.where(s == 0, b, lat0 + lat_steps * b + lat_steps - s)

    in_f, in_b = row_f(PROJ_TILE // ts), row_b(PROJ_TILE // ts)
    out_f, out_b = row_f(batch), row_b(batch)

    ahead = lambda g: jnp.minimum(g, steps - 1)
    behind = lambda g: jnp.maximum(g - 1, 0)

    def spec(stream, rfn, when):
        return pl.BlockSpec((SCAN_HEADS, ts, LANES),
                            lambda b, hg, g: (stream * groups + hg, rfn(b, when(g)), 0))

    kern = functools.partial(_scan_kernel, n_chunks=n_chunks)
    operands = pltpu.VMEM((2, 4, SCAN_HEADS, ts, LANES), BF16)
    decays = pltpu.VMEM((2, n_chunks, SCAN_HEADS, 1, LANES), F32)
    return pl.pallas_call(
        kern,
        grid=(batch, groups, steps + 1),
        in_specs=[spec(0, in_f, ahead), spec(1, in_f, ahead), spec(3, in_f, behind),
                  spec(0, in_b, ahead), spec(2, in_b, ahead), spec(3, in_b, behind)],
        out_specs=[pl.BlockSpec((SCAN_HEADS, ts, LANES), lambda b, hg, g: (hg, out_f(b, behind(g)), 0)),
                   pl.BlockSpec((SCAN_HEADS, ts, LANES), lambda b, hg, g: (hg, out_b(b, behind(g)), 0)),
                   pl.BlockSpec((None, None, 2, SCAN_HEADS, 1, LANES), lambda b, hg, g: (b, hg, 0, 0, 0, 0))],
        out_shape=[jax.ShapeDtypeStruct((e // LANES, rows, LANES), BF16)] * 2
        + [jax.ShapeDtypeStruct((batch, groups, 2, SCAN_HEADS, 1, LANES), F32)],
        scratch_shapes=[pltpu.VMEM((2, SCAN_HEADS, HEAD_DIM, HEAD_DIM), F32),
                        operands, operands, decays, decays,
                        pltpu.VMEM((2, SCAN_HEADS, 1, LANES), F32)],
        compiler_params=_cparams(("arbitrary", "arbitrary", "arbitrary")),
        name="hgrn_scan",
    )(a, a, a, a, a, a)


def _scan_exact_kernel(q_ref, k_ref, v_ref, o_ref, st_ref, *, n_chunks, reverse):
    c_len = SCAN_CHUNK

    @pl.when(pl.program_id(2) == 0)
    def _():
        st_ref[...] = jnp.zeros_like(st_ref)

    lane = lax.broadcasted_iota(jnp.int32, (1, c_len), 1)

    def chunk(ci, carry):
        c = (n_chunks - 1 - ci) if reverse else ci
        rows = pl.ds(pl.multiple_of(c * c_len, c_len), c_len)
        q = q_ref[rows, :].astype(F32)
        k = k_ref[rows, :].astype(F32)
        f = 1.0 - k
        vt = v_ref[rows, :].astype(F32).T
        st = st_ref[...]
        ot = jnp.zeros((HEAD_DIM, c_len), F32)
        for step in range(c_len):
            t = c_len - 1 - step if reverse else step
            st = st * f[t:t + 1, :] + vt[:, t:t + 1] * k[t:t + 1, :]
            o_col = jnp.sum(st * q[t:t + 1, :], axis=-1, keepdims=True)
            ot = ot + o_col * (lane == t).astype(F32)
        st_ref[...] = st
        o_ref[rows, :] = ot.T.astype(BF16)
        return carry

    lax.fori_loop(0, n_chunks, chunk, 0)


def _scan_exact(a, *, batch, n_lat, n_ctx, e):
    ts = n_ctx
    lat_steps = n_lat // ts
    steps = 1 + lat_steps
    heads = e // HEAD_DIM
    rows = batch * (n_ctx + n_lat)
    outs = []
    for reverse, k_stream in ((False, 1), (True, 2)):
        def row(lat0, reverse=reverse):
            if reverse:
                return lambda b, s: jnp.where(s == 0, b, lat0 + lat_steps * b + lat_steps - s)
            return lambda b, s: jnp.where(s == 0, b, lat0 + lat_steps * b + s - 1)

        rin, rout = row(PROJ_TILE // ts), row(batch)

        def spec(stream, rin=rin):
            return pl.BlockSpec((None, ts, LANES), lambda b, h, s: (stream * heads + h, rin(b, s), 0))

        outs.append(pl.pallas_call(
            functools.partial(_scan_exact_kernel, n_chunks=ts // SCAN_CHUNK, reverse=reverse),
            grid=(batch, heads, steps),
            in_specs=[spec(0), spec(k_stream), spec(3)],
            out_specs=pl.BlockSpec((None, ts, LANES), lambda b, h, s, rout=rout: (h, rout(b, s), 0)),
            out_shape=jax.ShapeDtypeStruct((heads, rows, LANES), BF16),
            scratch_shapes=[pltpu.VMEM((HEAD_DIM, HEAD_DIM), F32)],
            compiler_params=_cparams(("arbitrary", "arbitrary", "arbitrary")),
            name="hgrn_scan_exact_bwd" if reverse else "hgrn_scan_exact_fwd",
        )(a, a, a))
    return outs


OUT0_SLOTS = 3


def _out0_kernel(of_hbm, ob_hbm, a_hbm, ctx_ref, x_ref, mod_ref, hg_ref, w_ref,
                 xo_ref, co_ref, of_buf, ob_buf, z_buf, sems, y_ref,
                 *, d, e, tiles_per_batch, tm, skip, z_slab0):
    i = pl.program_id(0)
    n = pl.num_programs(0)
    heads = e // HEAD_DIM

    def copies(step, slot):
        rows = pl.ds(pl.multiple_of(step * tm, tm), tm)
        z_rows = pl.ds(pl.multiple_of(jnp.where(step == 0, 0, step + skip) * tm, tm), tm)
        return (
            pltpu.make_async_copy(of_hbm.at[:, rows, :], of_buf.at[slot], sems.at[0, slot]),
            pltpu.make_async_copy(ob_hbm.at[:, rows, :], ob_buf.at[slot], sems.at[1, slot]),
            pltpu.make_async_copy(a_hbm.at[pl.ds(z_slab0, heads), z_rows, :], z_buf.at[slot],
                                  sems.at[2, slot]),
        )

    @pl.when(i == 0)
    def _():
        for first in range(OUT0_SLOTS - 1):
            for cp in copies(first, first):
                cp.start()

    ahead = i + OUT0_SLOTS - 1

    @pl.when(ahead < n)
    def _():
        for cp in copies(ahead, ahead % OUT0_SLOTS):
            cp.start()

    for cp in copies(i, i % OUT0_SLOTS):
        cp.wait()

    def body(of_ref, ob_ref, z_ref):
        group = 4
        mix = None
        for h0 in range(0, heads, group):
            for h in range(h0, h0 + group):
                lanes = slice(h * HEAD_DIM, (h + 1) * HEAD_DIM)
                o = of_ref[h].astype(F32) + ob_ref[h].astype(F32)
                var = jnp.mean(o * o, axis=-1, keepdims=True)
                yn = (o * lax.rsqrt(var + EPS)) * hg_ref[...]
                y_ref[:, lanes] = (yn * z_ref[h].astype(F32)).astype(BF16)
            cols = slice(h0 * HEAD_DIM, (h0 + group) * HEAD_DIM)
            part = jnp.dot(y_ref[:, cols], w_ref[cols, :], preferred_element_type=F32)
            mix = part if mix is None else mix + part

        @pl.when(i == 0)
        def _():
            gate = mod_ref[2:3, 2 * d:3 * d]
            co_ref[...] = ctx_ref[...] + gate * mix

        @pl.when(i > 0)
        def _():
            gate = mod_ref[pl.ds((i - 1) // tiles_per_batch, 1), 2 * d:3 * d]
            xo_ref[...] = x_ref[...] + gate * mix

    for slot in range(OUT0_SLOTS):
        @pl.when(i % OUT0_SLOTS == slot)
        def _(slot=slot):
            body(of_buf.at[slot], ob_buf.at[slot], z_buf.at[slot])


def _out0(o_f, o_b, a, ctx2d, x2d, mods, hg_norm_g, w_bf, *, n_lat):
    rows_ctx, d = ctx2d.shape
    rows_lat = x2d.shape[0]
    heads = o_f.shape[0]
    e = heads * HEAD_DIM
    tm = ROW_TILE
    assert rows_ctx == tm and PROJ_TILE % tm == 0
    n_tiles = 1 + rows_lat // tm
    skip = PROJ_TILE // tm - 1
    assert n_tiles >= OUT0_SLOTS
    kern = functools.partial(_out0_kernel, d=d, e=e, tiles_per_batch=n_lat // tm, tm=tm, skip=skip,
                             z_slab0=(HG_STREAMS - 1) * heads)
    lat_idx = lambda i: (jnp.maximum(i - 1, 0), 0)
    ring = pltpu.VMEM((OUT0_SLOTS, heads, tm, HEAD_DIM), BF16)
    return pl.pallas_call(
        kern,
        grid=(n_tiles,),
        in_specs=[
            pl.BlockSpec(memory_space=pl.ANY),
            pl.BlockSpec(memory_space=pl.ANY),
            pl.BlockSpec(memory_space=pl.ANY),
            pl.BlockSpec((tm, d), lambda i: (0, 0)),
            pl.BlockSpec((tm, d), lat_idx),
            pl.BlockSpec((None, MOD_ROWS, 3 * d), lambda i: (0, 0, 0)),
            pl.BlockSpec((1, HEAD_DIM), lambda i: (0, 0)),
            pl.BlockSpec((e, d), lambda i: (0, 0)),
        ],
        out_specs=[pl.BlockSpec((tm, d), lat_idx), pl.BlockSpec((tm, d), lambda i: (0, 0))],
        out_shape=[jax.ShapeDtypeStruct((rows_lat, d), F32), jax.ShapeDtypeStruct((rows_ctx, d), F32)],
        scratch_shapes=[ring, ring, ring, pltpu.SemaphoreType.DMA((3, OUT0_SLOTS)),
                        pltpu.VMEM((tm, e), BF16)],
        compiler_params=_cparams(("arbitrary",)),
        name="readout0",
    )(o_f, o_b, a, ctx2d, x2d, mods, hg_norm_g, w_bf)


def _proj1_kernel(x_ref, mod_ref, ng_ref, w_ref, u_ref, z_ref, h_ref,
                  *, d, e, tiles_per_batch, sub, n2):
    i = pl.program_id(0)
    tm = h_ref.shape[0]
    m = mod_ref[pl.ds(i // tiles_per_batch, 1), :]
    h_ref[...] = _modulated_norm(x_ref[...], ng_ref[...], m[:, 0:d], m[:, d:2 * d]).astype(BF16)

    def acc(c):
        return jnp.dot(h_ref[...], w_ref[:, c * sub:(c + 1) * sub], preferred_element_type=F32)

    nsub = e // sub

    for c in range(nsub):
        a = acc(c)
        for n1l in range(tm // n2):
            for m8 in range(n2 // SUBLANES):
                src = n1l * n2 + m8 * SUBLANES
                for cl in range(sub // LANES):
                    u_ref[m8, c * (sub // LANES) + cl, n1l * SUBLANES:(n1l + 1) * SUBLANES, :] = (
                        a[src:src + SUBLANES, cl * LANES:(cl + 1) * LANES])

    for c in range(nsub):
        z_ref[:, c * sub:(c + 1) * sub] = _silu_half(acc(nsub + c)).astype(BF16)


def _proj1(x2d, mods, norm_g, w_bf, *, batch, n_lat, layer):
    rows, d = x2d.shape
    e = w_bf.shape[1] // 2
    tm = PROJ_TILE
    n1, n2 = _dft_factors(n_lat)
    q = n2 // SUBLANES
    tpb = n_lat // tm
    rows_blk = (tm // n2) * SUBLANES
    kern = functools.partial(_proj1_kernel, d=d, e=e, tiles_per_batch=tpb, sub=512, n2=n2)
    return pl.pallas_call(
        kern,
        grid=(rows // tm,),
        in_specs=[
            pl.BlockSpec((tm, d), lambda i: (i, 0)),
            pl.BlockSpec((None, MOD_ROWS, 3 * d), lambda i: (layer, 0, 0)),
            pl.BlockSpec((None, 1, d), lambda i: (layer, 0, 0)),
            pl.BlockSpec((d, 2 * e), lambda i: (0, 0), pipeline_mode=pl.Buffered(1)),
        ],
        out_specs=[pl.BlockSpec((None, q, e // LANES, rows_blk, LANES),
                                lambda i: (i // tpb, 0, 0, i % tpb, 0)),
                   pl.BlockSpec((tm, e), lambda i: (i, 0))],
        out_shape=[jax.ShapeDtypeStruct((batch, q, e // LANES, n1 * SUBLANES, LANES), F32),
                   jax.ShapeDtypeStruct((rows, e), BF16)],
        scratch_shapes=[pltpu.VMEM((tm, d), BF16)],
        compiler_params=_cparams(("arbitrary",)),
        name="proj1",
    )(x2d, mods, norm_g, w_bf)


def _dft_factors(n):
    n1 = 128 if n % 128 == 0 else n
    return n1, n // n1


def _dft_tables(n, group_dim):
    n1, n2 = _dft_factors(n)
    a2 = np.arange(n2, dtype=np.float64)[:, None, None]
    k1 = np.arange(n1, dtype=np.float64)[None, :, None]
    m1 = np.arange(n1, dtype=np.float64)[None, None, :]
    ang = -2.0 * np.pi * (a2 * k1 / n + m1 * k1 / n1)
    stage1 = np.concatenate([np.cos(ang), np.sin(ang)], axis=1) / np.sqrt(n1)
    kk = np.arange(n2, dtype=np.float64)
    ang2 = -2.0 * np.pi * np.outer(kk, kk) / n2
    fr, fi = np.cos(ang2) / np.sqrt(n2), np.sin(ang2) / np.sqrt(n2)
    stage2 = np.stack([np.concatenate([fr, fi], axis=0),
                       np.concatenate([-fi, fr], axis=0)])
    cc = np.arange(group_dim, dtype=np.float64)
    ang3 = 2.0 * np.pi * np.outer(cc, cc) / group_dim
    chan = np.concatenate([np.cos(ang3), np.sin(ang3)], axis=0) / np.sqrt(group_dim)
    to = lambda t: jnp.asarray(t, dtype=F32).astype(BF16)
    return to(stage1), to(stage2), to(chan)


def _dft_kernel(m_ref, f_ref, cs_ref, x_ref, y_ref, tr_ref, ti_ref, ys_ref, *, n1, n2, gd, kt_step):
    half = pl.program_id(2)
    q, nlb = x_ref.shape[0], x_ref.shape[1]

    @pl.when(half == 0)
    def _():
        def stage1(m, carry):
            for r in range(SUBLANES):
                rows = pl.ds(r, n1, stride=SUBLANES)
                xr = jnp.concatenate([x_ref[m, lb, rows, :] for lb in range(nlb)], axis=1)
                t = jnp.dot(m_ref[m, r], xr.astype(BF16), preferred_element_type=F32)
                for lb in range(nlb):
                    lanes = slice(lb * LANES, (lb + 1) * LANES)
                    tr_ref[m, lb, rows, :] = t[0:n1, lanes]
                    ti_ref[m, lb, rows, :] = t[n1:2 * n1, lanes]
            return carry

        lax.fori_loop(0, q, stage1, 0)

    width = nlb * LANES

    def gather(ref, kt):
        cols = []
        for j in range(SUBLANES):
            start = pl.multiple_of((kt * SUBLANES + j) * SUBLANES, SUBLANES)
            cols += [ref[:, lb, pl.ds(start, SUBLANES), :].reshape(n2, LANES) for lb in range(nlb)]
        return jnp.concatenate(cols, axis=1).astype(BF16)

    def by_k1(g):
        return jnp.concatenate([g[:, j * width:(j + 1) * width] for j in range(SUBLANES)], axis=0)

    def stage2(kl, carry):
        kt = half * kt_step + kl
        g = (jnp.dot(f_ref[0], gather(tr_ref, kt), preferred_element_type=F32)
             + jnp.dot(f_ref[1], gather(ti_ref, kt), preferred_element_type=F32))
        gr = by_k1(g[0:n2]).astype(BF16)
        gi = by_k1(g[n2:2 * n2]).astype(BF16)
        yg = (jnp.dot(gr, cs_ref[0:gd, :], preferred_element_type=F32)
              + jnp.dot(gi, cs_ref[gd:2 * gd, :], preferred_element_type=F32))
        for j in range(SUBLANES):
            rows = pl.ds(j, n2, stride=SUBLANES)
            for lb in range(nlb):
                ys_ref[lb, rows, :] = yg[j * n2:(j + 1) * n2, lb * LANES:(lb + 1) * LANES]
        y_ref[kl] = ys_ref[...].astype(BF16)
        return carry

    lax.fori_loop(0, kt_step, stage2, 0)


def _dft(u, stage1, stage2, chan):
    batch, q, nlb, rows, _ = u.shape
    n2, two_n1, n1 = stage1.shape
    gd = chan.shape[1]
    glb = gd // LANES
    kt = n1 // SUBLANES
    kt_step = max(kt // 2, 1)
    m4 = stage1.reshape(q, SUBLANES, two_n1, n1)
    scratch = pltpu.VMEM((q, glb, rows, LANES), F32)
    return pl.pallas_call(
        functools.partial(_dft_kernel, n1=n1, n2=n2, gd=gd, kt_step=kt_step),
        grid=(batch, nlb // glb, kt // kt_step),
        in_specs=[pl.BlockSpec((q, SUBLANES, two_n1, n1), lambda b, g, h: (0, 0, 0, 0)),
                  pl.BlockSpec((2, 2 * n2, n2), lambda b, g, h: (0, 0, 0)),
                  pl.BlockSpec((2 * gd, gd), lambda b, g, h: (0, 0)),
                  pl.BlockSpec((None, q, glb, rows, LANES), lambda b, g, h: (b, 0, g, 0, 0))],
        out_specs=pl.BlockSpec((None, kt_step, glb, n2 * SUBLANES, LANES),
                               lambda b, g, h: (b, h, g, 0, 0)),
        out_shape=jax.ShapeDtypeStruct((batch, kt, nlb, n2 * SUBLANES, LANES), BF16),
        scratch_shapes=[scratch, scratch, pltpu.VMEM((glb, n2 * SUBLANES, LANES), F32)],
        compiler_params=_cparams(("arbitrary", "arbitrary", "arbitrary")),
        name="dft",
    )(m4, stage2, chan, u)


def _out1_kernel(y_hbm, z_hbm, x_ref, mod_ref, w_ref, fg_ref, o_ref, yin_buf, z_buf, sems, y_ref,
                 *, d, n1, tiles_per_batch):
    i = pl.program_id(0)
    n = pl.num_programs(0)
    _, kt, nlb, blk_rows, _ = yin_buf.shape
    tm = z_buf.shape[1]

    def copies(step, slot):
        y_rows = pl.ds(pl.multiple_of((step % tiles_per_batch) * blk_rows, blk_rows), blk_rows)
        z_rows = pl.ds(pl.multiple_of(step * tm, tm), tm)
        return (
            pltpu.make_async_copy(y_hbm.at[step // tiles_per_batch, :, :, y_rows, :],
                                  yin_buf.at[slot], sems.at[0, slot]),
            pltpu.make_async_copy(z_hbm.at[z_rows, :], z_buf.at[slot], sems.at[1, slot]),
        )

    @pl.when(i == 0)
    def _():
        for first in range(OUT0_SLOTS - 1):
            for cp in copies(first, first):
                cp.start()

    ahead = i + OUT0_SLOTS - 1

    @pl.when(ahead < n)
    def _():
        for cp in copies(ahead, ahead % OUT0_SLOTS):
            cp.start()

    for cp in copies(i, i % OUT0_SLOTS):
        cp.wait()

    def body(y_in_ref, z_ref):
        k2_per_tile = blk_rows // SUBLANES
        pair = 2 * SUBLANES
        for t in range(k2_per_tile // 2):
            tile = slice(t * pair, (t + 1) * pair)
            for kp in range(kt // 2):
                blocks = [jnp.concatenate([y_in_ref[2 * kp + a, lb, tile, :] for lb in range(nlb)],
                                          axis=1).astype(F32) for a in range(2)]
                for s2 in range(2):
                    k2l = 2 * t + s2
                    dst = slice(k2l * n1 + kp * pair, k2l * n1 + (kp + 1) * pair)
                    yv = jnp.concatenate([blk[s2 * SUBLANES:(s2 + 1) * SUBLANES] for blk in blocks],
                                         axis=0)
                    y_ref[dst, :] = (yv * z_ref[dst, :].astype(F32)).astype(BF16)
        mix = jnp.dot(y_ref[...], w_ref[...], preferred_element_type=F32)
        gate = mod_ref[pl.ds(i // tiles_per_batch, 1), 2 * d:3 * d]
        x = x_ref[...] + gate * mix
        var = jnp.mean(x * x, axis=-1, keepdims=True)
        o_ref[...] = (x * lax.rsqrt(var + EPS)) * fg_ref[...]

    for slot in range(OUT0_SLOTS):
        @pl.when(i % OUT0_SLOTS == slot)
        def _(slot=slot):
            body(yin_buf.at[slot], z_buf.at[slot])


def _out1(y_in, zs, x2d, mods, w_bf, final_g, *, n_lat, n1, layer):
    rows, d = x2d.shape
    batch, kt, nlb, _, _ = y_in.shape
    e = nlb * LANES
    tm = PROJ_TILE
    assert tm % n1 == 0
    tpb = n_lat // tm
    blk_rows = (tm // n1) * SUBLANES
    assert rows // tm >= OUT0_SLOTS
    kern = functools.partial(_out1_kernel, d=d, n1=n1, tiles_per_batch=tpb)
    return pl.pallas_call(
        kern,
        grid=(rows // tm,),
        in_specs=[
            pl.BlockSpec(memory_space=pl.ANY),
            pl.BlockSpec(memory_space=pl.ANY),
            pl.BlockSpec((tm, d), lambda i: (i, 0)),
            pl.BlockSpec((None, MOD_ROWS, 3 * d), lambda i: (layer, 0, 0)),
            pl.BlockSpec((e, d), lambda i: (0, 0), pipeline_mode=pl.Buffered(1)),
            pl.BlockSpec((1, d), lambda i: (0, 0)),
        ],
        out_specs=pl.BlockSpec((tm, d), lambda i: (i, 0)),
        out_shape=jax.ShapeDtypeStruct((rows, d), F32),
        scratch_shapes=[pltpu.VMEM((OUT0_SLOTS, kt, nlb, blk_rows, LANES), BF16),
                        pltpu.VMEM((OUT0_SLOTS, tm, e), BF16),
                        pltpu.SemaphoreType.DMA((2, OUT0_SLOTS)),
                        pltpu.VMEM((tm, e), BF16)],
        compiler_params=_cparams(("arbitrary",)),
        name="readout1",
    )(y_in, zs, x2d, mods, w_bf, final_g)


def kernel(x, c, ctx, c_ctx, ada_w, ada_b, norm_g, hg_w_in, hg_lb_logits, hg_norm_g, hg_w_out,
           ft_w_in, ft_w_out, final_g):
    batch, n_lat, d = x.shape
    n_ctx = ctx.shape[1]
    depth = ada_w.shape[0]
    e = hg_w_out.shape[1]
    assert depth == 2 and batch == 2 and batch + 1 <= MOD_ROWS

    cv = jnp.concatenate([c, c_ctx[None, :], jnp.zeros((MOD_ROWS - batch - 1, d), F32)], axis=0)
    mods = _ada_table(cv, ada_w, ada_b)
    ng = norm_g.reshape(depth, 1, d)
    x2d = x.reshape(batch * n_lat, d)
    ctx2d = ctx.reshape(batch * n_ctx, d)

    ctx_pad = jnp.pad(ctx2d, ((0, PROJ_TILE - batch * n_ctx), (0, 0)))
    half_cols = jnp.where(jnp.arange(HG_STREAMS * e) // e == 3, 1.0, 0.5).astype(F32)
    a = _proj0(ctx_pad, x2d, mods, ng, (hg_w_in[0] * half_cols).astype(BF16), hg_lb_logits,
                  n_lat=n_lat, lb_index=0)
    o_f, o_b, span = _scan(a, batch=batch, n_lat=n_lat, n_ctx=n_ctx, e=e)
    o_f, o_b = lax.cond(jnp.logical_not(jnp.max(span) <= SPAN_LIMIT),
                        lambda: tuple(_scan_exact(a, batch=batch, n_lat=n_lat, n_ctx=n_ctx, e=e)),
                        lambda: (o_f, o_b))
    x1, _ctx1 = _out0(o_f, o_b, a, ctx2d, x2d, mods, hg_norm_g[0:1], hg_w_out[0].astype(BF16),
                      n_lat=n_lat)

    z_half = jnp.where(jnp.arange(2 * e) < e, 1.0, 0.5).astype(F32)
    u, zs = _proj1(x1, mods, ng, (ft_w_in[0] * z_half).astype(BF16), batch=batch, n_lat=n_lat,
                   layer=1)
    stage1, stage2, chan = _dft_tables(n_lat, e // FT_GROUPS)
    n1 = stage1.shape[2]
    y = _dft(u, stage1, stage2, chan)
    out = _out1(y, zs, x1, mods, ft_w_out[0].astype(BF16), final_g.reshape(1, d),
                n_lat=n_lat, n1=n1, layer=1)
    return out.reshape(batch, n_lat, d)
```

```python
import functools

import numpy as np
import jax
import jax.numpy as jnp
from jax import lax
from jax.experimental import pallas as pl
from jax.experimental.pallas import tpu as pltpu

F32 = jnp.float32
BF16 = jnp.bfloat16

EPS = 1e-6
LOG2E = 1.4426950408889634
SPAN_LIMIT = 100.0
HEAD_DIM = 128
HG_STREAMS = 5
FT_GROUPS = 8
SCAN_CHUNK = 64
SCAN_HEADS = 16
ROW_TILE = 512
PROJ_TILE = 1024
MOD_ROWS = 8
SUBLANES = 8
LANES = 128
VMEM_LIMIT = 56 * 1024 * 1024


def _cparams(sem):
    return pltpu.CompilerParams(dimension_semantics=sem, vmem_limit_bytes=VMEM_LIMIT)


def _sigmoid(x):
    return 0.5 * jnp.tanh(0.5 * x) + 0.5


def _silu(x):
    return _silu_half(0.5 * x)


def _silu_half(h):
    return h + h * jnp.tanh(h)


def _modulated_norm(x, g, shift, scale):
    var = jnp.mean(x * x, axis=-1, keepdims=True)
    return (x * lax.rsqrt(var + EPS)) * (g * (1.0 + scale)) + shift


def _ada_kernel(cv_ref, w_ref, b_ref, o_ref):
    a = _silu(cv_ref[...])
    o_ref[...] = jnp.dot(a, w_ref[...], preferred_element_type=F32,
                         precision=lax.Precision.HIGHEST) + b_ref[...]


def _ada_table(cv, ada_w, ada_b):
    depth, d, d3 = ada_w.shape
    tn = 1024
    return pl.pallas_call(
        _ada_kernel,
        grid=(depth, d3 // tn),
        in_specs=[
            pl.BlockSpec((MOD_ROWS, d), lambda l, j: (0, 0)),
            pl.BlockSpec((None, d, tn), lambda l, j: (l, 0, j)),
            pl.BlockSpec((None, 1, tn), lambda l, j: (l, 0, j)),
        ],
        out_specs=pl.BlockSpec((None, MOD_ROWS, tn), lambda l, j: (l, 0, j)),
        out_shape=jax.ShapeDtypeStruct((depth, MOD_ROWS, d3), F32),
        compiler_params=_cparams(("arbitrary", "arbitrary")),
        name="ada_table",
    )(cv, ada_w, ada_b.reshape(depth, 1, d3))


def _proj0_kernel(ctx_ref, x_ref, mod_ref, ng_ref, w_ref, lbl_ref, a_ref, h_ref,
                  *, d, e, tiles_per_batch, lb_index, sub):
    i = pl.program_id(0)
    j = pl.program_id(1)
    tm = h_ref.shape[0]

    @pl.when(j == 0)
    def _():
        is_ctx = i == 0
        m = mod_ref[pl.ds(jnp.where(is_ctx, 2, (i - 1) // tiles_per_batch), 1), :]
        src = jnp.where(is_ctx, ctx_ref[...], x_ref[...])
        h_ref[...] = _modulated_norm(src, ng_ref[...], m[:, 0:d], m[:, d:2 * d]).astype(BF16)

    def acc(c):
        return jnp.dot(h_ref[...], w_ref[:, c * sub:(c + 1) * sub], preferred_element_type=F32)

    nsub = w_ref.shape[1] // sub
    per_sub = sub // LANES
    per_stream = e // sub

    def put(c, val):
        for cl in range(per_sub):
            a_ref[c * per_sub + cl] = val[:, cl * LANES:(cl + 1) * LANES]

    def half_gate(dirn, cols):
        logits = lbl_ref[:, dirn:dirn + 1, cols]
        mx = jnp.max(logits, axis=0, keepdims=True)
        ex = jnp.exp(logits - mx)
        p = ex / jnp.sum(ex, axis=0, keepdims=True)
        return 0.5 * (1.0 - jnp.sum(p[0:lb_index + 1], axis=0))

    for step in range(pl.cdiv(HG_STREAMS * e, w_ref.shape[1])):
        @pl.when(j == step)
        def _(step=step):
            for c in range(nsub):
                stream, cs = divmod(step * nsub + c, per_stream)
                if stream in (0, 4):
                    put(c, _silu_half(acc(c)).astype(BF16))
                elif stream == 3:
                    put(c, acc(c).astype(BF16))
                else:
                    half = half_gate(stream - 1, slice(cs * sub, (cs + 1) * sub))
                    put(c, (half * (1.0 - jnp.tanh(acc(c)))).astype(BF16))


def _proj0(ctx2d, x2d, mods, norm_g, w_bf, lb_logits, *, n_lat, lb_index):
    rows_ctx, d = ctx2d.shape
    rows_lat = x2d.shape[0]
    e = w_bf.shape[1] // HG_STREAMS
    tm = PROJ_TILE
    assert rows_ctx == tm and n_lat % tm == 0
    n_tiles = 1 + rows_lat // tm
    rows = tm + rows_lat
    nl = lb_logits.shape[0]
    sub = 512
    col_steps = HG_STREAMS - 1
    tn = HG_STREAMS * e // col_steps
    assert tn % sub == 0 and e % sub == 0
    kern = functools.partial(_proj0_kernel, d=d, e=e, tiles_per_batch=n_lat // tm,
                             lb_index=lb_index, sub=sub)
    return pl.pallas_call(
        kern,
        grid=(n_tiles, col_steps),
        in_specs=[
            pl.BlockSpec((tm, d), lambda i, j: (0, 0)),
            pl.BlockSpec((tm, d), lambda i, j: (jnp.maximum(i - 1, 0), 0)),
            pl.BlockSpec((None, MOD_ROWS, 3 * d), lambda i, j: (0, 0, 0)),
            pl.BlockSpec((None, 1, d), lambda i, j: (0, 0, 0)),
            pl.BlockSpec((d, tn), lambda i, j: (0, j)),
            pl.BlockSpec((nl, 2, e), lambda i, j: (0, 0, 0)),
        ],
        out_specs=pl.BlockSpec((tn // LANES, tm, LANES), lambda i, j: (j, i, 0)),
        out_shape=jax.ShapeDtypeStruct((HG_STREAMS * (e // LANES), rows, LANES), BF16),
        scratch_shapes=[pltpu.VMEM((tm, d), BF16)],
        compiler_params=_cparams(("arbitrary", "arbitrary")),
        name="proj0",
    )(ctx2d, x2d, mods, norm_g, w_bf, lb_logits)


def _scan_kernel(qf_ref, kf_ref, vf_ref, qb_ref, kb_ref, vb_ref, of_ref, ob_ref, span_ref,
                 st_ref, pa_ref, pb_ref, da_ref, db_ref, sp_ref, *, n_chunks):
    s = pl.program_id(2)
    c_len = SCAN_CHUNK

    @pl.when(s == 0)
    def _():
        st_ref[...] = jnp.zeros_like(st_ref)
        pb_ref[...] = jnp.zeros_like(pb_ref)
        db_ref[...] = jnp.zeros_like(db_ref)
        sp_ref[...] = jnp.zeros_like(sp_ref)

    row = lax.broadcasted_iota(jnp.int32, (c_len, c_len), 0)
    col = lax.broadcasted_iota(jnp.int32, (c_len, c_len), 1)
    causal = (col <= row, col >= row)
    tri = tuple(m.astype(F32).astype(BF16) for m in causal)
    end_row = (c_len - 1, 0)
    mid = c_len // 2
    qk_refs = ((qf_ref, kf_ref), (qb_ref, kb_ref))
    vo_refs = ((vf_ref, of_ref), (vb_ref, ob_ref))
    nt = (((1,), (1,)), ((), ()))
    tn = (((0,), (0,)), ((), ()))
    QT, KT, QH, KH = range(4)

    def chunk_of(dirn, r):
        return r if dirn == 0 else n_chunks - 1 - r

    def prepare(p_ref, d_ref, dirn, r, hh):
        q_ref, k_ref = qk_refs[dirn]
        c = chunk_of(dirn, r)
        rows = slice(c * c_len, (c + 1) * c_len)
        q = q_ref[hh, rows, :].astype(F32)
        k = k_ref[hh, rows, :].astype(F32)
        b = jnp.dot(tri[dirn], jnp.log(1.0 - k).astype(BF16), preferred_element_type=F32)
        tot = b[end_row[dirn]:end_row[dirn] + 1, :]
        ref = b[mid:mid + 1, :]
        dl = (b - ref) * LOG2E
        ends = jnp.maximum(jnp.abs(dl[0:1, :]), jnp.abs(dl[c_len - 1:c_len, :]))
        sp_ref[dirn, hh] = jnp.maximum(sp_ref[dirn, hh], ends)
        qt = q * jnp.exp2(dl)
        kt = k * jnp.exp2(-dl)
        p_ref[dirn, QT, hh, rows, :] = qt.astype(BF16)
        p_ref[dirn, KT, hh, rows, :] = kt.astype(BF16)
        p_ref[dirn, QH, hh, rows, :] = (qt * jnp.exp(ref)).astype(BF16)
        p_ref[dirn, KH, hh, rows, :] = (kt * jnp.exp(tot - ref)).astype(BF16)
        d_ref[dirn, c, hh] = jnp.exp(tot)

    def scores(p_ref, dirn, r, hh):
        v_ref = vo_refs[dirn][0]
        c = chunk_of(dirn, r)
        rows = slice(c * c_len, (c + 1) * c_len)
        sc = lax.dot_general(p_ref[dirn, QT, hh, rows, :], p_ref[dirn, KT, hh, rows, :],
                             nt, preferred_element_type=F32)
        prob = jnp.where(causal[dirn], sc, 0.0).astype(BF16)
        upd = lax.dot_general(v_ref[hh, rows, :], p_ref[dirn, KH, hh, rows, :],
                              tn, preferred_element_type=F32)
        return prob, upd

    def outputs(p_ref, d_ref, dirn, r, hh, prob, upd):
        v_ref, o_ref = vo_refs[dirn]
        c = chunk_of(dirn, r)
        rows = slice(c * c_len, (c + 1) * c_len)
        st = st_ref[dirn, hh]
        o = (jnp.dot(prob, v_ref[hh, rows, :], preferred_element_type=F32)
             + lax.dot_general(p_ref[dirn, QH, hh, rows, :], st.astype(BF16), nt,
                               preferred_element_type=F32))
        o_ref[hh, rows, :] = o.astype(BF16)
        st_ref[dirn, hh] = st * d_ref[dirn, c, hh] + upd

    def step(p_new, d_new, p_old, d_old):
        heads = range(SCAN_HEADS)
        pu = {(dirn, hh): scores(p_old, dirn, 0, hh) for hh in heads for dirn in range(2)}
        for r in range(n_chunks):
            pu_next = {}
            for hh in heads:
                for dirn in range(2):
                    if r + 1 < n_chunks:
                        pu_next[dirn, hh] = scores(p_old, dirn, r + 1, hh)
                    outputs(p_old, d_old, dirn, r, hh, *pu[dirn, hh])
                    prepare(p_new, d_new, dirn, r, hh)
            pu = pu_next

    @pl.when(s % 2 == 0)
    def _():
        step(pa_ref, da_ref, pb_ref, db_ref)

    @pl.when(s % 2 == 1)
    def _():
        step(pb_ref, db_ref, pa_ref, da_ref)

    @pl.when(s == pl.num_programs(2) - 1)
    def _():
        span_ref[...] = sp_ref[...]


def _scan(a, *, batch, n_lat, n_ctx, e):
    ts = n_ctx
    assert ts % SCAN_CHUNK == 0 and n_lat % ts == 0
    lat_steps = n_lat // ts
    steps = 1 + lat_steps
    n_chunks = ts // SCAN_CHUNK
    assert HEAD_DIM == LANES
    groups = e // (SCAN_HEADS * HEAD_DIM)
    rows = batch * (n_ctx + n_lat)
    assert batch * n_ctx <= PROJ_TILE and PROJ_TILE % ts == 0

    def row_f(lat0):
        return lambda b, s: jnp.where(s == 0, b, lat0 + lat_steps * b + s - 1)

    def row_b(lat0):
        return lambda b, s: jnp.where(s == 0, b, lat0 + lat_steps * b + lat_steps - s)

    in_f, in_b = row_f(PROJ_TILE // ts), row_b(PROJ_TILE // ts)
    out_f, out_b = row_f(batch), row_b(batch)

    ahead = lambda g: jnp.minimum(g, steps - 1)
    behind = lambda g: jnp.maximum(g - 1, 0)

    def spec(stream, rfn, when):
        return pl.BlockSpec((SCAN_HEADS, ts, LANES),
                            lambda b, hg, g: (stream * groups + hg, rfn(b, when(g)), 0))

    kern = functools.partial(_scan_kernel, n_chunks=n_chunks)
    operands = pltpu.VMEM((2, 4, SCAN_HEADS, ts, LANES), BF16)
    decays = pltpu.VMEM((2, n_chunks, SCAN_HEADS, 1, LANES), F32)
    return pl.pallas_call(
        kern,
        grid=(batch, groups, steps + 1),
        in_specs=[spec(0, in_f, ahead), spec(1, in_f, ahead), spec(3, in_f, behind),
                  spec(0, in_b, ahead), spec(2, in_b, ahead), spec(3, in_b, behind)],
        out_specs=[pl.BlockSpec((SCAN_HEADS, ts, LANES), lambda b, hg, g: (hg, out_f(b, behind(g)), 0)),
                   pl.BlockSpec((SCAN_HEADS, ts, LANES), lambda b, hg, g: (hg, out_b(b, behind(g)), 0)),
                   pl.BlockSpec((None, None, 2, SCAN_HEADS, 1, LANES), lambda b, hg, g: (b, hg, 0, 0, 0, 0))],
        out_shape=[jax.ShapeDtypeStruct((e // LANES, rows, LANES), BF16)] * 2
        + [jax.ShapeDtypeStruct((batch, groups, 2, SCAN_HEADS, 1, LANES), F32)],
        scratch_shapes=[pltpu.VMEM((2, SCAN_HEADS, HEAD_DIM, HEAD_DIM), F32),
                        operands, operands, decays, decays,
                        pltpu.VMEM((2, SCAN_HEADS, 1, LANES), F32)],
        compiler_params=_cparams(("arbitrary", "arbitrary", "arbitrary")),
        name="hgrn_scan",
    )(a, a, a, a, a, a)


def _scan_exact_kernel(q_ref, k_ref, v_ref, o_ref, st_ref, *, n_chunks, reverse):
    c_len = SCAN_CHUNK

    @pl.when(pl.program_id(2) == 0)
    def _():
        st_ref[...] = jnp.zeros_like(st_ref)

    lane = lax.broadcasted_iota(jnp.int32, (1, c_len), 1)

    def chunk(ci, carry):
        c = (n_chunks - 1 - ci) if reverse else ci
        rows = pl.ds(pl.multiple_of(c * c_len, c_len), c_len)
        q = q_ref[rows, :].astype(F32)
        k = k_ref[rows, :].astype(F32)
        f = 1.0 - k
        vt = v_ref[rows, :].astype(F32).T
        st = st_ref[...]
        ot = jnp.zeros((HEAD_DIM, c_len), F32)
        for step in range(c_len):
            t = c_len - 1 - step if reverse else step
            st = st * f[t:t + 1, :] + vt[:, t:t + 1] * k[t:t + 1, :]
            o_col = jnp.sum(st * q[t:t + 1, :], axis=-1, keepdims=True)
            ot = ot + o_col * (lane == t).astype(F32)
        st_ref[...] = st
        o_ref[rows, :] = ot.T.astype(BF16)
        return carry

    lax.fori_loop(0, n_chunks, chunk, 0)


def _scan_exact(a, *, batch, n_lat, n_ctx, e):
    ts = n_ctx
    lat_steps = n_lat // ts
    steps = 1 + lat_steps
    heads = e // HEAD_DIM
    rows = batch * (n_ctx + n_lat)
    outs = []
    for reverse, k_stream in ((False, 1), (True, 2)):
        def row(lat0, reverse=reverse):
            if reverse:
                return lambda b, s: jnp.where(s == 0, b, lat0 + lat_steps * b + lat_steps - s)
            return lambda b, s: jnp.where(s == 0, b, lat0 + lat_steps * b + s - 1)

        rin, rout = row(PROJ_TILE // ts), row(batch)

        def spec(stream, rin=rin):
            return pl.BlockSpec((None, ts, LANES), lambda b, h, s: (stream * heads + h, rin(b, s), 0))

        outs.append(pl.pallas_call(
            functools.partial(_scan_exact_kernel, n_chunks=ts // SCAN_CHUNK, reverse=reverse),
            grid=(batch, heads, steps),
            in_specs=[spec(0), spec(k_stream), spec(3)],
            out_specs=pl.BlockSpec((None, ts, LANES), lambda b, h, s, rout=rout: (h, rout(b, s), 0)),
            out_shape=jax.ShapeDtypeStruct((heads, rows, LANES), BF16),
            scratch_shapes=[pltpu.VMEM((HEAD_DIM, HEAD_DIM), F32)],
            compiler_params=_cparams(("arbitrary", "arbitrary", "arbitrary")),
            name="hgrn_scan_exact_bwd" if reverse else "hgrn_scan_exact_fwd",
        )(a, a, a))
    return outs


OUT0_SLOTS = 3


def _out0_kernel(of_hbm, ob_hbm, a_hbm, ctx_ref, x_ref, mod_ref, hg_ref, w_ref,
                 xo_ref, co_ref, of_buf, ob_buf, z_buf, sems, y_ref,
                 *, d, e, tiles_per_batch, tm, skip, z_slab0):
    i = pl.program_id(0)
    n = pl.num_programs(0)
    heads = e // HEAD_DIM

    def copies(step, slot):
        rows = pl.ds(pl.multiple_of(step * tm, tm), tm)
        z_rows = pl.ds(pl.multiple_of(jnp.where(step == 0, 0, step + skip) * tm, tm), tm)
        return (
            pltpu.make_async_copy(of_hbm.at[:, rows, :], of_buf.at[slot], sems.at[0, slot]),
            pltpu.make_async_copy(ob_hbm.at[:, rows, :], ob_buf.at[slot], sems.at[1, slot]),
            pltpu.make_async_copy(a_hbm.at[pl.ds(z_slab0, heads), z_rows, :], z_buf.at[slot],
                                  sems.at[2, slot]),
        )

    @pl.when(i == 0)
    def _():
        for first in range(OUT0_SLOTS - 1):
            for stream, cp in enumerate(copies(first, first)):
                cp.start(priority=stream % 2)

    ahead = i + OUT0_SLOTS - 1

    @pl.when(ahead < n)
    def _():
        for stream, cp in enumerate(copies(ahead, ahead % OUT0_SLOTS)):
            cp.start(priority=stream % 2)

    for cp in copies(i, i % OUT0_SLOTS):
        cp.wait()

    def body(of_ref, ob_ref, z_ref):
        group = 4
        mix = None
        for h0 in range(0, heads, group):
            for h in range(h0, h0 + group):
                lanes = slice(h * HEAD_DIM, (h + 1) * HEAD_DIM)
                o = of_ref[h].astype(F32) + ob_ref[h].astype(F32)
                var = jnp.mean(o * o, axis=-1, keepdims=True)
                yn = (o * lax.rsqrt(var + EPS)) * hg_ref[...]
                y_ref[:, lanes] = (yn * z_ref[h].astype(F32)).astype(BF16)
            cols = slice(h0 * HEAD_DIM, (h0 + group) * HEAD_DIM)
            part = jnp.dot(y_ref[:, cols], w_ref[cols, :], preferred_element_type=F32)
            mix = part if mix is None else mix + part

        @pl.when(i == 0)
        def _():
            gate = mod_ref[2:3, 2 * d:3 * d]
            co_ref[...] = ctx_ref[...] + gate * mix

        @pl.when(i > 0)
        def _():
            gate = mod_ref[pl.ds((i - 1) // tiles_per_batch, 1), 2 * d:3 * d]
            xo_ref[...] = x_ref[...] + gate * mix

    for slot in range(OUT0_SLOTS):
        @pl.when(i % OUT0_SLOTS == slot)
        def _(slot=slot):
            body(of_buf.at[slot], ob_buf.at[slot], z_buf.at[slot])


def _out0(o_f, o_b, a, ctx2d, x2d, mods, hg_norm_g, w_bf, *, n_lat):
    rows_ctx, d = ctx2d.shape
    rows_lat = x2d.shape[0]
    heads = o_f.shape[0]
    e = heads * HEAD_DIM
    tm = ROW_TILE
    assert rows_ctx == tm and PROJ_TILE % tm == 0
    n_tiles = 1 + rows_lat // tm
    skip = PROJ_TILE // tm - 1
    assert n_tiles >= OUT0_SLOTS
    kern = functools.partial(_out0_kernel, d=d, e=e, tiles_per_batch=n_lat // tm, tm=tm, skip=skip,
                             z_slab0=(HG_STREAMS - 1) * heads)
    lat_idx = lambda i: (jnp.maximum(i - 1, 0), 0)
    ring = pltpu.VMEM((OUT0_SLOTS, heads, tm, HEAD_DIM), BF16)
    return pl.pallas_call(
        kern,
        grid=(n_tiles,),
        in_specs=[
            pl.BlockSpec(memory_space=pl.ANY),
            pl.BlockSpec(memory_space=pl.ANY),
            pl.BlockSpec(memory_space=pl.ANY),
            pl.BlockSpec((tm, d), lambda i: (0, 0)),
            pl.BlockSpec((tm, d), lat_idx),
            pl.BlockSpec((None, MOD_ROWS, 3 * d), lambda i: (0, 0, 0)),
            pl.BlockSpec((1, HEAD_DIM), lambda i: (0, 0)),
            pl.BlockSpec((e, d), lambda i: (0, 0)),
        ],
        out_specs=[pl.BlockSpec((tm, d), lat_idx), pl.BlockSpec((tm, d), lambda i: (0, 0))],
        out_shape=[jax.ShapeDtypeStruct((rows_lat, d), F32), jax.ShapeDtypeStruct((rows_ctx, d), F32)],
        scratch_shapes=[ring, ring, ring, pltpu.SemaphoreType.DMA((3, OUT0_SLOTS)),
                        pltpu.VMEM((tm, e), BF16)],
        compiler_params=_cparams(("arbitrary",)),
        name="readout0",
    )(o_f, o_b, a, ctx2d, x2d, mods, hg_norm_g, w_bf)


def _proj1_kernel(x_ref, mod_ref, ng_ref, w_ref, u_ref, z_ref, h_ref,
                  *, d, e, tiles_per_batch, sub, n2):
    i = pl.program_id(0)
    tm = h_ref.shape[0]
    m = mod_ref[pl.ds(i // tiles_per_batch, 1), :]
    h_ref[...] = _modulated_norm(x_ref[...], ng_ref[...], m[:, 0:d], m[:, d:2 * d]).astype(BF16)

    def acc(c):
        return jnp.dot(h_ref[...], w_ref[:, c * sub:(c + 1) * sub], preferred_element_type=F32)

    nsub = e // sub

    for c in range(nsub):
        a = acc(c)
        for n1l in range(tm // n2):
            for m8 in range(n2 // SUBLANES):
                src = n1l * n2 + m8 * SUBLANES
                for cl in range(sub // LANES):
                    u_ref[m8, c * (sub // LANES) + cl, n1l * SUBLANES:(n1l + 1) * SUBLANES, :] = (
                        a[src:src + SUBLANES, cl * LANES:(cl + 1) * LANES])

    for c in range(nsub):
        z_ref[:, c * sub:(c + 1) * sub] = _silu_half(acc(nsub + c)).astype(BF16)


def _proj1(x2d, mods, norm_g, w_bf, *, batch, n_lat, layer):
    rows, d = x2d.shape
    e = w_bf.shape[1] // 2
    tm = PROJ_TILE
    n1, n2 = _dft_factors(n_lat)
    q = n2 // SUBLANES
    tpb = n_lat // tm
    rows_blk = (tm // n2) * SUBLANES
    kern = functools.partial(_proj1_kernel, d=d, e=e, tiles_per_batch=tpb, sub=512, n2=n2)
    return pl.pallas_call(
        kern,
        grid=(rows // tm,),
        in_specs=[
            pl.BlockSpec((tm, d), lambda i: (i, 0)),
            pl.BlockSpec((None, MOD_ROWS, 3 * d), lambda i: (layer, 0, 0)),
            pl.BlockSpec((None, 1, d), lambda i: (layer, 0, 0)),
            pl.BlockSpec((d, 2 * e), lambda i: (0, 0), pipeline_mode=pl.Buffered(1)),
        ],
        out_specs=[pl.BlockSpec((None, q, e // LANES, rows_blk, LANES),
                                lambda i: (i // tpb, 0, 0, i % tpb, 0)),
                   pl.BlockSpec((tm, e), lambda i: (i, 0))],
        out_shape=[jax.ShapeDtypeStruct((batch, q, e // LANES, n1 * SUBLANES, LANES), F32),
                   jax.ShapeDtypeStruct((rows, e), BF16)],
        scratch_shapes=[pltpu.VMEM((tm, d), BF16)],
        compiler_params=_cparams(("arbitrary",)),
        name="proj1",
    )(x2d, mods, norm_g, w_bf)


def _dft_factors(n):
    n1 = 128 if n % 128 == 0 else n
    return n1, n // n1


def _dft_tables(n, group_dim):
    n1, n2 = _dft_factors(n)
    a2 = np.arange(n2, dtype=np.float64)[:, None, None]
    k1 = np.arange(n1, dtype=np.float64)[None, :, None]
    m1 = np.arange(n1, dtype=np.float64)[None, None, :]
    ang = -2.0 * np.pi * (a2 * k1 / n + m1 * k1 / n1)
    stage1 = np.concatenate([np.cos(ang), np.sin(ang)], axis=1) / np.sqrt(n1)
    kk = np.arange(n2, dtype=np.float64)
    ang2 = -2.0 * np.pi * np.outer(kk, kk) / n2
    fr, fi = np.cos(ang2) / np.sqrt(n2), np.sin(ang2) / np.sqrt(n2)
    stage2 = np.stack([np.concatenate([fr, fi], axis=0),
                       np.concatenate([-fi, fr], axis=0)])
    cc = np.arange(group_dim, dtype=np.float64)
    ang3 = 2.0 * np.pi * np.outer(cc, cc) / group_dim
    chan = np.concatenate([np.cos(ang3), np.sin(ang3)], axis=0) / np.sqrt(group_dim)
    to = lambda t: jnp.asarray(t, dtype=F32).astype(BF16)
    return to(stage1), to(stage2), to(chan)


def _dft_kernel(m_ref, f_ref, cs_ref, x_ref, y_ref, tr_ref, ti_ref, ys_ref, *, n1, n2, gd, kt_step):
    half = pl.program_id(2)
    q, nlb = x_ref.shape[0], x_ref.shape[1]

    @pl.when(half == 0)
    def _():
        def stage1(m, carry):
            for r in range(SUBLANES):
                rows = pl.ds(r, n1, stride=SUBLANES)
                xr = jnp.concatenate([x_ref[m, lb, rows, :] for lb in range(nlb)], axis=1)
                t = jnp.dot(m_ref[m, r], xr.astype(BF16), preferred_element_type=F32)
                for lb in range(nlb):
                    lanes = slice(lb * LANES, (lb + 1) * LANES)
                    tr_ref[m, lb, rows, :] = t[0:n1, lanes]
                    ti_ref[m, lb, rows, :] = t[n1:2 * n1, lanes]
            return carry

        lax.fori_loop(0, q, stage1, 0)

    width = nlb * LANES

    def gather(ref, kt):
        cols = []
        for j in range(SUBLANES):
            start = pl.multiple_of((kt * SUBLANES + j) * SUBLANES, SUBLANES)
            cols += [ref[:, lb, pl.ds(start, SUBLANES), :].reshape(n2, LANES) for lb in range(nlb)]
        return jnp.concatenate(cols, axis=1).astype(BF16)

    def by_k1(g):
        return jnp.concatenate([g[:, j * width:(j + 1) * width] for j in range(SUBLANES)], axis=0)

    def stage2(kl, carry):
        kt = half * kt_step + kl
        g = (jnp.dot(f_ref[0], gather(tr_ref, kt), preferred_element_type=F32)
             + jnp.dot(f_ref[1], gather(ti_ref, kt), preferred_element_type=F32))
        gr = by_k1(g[0:n2]).astype(BF16)
        gi = by_k1(g[n2:2 * n2]).astype(BF16)
        yg = (jnp.dot(gr, cs_ref[0:gd, :], preferred_element_type=F32)
              + jnp.dot(gi, cs_ref[gd:2 * gd, :], preferred_element_type=F32))
        for j in range(SUBLANES):
            rows = pl.ds(j, n2, stride=SUBLANES)
            for lb in range(nlb):
                ys_ref[lb, rows, :] = yg[j * n2:(j + 1) * n2, lb * LANES:(lb + 1) * LANES]
        y_ref[kl] = ys_ref[...].astype(BF16)
        return carry

    lax.fori_loop(0, kt_step, stage2, 0)


def _dft(u, stage1, stage2, chan):
    batch, q, nlb, rows, _ = u.shape
    n2, two_n1, n1 = stage1.shape
    gd = chan.shape[1]
    glb = gd // LANES
    kt = n1 // SUBLANES
    kt_step = max(kt // 2, 1)
    m4 = stage1.reshape(q, SUBLANES, two_n1, n1)
    scratch = pltpu.VMEM((q, glb, rows, LANES), F32)
    return pl.pallas_call(
        functools.partial(_dft_kernel, n1=n1, n2=n2, gd=gd, kt_step=kt_step),
        grid=(batch, nlb // glb, kt // kt_step),
        in_specs=[pl.BlockSpec((q, SUBLANES, two_n1, n1), lambda b, g, h: (0, 0, 0, 0)),
                  pl.BlockSpec((2, 2 * n2, n2), lambda b, g, h: (0, 0, 0)),
                  pl.BlockSpec((2 * gd, gd), lambda b, g, h: (0, 0)),
                  pl.BlockSpec((None, q, glb, rows, LANES), lambda b, g, h: (b, 0, g, 0, 0))],
        out_specs=pl.BlockSpec((None, kt_step, glb, n2 * SUBLANES, LANES),
                               lambda b, g, h: (b, h, g, 0, 0)),
        out_shape=jax.ShapeDtypeStruct((batch, kt, nlb, n2 * SUBLANES, LANES), BF16),
        scratch_shapes=[scratch, scratch, pltpu.VMEM((glb, n2 * SUBLANES, LANES), F32)],
        compiler_params=_cparams(("arbitrary", "arbitrary", "arbitrary")),
        name="dft",
    )(m4, stage2, chan, u)


def _out1_kernel(y_hbm, z_hbm, x_ref, mod_ref, w_ref, fg_ref, o_ref, yin_buf, z_buf, sems, y_ref,
                 *, d, n1, tiles_per_batch):
    i = pl.program_id(0)
    n = pl.num_programs(0)
    _, kt, nlb, blk_rows, _ = yin_buf.shape
    tm = z_buf.shape[1]

    def copies(step, slot):
        y_rows = pl.ds(pl.multiple_of((step % tiles_per_batch) * blk_rows, blk_rows), blk_rows)
        z_rows = pl.ds(pl.multiple_of(step * tm, tm), tm)
        return (
            pltpu.make_async_copy(y_hbm.at[step // tiles_per_batch, :, :, y_rows, :],
                                  yin_buf.at[slot], sems.at[0, slot]),
            pltpu.make_async_copy(z_hbm.at[z_rows, :], z_buf.at[slot], sems.at[1, slot]),
        )

    @pl.when(i == 0)
    def _():
        for first in range(OUT0_SLOTS - 1):
            for stream, cp in enumerate(copies(first, first)):
                cp.start(priority=stream % 2)

    ahead = i + OUT0_SLOTS - 1

    @pl.when(ahead < n)
    def _():
        for stream, cp in enumerate(copies(ahead, ahead % OUT0_SLOTS)):
            cp.start(priority=stream % 2)

    for cp in copies(i, i % OUT0_SLOTS):
        cp.wait()

    def body(y_in_ref, z_ref):
        k2_per_tile = blk_rows // SUBLANES
        pair = 2 * SUBLANES
        for t in range(k2_per_tile // 2):
            tile = slice(t * pair, (t + 1) * pair)
            for kp in range(kt // 2):
                blocks = [jnp.concatenate([y_in_ref[2 * kp + a, lb, tile, :] for lb in range(nlb)],
                                          axis=1).astype(F32) for a in range(2)]
                for s2 in range(2):
                    k2l = 2 * t + s2
                    dst = slice(k2l * n1 + kp * pair, k2l * n1 + (kp + 1) * pair)
                    yv = jnp.concatenate([blk[s2 * SUBLANES:(s2 + 1) * SUBLANES] for blk in blocks],
                                         axis=0)
                    y_ref[dst, :] = (yv * z_ref[dst, :].astype(F32)).astype(BF16)
        mix = jnp.dot(y_ref[...], w_ref[...], preferred_element_type=F32)
        gate = mod_ref[pl.ds(i // tiles_per_batch, 1), 2 * d:3 * d]
        x = x_ref[...] + gate * mix
        var = jnp.mean(x * x, axis=-1, keepdims=True)
        o_ref[...] = (x * lax.rsqrt(var + EPS)) * fg_ref[...]

    for slot in range(OUT0_SLOTS):
        @pl.when(i % OUT0_SLOTS == slot)
        def _(slot=slot):
            body(yin_buf.at[slot], z_buf.at[slot])


def _out1(y_in, zs, x2d, mods, w_bf, final_g, *, n_lat, n1, layer):
    rows, d = x2d.shape
    batch, kt, nlb, _, _ = y_in.shape
    e = nlb * LANES
    tm = PROJ_TILE
    assert tm % n1 == 0
    tpb = n_lat // tm
    blk_rows = (tm // n1) * SUBLANES
    assert rows // tm >= OUT0_SLOTS
    kern = functools.partial(_out1_kernel, d=d, n1=n1, tiles_per_batch=tpb)
    return pl.pallas_call(
        kern,
        grid=(rows // tm,),
        in_specs=[
            pl.BlockSpec(memory_space=pl.ANY),
            pl.BlockSpec(memory_space=pl.ANY),
            pl.BlockSpec((tm, d), lambda i: (i, 0)),
            pl.BlockSpec((None, MOD_ROWS, 3 * d), lambda i: (layer, 0, 0)),
            pl.BlockSpec((e, d), lambda i: (0, 0), pipeline_mode=pl.Buffered(1)),
            pl.BlockSpec((1, d), lambda i: (0, 0)),
        ],
        out_specs=pl.BlockSpec((tm, d), lambda i: (i, 0)),
        out_shape=jax.ShapeDtypeStruct((rows, d), F32),
        scratch_shapes=[pltpu.VMEM((OUT0_SLOTS, kt, nlb, blk_rows, LANES), BF16),
                        pltpu.VMEM((OUT0_SLOTS, tm, e), BF16),
                        pltpu.SemaphoreType.DMA((2, OUT0_SLOTS)),
                        pltpu.VMEM((tm, e), BF16)],
        compiler_params=_cparams(("arbitrary",)),
        name="readout1",
    )(y_in, zs, x2d, mods, w_bf, final_g)


def kernel(x, c, ctx, c_ctx, ada_w, ada_b, norm_g, hg_w_in, hg_lb_logits, hg_norm_g, hg_w_out,
           ft_w_in, ft_w_out, final_g):
    batch, n_lat, d = x.shape
    n_ctx = ctx.shape[1]
    depth = ada_w.shape[0]
    e = hg_w_out.shape[1]
    assert depth == 2 and batch == 2 and batch + 1 <= MOD_ROWS

    cv = jnp.concatenate([c, c_ctx[None, :], jnp.zeros((MOD_ROWS - batch - 1, d), F32)], axis=0)
    mods = _ada_table(cv, ada_w, ada_b)
    ng = norm_g.reshape(depth, 1, d)
    x2d = x.reshape(batch * n_lat, d)
    ctx2d = ctx.reshape(batch * n_ctx, d)

    ctx_pad = jnp.pad(ctx2d, ((0, PROJ_TILE - batch * n_ctx), (0, 0)))
    half_cols = jnp.where(jnp.arange(HG_STREAMS * e) // e == 3, 1.0, 0.5).astype(F32)
    a = _proj0(ctx_pad, x2d, mods, ng, (hg_w_in[0] * half_cols).astype(BF16), hg_lb_logits,
                  n_lat=n_lat, lb_index=0)
    o_f, o_b, span = _scan(a, batch=batch, n_lat=n_lat, n_ctx=n_ctx, e=e)
    o_f, o_b = lax.cond(jnp.logical_not(jnp.max(span) <= SPAN_LIMIT),
                        lambda: tuple(_scan_exact(a, batch=batch, n_lat=n_lat, n_ctx=n_ctx, e=e)),
                        lambda: (o_f, o_b))
    x1, _ctx1 = _out0(o_f, o_b, a, ctx2d, x2d, mods, hg_norm_g[0:1], hg_w_out[0].astype(BF16),
                      n_lat=n_lat)

    z_half = jnp.where(jnp.arange(2 * e) < e, 1.0, 0.5).astype(F32)
    u, zs = _proj1(x1, mods, ng, (ft_w_in[0] * z_half).astype(BF16), batch=batch, n_lat=n_lat,
                   layer=1)
    stage1, stage2, chan = _dft_tables(n_lat, e // FT_GROUPS)
    n1 = stage1.shape[2]
    y = _dft(u, stage1, stage2, chan)
    out = _out1(y, zs, x1, mods, ft_w_out[0].astype(BF16), final_g.reshape(1, d),
                n_lat=n_lat, n1=n1, layer=1)
    return out.reshape(batch, n_lat, d)
```
